```python
import math
import jax, jax.numpy as jnp
from jax import lax
import numpy as np

D_MODEL = 1024
BATCH = 8
SEQ = 8192
DEPTH = 2

N_MIXERS = 2
CONV_WIDTH = 31
N_HEADS = 16
N_KV_HEADS = 2
HEAD_DIM = 64
GROUP = N_HEADS // N_KV_HEADS
WINDOW = 128
BLOCK = WINDOW
QKV_DIM = (N_HEADS + 2 * N_KV_HEADS) * HEAD_DIM
ATTN_DIM = N_HEADS * HEAD_DIM
D_FF = 4 * D_MODEL
REL_BUCKETS = 32
REL_MAX_DIST = 128
NORM_EPS = 1e-6
N_CONV_LAYERS = (DEPTH + 1) // 2
N_ATTN_LAYERS = DEPTH // 2
NEG_INF = -1e30

kernel_name = "hybrid_conformer_conv_swa_sink_trunk"


def rms_norm(x, g):
    xf = x.astype(jnp.float32)
    y = xf * lax.rsqrt(jnp.mean(xf * xf, axis=-1, keepdims=True) + NORM_EPS)
    return (y * g.astype(jnp.float32)).astype(x.dtype)


def layer_norm(x, g, b):
    xf = x.astype(jnp.float32)
    mu = jnp.mean(xf, axis=-1, keepdims=True)
    xc = xf - mu
    y = xc * lax.rsqrt(jnp.mean(xc * xc, axis=-1, keepdims=True) + NORM_EPS)
    return (y * g.astype(jnp.float32) + b.astype(jnp.float32)).astype(x.dtype)


def t5_causal_bucket(dist):
    n = np.maximum(dist, 0)
    max_exact = REL_BUCKETS // 2
    large = max_exact + (np.log(np.maximum(n, 1).astype(np.float32) / max_exact)
                         / math.log(REL_MAX_DIST / max_exact)
                         * (REL_BUCKETS - max_exact)).astype(np.int32)
    large = np.minimum(large, REL_BUCKETS - 1)
    return np.where(n < max_exact, n, large).astype(np.int32)


def conformer_conv(x, norm_g, w_in, b_in, dw, dw_b, ln_g, ln_b, w_out, b_out):
    h = rms_norm(x, norm_g)
    u = h @ w_in + b_in
    a, gate = jnp.split(u, 2, axis=-1)
    u = a * jax.nn.sigmoid(gate)
    u = lax.conv_general_dilated(
        u, dw[:, None, :].astype(u.dtype), window_strides=(1,),
        padding=[(CONV_WIDTH - 1, 0)],
        dimension_numbers=("NWC", "WIO", "NWC"),
        feature_group_count=D_MODEL) + dw_b
    u = jax.nn.silu(layer_norm(u, ln_g, ln_b))
    return u @ w_out + b_out


def swa_sink_attention(x, norm_g, w_qkv, b_qkv, q_g, k_g, sinks, w_o, b_o, rel_bias):
    B, S, _ = x.shape
    nb = S // BLOCK
    h = rms_norm(x, norm_g)
    qkv = h @ w_qkv + b_qkv
    q, k, v = jnp.split(qkv, [ATTN_DIM, ATTN_DIM + N_KV_HEADS * HEAD_DIM], axis=-1)
    q = rms_norm(q.reshape(B, S, N_HEADS, HEAD_DIM), q_g)
    k = rms_norm(k.reshape(B, S, N_KV_HEADS, HEAD_DIM), k_g)
    v = v.reshape(B, S, N_KV_HEADS, HEAD_DIM)

    qb = q.reshape(B, nb, BLOCK, N_KV_HEADS, GROUP, HEAD_DIM)

    def band(t):
        tp = jnp.pad(t, ((0, 0), (BLOCK, 0), (0, 0), (0, 0)))
        tb = tp.reshape(B, nb + 1, BLOCK, N_KV_HEADS, HEAD_DIM)
        return jnp.concatenate([tb[:, :-1], tb[:, 1:]], axis=2)

    kband, vband = band(k), band(v)

    q_loc = np.arange(BLOCK)[:, None]
    k_loc = np.arange(2 * BLOCK)[None, :]
    dist = q_loc + BLOCK - k_loc
    in_window = (dist >= 0) & (dist < WINDOW)
    k_abs = (np.arange(nb)[:, None, None] - 1) * BLOCK + k_loc[None]
    mask = jnp.asarray(in_window[None] & (k_abs >= 0))
    bias = rel_bias[jnp.asarray(t5_causal_bucket(dist))]
    bias = jnp.transpose(bias, (2, 0, 1)).reshape(N_KV_HEADS, GROUP, BLOCK, 2 * BLOCK)

    scale = 1.0 / math.sqrt(HEAD_DIM)
    logits = jnp.einsum("bnqkgd,bnskd->bnkgqs", qb, kband).astype(jnp.float32) * scale
    logits = logits + bias.astype(jnp.float32)[None, None]
    logits = jnp.where(mask[None, :, None, None], logits, NEG_INF)
    sink_col = jnp.broadcast_to(
        sinks.astype(jnp.float32).reshape(1, 1, N_KV_HEADS, GROUP, 1, 1),
        logits.shape[:-1] + (1,))
    probs = jax.nn.softmax(jnp.concatenate([logits, sink_col], axis=-1), axis=-1)[..., :-1]
    out = jnp.einsum("bnkgqs,bnskd->bnqkgd", probs.astype(vband.dtype), vband)
    out = out.reshape(B, S, ATTN_DIM)
    return out @ w_o + b_o


def sq_relu_mlp(x, norm_g, w_up, w_down):
    h = rms_norm(x, norm_g)
    return jnp.square(jax.nn.relu(h @ w_up)) @ w_down


def _fwd_setup_inputs(seed: int = 0) -> dict:
    key = jax.random.key(seed)
    ks = iter(jax.random.split(key, 32))
    f32 = jnp.float32

    def nrm(shape, scale):
        return jax.random.normal(next(ks), shape, f32) * scale

    NC, NA = N_CONV_LAYERS, N_ATTN_LAYERS
    return {
        "x": nrm((BATCH, SEQ, D_MODEL), 1.0),
        "conv_norm_g": 1.0 + nrm((NC, D_MODEL), 0.02),
        "conv_w_in": nrm((NC, D_MODEL, 2 * D_MODEL), D_MODEL ** -0.5),
        "conv_b_in": nrm((NC, 2 * D_MODEL), 0.02),
        "conv_dw": nrm((NC, CONV_WIDTH, D_MODEL), CONV_WIDTH ** -0.5),
        "conv_dw_b": nrm((NC, D_MODEL), 0.02),
        "conv_ln_g": 1.0 + nrm((NC, D_MODEL), 0.02),
        "conv_ln_b": nrm((NC, D_MODEL), 0.02),
        "conv_w_out": nrm((NC, D_MODEL, D_MODEL), D_MODEL ** -0.5),
        "conv_b_out": nrm((NC, D_MODEL), 0.02),
        "attn_norm_g": 1.0 + nrm((NA, D_MODEL), 0.02),
        "w_qkv": nrm((NA, D_MODEL, QKV_DIM), D_MODEL ** -0.5),
        "b_qkv": nrm((NA, QKV_DIM), 0.02),
        "q_norm_g": 1.0 + nrm((NA, HEAD_DIM), 0.02),
        "k_norm_g": 1.0 + nrm((NA, HEAD_DIM), 0.02),
        "sinks": nrm((NA, N_HEADS), 0.5),
        "w_o": nrm((NA, ATTN_DIM, D_MODEL), ATTN_DIM ** -0.5),
        "b_o": nrm((NA, D_MODEL), 0.02),
        "rel_bias": nrm((REL_BUCKETS, N_HEADS), 0.5),
        "mlp_norm_g": 1.0 + nrm((DEPTH, D_MODEL), 0.02),
        "w_up": nrm((DEPTH, D_MODEL, D_FF), D_MODEL ** -0.5),
        "w_down": nrm((DEPTH, D_FF, D_MODEL), D_FF ** -0.5),
    }


def _fwd_reference(x, conv_norm_g, conv_w_in, conv_b_in, conv_dw, conv_dw_b, conv_ln_g,
              conv_ln_b, conv_w_out, conv_b_out, attn_norm_g, w_qkv, b_qkv, q_norm_g,
              k_norm_g, sinks, w_o, b_o, rel_bias, mlp_norm_g, w_up, w_down):
    for i in range(DEPTH):
        j = i // N_MIXERS
        if i % N_MIXERS == 0:
            x = x + conformer_conv(x, conv_norm_g[j], conv_w_in[j], conv_b_in[j],
                                   conv_dw[j], conv_dw_b[j], conv_ln_g[j], conv_ln_b[j],
                                   conv_w_out[j], conv_b_out[j])
        else:
            x = x + swa_sink_attention(x, attn_norm_g[j], w_qkv[j], b_qkv[j],
                                       q_norm_g[j], k_norm_g[j], sinks[j], w_o[j],
                                       b_o[j], rel_bias)
        x = x + sq_relu_mlp(x, mlp_norm_g[i], w_up[i], w_down[i])
    return x


import jax as _jax
import jax.numpy as _jnp

TWIN_FORMAT = 'train_step'
FWD_PARAMS = ['x', 'conv_norm_g', 'conv_w_in', 'conv_b_in', 'conv_dw', 'conv_dw_b', 'conv_ln_g', 'conv_ln_b', 'conv_w_out', 'conv_b_out', 'attn_norm_g', 'w_qkv', 'b_qkv', 'q_norm_g', 'k_norm_g', 'sinks', 'w_o', 'b_o', 'rel_bias', 'mlp_norm_g', 'w_up', 'w_down']
TWIN_WEIGHTS = ['conv_norm_g', 'conv_w_in', 'conv_b_in', 'conv_dw', 'conv_dw_b', 'conv_ln_g', 'conv_ln_b', 'conv_w_out', 'conv_b_out', 'attn_norm_g', 'w_qkv', 'b_qkv', 'q_norm_g', 'k_norm_g', 'sinks', 'w_o', 'b_o', 'rel_bias', 'mlp_norm_g', 'w_up', 'w_down']
TWIN_DIFF_INPUT = 'x'
TWIN_INPUTS = ['x', 'conv_norm_g', 'conv_w_in', 'conv_b_in', 'conv_dw', 'conv_dw_b', 'conv_ln_g', 'conv_ln_b', 'conv_w_out', 'conv_b_out', 'attn_norm_g', 'w_qkv', 'b_qkv', 'q_norm_g', 'k_norm_g', 'sinks', 'w_o', 'b_o', 'rel_bias', 'mlp_norm_g', 'w_up', 'w_down', 'loss_target', 'm_conv_norm_g', 'm_conv_w_in', 'm_conv_b_in', 'm_conv_dw', 'm_conv_dw_b', 'm_conv_ln_g', 'm_conv_ln_b', 'm_conv_w_out', 'm_conv_b_out', 'm_attn_norm_g', 'm_w_qkv', 'm_b_qkv', 'm_q_norm_g', 'm_k_norm_g', 'm_sinks', 'm_w_o', 'm_b_o', 'm_rel_bias', 'm_mlp_norm_g', 'm_w_up', 'm_w_down', 'v_conv_norm_g', 'v_conv_w_in', 'v_conv_b_in', 'v_conv_dw', 'v_conv_dw_b', 'v_conv_ln_g', 'v_conv_ln_b', 'v_conv_w_out', 'v_conv_b_out', 'v_attn_norm_g', 'v_w_qkv', 'v_b_qkv', 'v_q_norm_g', 'v_k_norm_g', 'v_sinks', 'v_w_o', 'v_b_o', 'v_rel_bias', 'v_mlp_norm_g', 'v_w_up', 'v_w_down']
TWIN_OUTPUTS = ['loss', 'grad_x', 'grad_conv_norm_g', 'grad_conv_w_in', 'grad_conv_b_in', 'grad_conv_dw', 'grad_conv_dw_b', 'grad_conv_ln_g', 'grad_conv_ln_b', 'grad_conv_w_out', 'grad_conv_b_out', 'grad_attn_norm_g', 'grad_w_qkv', 'grad_b_qkv', 'grad_q_norm_g', 'grad_k_norm_g', 'grad_sinks', 'grad_w_o', 'grad_b_o', 'grad_rel_bias', 'grad_mlp_norm_g', 'grad_w_up', 'grad_w_down', 'delta_conv_norm_g', 'delta_conv_w_in', 'delta_conv_b_in', 'delta_conv_dw', 'delta_conv_dw_b', 'delta_conv_ln_g', 'delta_conv_ln_b', 'delta_conv_w_out', 'delta_conv_b_out', 'delta_attn_norm_g', 'delta_w_qkv', 'delta_b_qkv', 'delta_q_norm_g', 'delta_k_norm_g', 'delta_sinks', 'delta_w_o', 'delta_b_o', 'delta_rel_bias', 'delta_mlp_norm_g', 'delta_w_up', 'delta_w_down', 'new_m_conv_norm_g', 'new_m_conv_w_in', 'new_m_conv_b_in', 'new_m_conv_dw', 'new_m_conv_dw_b', 'new_m_conv_ln_g', 'new_m_conv_ln_b', 'new_m_conv_w_out', 'new_m_conv_b_out', 'new_m_attn_norm_g', 'new_m_w_qkv', 'new_m_b_qkv', 'new_m_q_norm_g', 'new_m_k_norm_g', 'new_m_sinks', 'new_m_w_o', 'new_m_b_o', 'new_m_rel_bias', 'new_m_mlp_norm_g', 'new_m_w_up', 'new_m_w_down', 'new_v_conv_norm_g', 'new_v_conv_w_in', 'new_v_conv_b_in', 'new_v_conv_dw', 'new_v_conv_dw_b', 'new_v_conv_ln_g', 'new_v_conv_ln_b', 'new_v_conv_w_out', 'new_v_conv_b_out', 'new_v_attn_norm_g', 'new_v_w_qkv', 'new_v_b_qkv', 'new_v_q_norm_g', 'new_v_k_norm_g', 'new_v_sinks', 'new_v_w_o', 'new_v_b_o', 'new_v_rel_bias', 'new_v_mlp_norm_g', 'new_v_w_up', 'new_v_w_down']
TWIN_LEAF_KINDS = {'loss': 'loss', 'grad_x': 'grad_x', 'grad_conv_norm_g': 'grad_w', 'grad_conv_w_in': 'grad_w', 'grad_conv_b_in': 'grad_w', 'grad_conv_dw': 'grad_w', 'grad_conv_dw_b': 'grad_w', 'grad_conv_ln_g': 'grad_w', 'grad_conv_ln_b': 'grad_w', 'grad_conv_w_out': 'grad_w', 'grad_conv_b_out': 'grad_w', 'grad_attn_norm_g': 'grad_w', 'grad_w_qkv': 'grad_w', 'grad_b_qkv': 'grad_w', 'grad_q_norm_g': 'grad_w', 'grad_k_norm_g': 'grad_w', 'grad_sinks': 'grad_w', 'grad_w_o': 'grad_w', 'grad_b_o': 'grad_w', 'grad_rel_bias': 'grad_w', 'grad_mlp_norm_g': 'grad_w', 'grad_w_up': 'grad_w', 'grad_w_down': 'grad_w', 'delta_conv_norm_g': 'delta_w', 'delta_conv_w_in': 'delta_w', 'delta_conv_b_in': 'delta_w', 'delta_conv_dw': 'delta_w', 'delta_conv_dw_b': 'delta_w', 'delta_conv_ln_g': 'delta_w', 'delta_conv_ln_b': 'delta_w', 'delta_conv_w_out': 'delta_w', 'delta_conv_b_out': 'delta_w', 'delta_attn_norm_g': 'delta_w', 'delta_w_qkv': 'delta_w', 'delta_b_qkv': 'delta_w', 'delta_q_norm_g': 'delta_w', 'delta_k_norm_g': 'delta_w', 'delta_sinks': 'delta_w', 'delta_w_o': 'delta_w', 'delta_b_o': 'delta_w', 'delta_rel_bias': 'delta_w', 'delta_mlp_norm_g': 'delta_w', 'delta_w_up': 'delta_w', 'delta_w_down': 'delta_w', 'new_m_conv_norm_g': 'new_m', 'new_m_conv_w_in': 'new_m', 'new_m_conv_b_in': 'new_m', 'new_m_conv_dw': 'new_m', 'new_m_conv_dw_b': 'new_m', 'new_m_conv_ln_g': 'new_m', 'new_m_conv_ln_b': 'new_m', 'new_m_conv_w_out': 'new_m', 'new_m_conv_b_out': 'new_m', 'new_m_attn_norm_g': 'new_m', 'new_m_w_qkv': 'new_m', 'new_m_b_qkv': 'new_m', 'new_m_q_norm_g': 'new_m', 'new_m_k_norm_g': 'new_m', 'new_m_sinks': 'new_m', 'new_m_w_o': 'new_m', 'new_m_b_o': 'new_m', 'new_m_rel_bias': 'new_m', 'new_m_mlp_norm_g': 'new_m', 'new_m_w_up': 'new_m', 'new_m_w_down': 'new_m', 'new_v_conv_norm_g': 'new_v', 'new_v_conv_w_in': 'new_v', 'new_v_conv_b_in': 'new_v', 'new_v_conv_dw': 'new_v', 'new_v_conv_dw_b': 'new_v', 'new_v_conv_ln_g': 'new_v', 'new_v_conv_ln_b': 'new_v', 'new_v_conv_w_out': 'new_v', 'new_v_conv_b_out': 'new_v', 'new_v_attn_norm_g': 'new_v', 'new_v_w_qkv': 'new_v', 'new_v_b_qkv': 'new_v', 'new_v_q_norm_g': 'new_v', 'new_v_k_norm_g': 'new_v', 'new_v_sinks': 'new_v', 'new_v_w_o': 'new_v', 'new_v_b_o': 'new_v', 'new_v_rel_bias': 'new_v', 'new_v_mlp_norm_g': 'new_v', 'new_v_w_up': 'new_v', 'new_v_w_down': 'new_v'}


def _forward(args):
    return _fwd_reference(*[args[k] for k in FWD_PARAMS])


def _output_shape():
    def fwd():
        inp = _fwd_setup_inputs(0)
        return _fwd_reference(*[inp[k] for k in FWD_PARAMS])
    out = _jax.eval_shape(fwd)
    return out.shape, out.dtype

N_MICROBATCH = 1
ADAM_LR = 0.001
ADAM_B1 = 0.9
ADAM_B2 = 0.999
ADAM_EPS = 1e-08
ADAM_WD = 0.01
ADAM_STEP = 10
PER_EXAMPLE_BATCH_AXIS = {'x': 0, 'loss_target': 0}
SHARED_INPUTS = []
_WEIGHT_DTYPES = {'conv_norm_g': _jnp.float32, 'conv_w_in': _jnp.float32, 'conv_b_in': _jnp.float32, 'conv_dw': _jnp.float32, 'conv_dw_b': _jnp.float32, 'conv_ln_g': _jnp.float32, 'conv_ln_b': _jnp.float32, 'conv_w_out': _jnp.float32, 'conv_b_out': _jnp.float32, 'attn_norm_g': _jnp.float32, 'w_qkv': _jnp.float32, 'b_qkv': _jnp.float32, 'q_norm_g': _jnp.float32, 'k_norm_g': _jnp.float32, 'sinks': _jnp.float32, 'w_o': _jnp.float32, 'b_o': _jnp.float32, 'rel_bias': _jnp.float32, 'mlp_norm_g': _jnp.float32, 'w_up': _jnp.float32, 'w_down': _jnp.float32}
MOMENT_SCALE = {'conv_norm_g': 1.834108e+00, 'conv_w_in': 1.006733e+00, 'conv_b_in': 4.033091e+01, 'conv_dw': 7.076243e+00, 'conv_dw_b': 1.088321e+02, 'conv_ln_g': 5.496488e+01, 'conv_ln_b': 6.858708e+01, 'conv_w_out': 2.830391e+01, 'conv_b_out': 1.327473e+02, 'attn_norm_g': 3.623034e+01, 'w_qkv': 3.041356e+01, 'b_qkv': 8.907172e+01, 'q_norm_g': 1.099217e+01, 'k_norm_g': 1.112730e+01, 'sinks': 2.743669e+00, 'w_o': 2.891330e+01, 'b_o': 7.788911e+01, 'rel_bias': 1.159794e+00, 'mlp_norm_g': 1.996984e+02, 'w_up': 1.386137e+01, 'w_down': 4.522799e+01}


def _to_microbatches(a, axis):
    t = _jnp.moveaxis(a, axis, 0)
    t = t.reshape((N_MICROBATCH, t.shape[0] // N_MICROBATCH) + t.shape[1:])
    return _jnp.moveaxis(t, 1, axis + 1)


def setup_inputs(seed: int = 0) -> dict:
    inp = _fwd_setup_inputs(seed)
    key = _jax.random.fold_in(_jax.random.key(seed), 7919)
    shape, _ = _output_shape()
    out = dict(inp)
    out["loss_target"] = _jax.random.normal(_jax.random.fold_in(key, 0), shape, _jnp.float32)
    for i, name in enumerate(TWIN_WEIGHTS):
        w = inp[name].astype(_jnp.float32)
        if MOMENT_SCALE is None:
            s = _jnp.sqrt(_jnp.mean(_jnp.square(w)) + 1e-30)
        else:
            s = MOMENT_SCALE[name]
        km, kv = _jax.random.split(_jax.random.fold_in(key, i + 1))
        out[name] = w
        out["m_" + name] = s * _jax.random.normal(km, w.shape, _jnp.float32)
        out["v_" + name] = (s * s) * _jax.random.uniform(kv, w.shape, _jnp.float32, 0.5, 1.5)
    if N_MICROBATCH > 1:
        for name, axis in PER_EXAMPLE_BATCH_AXIS.items():
            out[name] = _to_microbatches(out[name], axis)
    return {'x': out['x'], 'conv_norm_g': out['conv_norm_g'], 'conv_w_in': out['conv_w_in'], 'conv_b_in': out['conv_b_in'], 'conv_dw': out['conv_dw'], 'conv_dw_b': out['conv_dw_b'], 'conv_ln_g': out['conv_ln_g'], 'conv_ln_b': out['conv_ln_b'], 'conv_w_out': out['conv_w_out'], 'conv_b_out': out['conv_b_out'], 'attn_norm_g': out['attn_norm_g'], 'w_qkv': out['w_qkv'], 'b_qkv': out['b_qkv'], 'q_norm_g': out['q_norm_g'], 'k_norm_g': out['k_norm_g'], 'sinks': out['sinks'], 'w_o': out['w_o'], 'b_o': out['b_o'], 'rel_bias': out['rel_bias'], 'mlp_norm_g': out['mlp_norm_g'], 'w_up': out['w_up'], 'w_down': out['w_down'], 'loss_target': out['loss_target'], 'm_conv_norm_g': out['m_conv_norm_g'], 'm_conv_w_in': out['m_conv_w_in'], 'm_conv_b_in': out['m_conv_b_in'], 'm_conv_dw': out['m_conv_dw'], 'm_conv_dw_b': out['m_conv_dw_b'], 'm_conv_ln_g': out['m_conv_ln_g'], 'm_conv_ln_b': out['m_conv_ln_b'], 'm_conv_w_out': out['m_conv_w_out'], 'm_conv_b_out': out['m_conv_b_out'], 'm_attn_norm_g': out['m_attn_norm_g'], 'm_w_qkv': out['m_w_qkv'], 'm_b_qkv': out['m_b_qkv'], 'm_q_norm_g': out['m_q_norm_g'], 'm_k_norm_g': out['m_k_norm_g'], 'm_sinks': out['m_sinks'], 'm_w_o': out['m_w_o'], 'm_b_o': out['m_b_o'], 'm_rel_bias': out['m_rel_bias'], 'm_mlp_norm_g': out['m_mlp_norm_g'], 'm_w_up': out['m_w_up'], 'm_w_down': out['m_w_down'], 'v_conv_norm_g': out['v_conv_norm_g'], 'v_conv_w_in': out['v_conv_w_in'], 'v_conv_b_in': out['v_conv_b_in'], 'v_conv_dw': out['v_conv_dw'], 'v_conv_dw_b': out['v_conv_dw_b'], 'v_conv_ln_g': out['v_conv_ln_g'], 'v_conv_ln_b': out['v_conv_ln_b'], 'v_conv_w_out': out['v_conv_w_out'], 'v_conv_b_out': out['v_conv_b_out'], 'v_attn_norm_g': out['v_attn_norm_g'], 'v_w_qkv': out['v_w_qkv'], 'v_b_qkv': out['v_b_qkv'], 'v_q_norm_g': out['v_q_norm_g'], 'v_k_norm_g': out['v_k_norm_g'], 'v_sinks': out['v_sinks'], 'v_w_o': out['v_w_o'], 'v_b_o': out['v_b_o'], 'v_rel_bias': out['v_rel_bias'], 'v_mlp_norm_g': out['v_mlp_norm_g'], 'v_w_up': out['v_w_up'], 'v_w_down': out['v_w_down']}


def _loss(weights, diff, rest, loss_target):
    with _jax.named_scope("forward"):
        args = {**rest, TWIN_DIFF_INPUT: diff, **{k: w.astype(_WEIGHT_DTYPES[k]) for k, w in weights.items()}}
        y = _forward(args)
    with _jax.named_scope("loss_head"):
        err = _jnp.square(y.astype(_jnp.float32) - loss_target)
        return 0.5 * _jnp.sum(_jnp.mean(err, axis=-1)) if err.ndim else 0.5 * err


def _adamw(w, g, m, v):
    m = ADAM_B1 * m + (1.0 - ADAM_B1) * g
    v = ADAM_B2 * v + (1.0 - ADAM_B2) * _jnp.square(g)
    m_hat = m / (1.0 - ADAM_B1 ** ADAM_STEP)
    v_hat = v / (1.0 - ADAM_B2 ** ADAM_STEP)
    delta = -ADAM_LR * (m_hat / (_jnp.sqrt(v_hat) + ADAM_EPS) + ADAM_WD * w)
    return delta, m, v


def reference(x, conv_norm_g, conv_w_in, conv_b_in, conv_dw, conv_dw_b, conv_ln_g, conv_ln_b, conv_w_out, conv_b_out, attn_norm_g, w_qkv, b_qkv, q_norm_g, k_norm_g, sinks, w_o, b_o, rel_bias, mlp_norm_g, w_up, w_down, loss_target, m_conv_norm_g, m_conv_w_in, m_conv_b_in, m_conv_dw, m_conv_dw_b, m_conv_ln_g, m_conv_ln_b, m_conv_w_out, m_conv_b_out, m_attn_norm_g, m_w_qkv, m_b_qkv, m_q_norm_g, m_k_norm_g, m_sinks, m_w_o, m_b_o, m_rel_bias, m_mlp_norm_g, m_w_up, m_w_down, v_conv_norm_g, v_conv_w_in, v_conv_b_in, v_conv_dw, v_conv_dw_b, v_conv_ln_g, v_conv_ln_b, v_conv_w_out, v_conv_b_out, v_attn_norm_g, v_w_qkv, v_b_qkv, v_q_norm_g, v_k_norm_g, v_sinks, v_w_o, v_b_o, v_rel_bias, v_mlp_norm_g, v_w_up, v_w_down):
    given = dict(x=x, conv_norm_g=conv_norm_g, conv_w_in=conv_w_in, conv_b_in=conv_b_in, conv_dw=conv_dw, conv_dw_b=conv_dw_b, conv_ln_g=conv_ln_g, conv_ln_b=conv_ln_b, conv_w_out=conv_w_out, conv_b_out=conv_b_out, attn_norm_g=attn_norm_g, w_qkv=w_qkv, b_qkv=b_qkv, q_norm_g=q_norm_g, k_norm_g=k_norm_g, sinks=sinks, w_o=w_o, b_o=b_o, rel_bias=rel_bias, mlp_norm_g=mlp_norm_g, w_up=w_up, w_down=w_down, loss_target=loss_target, m_conv_norm_g=m_conv_norm_g, m_conv_w_in=m_conv_w_in, m_conv_b_in=m_conv_b_in, m_conv_dw=m_conv_dw, m_conv_dw_b=m_conv_dw_b, m_conv_ln_g=m_conv_ln_g, m_conv_ln_b=m_conv_ln_b, m_conv_w_out=m_conv_w_out, m_conv_b_out=m_conv_b_out, m_attn_norm_g=m_attn_norm_g, m_w_qkv=m_w_qkv, m_b_qkv=m_b_qkv, m_q_norm_g=m_q_norm_g, m_k_norm_g=m_k_norm_g, m_sinks=m_sinks, m_w_o=m_w_o, m_b_o=m_b_o, m_rel_bias=m_rel_bias, m_mlp_norm_g=m_mlp_norm_g, m_w_up=m_w_up, m_w_down=m_w_down, v_conv_norm_g=v_conv_norm_g, v_conv_w_in=v_conv_w_in, v_conv_b_in=v_conv_b_in, v_conv_dw=v_conv_dw, v_conv_dw_b=v_conv_dw_b, v_conv_ln_g=v_conv_ln_g, v_conv_ln_b=v_conv_ln_b, v_conv_w_out=v_conv_w_out, v_conv_b_out=v_conv_b_out, v_attn_norm_g=v_attn_norm_g, v_w_qkv=v_w_qkv, v_b_qkv=v_b_qkv, v_q_norm_g=v_q_norm_g, v_k_norm_g=v_k_norm_g, v_sinks=v_sinks, v_w_o=v_w_o, v_b_o=v_b_o, v_rel_bias=v_rel_bias, v_mlp_norm_g=v_mlp_norm_g, v_w_up=v_w_up, v_w_down=v_w_down)
    weights = {n: given[n] for n in TWIN_WEIGHTS}
    shared = {n: given[n] for n in SHARED_INPUTS}
    per_example = {n: given[n] for n in ['x']}
    grad_fn = _jax.value_and_grad(_loss, argnums=(0, 1))

    def one_microbatch(ex, loss_target):
        ex = dict(ex)
        diff = ex.pop(TWIN_DIFF_INPUT)
        return grad_fn(weights, diff, {**shared, **ex}, loss_target)

    if N_MICROBATCH == 1:
        loss, (grad_w, grad_x) = one_microbatch(per_example, given["loss_target"])
    else:
        def body(carry, xs):
            loss_sum, grad_sum = carry
            l_k, (gw_k, gx_k) = one_microbatch(xs[0], xs[1])
            with _jax.named_scope("update"):
                return (loss_sum + l_k, _jax.tree.map(_jnp.add, grad_sum, gw_k)), gx_k

        init = (_jnp.zeros((), _jnp.float32), _jax.tree.map(_jnp.zeros_like, weights))
        (loss, grad_w), grad_x = _jax.lax.scan(body, init, (per_example, given["loss_target"]))
    with _jax.named_scope("update"):
        delta_w, new_m, new_v = {}, {}, {}
        for n in TWIN_WEIGHTS:
            delta_w[n], new_m[n], new_v[n] = _adamw(weights[n], grad_w[n], given["m_" + n], given["v_" + n])
    return (loss, grad_x, *[grad_w[n] for n in TWIN_WEIGHTS], *[delta_w[n] for n in TWIN_WEIGHTS],
            *[new_m[n] for n in TWIN_WEIGHTS], *[new_v[n] for n in TWIN_WEIGHTS])
```

```python
import math

import numpy as np
import jax
import jax.numpy as jnp
from jax import lax
from jax.experimental import pallas as pl
from jax.experimental.pallas import tpu as pltpu

F32 = jnp.float32
BF16 = jnp.bfloat16

D = 1024
DFF = 4096
NH, NKV, HD = 16, 2, 64
KV = NKV * HD
QKV = D + 2 * KV
CONVW = 31
WIN = 128
NBKT = 32
EPS = 1e-6
NEG = -1e30
NDEV = 8
FFB = DFF // NDEV
HALO = 32

LR, B1, B2, AEPS, WD, STEP = 0.001, 0.9, 0.999, 1e-08, 0.01, 10

PACK = (("conv_w_in", 256), ("conv_w_out", 128), ("w_qkv", 160), ("w_o", 128), ("w_up", 1024), ("w_down", 1024))
PACK_ROWS = sum(r for _, r in PACK)
PACK_CH = 272
MESH = pl.DeviceIdType.MESH


def _cp(vmem_mb, n_axes=1):
    return pltpu.CompilerParams(dimension_semantics=("arbitrary",) * n_axes, vmem_limit_bytes=vmem_mb << 20)


def _const(shape):
    nd = len(shape)
    return pl.BlockSpec(shape, lambda *_: (0,) * nd, pipeline_mode=pl.Buffered(1))


def _acc(shape):
    nd = len(shape)
    return pl.BlockSpec(shape, lambda *_: (0,) * nd)


def _rows(tm, n):
    return pl.BlockSpec((tm, n), lambda i: (i, 0))


def _sds(shape, dtype):
    return jax.ShapeDtypeStruct(shape, dtype)


def _dot(a, b):
    return jnp.dot(a, b, preferred_element_type=F32)


def _dot_nt(a, b):
    return lax.dot_general(a, b, (((1,), (1,)), ((), ())), preferred_element_type=F32)


def _dot_tn(a, b):
    return lax.dot_general(a, b, (((0,), (0,)), ((), ())), preferred_element_type=F32)


def _dot_hi(x, e):
    x1 = x.astype(BF16)
    r1 = x - x1.astype(F32)
    x2 = r1.astype(BF16)
    x3 = (r1 - x2.astype(F32)).astype(BF16)
    return _dot(x1, e) + _dot(x2, e) + _dot(x3, e)


def _rms(x):
    return lax.rsqrt(jnp.mean(x * x, axis=-1, keepdims=True) + EPS)


def _rms_bwd(dh, x, r, g):
    xh = x * r
    dxh = dh * g
    dx = r * (dxh - xh * jnp.mean(dxh * xh, axis=-1, keepdims=True))
    return dx, jnp.sum(dh * xh, axis=0, keepdims=True)


def _colsum(a):
    return jnp.sum(a, axis=0, keepdims=True)


def _first(ref, val):
    @pl.when(pl.program_id(0) == 0)
    def _():
        ref[...] = jnp.zeros_like(ref)
    ref[...] += val


def _first_cols(ref, vals):
    @pl.when(pl.program_id(0) == 0)
    def _():
        ref[...] = jnp.zeros_like(ref)
    for c0, val in vals:
        ref[:, c0:c0 + val.shape[1]] += val


def _conv_in_fwd(x, g, w, b):
    S = x.shape[0]
    tm = min(512, S)

    def body(x_ref, g_ref, w_ref, b_ref, h_ref, u_ref, glu_ref):
        xv = x_ref[...]
        hb = (xv * _rms(xv) * g_ref[...]).astype(BF16)
        h_ref[...] = hb
        u = _dot(hb, w_ref[...]) + b_ref[...]
        u_ref[...] = u
        glu_ref[...] = u[:, :D] * jax.nn.sigmoid(u[:, D:])

    return pl.pallas_call(
        body, name="conv_in_fwd", grid=(S // tm,),
        in_specs=[_rows(tm, D), _const((1, D)), _const((D, 2 * D)), _const((1, 2 * D))],
        out_specs=[_rows(tm, D), _rows(tm, 2 * D), _rows(tm, D)],
        out_shape=[_sds((S, D), BF16), _sds((S, 2 * D), F32), _sds((S, D), F32)],
        compiler_params=_cp(48))(x, g, w, b)


def _fill_ext(ext, halo_ref, cur_ref, tm, keep):
    ext[0:HALO, :] = jnp.where(keep, halo_ref[...], 0.0)
    ext[HALO:HALO + tm, :] = cur_ref[...]


def _conv_mid_fwd(glu, x, dw, dwb, lng, lnb, w, b):
    S = x.shape[0]
    tm = min(256, S)
    hb = tm // HALO

    def body(gl_ref, halo_ref, x_ref, dw_ref, dwb_ref, lng_ref, lnb_ref, w_ref, b_ref, c_ref, s_ref, xo_ref, ext):
        i = pl.program_id(0)
        _fill_ext(ext, halo_ref, gl_ref, tm, i > 0)
        for r0 in range(0, tm, 32):
            for c0 in range(0, D, 512):
                acc = jnp.zeros((32, 512), F32) + dwb_ref[:, c0:c0 + 512]
                for k in range(CONVW):
                    acc = acc + dw_ref[k:k + 1, c0:c0 + 512] * ext[r0 + 2 + k:r0 + 2 + k + 32, c0:c0 + 512]
                c_ref[r0:r0 + 32, c0:c0 + 512] = acc
        c = c_ref[...]
        mu = jnp.mean(c, axis=-1, keepdims=True)
        xc = c - mu
        y = xc * lax.rsqrt(jnp.mean(xc * xc, axis=-1, keepdims=True) + EPS) * lng_ref[...] + lnb_ref[...]
        sb = (y * jax.nn.sigmoid(y)).astype(BF16)
        s_ref[...] = sb
        xo_ref[...] = x_ref[...] + _dot(sb, w_ref[...]) + b_ref[...]

    return pl.pallas_call(
        body, name="conv_mid_fwd", grid=(S // tm,),
        in_specs=[_rows(tm, D), pl.BlockSpec((HALO, D), lambda i: (jnp.maximum(i * hb - 1, 0), 0)), _rows(tm, D),
                  _const((32, D)), _const((1, D)), _const((1, D)), _const((1, D)), _const((D, D)), _const((1, D))],
        out_specs=[_rows(tm, D), _rows(tm, D), _rows(tm, D)],
        out_shape=[_sds((S, D), F32), _sds((S, D), BF16), _sds((S, D), F32)],
        scratch_shapes=[pltpu.VMEM((tm + HALO, D), F32)],
        compiler_params=_cp(40))(glu, glu, x, dw, dwb, lng, lnb, w, b)


def _mlp_fwd(x, g, wu, wd, target=None):
    S = x.shape[0]
    tm = min(512, S)
    last = target is not None

    def body(*refs):
        if last:
            x_ref, g_ref, wu_ref, wd_ref, t_ref, h_ref, act_ref, dy_ref, dyb_ref, sq_ref, acc = refs
        else:
            x_ref, g_ref, wu_ref, wd_ref, h_ref, act_ref, xo_ref, acc = refs
        xv = x_ref[...]
        hb = (xv * _rms(xv) * g_ref[...]).astype(BF16)
        h_ref[...] = hb
        for j in range(NDEV):
            a = jnp.square(jnp.maximum(_dot(hb, wu_ref[j]), 0.0)).astype(BF16)
            act_ref[:, j * FFB:(j + 1) * FFB] = a
            if j == 0:
                acc[...] = _dot(a, wd_ref[j])
            else:
                acc[...] += _dot(a, wd_ref[j])
        y = xv + acc[...]
        if last:
            diff = y - t_ref[...]
            dy = diff * (1.0 / D)
            dy_ref[...] = dy
            dyb_ref[...] = dy.astype(BF16)
            _first(sq_ref, _colsum(diff * diff))
        else:
            xo_ref[...] = y

    in_specs = [_rows(tm, D), _const((1, D)), _const((NDEV, D, FFB)), _const((NDEV, FFB, D))]
    args = [x, g, wu, wd]
    out_specs = [_rows(tm, D), _rows(tm, DFF)]
    out_shape = [_sds((S, D), BF16), _sds((S, DFF), BF16)]
    if last:
        in_specs.append(_rows(tm, D))
        args.append(target)
        out_specs += [_rows(tm, D), _rows(tm, D), _acc((1, D))]
        out_shape += [_sds((S, D), F32), _sds((S, D), BF16), _sds((1, D), F32)]
    else:
        out_specs.append(_rows(tm, D))
        out_shape.append(_sds((S, D), F32))
    return pl.pallas_call(
        body, name="mlp_fwd_loss" if last else "mlp_fwd", grid=(S // tm,),
        in_specs=in_specs, out_specs=out_specs, out_shape=out_shape,
        scratch_shapes=[pltpu.VMEM((tm, D), F32)],
        compiler_params=_cp(52))(*args)


def _mlp_bwd(dy, dyb, x, g, act, wu, wd):
    S = x.shape[0]
    tm = min(256, S)

    def body(dy_ref, dyb_ref, x_ref, g_ref, act_ref, wu_ref, wd_ref, dup_ref, dx_ref, dxb_ref, dg_ref, acc):
        db = dyb_ref[...]
        for j in range(NDEV):
            dact = _dot_nt(db, wd_ref[j])
            a = act_ref[:, j * FFB:(j + 1) * FFB].astype(F32)
            dup = (dact * (2.0 * jnp.sqrt(a))).astype(BF16)
            dup_ref[:, j * FFB:(j + 1) * FFB] = dup
            if j == 0:
                acc[...] = _dot_nt(dup, wu_ref[j])
            else:
                acc[...] += _dot_nt(dup, wu_ref[j])
        xv = x_ref[...]
        dxn, dg = _rms_bwd(acc[...], xv, _rms(xv), g_ref[...])
        dx = dy_ref[...] + dxn
        dx_ref[...] = dx
        dxb_ref[...] = dx.astype(BF16)
        _first(dg_ref, dg)

    return pl.pallas_call(
        body, name="mlp_bwd", grid=(S // tm,),
        in_specs=[_rows(tm, D), _rows(tm, D), _rows(tm, D), _const((1, D)), _rows(tm, DFF),
                  _const((NDEV, D, FFB)), _const((NDEV, FFB, D))],
        out_specs=[_rows(tm, DFF), _rows(tm, D), _rows(tm, D), _acc((1, D))],
        out_shape=[_sds((S, DFF), BF16), _sds((S, D), F32), _sds((S, D), BF16), _sds((1, D), F32)],
        scratch_shapes=[pltpu.VMEM((tm, D), F32)],
        compiler_params=_cp(52))(dy, dyb, x, g, act, wu, wd)


def _wgrad(a, b, tm, tn, name):
    S, M = a.shape
    N = b.shape[1]
    tk = min(1024, S)
    nk = S // tk

    def body(a_ref, b_ref, o_ref, acc):
        k = pl.program_id(2)

        @pl.when(k == 0)
        def _():
            acc[...] = jnp.zeros_like(acc)

        acc[...] += _dot_tn(a_ref[...], b_ref[...])

        @pl.when(k == nk - 1)
        def _():
            o_ref[...] = acc[...].astype(BF16)

    return pl.pallas_call(
        body, name=name, grid=(M // tm, N // tn, nk),
        in_specs=[pl.BlockSpec((tk, tm), lambda i, j, k: (k, i)), pl.BlockSpec((tk, tn), lambda i, j, k: (k, j))],
        out_specs=pl.BlockSpec((tm, tn), lambda i, j, k: (i, j)),
        out_shape=_sds((M, N), BF16),
        scratch_shapes=[pltpu.VMEM((tm, tn), F32)],
        compiler_params=_cp(40, 3))(a, b)


def _bucket_table():
    q = np.arange(WIN)[:, None]
    k = np.arange(2 * WIN)[None, :]
    dist = q + WIN - k
    n = np.maximum(dist, 0)
    max_exact = NBKT // 2
    large = max_exact + (np.log(np.maximum(n, 1).astype(np.float32) / max_exact)
                         / math.log(WIN / max_exact) * (NBKT - max_exact)).astype(np.int32)
    large = np.minimum(large, NBKT - 1)
    bkt = np.where(n < max_exact, n, large).astype(np.int32)
    return np.where((dist >= 0) & (dist < WIN), bkt, -1).astype(np.int32)


def _seg_mats():
    e16 = np.zeros((D, 128), np.float32)
    e16[np.arange(D), np.arange(D) // HD] = 1.0
    e2 = np.zeros((KV, 128), np.float32)
    e2[np.arange(KV), np.arange(KV) // HD] = 1.0
    fold = np.zeros((D, 128), np.float32)
    fold[np.arange(D), np.arange(D) % HD] = 1.0
    fold2 = np.zeros((KV, 128), np.float32)
    fold2[np.arange(KV), np.arange(KV) % HD] = 1.0
    return [jnp.asarray(m, BF16) for m in (e16, e16.T, e2, e2.T, fold, fold2)]


def _head_rms(t, e, et):
    r = lax.rsqrt(_dot_hi(t * t, e) * (1.0 / HD) + EPS)
    return _dot_hi(r, et)


def _attn_qkv_fwd(x, g, w, b, qg, kg, e16, e16t, e2, e2t):
    S = x.shape[0]
    tm = min(512, S)

    def body(x_ref, g_ref, w_ref, b_ref, qg_ref, kg_ref, e16_ref, e16t_ref, e2_ref, e2t_ref,
             h_ref, raw_ref, qn_ref, kn_ref, v_ref):
        xv = x_ref[...]
        hb = (xv * _rms(xv) * g_ref[...]).astype(BF16)
        h_ref[...] = hb
        raw = _dot(hb, w_ref[...]) + b_ref[...]
        raw_ref[...] = raw
        q = raw[:, :D]
        k = raw[:, D:D + KV]
        qn_ref[...] = (q * _head_rms(q, e16_ref[...], e16t_ref[...]) * qg_ref[...] * 0.125).astype(BF16)
        kn_ref[...] = (k * _head_rms(k, e2_ref[...], e2t_ref[...]) * kg_ref[...]).astype(BF16)
        v_ref[...] = raw[:, D + KV:].astype(BF16)

    return pl.pallas_call(
        body, name="attn_qkv_fwd", grid=(S // tm,),
        in_specs=[_rows(tm, D), _const((1, D)), _const((D, QKV)), _const((1, QKV)), _const((1, D)), _const((1, KV)),
                  _const((D, 128)), _const((128, D)), _const((KV, 128)), _const((128, KV))],
        out_specs=[_rows(tm, D), _rows(tm, QKV), _rows(tm, D), _rows(tm, KV), _rows(tm, KV)],
        out_shape=[_sds((S, D), BF16), _sds((S, QKV), F32), _sds((S, D), BF16), _sds((S, KV), BF16), _sds((S, KV), BF16)],
        compiler_params=_cp(48))(x, g, w, b, qg, kg, e16, e16t, e2, e2t)


def _build_bias(bkt_ref, rb_ref, bias_sc):
    bkt = bkt_ref[...]
    for h in range(NH):
        def add(bk, acc, h=h):
            return acc + jnp.where(bkt == bk, rb_ref[bk, h], 0.0)
        bias_sc[h] = lax.fori_loop(0, NBKT, add, jnp.where(bkt < 0, NEG, 0.0).astype(F32))


def _stacks(f, mlo):
    r = pltpu.roll(f, HD, 1)
    z = jnp.zeros_like(f)
    return ((jnp.where(mlo, f, z).astype(BF16), jnp.where(mlo, z, r).astype(BF16)),
            (jnp.where(mlo, r, z).astype(BF16), jnp.where(mlo, z, f).astype(BF16)))


def _unstack(d, mlo):
    z = jnp.zeros_like(d[0][0])
    return (jnp.where(mlo, d[0][0], z) + pltpu.roll(jnp.where(mlo, z, d[0][1]), HD, 1)
            + pltpu.roll(jnp.where(mlo, d[1][0], z), HD, 1) + jnp.where(mlo, z, d[1][1]))


def _softmax_sink(l, sk):
    m = jnp.maximum(jnp.max(l, axis=1, keepdims=True), sk)
    ex = jnp.exp(l - m)
    es = jnp.exp(sk - m)
    inv = 1.0 / (jnp.sum(ex, axis=1, keepdims=True) + es)
    return ex * inv, es * inv


def _attn_fwd(qn, kp, vp, bkt, rel_bias, sinks, wo, bo, x):
    S = x.shape[0]
    tq = min(512, S)
    nblk = tq // WIN

    def body(q_ref, k_ref, v_ref, bkt_ref, rb_ref, sk_ref, wo_ref, bo_ref, x_ref, o_ref, xo_ref, bias_sc):
        i = pl.program_id(0)

        @pl.when(i == 0)
        def _():
            _build_bias(bkt_ref, rb_ref, bias_sc)

        mlo = lax.broadcasted_iota(jnp.int32, (2 * WIN, KV), 1) < HD
        col = lax.broadcasted_iota(jnp.int32, (WIN, 2 * WIN), 1)

        def blk(bb, carry):
            r0 = pl.multiple_of(bb * WIN, WIN)
            g0 = pl.multiple_of(i * tq + bb * WIN, WIN)
            ks = _stacks(k_ref[pl.ds(g0, 2 * WIN), :].astype(F32), mlo)
            vs = _stacks(v_ref[pl.ds(g0, 2 * WIN), :].astype(F32), mlo)
            first = jnp.where((g0 == 0) & (col < WIN), NEG, 0.0)
            for p in range(NH // 2):
                g = p // 4
                qp = q_ref[pl.ds(r0, WIN), p * 128:(p + 1) * 128]
                op = jnp.zeros((WIN, 128), F32)
                for e in range(2):
                    h = 2 * p + e
                    l = _dot_nt(qp, ks[g][e]) + bias_sc[h] + first
                    pr, _ = _softmax_sink(l, sk_ref[0, h])
                    op = op + _dot(pr.astype(BF16), vs[g][e])
                o_ref[pl.ds(r0, WIN), p * 128:(p + 1) * 128] = op.astype(BF16)
            return carry

        lax.fori_loop(0, nblk, blk, 0)
        xo_ref[...] = x_ref[...] + _dot(o_ref[...], wo_ref[...]) + bo_ref[...]

    smem = pl.BlockSpec(memory_space=pltpu.SMEM)
    return pl.pallas_call(
        body, name="attn_fwd", grid=(S // tq,),
        in_specs=[_rows(tq, D), _const((S + WIN, KV)), _const((S + WIN, KV)), _const((WIN, 2 * WIN)), smem, smem,
                  _const((D, D)), _const((1, D)), _rows(tq, D)],
        out_specs=[_rows(tq, D), _rows(tq, D)],
        out_shape=[_sds((S, D), BF16), _sds((S, D), F32)],
        scratch_shapes=[pltpu.VMEM((NH, WIN, 2 * WIN), F32)],
        compiler_params=_cp(40))(qn, kp, vp, bkt, rel_bias, sinks, wo, bo, x)


def _attn_bwd(dx, dxb, qn, kp, vp, bkt, rel_bias, sinks, wo):
    S = dx.shape[0]
    tq = min(512, S)
    nblk = tq // WIN
    nsteps = S // tq

    def body(dx_ref, dxb_ref, q_ref, k_ref, v_ref, bkt_ref, rb_ref, sk_ref, wo_ref,
             dq_ref, dk_ref, dv_ref, dsk_ref, drb_ref, dbo_ref, bias_sc, dbias_sc, do_sc, dsk_sc):
        i = pl.program_id(0)

        @pl.when(i == 0)
        def _():
            _build_bias(bkt_ref, rb_ref, bias_sc)
            dbias_sc[...] = jnp.zeros_like(dbias_sc)
            dsk_sc[...] = jnp.zeros_like(dsk_sc)
            dk_ref[...] = jnp.zeros_like(dk_ref)
            dv_ref[...] = jnp.zeros_like(dv_ref)

        _first(dbo_ref, _colsum(dx_ref[...]))
        do_sc[...] = _dot_nt(dxb_ref[...], wo_ref[...]).astype(BF16)

        mlo = lax.broadcasted_iota(jnp.int32, (2 * WIN, KV), 1) < HD
        col = lax.broadcasted_iota(jnp.int32, (WIN, 2 * WIN), 1)
        lane = lax.broadcasted_iota(jnp.int32, (WIN, 128), 1)

        def blk(bb, carry):
            r0 = pl.multiple_of(bb * WIN, WIN)
            g0 = pl.multiple_of(i * tq + bb * WIN, WIN)
            ks = _stacks(k_ref[pl.ds(g0, 2 * WIN), :].astype(F32), mlo)
            vs = _stacks(v_ref[pl.ds(g0, 2 * WIN), :].astype(F32), mlo)
            first = jnp.where((g0 == 0) & (col < WIN), NEG, 0.0)
            zero = jnp.zeros((2 * WIN, KV), F32)
            dks = [[zero, zero], [zero, zero]]
            dvs = [[zero, zero], [zero, zero]]
            dsk = jnp.zeros((WIN, 128), F32)
            for p in range(NH // 2):
                g = p // 4
                qp = q_ref[pl.ds(r0, WIN), p * 128:(p + 1) * 128]
                dop = do_sc[pl.ds(r0, WIN), p * 128:(p + 1) * 128]
                dqp = jnp.zeros((WIN, 128), F32)
                for e in range(2):
                    h = 2 * p + e
                    l = _dot_nt(qp, ks[g][e]) + bias_sc[h] + first
                    pr, ps = _softmax_sink(l, sk_ref[0, h])
                    dp = _dot_nt(dop, vs[g][e])
                    dr = jnp.sum(pr * dp, axis=1, keepdims=True)
                    dl = pr * (dp - dr)
                    dsk = dsk - jnp.where(lane == h, ps * dr, 0.0)
                    dbias_sc[h] += dl
                    dlb = dl.astype(BF16)
                    dqp = dqp + _dot(dlb, ks[g][e])
                    dks[g][e] = dks[g][e] + _dot_tn(dlb, qp)
                    dvs[g][e] = dvs[g][e] + _dot_tn(pr.astype(BF16), dop)
                dq_ref[pl.ds(r0, WIN), p * 128:(p + 1) * 128] = dqp
            dk_ref[pl.ds(g0, 2 * WIN), :] += _unstack(dks, mlo)
            dv_ref[pl.ds(g0, 2 * WIN), :] += _unstack(dvs, mlo)
            dsk_sc[...] += dsk
            return carry

        lax.fori_loop(0, nblk, blk, 0)

        @pl.when(i == nsteps - 1)
        def _():
            dsk_ref[...] = _colsum(dsk_sc[...])
            bkt = bkt_ref[...]
            rid = lax.broadcasted_iota(jnp.int32, (NBKT, 128), 0)
            lid = lax.broadcasted_iota(jnp.int32, (NBKT, 128), 1)

            def per_bucket(bk, acc):
                mb = bkt == bk
                for h in range(NH):
                    t = jnp.sum(jnp.where(mb, dbias_sc[h], 0.0), axis=0, keepdims=True)
                    t = jnp.sum(t, axis=1, keepdims=True)
                    acc = acc + jnp.where((rid == bk) & (lid == h), t, 0.0)
                return acc

            drb_ref[...] = lax.fori_loop(0, NBKT, per_bucket, jnp.zeros((NBKT, 128), F32))

    smem = pl.BlockSpec(memory_space=pltpu.SMEM)
    return pl.pallas_call(
        body, name="attn_bwd", grid=(nsteps,),
        in_specs=[_rows(tq, D), _rows(tq, D), _rows(tq, D), _const((S + WIN, KV)), _const((S + WIN, KV)),
                  _const((WIN, 2 * WIN)), smem, smem, _const((D, D))],
        out_specs=[_rows(tq, D), _acc((S + WIN, KV)), _acc((S + WIN, KV)), _acc((1, 128)), _acc((NBKT, 128)),
                   _acc((1, D))],
        out_shape=[_sds((S, D), F32), _sds((S + WIN, KV), F32), _sds((S + WIN, KV), F32), _sds((1, 128), F32),
                   _sds((NBKT, 128), F32), _sds((1, D), F32)],
        scratch_shapes=[pltpu.VMEM((NH, WIN, 2 * WIN), F32), pltpu.VMEM((NH, WIN, 2 * WIN), F32),
                        pltpu.VMEM((tq, D), BF16), pltpu.VMEM((WIN, 128), F32)],
        compiler_params=_cp(48))(dx, dxb, qn, kp, vp, bkt, rel_bias, sinks, wo)


def _head_norm_bwd(dn, t, e, et, gt):
    r = _head_rms(t, e, et)
    th = t * r
    dth = dn * gt
    dt = r * (dth - th * _dot_hi(_dot_hi(dth * th, e) * (1.0 / HD), et))
    return dt, _colsum(dn * th)


def _attn_qkv_bwd(dqs, dk, dv, raw, x, dxo, g, w, qg, kg, e16, e16t, e2, e2t, fold, fold2):
    S = x.shape[0]
    tm = min(512, S)

    def body(dq_ref, dk_ref, dv_ref, raw_ref, x_ref, dxo_ref, g_ref, w_ref, qg_ref, kg_ref,
             e16_ref, e16t_ref, e2_ref, e2t_ref, fold_ref, fold2_ref,
             dqkv_ref, dx_ref, dxb_ref, db_ref, dg_ref, dqg_ref, dkg_ref):
        dq, cq = _head_norm_bwd(dq_ref[...] * 0.125, raw_ref[:, :D], e16_ref[...], e16t_ref[...], qg_ref[...])
        dk_, ck = _head_norm_bwd(dk_ref[...], raw_ref[:, D:D + KV], e2_ref[...], e2t_ref[...], kg_ref[...])
        dv_ = dv_ref[...]
        _first(dqg_ref, _dot_hi(cq, fold_ref[...]))
        _first(dkg_ref, _dot_hi(ck, fold2_ref[...]))
        _first_cols(db_ref, [(0, _colsum(dq)), (D, _colsum(dk_)), (D + KV, _colsum(dv_))])
        dqb, dkb, dvb = dq.astype(BF16), dk_.astype(BF16), dv_.astype(BF16)
        dqkv_ref[:, :D] = dqb
        dqkv_ref[:, D:D + KV] = dkb
        dqkv_ref[:, D + KV:] = dvb
        dh = (_dot_nt(dqb, w_ref[:, :D]) + _dot_nt(dkb, w_ref[:, D:D + KV]) + _dot_nt(dvb, w_ref[:, D + KV:]))
        xv = x_ref[...]
        dxn, dg = _rms_bwd(dh, xv, _rms(xv), g_ref[...])
        dx = dxo_ref[...] + dxn
        dx_ref[...] = dx
        dxb_ref[...] = dx.astype(BF16)
        _first(dg_ref, dg)

    return pl.pallas_call(
        body, name="attn_qkv_bwd", grid=(S // tm,),
        in_specs=[_rows(tm, D), _rows(tm, KV), _rows(tm, KV), _rows(tm, QKV), _rows(tm, D), _rows(tm, D),
                  _const((1, D)), _const((D, QKV)), _const((1, D)), _const((1, KV)),
                  _const((D, 128)), _const((128, D)), _const((KV, 128)), _const((128, KV)),
                  _const((D, 128)), _const((KV, 128))],
        out_specs=[_rows(tm, QKV), _rows(tm, D), _rows(tm, D), _acc((1, QKV)), _acc((1, D)), _acc((1, 128)), _acc((1, 128))],
        out_shape=[_sds((S, QKV), BF16), _sds((S, D), F32), _sds((S, D), BF16), _sds((1, QKV), F32), _sds((1, D), F32),
                   _sds((1, 128), F32), _sds((1, 128), F32)],
        compiler_params=_cp(48))(dqs, dk, dv, raw, x, dxo, g, w, qg, kg, e16, e16t, e2, e2t, fold, fold2)


def _conv_mid_bwd(dx, dxb, c, lng, lnb, w):
    S = dx.shape[0]
    tm = min(512, S)

    def body(dx_ref, dxb_ref, c_ref, lng_ref, lnb_ref, w_ref, dc_ref, dbo_ref, dlg_ref, dlb_ref, ddwb_ref):
        _first(dbo_ref, _colsum(dx_ref[...]))
        ds = _dot_nt(dxb_ref[...], w_ref[...])
        c = c_ref[...]
        xc = c - jnp.mean(c, axis=-1, keepdims=True)
        rstd = lax.rsqrt(jnp.mean(xc * xc, axis=-1, keepdims=True) + EPS)
        ch = xc * rstd
        y = ch * lng_ref[...] + lnb_ref[...]
        sg = jax.nn.sigmoid(y)
        dy = ds * (sg * (1.0 + y * (1.0 - sg)))
        _first(dlg_ref, _colsum(dy * ch))
        _first(dlb_ref, _colsum(dy))
        dch = dy * lng_ref[...]
        dc = rstd * (dch - jnp.mean(dch, axis=-1, keepdims=True) - ch * jnp.mean(dch * ch, axis=-1, keepdims=True))
        dc_ref[...] = dc
        _first(ddwb_ref, _colsum(dc))

    return pl.pallas_call(
        body, name="conv_mid_bwd", grid=(S // tm,),
        in_specs=[_rows(tm, D), _rows(tm, D), _rows(tm, D), _const((1, D)), _const((1, D)), _const((D, D))],
        out_specs=[_rows(tm, D), _acc((1, D)), _acc((1, D)), _acc((1, D)), _acc((1, D))],
        out_shape=[_sds((S, D), F32)] + [_sds((1, D), F32)] * 4,
        compiler_params=_cp(40))(dx, dxb, c, lng, lnb, w)


def _dwconv_bwd(dc, glu, dw):
    S = dc.shape[0]
    tm = min(256, S)
    hb = tm // HALO
    nsteps = S // tm

    def body(dc_ref, nxt_ref, gl_ref, halo_ref, dw_ref, dgl_ref, ddw_ref, ext, dext):
        i = pl.program_id(0)
        _fill_ext(ext, halo_ref, gl_ref, tm, i > 0)
        dext[0:tm, :] = dc_ref[...]
        dext[tm:tm + HALO, :] = jnp.where(i < nsteps - 1, nxt_ref[...], 0.0)

        @pl.when(i == 0)
        def _():
            ddw_ref[...] = jnp.zeros_like(ddw_ref)

        for c0 in range(0, D, 512):
            for r0 in range(0, tm, 32):
                acc = jnp.zeros((32, 512), F32)
                for j in range(CONVW):
                    acc = acc + dw_ref[CONVW - 1 - j:CONVW - j, c0:c0 + 512] * dext[r0 + j:r0 + j + 32, c0:c0 + 512]
                dgl_ref[r0:r0 + 32, c0:c0 + 512] = acc
            for k in range(CONVW):
                acc = jnp.zeros((32, 512), F32)
                for r0 in range(0, tm, 32):
                    acc = acc + dext[r0:r0 + 32, c0:c0 + 512] * ext[r0 + 2 + k:r0 + 2 + k + 32, c0:c0 + 512]
                ddw_ref[k:k + 1, c0:c0 + 512] += _colsum(acc)

    return pl.pallas_call(
        body, name="dwconv_bwd", grid=(nsteps,),
        in_specs=[_rows(tm, D), pl.BlockSpec((HALO, D), lambda i: (jnp.minimum((i + 1) * hb, S // HALO - 1), 0)),
                  _rows(tm, D), pl.BlockSpec((HALO, D), lambda i: (jnp.maximum(i * hb - 1, 0), 0)), _const((32, D))],
        out_specs=[_rows(tm, D), _acc((32, D))],
        out_shape=[_sds((S, D), F32), _sds((32, D), F32)],
        scratch_shapes=[pltpu.VMEM((tm + HALO, D), F32), pltpu.VMEM((tm + HALO, D), F32)],
        compiler_params=_cp(40))(dc, dc, glu, glu, dw)


def _conv_in_bwd(dglu, u, x, dxo, g, w):
    S = x.shape[0]
    tm = min(512, S)

    def body(dgl_ref, u_ref, x_ref, dxo_ref, g_ref, w_ref, du_ref, dx_ref, db_ref, dg_ref):
        dgl = dgl_ref[...]
        a = u_ref[:, :D]
        sg = jax.nn.sigmoid(u_ref[:, D:])
        da = dgl * sg
        dgt = dgl * a * sg * (1.0 - sg)
        _first_cols(db_ref, [(0, _colsum(da)), (D, _colsum(dgt))])
        dab, dgb = da.astype(BF16), dgt.astype(BF16)
        du_ref[:, :D] = dab
        du_ref[:, D:] = dgb
        dh = _dot_nt(dab, w_ref[:, :D]) + _dot_nt(dgb, w_ref[:, D:])
        xv = x_ref[...]
        dxn, dg = _rms_bwd(dh, xv, _rms(xv), g_ref[...])
        dx_ref[...] = dxo_ref[...] + dxn
        _first(dg_ref, dg)

    return pl.pallas_call(
        body, name="conv_in_bwd", grid=(S // tm,),
        in_specs=[_rows(tm, D), _rows(tm, 2 * D), _rows(tm, D), _rows(tm, D), _const((1, D)), _const((D, 2 * D))],
        out_specs=[_rows(tm, 2 * D), _rows(tm, D), _acc((1, 2 * D)), _acc((1, D))],
        out_shape=[_sds((S, 2 * D), BF16), _sds((S, D), F32), _sds((1, 2 * D), F32), _sds((1, D), F32)],
        compiler_params=_cp(48))(dglu, u, x, dxo, g, w)


def _coords():
    return lax.axis_index("x"), lax.axis_index("y"), lax.axis_index("c")


def _all_gather(wpack, spack):
    R, RS_ = wpack.shape[0], spack.shape[0]

    def body(w_ref, s_ref, gw_ref, gs_ref, stage, send_sems, recv_sems, local_sem):
        x, y, c = _coords()
        me, sib = (x, y, c), (x, y, 1 - c)
        chips = [(1 - x, y), (x, 1 - y), (1 - x, 1 - y)]

        def cast(t, carry):
            r = pl.multiple_of(t * PACK_CH, PACK_CH)
            stage[pl.ds(r, PACK_CH), :] = w_ref[pl.ds(r, PACK_CH), :].astype(BF16)
            return carry

        lax.fori_loop(0, R // PACK_CH, cast, 0)

        def slot(px, py, pc):
            return 4 * px + 2 * py + pc

        def copies(k, block, to, own=False):
            idx = slot(*block)
            return [
                pltpu.make_async_remote_copy(
                    src_ref=stage if own else gw_ref.at[idx], dst_ref=gw_ref.at[idx],
                    send_sem=send_sems.at[0, k], recv_sem=recv_sems.at[0, k], device_id=to, device_id_type=MESH),
                pltpu.make_async_remote_copy(
                    src_ref=s_ref if own else gs_ref.at[idx], dst_ref=gs_ref.at[idx],
                    send_sem=send_sems.at[1, k], recv_sem=recv_sems.at[1, k], device_id=to, device_id_type=MESH)]

        mine = pltpu.make_async_copy(stage, gw_ref.at[slot(*me)], local_sem)
        mine.start()
        gs_ref[slot(*me)] = s_ref[...]
        first = copies(0, me, sib, own=True)
        for j, chip in enumerate(chips):
            first += copies(1 + j, me, (*chip, c), own=True)
        for cp in first:
            cp.start()
        passed = []
        for j, chip in enumerate(chips):
            for cp in copies(1 + j, (*chip, c), me):
                cp.wait_recv()
            fwd = copies(4 + j, (*chip, c), sib)
            for cp in fwd:
                cp.start()
            passed += fwd
        for cp in copies(0, sib, me):
            cp.wait_recv()
        for j, chip in enumerate(chips):
            for cp in copies(4 + j, (*chip, 1 - c), me):
                cp.wait_recv()
        for cp in first + passed:
            cp.wait_send()
        mine.wait()

    vmem = pl.BlockSpec(memory_space=pltpu.VMEM)
    return pl.pallas_call(
        body, name="all_gather_weights",
        in_specs=[vmem, vmem],
        out_specs=[pl.BlockSpec(memory_space=pl.ANY), vmem],
        out_shape=[_sds((NDEV, R, D), BF16), _sds((NDEV, RS_, 128), F32)],
        scratch_shapes=[pltpu.VMEM((R, D), BF16), pltpu.SemaphoreType.DMA((2, 7)), pltpu.SemaphoreType.DMA((2, 7)),
                        pltpu.SemaphoreType.DMA],
        compiler_params=pltpu.CompilerParams(vmem_limit_bytes=40 << 20))(wpack, spack)


def _reduce_scatter(gpack, spack):
    R, RS_ = gpack.shape[1], spack.shape[0]
    nch = R // PACK_CH

    def body(g_ref, s_ref, red_ref, sred_ref, land_ref, sland, buf, send_sems, recv_sems, local_sem, buf_sems):
        x, y, c = _coords()
        me = 4 * x + 2 * y + c
        sends = []
        for k in range(1, NDEV):
            kx, ky, kc = (k >> 2) & 1, (k >> 1) & 1, k & 1
            peer = (x ^ kx, y ^ ky, c ^ kc)
            pidx = me ^ k
            cw = pltpu.make_async_remote_copy(
                src_ref=g_ref.at[pidx], dst_ref=land_ref.at[me], send_sem=send_sems.at[0, k - 1],
                recv_sem=recv_sems.at[0, k - 1], device_id=peer, device_id_type=MESH)
            cs = pltpu.make_async_remote_copy(
                src_ref=s_ref, dst_ref=sland.at[me], send_sem=send_sems.at[1, k - 1],
                recv_sem=recv_sems.at[1, k - 1], device_id=peer, device_id_type=MESH)
            cw.start()
            cs.start()
            sends += [cw, cs]
        mine = pltpu.make_async_copy(g_ref.at[me], land_ref.at[me], local_sem)
        mine.start()
        sland[me] = s_ref[...]
        for k in range(1, NDEV):
            kx, ky, kc = (k >> 2) & 1, (k >> 1) & 1, k & 1
            peer = (x ^ kx, y ^ ky, c ^ kc)
            pidx = me ^ k
            pltpu.make_async_remote_copy(
                src_ref=g_ref.at[pidx], dst_ref=land_ref.at[pidx], send_sem=send_sems.at[0, k - 1],
                recv_sem=recv_sems.at[0, k - 1], device_id=peer, device_id_type=MESH).wait_recv()
            pltpu.make_async_remote_copy(
                src_ref=s_ref, dst_ref=sland.at[pidx], send_sem=send_sems.at[1, k - 1],
                recv_sem=recv_sems.at[1, k - 1], device_id=peer, device_id_type=MESH).wait_recv()
        mine.wait()
        for cp in sends:
            cp.wait_send()

        acc = sland[0]
        for s in range(1, NDEV):
            acc = acc + sland[s]
        sred_ref[...] = acc

        def fetch(t, slot):
            r = pl.multiple_of(t * PACK_CH, PACK_CH)
            return pltpu.make_async_copy(land_ref.at[:, pl.ds(r, PACK_CH), :], buf.at[slot], buf_sems.at[slot])

        fetch(0, 0).start()

        def step(t, carry):
            slot = t % 2

            @pl.when(t + 1 < nch)
            def _():
                fetch(t + 1, 1 - slot).start()

            fetch(t, slot).wait()
            a = buf[slot, 0].astype(F32)
            for s in range(1, NDEV):
                a = a + buf[slot, s].astype(F32)
            red_ref[pl.ds(pl.multiple_of(t * PACK_CH, PACK_CH), PACK_CH), :] = a
            return carry

        lax.fori_loop(0, nch, step, 0)

    vmem = pl.BlockSpec(memory_space=pltpu.VMEM)
    hbm = pl.BlockSpec(memory_space=pl.ANY)
    red, sred, _ = pl.pallas_call(
        body, name="reduce_scatter_grads",
        in_specs=[hbm, vmem],
        out_specs=[vmem, vmem, hbm],
        out_shape=[_sds((R, D), F32), _sds((RS_, 128), F32), _sds((NDEV, R, D), BF16)],
        scratch_shapes=[pltpu.VMEM((NDEV, RS_, 128), F32), pltpu.VMEM((2, NDEV, PACK_CH, D), BF16),
                        pltpu.SemaphoreType.DMA((2, 7)), pltpu.SemaphoreType.DMA((2, 7)), pltpu.SemaphoreType.DMA,
                        pltpu.SemaphoreType.DMA((2,))],
        compiler_params=pltpu.CompilerParams(vmem_limit_bytes=48 << 20))(gpack, spack)
    return red, sred


def _adamw(w, g, m, v, name):
    R, L = w.shape
    tr = R
    for cand in (680, 512, 392, 256, 136, 64, 8):
        if R % cand == 0:
            tr = cand
            break

    def body(w_ref, g_ref, m_ref, v_ref, d_ref, nm_ref, nv_ref):
        gv = g_ref[...]
        nm = B1 * m_ref[...] + (1.0 - B1) * gv
        nv = B2 * v_ref[...] + (1.0 - B2) * jnp.square(gv)
        m_hat = nm / (1.0 - B1 ** STEP)
        v_hat = nv / (1.0 - B2 ** STEP)
        d_ref[...] = -LR * (m_hat / (jnp.sqrt(v_hat) + AEPS) + WD * w_ref[...])
        nm_ref[...] = nm
        nv_ref[...] = nv

    spec = pl.BlockSpec((tr, L), lambda i: (i, 0))
    return pl.pallas_call(
        body, name=name, grid=(R // tr,), in_specs=[spec] * 4, out_specs=[spec] * 3,
        out_shape=[_sds((R, L), F32)] * 3, compiler_params=_cp(48))(w, g, m, v)


BIG = ("conv_w_in", "conv_w_out", "w_qkv", "w_o", "w_up", "w_down")
SMALL = (("conv_norm_g", (1, D), False), ("conv_b_in", (1, 2 * D), False), ("conv_dw", (1, CONVW, D), True),
         ("conv_dw_b", (1, D), False), ("conv_ln_g", (1, D), False), ("conv_ln_b", (1, D), False),
         ("conv_b_out", (1, D), False), ("attn_norm_g", (1, D), True), ("b_qkv", (1, QKV), True),
         ("q_norm_g", (1, HD), False), ("k_norm_g", (1, HD), False), ("sinks", (1, NH), False),
         ("b_o", (1, D), True), ("rel_bias", (NBKT, NH), False), ("mlp_norm_g", (2, D), False))


def _lanes(a):
    flat = a.reshape(-1)
    n = flat.shape[0]
    rows = -(-n // 128)
    return jnp.pad(flat, (0, rows * 128 - n)).reshape(rows, 128)


def _pack_small(arrs):
    rows = jnp.concatenate([_lanes(a) for a in arrs], axis=0)
    pad = -rows.shape[0] % 8
    return jnp.pad(rows, ((0, pad), (0, 0)))


def _unpack_small(pack, shapes):
    out, r = [], 0
    for shp in shapes:
        n = int(np.prod(shp))
        rows = -(-n // 128)
        out.append(pack[r:r + rows].reshape(-1)[:n].reshape(shp))
        r += rows
    return out


def _pack_big(p):
    return jnp.concatenate([p["conv_w_in"].reshape(256, D), p["conv_w_out"].reshape(128, D), p["w_qkv"].reshape(160, D),
                            p["w_o"].reshape(128, D), p["w_up"].reshape(1024, D), p["w_down"].reshape(1024, D)], axis=0)


def _unpack_big(pack):
    return {"conv_w_in": pack[0:256].reshape(1, D, 256), "conv_w_out": pack[256:384].reshape(1, 128, D),
            "w_qkv": pack[384:544].reshape(1, D, 160), "w_o": pack[544:672].reshape(1, 128, D),
            "w_up": pack[672:1696].reshape(2, D, FFB), "w_down": pack[1696:2720].reshape(2, FFB, D)}


def _cols_to_blocks(a, n):
    M = a.shape[0]
    return a.reshape(M, NDEV, n).transpose(1, 0, 2).reshape(NDEV, M * n // D, D)


def kernel(x, conv_norm_g, conv_w_in, conv_b_in, conv_dw, conv_dw_b, conv_ln_g, conv_ln_b, conv_w_out, conv_b_out, attn_norm_g, w_qkv, b_qkv, q_norm_g, k_norm_g, sinks, w_o, b_o, rel_bias, mlp_norm_g, w_up, w_down, loss_target, m_conv_norm_g, m_conv_w_in, m_conv_b_in, m_conv_dw, m_conv_dw_b, m_conv_ln_g, m_conv_ln_b, m_conv_w_out, m_conv_b_out, m_attn_norm_g, m_w_qkv, m_b_qkv, m_q_norm_g, m_k_norm_g, m_sinks, m_w_o, m_b_o, m_rel_bias, m_mlp_norm_g, m_w_up, m_w_down, v_conv_norm_g, v_conv_w_in, v_conv_b_in, v_conv_dw, v_conv_dw_b, v_conv_ln_g, v_conv_ln_b, v_conv_w_out, v_conv_b_out, v_attn_norm_g, v_w_qkv, v_b_qkv, v_q_norm_g, v_k_norm_g, v_sinks, v_w_o, v_b_o, v_rel_bias, v_mlp_norm_g, v_w_up, v_w_down):
    names = [n for n, _, _ in SMALL] + list(BIG)
    loc = dict(locals())
    W = {n: loc[n] for n in names}
    M_ = {n: loc["m_" + n] for n in names}
    V_ = {n: loc["v_" + n] for n in names}
    me = 4 * lax.axis_index("x") + 2 * lax.axis_index("y") + lax.axis_index("c")
    xs = x[0]
    S = xs.shape[0]

    sharded = [n for n, _, sh in SMALL if sh]
    gw, gs = _all_gather(_pack_big(W), _pack_small([W[n] for n in sharded]))
    w_in = gw[:, 0:256].reshape(NDEV, D, 256).transpose(1, 0, 2).reshape(D, 2 * D)
    w_out = gw[:, 256:384].reshape(D, D)
    wqkv = gw[:, 384:544].reshape(NDEV, D, 160).transpose(1, 0, 2).reshape(D, QKV)
    wo = gw[:, 544:672].reshape(D, D)
    wu = gw[:, 672:1696].reshape(NDEV, 2, D, FFB)
    wd = gw[:, 1696:2720].reshape(NDEV, 2, FFB, D)
    shard_shapes = {"conv_dw": (CONVW, 128), "attn_norm_g": (1, 128), "b_qkv": (1, 160), "b_o": (1, 128)}
    parts = [_unpack_small(gs[j], [shard_shapes[n] for n in sharded]) for j in range(NDEV)]
    full = {n: jnp.concatenate([parts[j][i] for j in range(NDEV)], axis=-1) for i, n in enumerate(sharded)}
    dw32 = jnp.pad(full["conv_dw"], ((0, 1), (0, 0)))
    attn_g, bqkv, bo = full["attn_norm_g"], full["b_qkv"], full["b_o"]
    qg = jnp.tile(q_norm_g, (1, NH))
    kg = jnp.tile(k_norm_g, (1, NKV))
    e16, e16t, e2, e2t, fold, fold2 = _seg_mats()
    bkt = jnp.asarray(_bucket_table())

    h0, u, glu = _conv_in_fwd(xs, conv_norm_g, w_in, conv_b_in)
    cc, sb, x1 = _conv_mid_fwd(glu, xs, dw32, conv_dw_b, conv_ln_g, conv_ln_b, w_out, conv_b_out)
    h1, act0, x2 = _mlp_fwd(x1, mlp_norm_g[0:1], wu[:, 0], wd[:, 0])
    h2, raw, qn, kn, vv = _attn_qkv_fwd(x2, attn_g, wqkv, bqkv, qg, kg, e16, e16t, e2, e2t)
    kp = jnp.pad(kn, ((WIN, 0), (0, 0)))
    vp = jnp.pad(vv, ((WIN, 0), (0, 0)))
    ob, x3 = _attn_fwd(qn, kp, vp, bkt, rel_bias, sinks, wo, bo, x2)
    h3, act1, dx4, dx4b, sq = _mlp_fwd(x3, mlp_norm_g[1:2], wu[:, 1], wd[:, 1], target=loss_target[0])
    loss = lax.psum(jnp.sum(sq) * (0.5 / D), ("x", "y", "c"))

    dup1, dx3, dx3b, dg_mlp1 = _mlp_bwd(dx4, dx4b, x3, mlp_norm_g[1:2], act1, wu[:, 1], wd[:, 1])
    g_wdn1 = _wgrad(act1, dx4b, 512, D, "wgrad_down1")
    g_wup1 = _wgrad(h3, dup1, D, 512, "wgrad_up1")
    dqs, dkp, dvp, d_sinks, d_rel, d_bo = _attn_bwd(dx3, dx3b, qn, kp, vp, bkt, rel_bias, sinks, wo)
    g_wo = _wgrad(ob, dx3b, D, 512, "wgrad_o")
    dqkv, dx2, dx2b, d_bqkv, d_attn_g, d_qg, d_kg = _attn_qkv_bwd(
        dqs, dkp[WIN:], dvp[WIN:], raw, x2, dx3, attn_g, wqkv, qg, kg, e16, e16t, e2, e2t, fold, fold2)
    g_wqkv = _wgrad(h2, dqkv, D, 640, "wgrad_qkv")
    dup0, dx1, dx1b, dg_mlp0 = _mlp_bwd(dx2, dx2b, x1, mlp_norm_g[0:1], act0, wu[:, 0], wd[:, 0])
    g_wdn0 = _wgrad(act0, dx2b, 512, D, "wgrad_down0")
    g_wup0 = _wgrad(h1, dup0, D, 512, "wgrad_up0")
    dc, d_bout, d_lng, d_lnb, d_dwb = _conv_mid_bwd(dx1, dx1b, cc, conv_ln_g, conv_ln_b, w_out)
    g_wout = _wgrad(sb, dx1b, D, 512, "wgrad_conv_out")
    dglu, d_dw = _dwconv_bwd(dc, glu, dw32)
    du, grad_x, d_bin, d_cng = _conv_in_bwd(dglu, u, xs, dx1, conv_norm_g, w_in)
    g_win = _wgrad(h0, du, D, 512, "wgrad_conv_in")

    gpack = jnp.concatenate([
        _cols_to_blocks(g_win, 256), g_wout.reshape(NDEV, 128, D), _cols_to_blocks(g_wqkv, 160),
        g_wo.reshape(NDEV, 128, D),
        jnp.stack([_cols_to_blocks(g_wup0, FFB), _cols_to_blocks(g_wup1, FFB)], axis=1).reshape(NDEV, 1024, D),
        jnp.stack([g_wdn0.reshape(NDEV, FFB, D), g_wdn1.reshape(NDEV, FFB, D)], axis=1).reshape(NDEV, 1024, D)], axis=1)
    small_grads = {
        "conv_norm_g": d_cng, "conv_b_in": d_bin, "conv_dw": d_dw[:CONVW][None], "conv_dw_b": d_dwb,
        "conv_ln_g": d_lng, "conv_ln_b": d_lnb, "conv_b_out": d_bout, "attn_norm_g": d_attn_g, "b_qkv": d_bqkv,
        "q_norm_g": d_qg[:, :HD], "k_norm_g": d_kg[:, :HD], "sinks": d_sinks[:, :NH], "b_o": d_bo,
        "rel_bias": d_rel[:, :NH], "mlp_norm_g": jnp.concatenate([dg_mlp0, dg_mlp1], axis=0)}
    red, sred = _reduce_scatter(gpack, _pack_small([small_grads[n] for n, _, _ in SMALL]))
    sg_full = _unpack_small(sred, [shp for _, shp, _ in SMALL])

    G = _unpack_big(red)
    for (n, shp, sh), gfull in zip(SMALL, sg_full):
        if sh:
            width = shp[-1] // NDEV
            G[n] = lax.dynamic_slice_in_dim(gfull, me * width, width, axis=gfull.ndim - 1)
        else:
            G[n] = gfull
    d_big, m_big, v_big = _adamw(_pack_big(W), red, _pack_big(M_), _pack_big(V_), "adamw_big")
    snames = [n for n, _, _ in SMALL]
    d_sm, m_sm, v_sm = _adamw(*[_pack_small([t[n] for n in snames]) for t in (W, G, M_, V_)], "adamw_small")
    sshapes = [W[n].shape for n in snames]
    outs = []
    for big, small in ((d_big, d_sm), (m_big, m_sm), (v_big, v_sm)):
        o = _unpack_big(big)
        o.update(dict(zip(snames, _unpack_small(small, sshapes))))
        outs.append(o)
    order = ["conv_norm_g", "conv_w_in", "conv_b_in", "conv_dw", "conv_dw_b", "conv_ln_g", "conv_ln_b", "conv_w_out",
             "conv_b_out", "attn_norm_g", "w_qkv", "b_qkv", "q_norm_g", "k_norm_g", "sinks", "w_o", "b_o", "rel_bias",
             "mlp_norm_g", "w_up", "w_down"]
    return (loss, grad_x[None], *[G[n] for n in order], *[outs[0][n] for n in order],
            *[outs[1][n] for n in order], *[outs[2][n] for n in order])
```

```python
import math

import numpy as np
import jax
import jax.numpy as jnp
from jax import lax
from jax.experimental import pallas as pl
from jax.experimental.pallas import tpu as pltpu

F32 = jnp.float32
BF16 = jnp.bfloat16

D = 1024
DFF = 4096
NH, NKV, HD = 16, 2, 64
NPAIR = NH // 2
KV = NKV * HD
QKV = D + 2 * KV
CONVW = 31
WIN = 128
NBKT = 32
EPS = 1e-6
NEG = -1e30
NDEV = 8
FFB = DFF // NDEV
HALO = 32

LR, B1, B2, AEPS, WD, STEP = 0.001, 0.9, 0.999, 1e-08, 0.01, 10

PACK = (("conv_w_in", 256), ("conv_w_out", 128), ("w_qkv", 160), ("w_o", 128), ("w_up", 1024), ("w_down", 1024))
PACK_ROWS = sum(r for _, r in PACK)
PACK_CH = 272
MESH = pl.DeviceIdType.MESH


def _cp(vmem_mb, n_axes=1):
    return pltpu.CompilerParams(dimension_semantics=("arbitrary",) * n_axes, vmem_limit_bytes=vmem_mb << 20)


def _const(shape):
    nd = len(shape)
    return pl.BlockSpec(shape, lambda *_: (0,) * nd, pipeline_mode=pl.Buffered(1))


def _acc(shape):
    nd = len(shape)
    return pl.BlockSpec(shape, lambda *_: (0,) * nd)


def _rows(tm, n):
    return pl.BlockSpec((tm, n), lambda i: (i, 0))


def _sds(shape, dtype):
    return jax.ShapeDtypeStruct(shape, dtype)


def _dot(a, b):
    return jnp.dot(a, b, preferred_element_type=F32)


def _dot_nt(a, b):
    return lax.dot_general(a, b, (((1,), (1,)), ((), ())), preferred_element_type=F32)


def _dot_tn(a, b):
    return lax.dot_general(a, b, (((0,), (0,)), ((), ())), preferred_element_type=F32)


def _dot_hi(x, e):
    x1 = x.astype(BF16)
    r1 = x - x1.astype(F32)
    x2 = r1.astype(BF16)
    x3 = (r1 - x2.astype(F32)).astype(BF16)
    return _dot(x1, e) + _dot(x2, e) + _dot(x3, e)


def _rms(x):
    return lax.rsqrt(jnp.mean(x * x, axis=-1, keepdims=True) + EPS)


def _rms_bwd(dh, x, r, g):
    xh = x * r
    dxh = dh * g
    dx = r * (dxh - xh * jnp.mean(dxh * xh, axis=-1, keepdims=True))
    return dx, jnp.sum(dh * xh, axis=0, keepdims=True)


def _colsum(a):
    return jnp.sum(a, axis=0, keepdims=True)


def _first(ref, val):
    @pl.when(pl.program_id(0) == 0)
    def _():
        ref[...] = jnp.zeros_like(ref)
    ref[...] += val


def _first_cols(ref, vals):
    @pl.when(pl.program_id(0) == 0)
    def _():
        ref[...] = jnp.zeros_like(ref)
    for c0, val in vals:
        ref[:, c0:c0 + val.shape[1]] += val


def _conv_in_fwd(x, g, w, b):
    S = x.shape[0]
    tm = min(512, S)

    def body(x_ref, g_ref, w_ref, b_ref, h_ref, u_ref, glu_ref):
        xv = x_ref[...]
        hb = (xv * _rms(xv) * g_ref[...]).astype(BF16)
        h_ref[...] = hb
        u = _dot(hb, w_ref[...]) + b_ref[...]
        u_ref[...] = u
        glu_ref[...] = u[:, :D] * jax.nn.sigmoid(u[:, D:])

    return pl.pallas_call(
        body, name="conv_in_fwd", grid=(S // tm,),
        in_specs=[_rows(tm, D), _const((1, D)), _const((D, 2 * D)), _const((1, 2 * D))],
        out_specs=[_rows(tm, D), _rows(tm, 2 * D), _rows(tm, D)],
        out_shape=[_sds((S, D), BF16), _sds((S, 2 * D), F32), _sds((S, D), F32)],
        compiler_params=_cp(48))(x, g, w, b)


def _fill_ext(ext, halo_ref, cur_ref, tm, keep):
    ext[0:HALO, :] = jnp.where(keep, halo_ref[...], 0.0)
    ext[HALO:HALO + tm, :] = cur_ref[...]


def _shift_copies(dst, src, n):
    for s_ in range(1, 8):
        dst[s_ - 1, 0:n, :] = src[s_:s_ + n, :]


def _tap(src, sh, o, r0, c0):
    a, s_ = divmod(o, 8)
    ref = src if s_ == 0 else sh.at[s_ - 1]
    return ref[r0 + 8 * a:r0 + 8 * a + 32, c0:c0 + 512]


def _conv_mid_fwd(glu, x, dw, dwb, lng, lnb, w, b):
    S = x.shape[0]
    tm = min(256, S)
    hb = tm // HALO

    def body(gl_ref, halo_ref, x_ref, dw_ref, dwb_ref, lng_ref, lnb_ref, w_ref, b_ref, c_ref, s_ref, xo_ref, ext, esh):
        i = pl.program_id(0)
        _fill_ext(ext, halo_ref, gl_ref, tm, i > 0)
        _shift_copies(esh, ext, tm + 24)
        for r0 in range(0, tm, 32):
            for c0 in range(0, D, 512):
                acc = jnp.zeros((32, 512), F32) + dwb_ref[:, c0:c0 + 512]
                for k in range(CONVW):
                    acc = acc + dw_ref[k:k + 1, c0:c0 + 512] * _tap(ext, esh, 2 + k, r0, c0)
                c_ref[r0:r0 + 32, c0:c0 + 512] = acc
        c = c_ref[...]
        mu = jnp.mean(c, axis=-1, keepdims=True)
        xc = c - mu
        y = xc * lax.rsqrt(jnp.mean(xc * xc, axis=-1, keepdims=True) + EPS) * lng_ref[...] + lnb_ref[...]
        sb = (y * jax.nn.sigmoid(y)).astype(BF16)
        s_ref[...] = sb
        xo_ref[...] = x_ref[...] + _dot(sb, w_ref[...]) + b_ref[...]

    return pl.pallas_call(
        body, name="conv_mid_fwd", grid=(S // tm,),
        in_specs=[_rows(tm, D), pl.BlockSpec((HALO, D), lambda i: (jnp.maximum(i * hb - 1, 0), 0)), _rows(tm, D),
                  _const((32, D)), _const((1, D)), _const((1, D)), _const((1, D)), _const((D, D)), _const((1, D))],
        out_specs=[_rows(tm, D), _rows(tm, D), _rows(tm, D)],
        out_shape=[_sds((S, D), F32), _sds((S, D), BF16), _sds((S, D), F32)],
        scratch_shapes=[pltpu.VMEM((tm + HALO, D), F32), pltpu.VMEM((7, tm + 24, D), F32)],
        compiler_params=_cp(48))(glu, glu, x, dw, dwb, lng, lnb, w, b)


def _mlp_fwd(x, g, wu, wd, target=None):
    S = x.shape[0]
    tm = min(512, S)
    last = target is not None

    def body(*refs):
        if last:
            x_ref, g_ref, wu_ref, wd_ref, t_ref, h_ref, act_ref, dy_ref, dyb_ref, sq_ref, acc = refs
        else:
            x_ref, g_ref, wu_ref, wd_ref, h_ref, act_ref, xo_ref, acc = refs
        xv = x_ref[...]
        hb = (xv * _rms(xv) * g_ref[...]).astype(BF16)
        h_ref[...] = hb
        for j in range(NDEV):
            a = jnp.square(jnp.maximum(_dot(hb, wu_ref[j]), 0.0)).astype(BF16)
            act_ref[:, j * FFB:(j + 1) * FFB] = a
            if j == 0:
                acc[...] = _dot(a, wd_ref[j])
            else:
                acc[...] += _dot(a, wd_ref[j])
        y = xv + acc[...]
        if last:
            diff = y - t_ref[...]
            dy = diff * (1.0 / D)
            dy_ref[...] = dy
            dyb_ref[...] = dy.astype(BF16)
            _first(sq_ref, _colsum(diff * diff))
        else:
            xo_ref[...] = y

    in_specs = [_rows(tm, D), _const((1, D)), _const((NDEV, D, FFB)), _const((NDEV, FFB, D))]
    args = [x, g, wu, wd]
    out_specs = [_rows(tm, D), _rows(tm, DFF)]
    out_shape = [_sds((S, D), BF16), _sds((S, DFF), BF16)]
    if last:
        in_specs.append(_rows(tm, D))
        args.append(target)
        out_specs += [_rows(tm, D), _rows(tm, D), _acc((1, D))]
        out_shape += [_sds((S, D), F32), _sds((S, D), BF16), _sds((1, D), F32)]
    else:
        out_specs.append(_rows(tm, D))
        out_shape.append(_sds((S, D), F32))
    return pl.pallas_call(
        body, name="mlp_fwd_loss" if last else "mlp_fwd", grid=(S // tm,),
        in_specs=in_specs, out_specs=out_specs, out_shape=out_shape,
        scratch_shapes=[pltpu.VMEM((tm, D), F32)],
        compiler_params=_cp(52))(*args)


def _mlp_bwd(dy, dyb, x, g, act, wu, wd):
    S = x.shape[0]
    tm = min(256, S)

    def body(dy_ref, dyb_ref, x_ref, g_ref, act_ref, wu_ref, wd_ref, dup_ref, dx_ref, dxb_ref, dg_ref, acc):
        db = dyb_ref[...]
        for j in range(NDEV):
            dact = _dot_nt(db, wd_ref[j])
            a = act_ref[:, j * FFB:(j + 1) * FFB].astype(F32)
            dup = (dact * (2.0 * jnp.sqrt(a))).astype(BF16)
            dup_ref[:, j * FFB:(j + 1) * FFB] = dup
            if j == 0:
                acc[...] = _dot_nt(dup, wu_ref[j])
            else:
                acc[...] += _dot_nt(dup, wu_ref[j])
        xv = x_ref[...]
        dxn, dg = _rms_bwd(acc[...], xv, _rms(xv), g_ref[...])
        dx = dy_ref[...] + dxn
        dx_ref[...] = dx
        dxb_ref[...] = dx.astype(BF16)
        _first(dg_ref, dg)

    return pl.pallas_call(
        body, name="mlp_bwd", grid=(S // tm,),
        in_specs=[_rows(tm, D), _rows(tm, D), _rows(tm, D), _const((1, D)), _rows(tm, DFF),
                  _const((NDEV, D, FFB)), _const((NDEV, FFB, D))],
        out_specs=[_rows(tm, DFF), _rows(tm, D), _rows(tm, D), _acc((1, D))],
        out_shape=[_sds((S, DFF), BF16), _sds((S, D), F32), _sds((S, D), BF16), _sds((1, D), F32)],
        scratch_shapes=[pltpu.VMEM((tm, D), F32)],
        compiler_params=_cp(52))(dy, dyb, x, g, act, wu, wd)


def _wgrad(a, b, tm, tn, name):
    S, M = a.shape
    N = b.shape[1]
    tk = min(1024, S)
    nk = S // tk

    def body(a_ref, b_ref, o_ref, acc):
        k = pl.program_id(2)

        @pl.when(k == 0)
        def _():
            acc[...] = jnp.zeros_like(acc)

        acc[...] += _dot_tn(a_ref[...], b_ref[...])

        @pl.when(k == nk - 1)
        def _():
            o_ref[...] = acc[...].astype(BF16)

    return pl.pallas_call(
        body, name=name, grid=(M // tm, N // tn, nk),
        in_specs=[pl.BlockSpec((tk, tm), lambda i, j, k: (k, i)), pl.BlockSpec((tk, tn), lambda i, j, k: (k, j))],
        out_specs=pl.BlockSpec((tm, tn), lambda i, j, k: (i, j)),
        out_shape=_sds((M, N), BF16),
        scratch_shapes=[pltpu.VMEM((tm, tn), F32)],
        compiler_params=_cp(40, 3))(a, b)


def _bucket_table():
    q = np.arange(WIN)[:, None]
    k = np.arange(2 * WIN)[None, :]
    dist = q + WIN - k
    n = np.maximum(dist, 0)
    max_exact = NBKT // 2
    large = max_exact + (np.log(np.maximum(n, 1).astype(np.float32) / max_exact)
                         / math.log(WIN / max_exact) * (NBKT - max_exact)).astype(np.int32)
    large = np.minimum(large, NBKT - 1)
    bkt = np.where(n < max_exact, n, large).astype(np.int32)
    return np.where((dist >= 0) & (dist < WIN), bkt, -1).astype(np.int32)


def _seg_mats():
    e16 = np.zeros((D, 128), np.float32)
    e16[np.arange(D), np.arange(D) // HD] = 1.0
    e2 = np.zeros((KV, 128), np.float32)
    e2[np.arange(KV), np.arange(KV) // HD] = 1.0
    fold = np.zeros((D, 128), np.float32)
    fold[np.arange(D), np.arange(D) % HD] = 1.0
    fold2 = np.zeros((KV, 128), np.float32)
    fold2[np.arange(KV), np.arange(KV) % HD] = 1.0
    return [jnp.asarray(m, BF16) for m in (e16, e16.T, e2, e2.T, fold, fold2)]


def _head_rms(t, e, et):
    r = lax.rsqrt(_dot_hi(t * t, e) * (1.0 / HD) + EPS)
    return _dot_hi(r, et)


def _attn_qkv_fwd(x, g, w, b, qg, kg, e16, e16t, e2, e2t):
    S = x.shape[0]
    tm = min(512, S)

    def body(x_ref, g_ref, w_ref, b_ref, qg_ref, kg_ref, e16_ref, e16t_ref, e2_ref, e2t_ref,
             h_ref, raw_ref, qn_ref, kn_ref, v_ref):
        xv = x_ref[...]
        hb = (xv * _rms(xv) * g_ref[...]).astype(BF16)
        h_ref[...] = hb
        raw = _dot(hb, w_ref[...]) + b_ref[...]
        raw_ref[...] = raw
        q = raw[:, :D]
        k = raw[:, D:D + KV]
        qn = (q * _head_rms(q, e16_ref[...], e16t_ref[...]) * qg_ref[...] * 0.125).astype(BF16)
        for bb in range(tm // WIN):
            for p in range(NPAIR):
                r = (bb * NPAIR + p) * WIN
                qn_ref[r:r + WIN, :] = qn[bb * WIN:(bb + 1) * WIN, p * 128:(p + 1) * 128]
        kn_ref[...] = (k * _head_rms(k, e2_ref[...], e2t_ref[...]) * kg_ref[...]).astype(BF16)
        v_ref[...] = raw[:, D + KV:].astype(BF16)

    return pl.pallas_call(
        body, name="attn_qkv_fwd", grid=(S // tm,),
        in_specs=[_rows(tm, D), _const((1, D)), _const((D, QKV)), _const((1, QKV)), _const((1, D)), _const((1, KV)),
                  _const((D, 128)), _const((128, D)), _const((KV, 128)), _const((128, KV))],
        out_specs=[_rows(tm, D), _rows(tm, QKV), _rows(tm * NPAIR, 128), _rows(tm, KV), _rows(tm, KV)],
        out_shape=[_sds((S, D), BF16), _sds((S, QKV), F32), _sds((S * NPAIR, 128), BF16), _sds((S, KV), BF16),
                   _sds((S, KV), BF16)],
        compiler_params=_cp(48))(x, g, w, b, qg, kg, e16, e16t, e2, e2t)


def _build_bias(bkt_ref, rb_ref, sk_ref, bias_sc, sk_sc):
    bkt = bkt_ref[...]
    row = lax.broadcasted_iota(jnp.int32, (2 * WIN, WIN), 0)
    for h in range(NH):
        p, e = h // 2, h % 2
        g, pp = p // 4, p % 4

        def add(bk, acc, h=h):
            return acc + jnp.where(bkt == bk, rb_ref[bk, h], 0.0)

        bias = lax.fori_loop(0, NBKT, add, jnp.where(bkt < 0, NEG, 0.0).astype(F32))
        bias_sc[0, g, e, :, pp * WIN:(pp + 1) * WIN] = bias
        bias_sc[1, g, e, :, pp * WIN:(pp + 1) * WIN] = jnp.where(row < WIN, NEG, bias)
        sk_sc[g, e, :, pp * WIN:(pp + 1) * WIN] = jnp.zeros((1, WIN), F32) + sk_ref[0, h]


def _stacks(f, mlo):
    r = pltpu.roll(f, HD, 1)
    z = jnp.zeros_like(f)
    return ((jnp.where(mlo, f, z).astype(BF16), jnp.where(mlo, z, r).astype(BF16)),
            (jnp.where(mlo, r, z).astype(BF16), jnp.where(mlo, z, f).astype(BF16)))


def _unstack(d, mlo):
    z = jnp.zeros_like(d[0][0])
    return (jnp.where(mlo, d[0][0], z) + pltpu.roll(jnp.where(mlo, z, d[0][1]), HD, 1)
            + pltpu.roll(jnp.where(mlo, d[1][0], z), HD, 1) + jnp.where(mlo, z, d[1][1]))


def _softmax_sink(l, sk):
    m = jnp.maximum(jnp.max(l, axis=0, keepdims=True), sk)
    ex = jnp.exp(l - m)
    es = jnp.exp(sk - m)
    inv = 1.0 / (jnp.sum(ex, axis=0, keepdims=True) + es)
    return ex * inv, es * inv


GROWS = 4 * WIN


def _attn_fwd(qn, kp, vp, bktt, rel_bias, sinks, wo, bo, x):
    S = x.shape[0]
    tq = min(512, S)
    nblk = tq // WIN

    def body(q_ref, k_ref, v_ref, bkt_ref, rb_ref, sk_ref, wo_ref, bo_ref, x_ref, o_ref, xo_ref, bias_sc, sk_sc):
        i = pl.program_id(0)

        @pl.when(i == 0)
        def _():
            _build_bias(bkt_ref, rb_ref, sk_ref, bias_sc, sk_sc)

        mlo = lax.broadcasted_iota(jnp.int32, (2 * WIN, KV), 1) < HD

        def blk(bb, carry):
            r0 = pl.multiple_of(bb * WIN, WIN)
            g0 = pl.multiple_of(i * tq + bb * WIN, WIN)
            first = (g0 == 0).astype(jnp.int32)
            ks = _stacks(k_ref[pl.ds(g0, 2 * WIN), :].astype(F32), mlo)
            vs = _stacks(v_ref[pl.ds(g0, 2 * WIN), :].astype(F32), mlo)
            for g in range(NKV):
                qs = q_ref[pl.ds(pl.multiple_of(bb * (NPAIR * WIN) + g * GROWS, GROWS), GROWS), :]
                og = jnp.zeros((GROWS, 128), F32)
                for e in range(2):
                    l = _dot_nt(ks[g][e], qs) + bias_sc[first, g, e]
                    pr, _ = _softmax_sink(l, sk_sc[g, e])
                    og = og + _dot_tn(pr.astype(BF16), vs[g][e])
                for pp in range(4):
                    p = 4 * g + pp
                    o_ref[pl.ds(r0, WIN), p * 128:(p + 1) * 128] = og[pp * WIN:(pp + 1) * WIN].astype(BF16)
            return carry

        lax.fori_loop(0, nblk, blk, 0)
        xo_ref[...] = x_ref[...] + _dot(o_ref[...], wo_ref[...]) + bo_ref[...]

    smem = pl.BlockSpec(memory_space=pltpu.SMEM)
    return pl.pallas_call(
        body, name="attn_fwd", grid=(S // tq,),
        in_specs=[_rows(tq * NPAIR, 128), _const((S + WIN, KV)), _const((S + WIN, KV)), _const((2 * WIN, WIN)), smem, smem,
                  _const((D, D)), _const((1, D)), _rows(tq, D)],
        out_specs=[_rows(tq, D), _rows(tq, D)],
        out_shape=[_sds((S, D), BF16), _sds((S, D), F32)],
        scratch_shapes=[pltpu.VMEM((2, NKV, 2, 2 * WIN, GROWS), F32), pltpu.VMEM((NKV, 2, 1, GROWS), F32)],
        compiler_params=_cp(48))(qn, kp, vp, bktt, rel_bias, sinks, wo, bo, x)


def _attn_bwd(dx, dxb, qn, kp, vp, bktt, bktt4, rel_bias, sinks, wo):
    S = dx.shape[0]
    tq = min(512, S)
    nblk = tq // WIN
    nsteps = S // tq

    def body(dx_ref, dxb_ref, q_ref, k_ref, v_ref, bkt_ref, bkt4_ref, rb_ref, sk_ref, wo_ref,
             dq_ref, dk_ref, dv_ref, dsk_ref, drb_ref, dbo_ref, bias_sc, sk_sc, dbias_sc, dsk_sc, do_sc):
        i = pl.program_id(0)

        @pl.when(i == 0)
        def _():
            _build_bias(bkt_ref, rb_ref, sk_ref, bias_sc, sk_sc)
            dbias_sc[...] = jnp.zeros_like(dbias_sc)
            dsk_sc[...] = jnp.zeros_like(dsk_sc)
            dk_ref[...] = jnp.zeros_like(dk_ref)
            dv_ref[...] = jnp.zeros_like(dv_ref)

        _first(dbo_ref, _colsum(dx_ref[...]))
        do = _dot_nt(dxb_ref[...], wo_ref[...]).astype(BF16)
        for bb in range(nblk):
            for p in range(NPAIR):
                r = (bb * NPAIR + p) * WIN
                do_sc[r:r + WIN, :] = do[bb * WIN:(bb + 1) * WIN, p * 128:(p + 1) * 128]

        mlo = lax.broadcasted_iota(jnp.int32, (2 * WIN, KV), 1) < HD

        def blk(bb, carry):
            g0 = pl.multiple_of(i * tq + bb * WIN, WIN)
            first = (g0 == 0).astype(jnp.int32)
            ks = _stacks(k_ref[pl.ds(g0, 2 * WIN), :].astype(F32), mlo)
            vs = _stacks(v_ref[pl.ds(g0, 2 * WIN), :].astype(F32), mlo)
            dks = [[None, None], [None, None]]
            dvs = [[None, None], [None, None]]
            for g in range(NKV):
                rows = pl.ds(pl.multiple_of(bb * (NPAIR * WIN) + g * GROWS, GROWS), GROWS)
                qs = q_ref[rows, :]
                dos = do_sc[rows, :]
                dqs = jnp.zeros((GROWS, 128), F32)
                for e in range(2):
                    l = _dot_nt(ks[g][e], qs) + bias_sc[first, g, e]
                    pr, ps = _softmax_sink(l, sk_sc[g, e])
                    dp = _dot_nt(vs[g][e], dos)
                    dr = jnp.sum(pr * dp, axis=0, keepdims=True)
                    dl = pr * (dp - dr)
                    dsk_sc[g, e] -= ps * dr
                    dbias_sc[g, e] += dl
                    dlb = dl.astype(BF16)
                    dqs = dqs + _dot_tn(dlb, ks[g][e])
                    dks[g][e] = _dot(dlb, qs)
                    dvs[g][e] = _dot(pr.astype(BF16), dos)
                dq_ref[rows, :] = dqs
            dk_ref[pl.ds(g0, 2 * WIN), :] += _unstack(dks, mlo)
            dv_ref[pl.ds(g0, 2 * WIN), :] += _unstack(dvs, mlo)
            return carry

        lax.fori_loop(0, nblk, blk, 0)

        @pl.when(i == nsteps - 1)
        def _():
            bkt4v = bkt4_ref[...]
            rid = lax.broadcasted_iota(jnp.int32, (NBKT, 128), 0)
            lid = lax.broadcasted_iota(jnp.int32, (NBKT, 128), 1)
            lid1 = lax.broadcasted_iota(jnp.int32, (1, 128), 1)
            dsk = jnp.zeros((1, 128), F32)
            for g in range(NKV):
                for e in range(2):
                    for pp in range(4):
                        h = 2 * (4 * g + pp) + e
                        t = jnp.sum(dsk_sc[g, e, :, pp * WIN:(pp + 1) * WIN], axis=1, keepdims=True)
                        dsk = dsk + jnp.where(lid1 == h, t, 0.0)
            dsk_ref[...] = dsk

            def per_bucket(bk, acc):
                mb = bkt4v == bk
                for g in range(NKV):
                    for e in range(2):
                        t = jnp.sum(jnp.where(mb, dbias_sc[g, e], 0.0), axis=0, keepdims=True)
                        for pp in range(4):
                            h = 2 * (4 * g + pp) + e
                            tt = jnp.sum(t[:, pp * WIN:(pp + 1) * WIN], axis=1, keepdims=True)
                            acc = acc + jnp.where((rid == bk) & (lid == h), tt, 0.0)
                return acc

            drb_ref[...] = lax.fori_loop(0, NBKT, per_bucket, jnp.zeros((NBKT, 128), F32))

    smem = pl.BlockSpec(memory_space=pltpu.SMEM)
    return pl.pallas_call(
        body, name="attn_bwd", grid=(nsteps,),
        in_specs=[_rows(tq, D), _rows(tq, D), _rows(tq * NPAIR, 128), _const((S + WIN, KV)), _const((S + WIN, KV)),
                  _const((2 * WIN, WIN)), _const((2 * WIN, GROWS)), smem, smem, _const((D, D))],
        out_specs=[_rows(tq * NPAIR, 128), _acc((S + WIN, KV)), _acc((S + WIN, KV)), _acc((1, 128)), _acc((NBKT, 128)),
                   _acc((1, D))],
        out_shape=[_sds((S * NPAIR, 128), F32), _sds((S + WIN, KV), F32), _sds((S + WIN, KV), F32), _sds((1, 128), F32),
                   _sds((NBKT, 128), F32), _sds((1, D), F32)],
        scratch_shapes=[pltpu.VMEM((2, NKV, 2, 2 * WIN, GROWS), F32), pltpu.VMEM((NKV, 2, 1, GROWS), F32),
                        pltpu.VMEM((NKV, 2, 2 * WIN, GROWS), F32), pltpu.VMEM((NKV, 2, 1, GROWS), F32),
                        pltpu.VMEM((tq * NPAIR, 128), BF16)],
        compiler_params=_cp(52))(dx, dxb, qn, kp, vp, bktt, bktt4, rel_bias, sinks, wo)


def _head_norm_bwd(dn, t, e, et, gt):
    r = _head_rms(t, e, et)
    th = t * r
    dth = dn * gt
    dt = r * (dth - th * _dot_hi(_dot_hi(dth * th, e) * (1.0 / HD), et))
    return dt, _colsum(dn * th)


def _attn_qkv_bwd(dqs, dk, dv, raw, x, dxo, g, w, qg, kg, e16, e16t, e2, e2t, fold, fold2):
    S = x.shape[0]
    tm = min(512, S)

    def body(dq_ref, dk_ref, dv_ref, raw_ref, x_ref, dxo_ref, g_ref, w_ref, qg_ref, kg_ref,
             e16_ref, e16t_ref, e2_ref, e2t_ref, fold_ref, fold2_ref,
             dqkv_ref, dx_ref, dxb_ref, db_ref, dg_ref, dqg_ref, dkg_ref, dq_sc):
        for bb in range(tm // WIN):
            for p in range(NPAIR):
                r = (bb * NPAIR + p) * WIN
                dq_sc[bb * WIN:(bb + 1) * WIN, p * 128:(p + 1) * 128] = dq_ref[r:r + WIN, :]
        dq, cq = _head_norm_bwd(dq_sc[...] * 0.125, raw_ref[:, :D], e16_ref[...], e16t_ref[...], qg_ref[...])
        dk_, ck = _head_norm_bwd(dk_ref[...], raw_ref[:, D:D + KV], e2_ref[...], e2t_ref[...], kg_ref[...])
        dv_ = dv_ref[...]
        _first(dqg_ref, _dot_hi(cq, fold_ref[...]))
        _first(dkg_ref, _dot_hi(ck, fold2_ref[...]))
        _first_cols(db_ref, [(0, _colsum(dq)), (D, _colsum(dk_)), (D + KV, _colsum(dv_))])
        dqb, dkb, dvb = dq.astype(BF16), dk_.astype(BF16), dv_.astype(BF16)
        dqkv_ref[:, :D] = dqb
        dqkv_ref[:, D:D + KV] = dkb
        dqkv_ref[:, D + KV:] = dvb
        dh = (_dot_nt(dqb, w_ref[:, :D]) + _dot_nt(dkb, w_ref[:, D:D + KV]) + _dot_nt(dvb, w_ref[:, D + KV:]))
        xv = x_ref[...]
        dxn, dg = _rms_bwd(dh, xv, _rms(xv), g_ref[...])
        dx = dxo_ref[...] + dxn
        dx_ref[...] = dx
        dxb_ref[...] = dx.astype(BF16)
        _first(dg_ref, dg)

    return pl.pallas_call(
        body, name="attn_qkv_bwd", grid=(S // tm,),
        in_specs=[_rows(tm * NPAIR, 128), _rows(tm, KV), _rows(tm, KV), _rows(tm, QKV), _rows(tm, D), _rows(tm, D),
                  _const((1, D)), _const((D, QKV)), _const((1, D)), _const((1, KV)),
                  _const((D, 128)), _const((128, D)), _const((KV, 128)), _const((128, KV)),
                  _const((D, 128)), _const((KV, 128))],
        out_specs=[_rows(tm, QKV), _rows(tm, D), _rows(tm, D), _acc((1, QKV)), _acc((1, D)), _acc((1, 128)), _acc((1, 128))],
        out_shape=[_sds((S, QKV), BF16), _sds((S, D), F32), _sds((S, D), BF16), _sds((1, QKV), F32), _sds((1, D), F32),
                   _sds((1, 128), F32), _sds((1, 128), F32)],
        scratch_shapes=[pltpu.VMEM((tm, D), F32)],
        compiler_params=_cp(48))(dqs, dk, dv, raw, x, dxo, g, w, qg, kg, e16, e16t, e2, e2t, fold, fold2)


def _conv_mid_bwd(dx, dxb, c, lng, lnb, w):
    S = dx.shape[0]
    tm = min(512, S)

    def body(dx_ref, dxb_ref, c_ref, lng_ref, lnb_ref, w_ref, dc_ref, dbo_ref, dlg_ref, dlb_ref, ddwb_ref):
        _first(dbo_ref, _colsum(dx_ref[...]))
        ds = _dot_nt(dxb_ref[...], w_ref[...])
        c = c_ref[...]
        xc = c - jnp.mean(c, axis=-1, keepdims=True)
        rstd = lax.rsqrt(jnp.mean(xc * xc, axis=-1, keepdims=True) + EPS)
        ch = xc * rstd
        y = ch * lng_ref[...] + lnb_ref[...]
        sg = jax.nn.sigmoid(y)
        dy = ds * (sg * (1.0 + y * (1.0 - sg)))
        _first(dlg_ref, _colsum(dy * ch))
        _first(dlb_ref, _colsum(dy))
        dch = dy * lng_ref[...]
        dc = rstd * (dch - jnp.mean(dch, axis=-1, keepdims=True) - ch * jnp.mean(dch * ch, axis=-1, keepdims=True))
        dc_ref[...] = dc
        _first(ddwb_ref, _colsum(dc))

    return pl.pallas_call(
        body, name="conv_mid_bwd", grid=(S // tm,),
        in_specs=[_rows(tm, D), _rows(tm, D), _rows(tm, D), _const((1, D)), _const((1, D)), _const((D, D))],
        out_specs=[_rows(tm, D), _acc((1, D)), _acc((1, D)), _acc((1, D)), _acc((1, D))],
        out_shape=[_sds((S, D), F32)] + [_sds((1, D), F32)] * 4,
        compiler_params=_cp(40))(dx, dxb, c, lng, lnb, w)


def _dwconv_bwd(dc, glu, dw):
    S = dc.shape[0]
    tm = min(256, S)
    hb = tm // HALO
    nsteps = S // tm

    def body(dc_ref, nxt_ref, gl_ref, halo_ref, dw_ref, dgl_ref, ddw_ref, ext, dext, esh, dsh):
        i = pl.program_id(0)
        _fill_ext(ext, halo_ref, gl_ref, tm, i > 0)
        dext[0:tm, :] = dc_ref[...]
        dext[tm:tm + HALO, :] = jnp.where(i < nsteps - 1, nxt_ref[...], 0.0)
        _shift_copies(esh, ext, tm + 24)
        _shift_copies(dsh, dext, tm + 24)

        @pl.when(i == 0)
        def _():
            ddw_ref[...] = jnp.zeros_like(ddw_ref)

        for c0 in range(0, D, 512):
            for r0 in range(0, tm, 32):
                acc = jnp.zeros((32, 512), F32)
                for j in range(CONVW):
                    acc = acc + dw_ref[CONVW - 1 - j:CONVW - j, c0:c0 + 512] * _tap(dext, dsh, j, r0, c0)
                dgl_ref[r0:r0 + 32, c0:c0 + 512] = acc
            for k in range(CONVW):
                acc = jnp.zeros((32, 512), F32)
                for r0 in range(0, tm, 32):
                    acc = acc + dext[r0:r0 + 32, c0:c0 + 512] * _tap(ext, esh, 2 + k, r0, c0)
                ddw_ref[k:k + 1, c0:c0 + 512] += _colsum(acc)

    return pl.pallas_call(
        body, name="dwconv_bwd", grid=(nsteps,),
        in_specs=[_rows(tm, D), pl.BlockSpec((HALO, D), lambda i: (jnp.minimum((i + 1) * hb, S // HALO - 1), 0)),
                  _rows(tm, D), pl.BlockSpec((HALO, D), lambda i: (jnp.maximum(i * hb - 1, 0), 0)), _const((32, D))],
        out_specs=[_rows(tm, D), _acc((32, D))],
        out_shape=[_sds((S, D), F32), _sds((32, D), F32)],
        scratch_shapes=[pltpu.VMEM((tm + HALO, D), F32), pltpu.VMEM((tm + HALO, D), F32),
                        pltpu.VMEM((7, tm + 24, D), F32), pltpu.VMEM((7, tm + 24, D), F32)],
        compiler_params=_cp(48))(dc, dc, glu, glu, dw)


def _conv_in_bwd(dglu, u, x, dxo, g, w):
    S = x.shape[0]
    tm = min(512, S)

    def body(dgl_ref, u_ref, x_ref, dxo_ref, g_ref, w_ref, du_ref, dx_ref, db_ref, dg_ref):
        dgl = dgl_ref[...]
        a = u_ref[:, :D]
        sg = jax.nn.sigmoid(u_ref[:, D:])
        da = dgl * sg
        dgt = dgl * a * sg * (1.0 - sg)
        _first_cols(db_ref, [(0, _colsum(da)), (D, _colsum(dgt))])
        dab, dgb = da.astype(BF16), dgt.astype(BF16)
        du_ref[:, :D] = dab
        du_ref[:, D:] = dgb
        dh = _dot_nt(dab, w_ref[:, :D]) + _dot_nt(dgb, w_ref[:, D:])
        xv = x_ref[...]
        dxn, dg = _rms_bwd(dh, xv, _rms(xv), g_ref[...])
        dx_ref[...] = dxo_ref[...] + dxn
        _first(dg_ref, dg)

    return pl.pallas_call(
        body, name="conv_in_bwd", grid=(S // tm,),
        in_specs=[_rows(tm, D), _rows(tm, 2 * D), _rows(tm, D), _rows(tm, D), _const((1, D)), _const((D, 2 * D))],
        out_specs=[_rows(tm, 2 * D), _rows(tm, D), _acc((1, 2 * D)), _acc((1, D))],
        out_shape=[_sds((S, 2 * D), BF16), _sds((S, D), F32), _sds((1, 2 * D), F32), _sds((1, D), F32)],
        compiler_params=_cp(48))(dglu, u, x, dxo, g, w)


def _coords():
    return lax.axis_index("x"), lax.axis_index("y"), lax.axis_index("c")


def _all_gather(wpack, spack):
    R, RS_ = wpack.shape[0], spack.shape[0]

    def body(w_ref, s_ref, gw_ref, gs_ref, stage, send_sems, recv_sems, local_sem):
        x, y, c = _coords()
        me, sib = (x, y, c), (x, y, 1 - c)
        chips = [(1 - x, y), (x, 1 - y), (1 - x, 1 - y)]

        def cast(t, carry):
            r = pl.multiple_of(t * PACK_CH, PACK_CH)
            stage[pl.ds(r, PACK_CH), :] = w_ref[pl.ds(r, PACK_CH), :].astype(BF16)
            return carry

        lax.fori_loop(0, R // PACK_CH, cast, 0)

        def slot(px, py, pc):
            return 4 * px + 2 * py + pc

        def copies(k, block, to, own=False):
            idx = slot(*block)
            return [
                pltpu.make_async_remote_copy(
                    src_ref=stage if own else gw_ref.at[idx], dst_ref=gw_ref.at[idx],
                    send_sem=send_sems.at[0, k], recv_sem=recv_sems.at[0, k], device_id=to, device_id_type=MESH),
                pltpu.make_async_remote_copy(
                    src_ref=s_ref if own else gs_ref.at[idx], dst_ref=gs_ref.at[idx],
                    send_sem=send_sems.at[1, k], recv_sem=recv_sems.at[1, k], device_id=to, device_id_type=MESH)]

        mine = pltpu.make_async_copy(stage, gw_ref.at[slot(*me)], local_sem)
        mine.start()
        gs_ref[slot(*me)] = s_ref[...]
        first = copies(0, me, sib, own=True)
        for j, chip in enumerate(chips):
            first += copies(1 + j, me, (*chip, c), own=True)
        for cp in first:
            cp.start()
        passed = []
        for j, chip in enumerate(chips):
            for cp in copies(1 + j, (*chip, c), me):
                cp.wait_recv()
            fwd = copies(4 + j, (*chip, c), sib)
            for cp in fwd:
                cp.start()
            passed += fwd
        for cp in copies(0, sib, me):
            cp.wait_recv()
        for j, chip in enumerate(chips):
            for cp in copies(4 + j, (*chip, 1 - c), me):
                cp.wait_recv()
        for cp in first + passed:
            cp.wait_send()
        mine.wait()

    vmem = pl.BlockSpec(memory_space=pltpu.VMEM)
    return pl.pallas_call(
        body, name="all_gather_weights",
        in_specs=[vmem, vmem],
        out_specs=[pl.BlockSpec(memory_space=pl.ANY), vmem],
        out_shape=[_sds((NDEV, R, D), BF16), _sds((NDEV, RS_, 128), F32)],
        scratch_shapes=[pltpu.VMEM((R, D), BF16), pltpu.SemaphoreType.DMA((2, 7)), pltpu.SemaphoreType.DMA((2, 7)),
                        pltpu.SemaphoreType.DMA],
        compiler_params=pltpu.CompilerParams(vmem_limit_bytes=40 << 20))(wpack, spack)


def _reduce_scatter(gpack, spack):
    R, RS_ = gpack.shape[1], spack.shape[0]
    nch = R // PACK_CH

    def body(g_ref, s_ref, red_ref, sred_ref, land_ref, sland, buf, send_sems, recv_sems, local_sem, buf_sems):
        x, y, c = _coords()
        me = 4 * x + 2 * y + c
        sends = []
        for k in range(1, NDEV):
            kx, ky, kc = (k >> 2) & 1, (k >> 1) & 1, k & 1
            peer = (x ^ kx, y ^ ky, c ^ kc)
            pidx = me ^ k
            cw = pltpu.make_async_remote_copy(
                src_ref=g_ref.at[pidx], dst_ref=land_ref.at[me], send_sem=send_sems.at[0, k - 1],
                recv_sem=recv_sems.at[0, k - 1], device_id=peer, device_id_type=MESH)
            cs = pltpu.make_async_remote_copy(
                src_ref=s_ref, dst_ref=sland.at[me], send_sem=send_sems.at[1, k - 1],
                recv_sem=recv_sems.at[1, k - 1], device_id=peer, device_id_type=MESH)
            cw.start()
            cs.start()
            sends += [cw, cs]
        mine = pltpu.make_async_copy(g_ref.at[me], land_ref.at[me], local_sem)
        mine.start()
        sland[me] = s_ref[...]
        for k in range(1, NDEV):
            kx, ky, kc = (k >> 2) & 1, (k >> 1) & 1, k & 1
            peer = (x ^ kx, y ^ ky, c ^ kc)
            pidx = me ^ k
            pltpu.make_async_remote_copy(
                src_ref=g_ref.at[pidx], dst_ref=land_ref.at[pidx], send_sem=send_sems.at[0, k - 1],
                recv_sem=recv_sems.at[0, k - 1], device_id=peer, device_id_type=MESH).wait_recv()
            pltpu.make_async_remote_copy(
                src_ref=s_ref, dst_ref=sland.at[pidx], send_sem=send_sems.at[1, k - 1],
                recv_sem=recv_sems.at[1, k - 1], device_id=peer, device_id_type=MESH).wait_recv()
        mine.wait()
        for cp in sends:
            cp.wait_send()

        acc = sland[0]
        for s in range(1, NDEV):
            acc = acc + sland[s]
        sred_ref[...] = acc

        def fetch(t, slot):
            r = pl.multiple_of(t * PACK_CH, PACK_CH)
            return pltpu.make_async_copy(land_ref.at[:, pl.ds(r, PACK_CH), :], buf.at[slot], buf_sems.at[slot])

        fetch(0, 0).start()

        def step(t, carry):
            slot = t % 2

            @pl.when(t + 1 < nch)
            def _():
                fetch(t + 1, 1 - slot).start()

            fetch(t, slot).wait()
            a = buf[slot, 0].astype(F32)
            for s in range(1, NDEV):
                a = a + buf[slot, s].astype(F32)
            red_ref[pl.ds(pl.multiple_of(t * PACK_CH, PACK_CH), PACK_CH), :] = a
            return carry

        lax.fori_loop(0, nch, step, 0)

    vmem = pl.BlockSpec(memory_space=pltpu.VMEM)
    hbm = pl.BlockSpec(memory_space=pl.ANY)
    red, sred, _ = pl.pallas_call(
        body, name="reduce_scatter_grads",
        in_specs=[hbm, vmem],
        out_specs=[vmem, vmem, hbm],
        out_shape=[_sds((R, D), F32), _sds((RS_, 128), F32), _sds((NDEV, R, D), BF16)],
        scratch_shapes=[pltpu.VMEM((NDEV, RS_, 128), F32), pltpu.VMEM((2, NDEV, PACK_CH, D), BF16),
                        pltpu.SemaphoreType.DMA((2, 7)), pltpu.SemaphoreType.DMA((2, 7)), pltpu.SemaphoreType.DMA,
                        pltpu.SemaphoreType.DMA((2,))],
        compiler_params=pltpu.CompilerParams(vmem_limit_bytes=48 << 20))(gpack, spack)
    return red, sred


def _adamw(w, g, m, v, name):
    R, L = w.shape
    tr = R
    for cand in (680, 512, 392, 256, 136, 64, 8):
        if R % cand == 0:
            tr = cand
            break

    def body(w_ref, g_ref, m_ref, v_ref, d_ref, nm_ref, nv_ref):
        gv = g_ref[...]
        nm = B1 * m_ref[...] + (1.0 - B1) * gv
        nv = B2 * v_ref[...] + (1.0 - B2) * jnp.square(gv)
        m_hat = nm / (1.0 - B1 ** STEP)
        v_hat = nv / (1.0 - B2 ** STEP)
        d_ref[...] = -LR * (m_hat / (jnp.sqrt(v_hat) + AEPS) + WD * w_ref[...])
        nm_ref[...] = nm
        nv_ref[...] = nv

    spec = pl.BlockSpec((tr, L), lambda i: (i, 0))
    return pl.pallas_call(
        body, name=name, grid=(R // tr,), in_specs=[spec] * 4, out_specs=[spec] * 3,
        out_shape=[_sds((R, L), F32)] * 3, compiler_params=_cp(48))(w, g, m, v)


BIG = ("conv_w_in", "conv_w_out", "w_qkv", "w_o", "w_up", "w_down")
SMALL = (("conv_norm_g", (1, D), False), ("conv_b_in", (1, 2 * D), False), ("conv_dw", (1, CONVW, D), True),
         ("conv_dw_b", (1, D), False), ("conv_ln_g", (1, D), False), ("conv_ln_b", (1, D), False),
         ("conv_b_out", (1, D), False), ("attn_norm_g", (1, D), True), ("b_qkv", (1, QKV), True),
         ("q_norm_g", (1, HD), False), ("k_norm_g", (1, HD), False), ("sinks", (1, NH), False),
         ("b_o", (1, D), True), ("rel_bias", (NBKT, NH), False), ("mlp_norm_g", (2, D), False))


def _lanes(a):
    flat = a.reshape(-1)
    n = flat.shape[0]
    rows = -(-n // 128)
    return jnp.pad(flat, (0, rows * 128 - n)).reshape(rows, 128)


def _pack_small(arrs):
    rows = jnp.concatenate([_lanes(a) for a in arrs], axis=0)
    pad = -rows.shape[0] % 8
    return jnp.pad(rows, ((0, pad), (0, 0)))


def _unpack_small(pack, shapes):
    out, r = [], 0
    for shp in shapes:
        n = int(np.prod(shp))
        rows = -(-n // 128)
        out.append(pack[r:r + rows].reshape(-1)[:n].reshape(shp))
        r += rows
    return out


def _pack_big(p):
    return jnp.concatenate([p["conv_w_in"].reshape(256, D), p["conv_w_out"].reshape(128, D), p["w_qkv"].reshape(160, D),
                            p["w_o"].reshape(128, D), p["w_up"].reshape(1024, D), p["w_down"].reshape(1024, D)], axis=0)


def _unpack_big(pack):
    return {"conv_w_in": pack[0:256].reshape(1, D, 256), "conv_w_out": pack[256:384].reshape(1, 128, D),
            "w_qkv": pack[384:544].reshape(1, D, 160), "w_o": pack[544:672].reshape(1, 128, D),
            "w_up": pack[672:1696].reshape(2, D, FFB), "w_down": pack[1696:2720].reshape(2, FFB, D)}


def _cols_to_blocks(a, n):
    M = a.shape[0]
    return a.reshape(M, NDEV, n).transpose(1, 0, 2).reshape(NDEV, M * n // D, D)


def kernel(x, conv_norm_g, conv_w_in, conv_b_in, conv_dw, conv_dw_b, conv_ln_g, conv_ln_b, conv_w_out, conv_b_out, attn_norm_g, w_qkv, b_qkv, q_norm_g, k_norm_g, sinks, w_o, b_o, rel_bias, mlp_norm_g, w_up, w_down, loss_target, m_conv_norm_g, m_conv_w_in, m_conv_b_in, m_conv_dw, m_conv_dw_b, m_conv_ln_g, m_conv_ln_b, m_conv_w_out, m_conv_b_out, m_attn_norm_g, m_w_qkv, m_b_qkv, m_q_norm_g, m_k_norm_g, m_sinks, m_w_o, m_b_o, m_rel_bias, m_mlp_norm_g, m_w_up, m_w_down, v_conv_norm_g, v_conv_w_in, v_conv_b_in, v_conv_dw, v_conv_dw_b, v_conv_ln_g, v_conv_ln_b, v_conv_w_out, v_conv_b_out, v_attn_norm_g, v_w_qkv, v_b_qkv, v_q_norm_g, v_k_norm_g, v_sinks, v_w_o, v_b_o, v_rel_bias, v_mlp_norm_g, v_w_up, v_w_down):
    names = [n for n, _, _ in SMALL] + list(BIG)
    loc = dict(locals())
    W = {n: loc[n] for n in names}
    M_ = {n: loc["m_" + n] for n in names}
    V_ = {n: loc["v_" + n] for n in names}
    me = 4 * lax.axis_index("x") + 2 * lax.axis_index("y") + lax.axis_index("c")
    xs = x[0]
    S = xs.shape[0]

    sharded = [n for n, _, sh in SMALL if sh]
    gw, gs = _all_gather(_pack_big(W), _pack_small([W[n] for n in sharded]))
    w_in = gw[:, 0:256].reshape(NDEV, D, 256).transpose(1, 0, 2).reshape(D, 2 * D)
    w_out = gw[:, 256:384].reshape(D, D)
    wqkv = gw[:, 384:544].reshape(NDEV, D, 160).transpose(1, 0, 2).reshape(D, QKV)
    wo = gw[:, 544:672].reshape(D, D)
    wu = gw[:, 672:1696].reshape(NDEV, 2, D, FFB)
    wd = gw[:, 1696:2720].reshape(NDEV, 2, FFB, D)
    shard_shapes = {"conv_dw": (CONVW, 128), "attn_norm_g": (1, 128), "b_qkv": (1, 160), "b_o": (1, 128)}
    parts = [_unpack_small(gs[j], [shard_shapes[n] for n in sharded]) for j in range(NDEV)]
    full = {n: jnp.concatenate([parts[j][i] for j in range(NDEV)], axis=-1) for i, n in enumerate(sharded)}
    dw32 = jnp.pad(full["conv_dw"], ((0, 1), (0, 0)))
    attn_g, bqkv, bo = full["attn_norm_g"], full["b_qkv"], full["b_o"]
    qg = jnp.tile(q_norm_g, (1, NH))
    kg = jnp.tile(k_norm_g, (1, NKV))
    e16, e16t, e2, e2t, fold, fold2 = _seg_mats()
    bkt = jnp.asarray(_bucket_table().T)
    bkt4 = jnp.asarray(np.tile(_bucket_table().T, (1, 4)))

    h0, u, glu = _conv_in_fwd(xs, conv_norm_g, w_in, conv_b_in)
    cc, sb, x1 = _conv_mid_fwd(glu, xs, dw32, conv_dw_b, conv_ln_g, conv_ln_b, w_out, conv_b_out)
    h1, act0, x2 = _mlp_fwd(x1, mlp_norm_g[0:1], wu[:, 0], wd[:, 0])
    h2, raw, qn, kn, vv = _attn_qkv_fwd(x2, attn_g, wqkv, bqkv, qg, kg, e16, e16t, e2, e2t)
    kp = jnp.pad(kn, ((WIN, 0), (0, 0)))
    vp = jnp.pad(vv, ((WIN, 0), (0, 0)))
    ob, x3 = _attn_fwd(qn, kp, vp, bkt, rel_bias, sinks, wo, bo, x2)
    h3, act1, dx4, dx4b, sq = _mlp_fwd(x3, mlp_norm_g[1:2], wu[:, 1], wd[:, 1], target=loss_target[0])
    loss = lax.psum(jnp.sum(sq) * (0.5 / D), ("x", "y", "c"))

    dup1, dx3, dx3b, dg_mlp1 = _mlp_bwd(dx4, dx4b, x3, mlp_norm_g[1:2], act1, wu[:, 1], wd[:, 1])
    g_wdn1 = _wgrad(act1, dx4b, 512, D, "wgrad_down1")
    g_wup1 = _wgrad(h3, dup1, D, 512, "wgrad_up1")
    dqs, dkp, dvp, d_sinks, d_rel, d_bo = _attn_bwd(dx3, dx3b, qn, kp, vp, bkt, bkt4, rel_bias, sinks, wo)
    g_wo = _wgrad(ob, dx3b, D, 512, "wgrad_o")
    dqkv, dx2, dx2b, d_bqkv, d_attn_g, d_qg, d_kg = _attn_qkv_bwd(
        dqs, dkp[WIN:], dvp[WIN:], raw, x2, dx3, attn_g, wqkv, qg, kg, e16, e16t, e2, e2t, fold, fold2)
    g_wqkv = _wgrad(h2, dqkv, D, 640, "wgrad_qkv")
    dup0, dx1, dx1b, dg_mlp0 = _mlp_bwd(dx2, dx2b, x1, mlp_norm_g[0:1], act0, wu[:, 0], wd[:, 0])
    g_wdn0 = _wgrad(act0, dx2b, 512, D, "wgrad_down0")
    g_wup0 = _wgrad(h1, dup0, D, 512, "wgrad_up0")
    dc, d_bout, d_lng, d_lnb, d_dwb = _conv_mid_bwd(dx1, dx1b, cc, conv_ln_g, conv_ln_b, w_out)
    g_wout = _wgrad(sb, dx1b, D, 512, "wgrad_conv_out")
    dglu, d_dw = _dwconv_bwd(dc, glu, dw32)
    du, grad_x, d_bin, d_cng = _conv_in_bwd(dglu, u, xs, dx1, conv_norm_g, w_in)
    g_win = _wgrad(h0, du, D, 512, "wgrad_conv_in")

    gpack = jnp.concatenate([
        _cols_to_blocks(g_win, 256), g_wout.reshape(NDEV, 128, D), _cols_to_blocks(g_wqkv, 160),
        g_wo.reshape(NDEV, 128, D),
        jnp.stack([_cols_to_blocks(g_wup0, FFB), _cols_to_blocks(g_wup1, FFB)], axis=1).reshape(NDEV, 1024, D),
        jnp.stack([g_wdn0.reshape(NDEV, FFB, D), g_wdn1.reshape(NDEV, FFB, D)], axis=1).reshape(NDEV, 1024, D)], axis=1)
    small_grads = {
        "conv_norm_g": d_cng, "conv_b_in": d_bin, "conv_dw": d_dw[:CONVW][None], "conv_dw_b": d_dwb,
        "conv_ln_g": d_lng, "conv_ln_b": d_lnb, "conv_b_out": d_bout, "attn_norm_g": d_attn_g, "b_qkv": d_bqkv,
        "q_norm_g": d_qg[:, :HD], "k_norm_g": d_kg[:, :HD], "sinks": d_sinks[:, :NH], "b_o": d_bo,
        "rel_bias": d_rel[:, :NH], "mlp_norm_g": jnp.concatenate([dg_mlp0, dg_mlp1], axis=0)}
    red, sred = _reduce_scatter(gpack, _pack_small([small_grads[n] for n, _, _ in SMALL]))
    sg_full = _unpack_small(sred, [shp for _, shp, _ in SMALL])

    G = _unpack_big(red)
    for (n, shp, sh), gfull in zip(SMALL, sg_full):
        if sh:
            width = shp[-1] // NDEV
            G[n] = lax.dynamic_slice_in_dim(gfull, me * width, width, axis=gfull.ndim - 1)
        else:
            G[n] = gfull
    d_big, m_big, v_big = _adamw(_pack_big(W), red, _pack_big(M_), _pack_big(V_), "adamw_big")
    snames = [n for n, _, _ in SMALL]
    d_sm, m_sm, v_sm = _adamw(*[_pack_small([t[n] for n in snames]) for t in (W, G, M_, V_)], "adamw_small")
    sshapes = [W[n].shape for n in snames]
    outs = []
    for big, small in ((d_big, d_sm), (m_big, m_sm), (v_big, v_sm)):
        o = _unpack_big(big)
        o.update(dict(zip(snames, _unpack_small(small, sshapes))))
        outs.append(o)
    order = ["conv_norm_g", "conv_w_in", "conv_b_in", "conv_dw", "conv_dw_b", "conv_ln_g", "conv_ln_b", "conv_w_out",
             "conv_b_out", "attn_norm_g", "w_qkv", "b_qkv", "q_norm_g", "k_norm_g", "sinks", "w_o", "b_o", "rel_bias",
             "mlp_norm_g", "w_up", "w_down"]
    return (loss, grad_x[None], *[G[n] for n in order], *[outs[0][n] for n in order],
            *[outs[1][n] for n in order], *[outs[2][n] for n in order])
```

```python
import math

import numpy as np
import jax
import jax.numpy as jnp
from jax import lax
from jax.experimental import pallas as pl
from jax.experimental.pallas import tpu as pltpu

F32 = jnp.float32
BF16 = jnp.bfloat16

D = 1024
DFF = 4096
NH, NKV, HD = 16, 2, 64
NPAIR = NH // 2
KV = NKV * HD
QKV = D + 2 * KV
CONVW = 31
WIN = 128
NBKT = 32
EPS = 1e-6
NEG = -1e30
NDEV = 8
FFB = DFF // NDEV
HALO = 32

LR, B1, B2, AEPS, WD, STEP = 0.001, 0.9, 0.999, 1e-08, 0.01, 10

PACK = (("conv_w_in", 256), ("conv_w_out", 128), ("up0", 512), ("dn0", 512), ("w_qkv", 160), ("w_o", 128),
        ("up1", 512), ("dn1", 512))
PACK_OFF = {n: sum(r for _, r in PACK[:i]) for i, (n, _) in enumerate(PACK)}
PACK_ROWS = sum(r for _, r in PACK)
PACK_CH = 272
AG_FIRST = (0, 384)
AG_MLP0 = (384, 1024)
AG_REST = (1408, 1312)
RS_MLP1 = (1696, 1024)
RS_ATTN = (1408, 288)
RS_MID = (256, 1152)
RS_LAST = (0, 256)
MESH = pl.DeviceIdType.MESH


def _cp(vmem_mb, n_axes=1):
    return pltpu.CompilerParams(dimension_semantics=("arbitrary",) * n_axes, vmem_limit_bytes=vmem_mb << 20)


def _const(shape):
    nd = len(shape)
    return pl.BlockSpec(shape, lambda *_: (0,) * nd, pipeline_mode=pl.Buffered(1))


def _acc(shape):
    nd = len(shape)
    return pl.BlockSpec(shape, lambda *_: (0,) * nd)


def _rows(tm, n):
    return pl.BlockSpec((tm, n), lambda i: (i, 0))


def _sds(shape, dtype):
    return jax.ShapeDtypeStruct(shape, dtype)


def _dot(a, b):
    return jnp.dot(a, b, preferred_element_type=F32)


def _dot_nt(a, b):
    return lax.dot_general(a, b, (((1,), (1,)), ((), ())), preferred_element_type=F32)


def _dot_tn(a, b):
    return lax.dot_general(a, b, (((0,), (0,)), ((), ())), preferred_element_type=F32)


def _dot_hi(x, e):
    x1 = x.astype(BF16)
    r1 = x - x1.astype(F32)
    x2 = r1.astype(BF16)
    x3 = (r1 - x2.astype(F32)).astype(BF16)
    return _dot(x1, e) + _dot(x2, e) + _dot(x3, e)


def _rms(x):
    return lax.rsqrt(jnp.mean(x * x, axis=-1, keepdims=True) + EPS)


def _rms_bwd(dh, x, r, g):
    xh = x * r
    dxh = dh * g
    dx = r * (dxh - xh * jnp.mean(dxh * xh, axis=-1, keepdims=True))
    return dx, jnp.sum(dh * xh, axis=0, keepdims=True)


def _colsum(a):
    return jnp.sum(a, axis=0, keepdims=True)


def _first(ref, val):
    @pl.when(pl.program_id(0) == 0)
    def _():
        ref[...] = jnp.zeros_like(ref)
    ref[...] += val


def _first_cols(ref, vals):
    @pl.when(pl.program_id(0) == 0)
    def _():
        ref[...] = jnp.zeros_like(ref)
    for c0, val in vals:
        ref[:, c0:c0 + val.shape[1]] += val


def _conv_in_fwd(x, g, w, b):
    S = x.shape[0]
    tm = min(512, S)

    def body(x_ref, g_ref, w_ref, b_ref, h_ref, u_ref, glu_ref):
        xv = x_ref[...]
        hb = (xv * _rms(xv) * g_ref[...]).astype(BF16)
        h_ref[...] = hb
        u = _dot(hb, w_ref[...]) + b_ref[...]
        u_ref[...] = u
        glu_ref[...] = u[:, :D] * jax.nn.sigmoid(u[:, D:])

    return pl.pallas_call(
        body, name="conv_in_fwd", grid=(S // tm,),
        in_specs=[_rows(tm, D), _const((1, D)), _const((D, 2 * D)), _const((1, 2 * D))],
        out_specs=[_rows(tm, D), _rows(tm, 2 * D), _rows(tm, D)],
        out_shape=[_sds((S, D), BF16), _sds((S, 2 * D), F32), _sds((S, D), F32)],
        compiler_params=_cp(48))(x, g, w, b)


def _fill_ext(ext, halo_ref, cur_ref, tm, keep):
    ext[0:HALO, :] = jnp.where(keep, halo_ref[...], 0.0)
    ext[HALO:HALO + tm, :] = cur_ref[...]


def _shift_copies(dst, src, n):
    for s_ in range(1, 8):
        dst[s_ - 1, 0:n, :] = src[s_:s_ + n, :]


def _tap(src, sh, o, r0, c0):
    a, s_ = divmod(o, 8)
    ref = src if s_ == 0 else sh.at[s_ - 1]
    return ref[r0 + 8 * a:r0 + 8 * a + 32, c0:c0 + 512]


def _conv_mid_fwd(glu, x, dw, dwb, lng, lnb, w, b, carry=None):
    S = x.shape[0]
    tm = min(256, S)
    hb = tm // HALO

    def body(gl_ref, halo_ref, x_ref, dw_ref, dwb_ref, lng_ref, lnb_ref, w_ref, b_ref, c_ref, s_ref, xo_ref, ext, esh):
        i = pl.program_id(0)
        _fill_ext(ext, halo_ref, gl_ref, tm, i > 0)
        _shift_copies(esh, ext, tm + 24)
        for r0 in range(0, tm, 32):
            for c0 in range(0, D, 512):
                acc = jnp.zeros((32, 512), F32) + dwb_ref[:, c0:c0 + 512]
                for k in range(CONVW):
                    acc = acc + dw_ref[k:k + 1, c0:c0 + 512] * _tap(ext, esh, 2 + k, r0, c0)
                c_ref[r0:r0 + 32, c0:c0 + 512] = acc
        c = c_ref[...]
        mu = jnp.mean(c, axis=-1, keepdims=True)
        xc = c - mu
        y = xc * lax.rsqrt(jnp.mean(xc * xc, axis=-1, keepdims=True) + EPS) * lng_ref[...] + lnb_ref[...]
        sb = (y * jax.nn.sigmoid(y)).astype(BF16)
        s_ref[...] = sb
        xo_ref[...] = x_ref[...] + _dot(sb, w_ref[...]) + b_ref[...]

    return _carried_call(
        body, carry, name="conv_mid_fwd", grid=(S // tm,),
        in_specs=[_rows(tm, D), pl.BlockSpec((HALO, D), lambda i: (jnp.maximum(i * hb - 1, 0), 0)), _rows(tm, D),
                  _const((32, D)), _const((1, D)), _const((1, D)), _const((1, D)), _const((D, D)), _const((1, D))],
        out_specs=[_rows(tm, D), _rows(tm, D), _rows(tm, D)],
        out_shape=[_sds((S, D), F32), _sds((S, D), BF16), _sds((S, D), F32)],
        scratch_shapes=[pltpu.VMEM((tm + HALO, D), F32), pltpu.VMEM((7, tm + 24, D), F32)],
        compiler_params=_cp(48), args=[glu, glu, x, dw, dwb, lng, lnb, w, b])


def _mlp_fwd(x, g, wu, wd, target=None, carry=None):
    S = x.shape[0]
    tm = min(512, S)
    last = target is not None

    def body(*refs):
        if last:
            x_ref, g_ref, wu_ref, wd_ref, t_ref, h_ref, act_ref, dy_ref, dyb_ref, sq_ref, acc = refs
        else:
            x_ref, g_ref, wu_ref, wd_ref, h_ref, act_ref, xo_ref, acc = refs
        xv = x_ref[...]
        hb = (xv * _rms(xv) * g_ref[...]).astype(BF16)
        h_ref[...] = hb
        for j in range(NDEV):
            a = jnp.square(jnp.maximum(_dot(hb, wu_ref[j]), 0.0)).astype(BF16)
            act_ref[:, j * FFB:(j + 1) * FFB] = a
            if j == 0:
                acc[...] = _dot(a, wd_ref[j])
            else:
                acc[...] += _dot(a, wd_ref[j])
        y = xv + acc[...]
        if last:
            diff = y - t_ref[...]
            dy = diff * (1.0 / D)
            dy_ref[...] = dy
            dyb_ref[...] = dy.astype(BF16)
            _first(sq_ref, _colsum(diff * diff))
        else:
            xo_ref[...] = y

    in_specs = [_rows(tm, D), _const((1, D)), _const((NDEV, D, FFB)), _const((NDEV, FFB, D))]
    args = [x, g, wu, wd]
    out_specs = [_rows(tm, D), _rows(tm, DFF)]
    out_shape = [_sds((S, D), BF16), _sds((S, DFF), BF16)]
    if last:
        in_specs.append(_rows(tm, D))
        args.append(target)
        out_specs += [_rows(tm, D), _rows(tm, D), _acc((1, D))]
        out_shape += [_sds((S, D), F32), _sds((S, D), BF16), _sds((1, D), F32)]
    else:
        out_specs.append(_rows(tm, D))
        out_shape.append(_sds((S, D), F32))
    return _carried_call(
        body, carry, name="mlp_fwd_loss" if last else "mlp_fwd", grid=(S // tm,),
        in_specs=in_specs, out_specs=out_specs, out_shape=out_shape,
        scratch_shapes=[pltpu.VMEM((tm, D), F32)],
        compiler_params=_cp(52), args=args)


def _mlp_bwd(dy, dyb, x, g, act, wu, wd, carry=None):
    S = x.shape[0]
    tm = min(256, S)

    def body(dy_ref, dyb_ref, x_ref, g_ref, act_ref, wu_ref, wd_ref, dup_ref, dx_ref, dxb_ref, dg_ref, acc):
        db = dyb_ref[...]
        for j in range(NDEV):
            dact = _dot_nt(db, wd_ref[j])
            a = act_ref[:, j * FFB:(j + 1) * FFB].astype(F32)
            dup = (dact * (2.0 * jnp.sqrt(a))).astype(BF16)
            dup_ref[:, j * FFB:(j + 1) * FFB] = dup
            if j == 0:
                acc[...] = _dot_nt(dup, wu_ref[j])
            else:
                acc[...] += _dot_nt(dup, wu_ref[j])
        xv = x_ref[...]
        dxn, dg = _rms_bwd(acc[...], xv, _rms(xv), g_ref[...])
        dx = dy_ref[...] + dxn
        dx_ref[...] = dx
        dxb_ref[...] = dx.astype(BF16)
        _first(dg_ref, dg)

    return _carried_call(
        body, carry, name="mlp_bwd", grid=(S // tm,),
        in_specs=[_rows(tm, D), _rows(tm, D), _rows(tm, D), _const((1, D)), _rows(tm, DFF),
                  _const((NDEV, D, FFB)), _const((NDEV, FFB, D))],
        out_specs=[_rows(tm, DFF), _rows(tm, D), _rows(tm, D), _acc((1, D))],
        out_shape=[_sds((S, DFF), BF16), _sds((S, D), F32), _sds((S, D), BF16), _sds((1, D), F32)],
        scratch_shapes=[pltpu.VMEM((tm, D), F32)],
        compiler_params=_cp(52), args=[dy, dyb, x, g, act, wu, wd])


def _wgrad(a, b, tm, tn, name):
    S, M = a.shape
    N = b.shape[1]
    tk = min(1024, S)
    nk = S // tk

    def body(a_ref, b_ref, o_ref, acc):
        k = pl.program_id(2)

        @pl.when(k == 0)
        def _():
            acc[...] = jnp.zeros_like(acc)

        acc[...] += _dot_tn(a_ref[...], b_ref[...])

        @pl.when(k == nk - 1)
        def _():
            o_ref[...] = acc[...].astype(BF16)

    return pl.pallas_call(
        body, name=name, grid=(M // tm, N // tn, nk),
        in_specs=[pl.BlockSpec((tk, tm), lambda i, j, k: (k, i)), pl.BlockSpec((tk, tn), lambda i, j, k: (k, j))],
        out_specs=pl.BlockSpec((tm, tn), lambda i, j, k: (i, j)),
        out_shape=_sds((M, N), BF16),
        scratch_shapes=[pltpu.VMEM((tm, tn), F32)],
        compiler_params=_cp(40, 3))(a, b)


def _bucket_table():
    q = np.arange(WIN)[:, None]
    k = np.arange(2 * WIN)[None, :]
    dist = q + WIN - k
    n = np.maximum(dist, 0)
    max_exact = NBKT // 2
    large = max_exact + (np.log(np.maximum(n, 1).astype(np.float32) / max_exact)
                         / math.log(WIN / max_exact) * (NBKT - max_exact)).astype(np.int32)
    large = np.minimum(large, NBKT - 1)
    bkt = np.where(n < max_exact, n, large).astype(np.int32)
    return np.where((dist >= 0) & (dist < WIN), bkt, -1).astype(np.int32)


def _seg_mats():
    e16 = np.zeros((D, 128), np.float32)
    e16[np.arange(D), np.arange(D) // HD] = 1.0
    e2 = np.zeros((KV, 128), np.float32)
    e2[np.arange(KV), np.arange(KV) // HD] = 1.0
    fold = np.zeros((D, 128), np.float32)
    fold[np.arange(D), np.arange(D) % HD] = 1.0
    fold2 = np.zeros((KV, 128), np.float32)
    fold2[np.arange(KV), np.arange(KV) % HD] = 1.0
    return [jnp.asarray(m, BF16) for m in (e16, e16.T, e2, e2.T, fold, fold2)]


def _head_rms(t, e, et):
    r = lax.rsqrt(_dot_hi(t * t, e) * (1.0 / HD) + EPS)
    return _dot_hi(r, et)


def _attn_qkv_fwd(x, g, w, b, qg, kg, e16, e16t, e2, e2t):
    S = x.shape[0]
    tm = min(512, S)

    def body(x_ref, g_ref, w_ref, b_ref, qg_ref, kg_ref, e16_ref, e16t_ref, e2_ref, e2t_ref,
             h_ref, raw_ref, qn_ref, kn_ref, v_ref):
        xv = x_ref[...]
        hb = (xv * _rms(xv) * g_ref[...]).astype(BF16)
        h_ref[...] = hb
        raw = _dot(hb, w_ref[...]) + b_ref[...]
        raw_ref[...] = raw
        q = raw[:, :D]
        k = raw[:, D:D + KV]
        qn = (q * _head_rms(q, e16_ref[...], e16t_ref[...]) * qg_ref[...] * 0.125).astype(BF16)
        for bb in range(tm // WIN):
            for p in range(NPAIR):
                r = (bb * NPAIR + p) * WIN
                qn_ref[r:r + WIN, :] = qn[bb * WIN:(bb + 1) * WIN, p * 128:(p + 1) * 128]
        kn_ref[...] = (k * _head_rms(k, e2_ref[...], e2t_ref[...]) * kg_ref[...]).astype(BF16)
        v_ref[...] = raw[:, D + KV:].astype(BF16)

    return pl.pallas_call(
        body, name="attn_qkv_fwd", grid=(S // tm,),
        in_specs=[_rows(tm, D), _const((1, D)), _const((D, QKV)), _const((1, QKV)), _const((1, D)), _const((1, KV)),
                  _const((D, 128)), _const((128, D)), _const((KV, 128)), _const((128, KV))],
        out_specs=[_rows(tm, D), _rows(tm, QKV), _rows(tm * NPAIR, 128), _rows(tm, KV), _rows(tm, KV)],
        out_shape=[_sds((S, D), BF16), _sds((S, QKV), F32), _sds((S * NPAIR, 128), BF16), _sds((S, KV), BF16),
                   _sds((S, KV), BF16)],
        compiler_params=_cp(48))(x, g, w, b, qg, kg, e16, e16t, e2, e2t)


def _build_bias(bkt_ref, rb_ref, sk_ref, bias_sc, sk_sc):
    bkt = bkt_ref[...]
    row = lax.broadcasted_iota(jnp.int32, (2 * WIN, WIN), 0)
    for h in range(NH):
        p, e = h // 2, h % 2
        g, pp = p // 4, p % 4

        def add(bk, acc, h=h):
            return acc + jnp.where(bkt == bk, rb_ref[bk, h], 0.0)

        bias = lax.fori_loop(0, NBKT, add, jnp.where(bkt < 0, NEG, 0.0).astype(F32))
        bias_sc[0, g, e, :, pp * WIN:(pp + 1) * WIN] = bias
        bias_sc[1, g, e, :, pp * WIN:(pp + 1) * WIN] = jnp.where(row < WIN, NEG, bias)
        sk_sc[g, e, :, pp * WIN:(pp + 1) * WIN] = jnp.zeros((1, WIN), F32) + sk_ref[0, h]


def _stacks(f, mlo):
    r = pltpu.roll(f, HD, 1)
    z = jnp.zeros_like(f)
    return ((jnp.where(mlo, f, z).astype(BF16), jnp.where(mlo, z, r).astype(BF16)),
            (jnp.where(mlo, r, z).astype(BF16), jnp.where(mlo, z, f).astype(BF16)))


def _unstack(d, mlo):
    z = jnp.zeros_like(d[0][0])
    return (jnp.where(mlo, d[0][0], z) + pltpu.roll(jnp.where(mlo, z, d[0][1]), HD, 1)
            + pltpu.roll(jnp.where(mlo, d[1][0], z), HD, 1) + jnp.where(mlo, z, d[1][1]))


def _softmax_sink(l, sk):
    m = jnp.maximum(jnp.max(l, axis=0, keepdims=True), sk)
    ex = jnp.exp(l - m)
    es = jnp.exp(sk - m)
    inv = 1.0 / (jnp.sum(ex, axis=0, keepdims=True) + es)
    return ex * inv, es * inv


GROWS = 4 * WIN


def _attn_fwd(qn, kp, vp, bktt, rel_bias, sinks, wo, bo, x):
    S = x.shape[0]
    tq = min(512, S)
    nblk = tq // WIN

    def body(q_ref, k_ref, v_ref, bkt_ref, rb_ref, sk_ref, wo_ref, bo_ref, x_ref, o_ref, xo_ref, bias_sc, sk_sc):
        i = pl.program_id(0)

        @pl.when(i == 0)
        def _():
            _build_bias(bkt_ref, rb_ref, sk_ref, bias_sc, sk_sc)

        mlo = lax.broadcasted_iota(jnp.int32, (2 * WIN, KV), 1) < HD

        def blk(bb, carry):
            r0 = pl.multiple_of(bb * WIN, WIN)
            g0 = pl.multiple_of(i * tq + bb * WIN, WIN)
            first = (g0 == 0).astype(jnp.int32)
            ks = _stacks(k_ref[pl.ds(g0, 2 * WIN), :].astype(F32), mlo)
            vs = _stacks(v_ref[pl.ds(g0, 2 * WIN), :].astype(F32), mlo)
            for g in range(NKV):
                qs = q_ref[pl.ds(pl.multiple_of(bb * (NPAIR * WIN) + g * GROWS, GROWS), GROWS), :]
                og = jnp.zeros((GROWS, 128), F32)
                for e in range(2):
                    l = _dot_nt(ks[g][e], qs) + bias_sc[first, g, e]
                    pr, _ = _softmax_sink(l, sk_sc[g, e])
                    og = og + _dot_tn(pr.astype(BF16), vs[g][e])
                for pp in range(4):
                    p = 4 * g + pp
                    o_ref[pl.ds(r0, WIN), p * 128:(p + 1) * 128] = og[pp * WIN:(pp + 1) * WIN].astype(BF16)
            return carry

        lax.fori_loop(0, nblk, blk, 0)
        xo_ref[...] = x_ref[...] + _dot(o_ref[...], wo_ref[...]) + bo_ref[...]

    smem = pl.BlockSpec(memory_space=pltpu.SMEM)
    return pl.pallas_call(
        body, name="attn_fwd", grid=(S // tq,),
        in_specs=[_rows(tq * NPAIR, 128), _const((S + WIN, KV)), _const((S + WIN, KV)), _const((2 * WIN, WIN)), smem, smem,
                  _const((D, D)), _const((1, D)), _rows(tq, D)],
        out_specs=[_rows(tq, D), _rows(tq, D)],
        out_shape=[_sds((S, D), BF16), _sds((S, D), F32)],
        scratch_shapes=[pltpu.VMEM((2, NKV, 2, 2 * WIN, GROWS), F32), pltpu.VMEM((NKV, 2, 1, GROWS), F32)],
        compiler_params=_cp(48))(qn, kp, vp, bktt, rel_bias, sinks, wo, bo, x)


def _attn_bwd(dx, dxb, qn, kp, vp, bktt, bktt4, rel_bias, sinks, wo, carry=None):
    S = dx.shape[0]
    tq = min(512, S)
    nblk = tq // WIN
    nsteps = S // tq

    def body(dx_ref, dxb_ref, q_ref, k_ref, v_ref, bkt_ref, bkt4_ref, rb_ref, sk_ref, wo_ref,
             dq_ref, dk_ref, dv_ref, dsk_ref, drb_ref, dbo_ref, bias_sc, sk_sc, dbias_sc, dsk_sc, do_sc):
        i = pl.program_id(0)

        @pl.when(i == 0)
        def _():
            _build_bias(bkt_ref, rb_ref, sk_ref, bias_sc, sk_sc)
            dbias_sc[...] = jnp.zeros_like(dbias_sc)
            dsk_sc[...] = jnp.zeros_like(dsk_sc)
            dk_ref[...] = jnp.zeros_like(dk_ref)
            dv_ref[...] = jnp.zeros_like(dv_ref)

        _first(dbo_ref, _colsum(dx_ref[...]))
        do = _dot_nt(dxb_ref[...], wo_ref[...]).astype(BF16)
        for bb in range(nblk):
            for p in range(NPAIR):
                r = (bb * NPAIR + p) * WIN
                do_sc[r:r + WIN, :] = do[bb * WIN:(bb + 1) * WIN, p * 128:(p + 1) * 128]

        mlo = lax.broadcasted_iota(jnp.int32, (2 * WIN, KV), 1) < HD

        def blk(bb, carry):
            g0 = pl.multiple_of(i * tq + bb * WIN, WIN)
            first = (g0 == 0).astype(jnp.int32)
            ks = _stacks(k_ref[pl.ds(g0, 2 * WIN), :].astype(F32), mlo)
            vs = _stacks(v_ref[pl.ds(g0, 2 * WIN), :].astype(F32), mlo)
            dks = [[None, None], [None, None]]
            dvs = [[None, None], [None, None]]
            for g in range(NKV):
                rows = pl.ds(pl.multiple_of(bb * (NPAIR * WIN) + g * GROWS, GROWS), GROWS)
                qs = q_ref[rows, :]
                dos = do_sc[rows, :]
                dqs = jnp.zeros((GROWS, 128), F32)
                for e in range(2):
                    l = _dot_nt(ks[g][e], qs) + bias_sc[first, g, e]
                    pr, ps = _softmax_sink(l, sk_sc[g, e])
                    dp = _dot_nt(vs[g][e], dos)
                    dr = jnp.sum(pr * dp, axis=0, keepdims=True)
                    dl = pr * (dp - dr)
                    dsk_sc[g, e] -= ps * dr
                    dbias_sc[g, e] += dl
                    dlb = dl.astype(BF16)
                    dqs = dqs + _dot_tn(dlb, ks[g][e])
                    dks[g][e] = _dot(dlb, qs)
                    dvs[g][e] = _dot(pr.astype(BF16), dos)
                dq_ref[rows, :] = dqs
            dk_ref[pl.ds(g0, 2 * WIN), :] += _unstack(dks, mlo)
            dv_ref[pl.ds(g0, 2 * WIN), :] += _unstack(dvs, mlo)
            return carry

        lax.fori_loop(0, nblk, blk, 0)

        @pl.when(i == nsteps - 1)
        def _():
            bkt4v = bkt4_ref[...]
            rid = lax.broadcasted_iota(jnp.int32, (NBKT, 128), 0)
            lid = lax.broadcasted_iota(jnp.int32, (NBKT, 128), 1)
            lid1 = lax.broadcasted_iota(jnp.int32, (1, 128), 1)
            dsk = jnp.zeros((1, 128), F32)
            for g in range(NKV):
                for e in range(2):
                    for pp in range(4):
                        h = 2 * (4 * g + pp) + e
                        t = jnp.sum(dsk_sc[g, e, :, pp * WIN:(pp + 1) * WIN], axis=1, keepdims=True)
                        dsk = dsk + jnp.where(lid1 == h, t, 0.0)
            dsk_ref[...] = dsk

            def per_bucket(bk, acc):
                mb = bkt4v == bk
                for g in range(NKV):
                    for e in range(2):
                        t = jnp.sum(jnp.where(mb, dbias_sc[g, e], 0.0), axis=0, keepdims=True)
                        for pp in range(4):
                            h = 2 * (4 * g + pp) + e
                            tt = jnp.sum(t[:, pp * WIN:(pp + 1) * WIN], axis=1, keepdims=True)
                            acc = acc + jnp.where((rid == bk) & (lid == h), tt, 0.0)
                return acc

            drb_ref[...] = lax.fori_loop(0, NBKT, per_bucket, jnp.zeros((NBKT, 128), F32))

    smem = pl.BlockSpec(memory_space=pltpu.SMEM)
    return _carried_call(
        body, carry, name="attn_bwd", grid=(nsteps,),
        in_specs=[_rows(tq, D), _rows(tq, D), _rows(tq * NPAIR, 128), _const((S + WIN, KV)), _const((S + WIN, KV)),
                  _const((2 * WIN, WIN)), _const((2 * WIN, GROWS)), smem, smem, _const((D, D))],
        out_specs=[_rows(tq * NPAIR, 128), _acc((S + WIN, KV)), _acc((S + WIN, KV)), _acc((1, 128)), _acc((NBKT, 128)),
                   _acc((1, D))],
        out_shape=[_sds((S * NPAIR, 128), F32), _sds((S + WIN, KV), F32), _sds((S + WIN, KV), F32), _sds((1, 128), F32),
                   _sds((NBKT, 128), F32), _sds((1, D), F32)],
        scratch_shapes=[pltpu.VMEM((2, NKV, 2, 2 * WIN, GROWS), F32), pltpu.VMEM((NKV, 2, 1, GROWS), F32),
                        pltpu.VMEM((NKV, 2, 2 * WIN, GROWS), F32), pltpu.VMEM((NKV, 2, 1, GROWS), F32),
                        pltpu.VMEM((tq * NPAIR, 128), BF16)],
        compiler_params=_cp(52), args=[dx, dxb, qn, kp, vp, bktt, bktt4, rel_bias, sinks, wo])


def _head_norm_bwd(dn, t, e, et, gt):
    r = _head_rms(t, e, et)
    th = t * r
    dth = dn * gt
    dt = r * (dth - th * _dot_hi(_dot_hi(dth * th, e) * (1.0 / HD), et))
    return dt, _colsum(dn * th)


def _attn_qkv_bwd(dqs, dk, dv, raw, x, dxo, g, w, qg, kg, e16, e16t, e2, e2t, fold, fold2):
    S = x.shape[0]
    tm = min(512, S)

    def body(dq_ref, dk_ref, dv_ref, raw_ref, x_ref, dxo_ref, g_ref, w_ref, qg_ref, kg_ref,
             e16_ref, e16t_ref, e2_ref, e2t_ref, fold_ref, fold2_ref,
             dqkv_ref, dx_ref, dxb_ref, db_ref, dg_ref, dqg_ref, dkg_ref, dq_sc):
        for bb in range(tm // WIN):
            for p in range(NPAIR):
                r = (bb * NPAIR + p) * WIN
                dq_sc[bb * WIN:(bb + 1) * WIN, p * 128:(p + 1) * 128] = dq_ref[r:r + WIN, :]
        dq, cq = _head_norm_bwd(dq_sc[...] * 0.125, raw_ref[:, :D], e16_ref[...], e16t_ref[...], qg_ref[...])
        dk_, ck = _head_norm_bwd(dk_ref[...], raw_ref[:, D:D + KV], e2_ref[...], e2t_ref[...], kg_ref[...])
        dv_ = dv_ref[...]
        _first(dqg_ref, _dot_hi(cq, fold_ref[...]))
        _first(dkg_ref, _dot_hi(ck, fold2_ref[...]))
        _first_cols(db_ref, [(0, _colsum(dq)), (D, _colsum(dk_)), (D + KV, _colsum(dv_))])
        dqb, dkb, dvb = dq.astype(BF16), dk_.astype(BF16), dv_.astype(BF16)
        dqkv_ref[:, :D] = dqb
        dqkv_ref[:, D:D + KV] = dkb
        dqkv_ref[:, D + KV:] = dvb
        dh = (_dot_nt(dqb, w_ref[:, :D]) + _dot_nt(dkb, w_ref[:, D:D + KV]) + _dot_nt(dvb, w_ref[:, D + KV:]))
        xv = x_ref[...]
        dxn, dg = _rms_bwd(dh, xv, _rms(xv), g_ref[...])
        dx = dxo_ref[...] + dxn
        dx_ref[...] = dx
        dxb_ref[...] = dx.astype(BF16)
        _first(dg_ref, dg)

    return pl.pallas_call(
        body, name="attn_qkv_bwd", grid=(S // tm,),
        in_specs=[_rows(tm * NPAIR, 128), _rows(tm, KV), _rows(tm, KV), _rows(tm, QKV), _rows(tm, D), _rows(tm, D),
                  _const((1, D)), _const((D, QKV)), _const((1, D)), _const((1, KV)),
                  _const((D, 128)), _const((128, D)), _const((KV, 128)), _const((128, KV)),
                  _const((D, 128)), _const((KV, 128))],
        out_specs=[_rows(tm, QKV), _rows(tm, D), _rows(tm, D), _acc((1, QKV)), _acc((1, D)), _acc((1, 128)), _acc((1, 128))],
        out_shape=[_sds((S, QKV), BF16), _sds((S, D), F32), _sds((S, D), BF16), _sds((1, QKV), F32), _sds((1, D), F32),
                   _sds((1, 128), F32), _sds((1, 128), F32)],
        scratch_shapes=[pltpu.VMEM((tm, D), F32)],
        compiler_params=_cp(48))(dqs, dk, dv, raw, x, dxo, g, w, qg, kg, e16, e16t, e2, e2t, fold, fold2)


def _conv_mid_bwd(dx, dxb, c, lng, lnb, w):
    S = dx.shape[0]
    tm = min(512, S)

    def body(dx_ref, dxb_ref, c_ref, lng_ref, lnb_ref, w_ref, dc_ref, dbo_ref, dlg_ref, dlb_ref, ddwb_ref):
        _first(dbo_ref, _colsum(dx_ref[...]))
        ds = _dot_nt(dxb_ref[...], w_ref[...])
        c = c_ref[...]
        xc = c - jnp.mean(c, axis=-1, keepdims=True)
        rstd = lax.rsqrt(jnp.mean(xc * xc, axis=-1, keepdims=True) + EPS)
        ch = xc * rstd
        y = ch * lng_ref[...] + lnb_ref[...]
        sg = jax.nn.sigmoid(y)
        dy = ds * (sg * (1.0 + y * (1.0 - sg)))
        _first(dlg_ref, _colsum(dy * ch))
        _first(dlb_ref, _colsum(dy))
        dch = dy * lng_ref[...]
        dc = rstd * (dch - jnp.mean(dch, axis=-1, keepdims=True) - ch * jnp.mean(dch * ch, axis=-1, keepdims=True))
        dc_ref[...] = dc
        _first(ddwb_ref, _colsum(dc))

    return pl.pallas_call(
        body, name="conv_mid_bwd", grid=(S // tm,),
        in_specs=[_rows(tm, D), _rows(tm, D), _rows(tm, D), _const((1, D)), _const((1, D)), _const((D, D))],
        out_specs=[_rows(tm, D), _acc((1, D)), _acc((1, D)), _acc((1, D)), _acc((1, D))],
        out_shape=[_sds((S, D), F32)] + [_sds((1, D), F32)] * 4,
        compiler_params=_cp(40))(dx, dxb, c, lng, lnb, w)


def _dwconv_bwd(dc, glu, dw, carry=None):
    S = dc.shape[0]
    tm = min(256, S)
    hb = tm // HALO
    nsteps = S // tm

    def body(dc_ref, nxt_ref, gl_ref, halo_ref, dw_ref, dgl_ref, ddw_ref, ext, dext, esh, dsh):
        i = pl.program_id(0)
        _fill_ext(ext, halo_ref, gl_ref, tm, i > 0)
        dext[0:tm, :] = dc_ref[...]
        dext[tm:tm + HALO, :] = jnp.where(i < nsteps - 1, nxt_ref[...], 0.0)
        _shift_copies(esh, ext, tm + 24)
        _shift_copies(dsh, dext, tm + 24)

        @pl.when(i == 0)
        def _():
            ddw_ref[...] = jnp.zeros_like(ddw_ref)

        for c0 in range(0, D, 512):
            for r0 in range(0, tm, 32):
                acc = jnp.zeros((32, 512), F32)
                for j in range(CONVW):
                    acc = acc + dw_ref[CONVW - 1 - j:CONVW - j, c0:c0 + 512] * _tap(dext, dsh, j, r0, c0)
                dgl_ref[r0:r0 + 32, c0:c0 + 512] = acc
            for k in range(CONVW):
                acc = jnp.zeros((32, 512), F32)
                for r0 in range(0, tm, 32):
                    acc = acc + dext[r0:r0 + 32, c0:c0 + 512] * _tap(ext, esh, 2 + k, r0, c0)
                ddw_ref[k:k + 1, c0:c0 + 512] += _colsum(acc)

    return _carried_call(
        body, carry, name="dwconv_bwd", grid=(nsteps,),
        in_specs=[_rows(tm, D), pl.BlockSpec((HALO, D), lambda i: (jnp.minimum((i + 1) * hb, S // HALO - 1), 0)),
                  _rows(tm, D), pl.BlockSpec((HALO, D), lambda i: (jnp.maximum(i * hb - 1, 0), 0)), _const((32, D))],
        out_specs=[_rows(tm, D), _acc((32, D))],
        out_shape=[_sds((S, D), F32), _sds((32, D), F32)],
        scratch_shapes=[pltpu.VMEM((tm + HALO, D), F32), pltpu.VMEM((tm + HALO, D), F32),
                        pltpu.VMEM((7, tm + 24, D), F32), pltpu.VMEM((7, tm + 24, D), F32)],
        compiler_params=_cp(48), args=[dc, dc, glu, glu, dw])


def _conv_in_bwd(dglu, u, x, dxo, g, w):
    S = x.shape[0]
    tm = min(512, S)

    def body(dgl_ref, u_ref, x_ref, dxo_ref, g_ref, w_ref, du_ref, dx_ref, db_ref, dg_ref):
        dgl = dgl_ref[...]
        a = u_ref[:, :D]
        sg = jax.nn.sigmoid(u_ref[:, D:])
        da = dgl * sg
        dgt = dgl * a * sg * (1.0 - sg)
        _first_cols(db_ref, [(0, _colsum(da)), (D, _colsum(dgt))])
        dab, dgb = da.astype(BF16), dgt.astype(BF16)
        du_ref[:, :D] = dab
        du_ref[:, D:] = dgb
        dh = _dot_nt(dab, w_ref[:, :D]) + _dot_nt(dgb, w_ref[:, D:])
        xv = x_ref[...]
        dxn, dg = _rms_bwd(dh, xv, _rms(xv), g_ref[...])
        dx_ref[...] = dxo_ref[...] + dxn
        _first(dg_ref, dg)

    return pl.pallas_call(
        body, name="conv_in_bwd", grid=(S // tm,),
        in_specs=[_rows(tm, D), _rows(tm, 2 * D), _rows(tm, D), _rows(tm, D), _const((1, D)), _const((D, 2 * D))],
        out_specs=[_rows(tm, 2 * D), _rows(tm, D), _acc((1, 2 * D)), _acc((1, D))],
        out_shape=[_sds((S, 2 * D), BF16), _sds((S, D), F32), _sds((1, 2 * D), F32), _sds((1, D), F32)],
        compiler_params=_cp(48))(dglu, u, x, dxo, g, w)


def _coords():
    return lax.axis_index("x"), lax.axis_index("y"), lax.axis_index("c")


def _split(refs, *counts):
    out, k = [], 0
    for n in counts:
        out.append(refs[k:k + n])
        k += n
    return out


def _carried_call(body, carry, *, name, grid, in_specs, out_specs, out_shape, scratch_shapes, compiler_params, args):
    if carry is None:
        return pl.pallas_call(body, name=name, grid=grid, in_specs=in_specs, out_specs=out_specs, out_shape=out_shape,
                              scratch_shapes=scratch_shapes, compiler_params=compiler_params)(*args)
    counts = (len(in_specs), len(carry.args), len(out_specs), len(carry.out_shape), len(scratch_shapes), len(carry.scratch))
    nsteps = grid[0]

    def wrapped(*refs):
        ins, cin, outs, cout, scr, cscr = _split(refs, *counts)
        i = pl.program_id(0)
        carry.top(i, nsteps, cin, cout, cscr)
        body(*ins, *outs, *scr)
        carry.bottom(i, nsteps, cin, cout, cscr)

    return pl.pallas_call(
        wrapped, name=name, grid=grid, in_specs=list(in_specs) + carry.in_specs,
        out_specs=list(out_specs) + carry.out_specs, out_shape=list(out_shape) + carry.out_shape,
        scratch_shapes=list(scratch_shapes) + carry.scratch, compiler_params=compiler_params)(*args, *carry.args)


class _Gather:
    def __init__(self, pack_bf, rows):
        self.r0, self.n = rows
        self.args = [pack_bf]
        self.in_specs = [pl.BlockSpec(memory_space=pl.ANY)]
        self.out_specs = [pl.BlockSpec(memory_space=pl.ANY)]
        self.out_shape = [_sds((NDEV, self.n, D), BF16)]
        self.scratch = [pltpu.SemaphoreType.DMA((7,)), pltpu.SemaphoreType.DMA((7,)), pltpu.SemaphoreType.DMA]

    def _copies(self, ins, outs, scr):
        src = ins[0].at[pl.ds(self.r0, self.n)]
        dst = outs[0]
        send, recv, lsem = scr
        x, y, c = _coords()
        me, sib = (x, y, c), (x, y, 1 - c)
        chips = [(1 - x, y), (x, 1 - y), (1 - x, 1 - y)]

        def cp(k, block, to, own=False):
            idx = 4 * block[0] + 2 * block[1] + block[2]
            return pltpu.make_async_remote_copy(
                src_ref=src if own else dst.at[idx], dst_ref=dst.at[idx], send_sem=send.at[k], recv_sem=recv.at[k],
                device_id=to, device_id_type=MESH)

        own = [cp(0, me, sib, True)] + [cp(1 + j, me, (*ch, c), True) for j, ch in enumerate(chips)]
        ici_in = [cp(1 + j, (*ch, c), me) for j, ch in enumerate(chips)]
        fwd = [cp(4 + j, (*ch, c), sib) for j, ch in enumerate(chips)]
        sib_in = [cp(0, sib, me)] + [cp(4 + j, (*ch, 1 - c), me) for j, ch in enumerate(chips)]
        local = pltpu.make_async_copy(src, dst.at[4 * x + 2 * y + c], lsem)
        return own, ici_in, fwd, sib_in, local

    def top(self, i, nsteps, ins, outs, scr):
        @pl.when(i == 0)
        def _():
            own, _, _, _, local = self._copies(ins, outs, scr)
            local.start()
            for cp in own:
                cp.start()

        @pl.when(i == nsteps // 2)
        def _():
            _, ici_in, fwd, _, _ = self._copies(ins, outs, scr)
            for a_, f_ in zip(ici_in, fwd):
                a_.wait_recv()
                f_.start()

    def bottom(self, i, nsteps, ins, outs, scr):
        @pl.when(i == nsteps - 1)
        def _():
            own, _, fwd, sib_in, local = self._copies(ins, outs, scr)
            for cp in sib_in:
                cp.wait_recv()
            for cp in own + fwd:
                cp.wait_send()
            local.wait()


class _Scatter:
    def __init__(self, src):
        self.args = [src]
        self.in_specs = [pl.BlockSpec(memory_space=pl.ANY)]
        self.out_specs = [pl.BlockSpec(memory_space=pl.ANY)]
        self.out_shape = [_sds(src.shape, BF16)]
        self.scratch = [pltpu.SemaphoreType.DMA((7,)), pltpu.SemaphoreType.DMA((7,)), pltpu.SemaphoreType.DMA]

    def _copies(self, ins, outs, scr):
        src, land = ins[0], outs[0]
        send, recv, lsem = scr
        x, y, c = _coords()
        me = 4 * x + 2 * y + c
        sends, recvs = [], []
        for k in range(1, NDEV):
            peer = (x ^ ((k >> 2) & 1), y ^ ((k >> 1) & 1), c ^ (k & 1))
            pidx = me ^ k
            sends.append(pltpu.make_async_remote_copy(
                src_ref=src.at[pidx], dst_ref=land.at[me], send_sem=send.at[k - 1], recv_sem=recv.at[k - 1],
                device_id=peer, device_id_type=MESH))
            recvs.append(pltpu.make_async_remote_copy(
                src_ref=src.at[pidx], dst_ref=land.at[pidx], send_sem=send.at[k - 1], recv_sem=recv.at[k - 1],
                device_id=peer, device_id_type=MESH))
        return sends, recvs, pltpu.make_async_copy(src.at[me], land.at[me], lsem)

    def top(self, i, nsteps, ins, outs, scr):
        @pl.when(i == 0)
        def _():
            sends, _, local = self._copies(ins, outs, scr)
            local.start()
            for cp in sends:
                cp.start()

    def bottom(self, i, nsteps, ins, outs, scr):
        @pl.when(i == nsteps - 1)
        def _():
            sends, recvs, local = self._copies(ins, outs, scr)
            for cp in recvs:
                cp.wait_recv()
            for cp in sends:
                cp.wait_send()
            local.wait()


def _gather_first(wpack, spack):
    R, RS_ = wpack.shape[0], spack.shape[0]
    r0, n = AG_FIRST

    def body(w_ref, s_ref, pb_ref, gw_ref, gs_ref, stage, send_sems, recv_sems, local_sems):
        x, y, c = _coords()
        me, sib = (x, y, c), (x, y, 1 - c)
        chips = [(1 - x, y), (x, 1 - y), (1 - x, 1 - y)]

        def cast(t, carry):
            r = pl.multiple_of(t * PACK_CH, PACK_CH)
            stage[pl.ds(r, PACK_CH), :] = w_ref[pl.ds(r, PACK_CH), :].astype(BF16)
            return carry

        lax.fori_loop(0, R // PACK_CH, cast, 0)

        def slot(px, py, pc):
            return 4 * px + 2 * py + pc

        def copies(k, block, to, own=False):
            idx = slot(*block)
            return [
                pltpu.make_async_remote_copy(
                    src_ref=stage.at[pl.ds(r0, n)] if own else gw_ref.at[idx], dst_ref=gw_ref.at[idx],
                    send_sem=send_sems.at[0, k], recv_sem=recv_sems.at[0, k], device_id=to, device_id_type=MESH),
                pltpu.make_async_remote_copy(
                    src_ref=s_ref if own else gs_ref.at[idx], dst_ref=gs_ref.at[idx],
                    send_sem=send_sems.at[1, k], recv_sem=recv_sems.at[1, k], device_id=to, device_id_type=MESH)]

        mine = pltpu.make_async_copy(stage.at[pl.ds(r0, n)], gw_ref.at[slot(*me)], local_sems.at[0])
        mine.start()
        whole = pltpu.make_async_copy(stage, pb_ref, local_sems.at[1])
        whole.start()
        gs_ref[slot(*me)] = s_ref[...]
        first = copies(0, me, sib, own=True)
        for j, chip in enumerate(chips):
            first += copies(1 + j, me, (*chip, c), own=True)
        for cp in first:
            cp.start()
        passed = []
        for j, chip in enumerate(chips):
            for cp in copies(1 + j, (*chip, c), me):
                cp.wait_recv()
            fwd = copies(4 + j, (*chip, c), sib)
            for cp in fwd:
                cp.start()
            passed += fwd
        for cp in copies(0, sib, me):
            cp.wait_recv()
        for j, chip in enumerate(chips):
            for cp in copies(4 + j, (*chip, 1 - c), me):
                cp.wait_recv()
        for cp in first + passed:
            cp.wait_send()
        mine.wait()
        whole.wait()

    vmem = pl.BlockSpec(memory_space=pltpu.VMEM)
    hbm = pl.BlockSpec(memory_space=pl.ANY)
    return pl.pallas_call(
        body, name="gather_first",
        in_specs=[vmem, vmem],
        out_specs=[hbm, hbm, vmem],
        out_shape=[_sds((R, D), BF16), _sds((NDEV, n, D), BF16), _sds((NDEV, RS_, 128), F32)],
        scratch_shapes=[pltpu.VMEM((R, D), BF16), pltpu.SemaphoreType.DMA((2, 7)), pltpu.SemaphoreType.DMA((2, 7)),
                        pltpu.SemaphoreType.DMA((2,))],
        compiler_params=pltpu.CompilerParams(vmem_limit_bytes=40 << 20))(wpack, spack)


def _scatter_last(gsrc, spack):
    n, RS_ = gsrc.shape[1], spack.shape[0]

    def body(g_ref, s_ref, land_ref, sred_ref, sland, send_sems, recv_sems, local_sem):
        x, y, c = _coords()
        me = 4 * x + 2 * y + c
        sends = []
        for k in range(1, NDEV):
            peer = (x ^ ((k >> 2) & 1), y ^ ((k >> 1) & 1), c ^ (k & 1))
            pidx = me ^ k
            cw = pltpu.make_async_remote_copy(
                src_ref=g_ref.at[pidx], dst_ref=land_ref.at[me], send_sem=send_sems.at[0, k - 1],
                recv_sem=recv_sems.at[0, k - 1], device_id=peer, device_id_type=MESH)
            cs = pltpu.make_async_remote_copy(
                src_ref=s_ref, dst_ref=sland.at[me], send_sem=send_sems.at[1, k - 1],
                recv_sem=recv_sems.at[1, k - 1], device_id=peer, device_id_type=MESH)
            cw.start()
            cs.start()
            sends += [cw, cs]
        mine = pltpu.make_async_copy(g_ref.at[me], land_ref.at[me], local_sem)
        mine.start()
        sland[me] = s_ref[...]
        for k in range(1, NDEV):
            peer = (x ^ ((k >> 2) & 1), y ^ ((k >> 1) & 1), c ^ (k & 1))
            pidx = me ^ k
            pltpu.make_async_remote_copy(
                src_ref=g_ref.at[pidx], dst_ref=land_ref.at[pidx], send_sem=send_sems.at[0, k - 1],
                recv_sem=recv_sems.at[0, k - 1], device_id=peer, device_id_type=MESH).wait_recv()
            pltpu.make_async_remote_copy(
                src_ref=s_ref, dst_ref=sland.at[pidx], send_sem=send_sems.at[1, k - 1],
                recv_sem=recv_sems.at[1, k - 1], device_id=peer, device_id_type=MESH).wait_recv()
        mine.wait()
        for cp in sends:
            cp.wait_send()
        acc = sland[0]
        for s_ in range(1, NDEV):
            acc = acc + sland[s_]
        sred_ref[...] = acc

    vmem = pl.BlockSpec(memory_space=pltpu.VMEM)
    hbm = pl.BlockSpec(memory_space=pl.ANY)
    return pl.pallas_call(
        body, name="scatter_last",
        in_specs=[hbm, vmem],
        out_specs=[hbm, vmem],
        out_shape=[_sds((NDEV, n, D), BF16), _sds((RS_, 128), F32)],
        scratch_shapes=[pltpu.VMEM((NDEV, RS_, 128), F32), pltpu.SemaphoreType.DMA((2, 7)),
                        pltpu.SemaphoreType.DMA((2, 7)), pltpu.SemaphoreType.DMA],
        compiler_params=pltpu.CompilerParams(vmem_limit_bytes=16 << 20))(gsrc, spack)


def _adam_math(w, g, m, v):
    nm = B1 * m + (1.0 - B1) * g
    nv = B2 * v + (1.0 - B2) * jnp.square(g)
    m_hat = nm / (1.0 - B1 ** STEP)
    v_hat = nv / (1.0 - B2 ** STEP)
    return -LR * (m_hat / (jnp.sqrt(v_hat) + AEPS) + WD * w), nm, nv


def _adamw(w, g, m, v, name):
    R, L = w.shape
    tr = next(c for c in (680, 512, 392, 256, 136, 64, 8) if R % c == 0)

    def body(w_ref, g_ref, m_ref, v_ref, d_ref, nm_ref, nv_ref):
        d_ref[...], nm_ref[...], nv_ref[...] = _adam_math(w_ref[...], g_ref[...], m_ref[...], v_ref[...])

    spec = pl.BlockSpec((tr, L), lambda i: (i, 0))
    return pl.pallas_call(
        body, name=name, grid=(R // tr,), in_specs=[spec] * 4, out_specs=[spec] * 3,
        out_shape=[_sds((R, L), F32)] * 3, compiler_params=_cp(48))(w, g, m, v)


def _reduce_adamw(land, w, m, v, name):
    n = land.shape[1]
    tr = next(c for c in (256, 192, 144, 128, 64, 32, 16) if n % c == 0)

    def body(l_ref, w_ref, m_ref, v_ref, g_ref, d_ref, nm_ref, nv_ref):
        g = l_ref[0].astype(F32)
        for s_ in range(1, NDEV):
            g = g + l_ref[s_].astype(F32)
        g_ref[...] = g
        d_ref[...], nm_ref[...], nv_ref[...] = _adam_math(w_ref[...], g, m_ref[...], v_ref[...])

    spec = pl.BlockSpec((tr, D), lambda i: (i, 0))
    return pl.pallas_call(
        body, name=name, grid=(n // tr,),
        in_specs=[pl.BlockSpec((NDEV, tr, D), lambda i: (0, i, 0))] + [spec] * 3, out_specs=[spec] * 4,
        out_shape=[_sds((n, D), F32)] * 4, compiler_params=_cp(48))(land, w, m, v)


BIG = ("conv_w_in", "conv_w_out", "w_qkv", "w_o", "w_up", "w_down")
SMALL = (("conv_norm_g", (1, D), False), ("conv_b_in", (1, 2 * D), False), ("conv_dw", (1, CONVW, D), True),
         ("conv_dw_b", (1, D), False), ("conv_ln_g", (1, D), False), ("conv_ln_b", (1, D), False),
         ("conv_b_out", (1, D), False), ("attn_norm_g", (1, D), True), ("b_qkv", (1, QKV), True),
         ("q_norm_g", (1, HD), False), ("k_norm_g", (1, HD), False), ("sinks", (1, NH), False),
         ("b_o", (1, D), True), ("rel_bias", (NBKT, NH), False), ("mlp_norm_g", (2, D), False))


def _lanes(a):
    flat = a.reshape(-1)
    n = flat.shape[0]
    rows = -(-n // 128)
    return jnp.pad(flat, (0, rows * 128 - n)).reshape(rows, 128)


def _pack_small(arrs):
    rows = jnp.concatenate([_lanes(a) for a in arrs], axis=0)
    pad = -rows.shape[0] % 8
    return jnp.pad(rows, ((0, pad), (0, 0)))


def _unpack_small(pack, shapes):
    out, r = [], 0
    for shp in shapes:
        n = int(np.prod(shp))
        rows = -(-n // 128)
        out.append(pack[r:r + rows].reshape(-1)[:n].reshape(shp))
        r += rows
    return out


def _pack_big(p):
    return jnp.concatenate([p["conv_w_in"].reshape(256, D), p["conv_w_out"].reshape(128, D),
                            p["w_up"][0].reshape(FFB, D), p["w_down"][0], p["w_qkv"].reshape(160, D),
                            p["w_o"].reshape(128, D), p["w_up"][1].reshape(FFB, D), p["w_down"][1]], axis=0)


def _unpack_big(pack):
    o = PACK_OFF
    part = lambda n, r: pack[o[n]:o[n] + r]
    return {"conv_w_in": part("conv_w_in", 256).reshape(1, D, 256), "conv_w_out": part("conv_w_out", 128).reshape(1, 128, D),
            "w_qkv": part("w_qkv", 160).reshape(1, D, 160), "w_o": part("w_o", 128).reshape(1, 128, D),
            "w_up": jnp.stack([part("up0", FFB).reshape(D, FFB), part("up1", FFB).reshape(D, FFB)]),
            "w_down": jnp.stack([part("dn0", FFB), part("dn1", FFB)])}


def _cols_to_blocks(a, n):
    M = a.shape[0]
    return a.reshape(M, NDEV, n).transpose(1, 0, 2).reshape(NDEV, M * n // D, D)


def kernel(x, conv_norm_g, conv_w_in, conv_b_in, conv_dw, conv_dw_b, conv_ln_g, conv_ln_b, conv_w_out, conv_b_out, attn_norm_g, w_qkv, b_qkv, q_norm_g, k_norm_g, sinks, w_o, b_o, rel_bias, mlp_norm_g, w_up, w_down, loss_target, m_conv_norm_g, m_conv_w_in, m_conv_b_in, m_conv_dw, m_conv_dw_b, m_conv_ln_g, m_conv_ln_b, m_conv_w_out, m_conv_b_out, m_attn_norm_g, m_w_qkv, m_b_qkv, m_q_norm_g, m_k_norm_g, m_sinks, m_w_o, m_b_o, m_rel_bias, m_mlp_norm_g, m_w_up, m_w_down, v_conv_norm_g, v_conv_w_in, v_conv_b_in, v_conv_dw, v_conv_dw_b, v_conv_ln_g, v_conv_ln_b, v_conv_w_out, v_conv_b_out, v_attn_norm_g, v_w_qkv, v_b_qkv, v_q_norm_g, v_k_norm_g, v_sinks, v_w_o, v_b_o, v_rel_bias, v_mlp_norm_g, v_w_up, v_w_down):
    names = [n for n, _, _ in SMALL] + list(BIG)
    loc = dict(locals())
    W = {n: loc[n] for n in names}
    M_ = {n: loc["m_" + n] for n in names}
    V_ = {n: loc["v_" + n] for n in names}
    me = 4 * lax.axis_index("x") + 2 * lax.axis_index("y") + lax.axis_index("c")
    xs = x[0]
    S = xs.shape[0]

    sharded = [n for n, _, sh in SMALL if sh]
    wpack = _pack_big(W)
    pack_bf, gw0, gs = _gather_first(wpack, _pack_small([W[n] for n in sharded]))
    w_in = gw0[:, 0:256].reshape(NDEV, D, 256).transpose(1, 0, 2).reshape(D, 2 * D)
    w_out = gw0[:, 256:384].reshape(D, D)
    shard_shapes = {"conv_dw": (CONVW, 128), "attn_norm_g": (1, 128), "b_qkv": (1, 160), "b_o": (1, 128)}
    parts = [_unpack_small(gs[j], [shard_shapes[n] for n in sharded]) for j in range(NDEV)]
    full = {n: jnp.concatenate([parts[j][i] for j in range(NDEV)], axis=-1) for i, n in enumerate(sharded)}
    dw32 = jnp.pad(full["conv_dw"], ((0, 1), (0, 0)))
    attn_g, bqkv, bo = full["attn_norm_g"], full["b_qkv"], full["b_o"]
    qg = jnp.tile(q_norm_g, (1, NH))
    kg = jnp.tile(k_norm_g, (1, NKV))
    e16, e16t, e2, e2t, fold, fold2 = _seg_mats()
    bkt = jnp.asarray(_bucket_table().T)
    bkt4 = jnp.asarray(np.tile(_bucket_table().T, (1, 4)))

    h0, u, glu = _conv_in_fwd(xs, conv_norm_g, w_in, conv_b_in)
    cc, sb, x1, gw1 = _conv_mid_fwd(glu, xs, dw32, conv_dw_b, conv_ln_g, conv_ln_b, w_out, conv_b_out,
                                    carry=_Gather(pack_bf, AG_MLP0))
    wu0, wd0 = gw1[:, 0:FFB].reshape(NDEV, D, FFB), gw1[:, FFB:2 * FFB]
    h1, act0, x2, gw2 = _mlp_fwd(x1, mlp_norm_g[0:1], wu0, wd0, carry=_Gather(pack_bf, AG_REST))
    wqkv = gw2[:, 0:160].reshape(NDEV, D, 160).transpose(1, 0, 2).reshape(D, QKV)
    wo = gw2[:, 160:288].reshape(D, D)
    wu1, wd1 = gw2[:, 288:288 + FFB].reshape(NDEV, D, FFB), gw2[:, 288 + FFB:288 + 2 * FFB]
    h2, raw, qn, kn, vv = _attn_qkv_fwd(x2, attn_g, wqkv, bqkv, qg, kg, e16, e16t, e2, e2t)
    kp = jnp.pad(kn, ((WIN, 0), (0, 0)))
    vp = jnp.pad(vv, ((WIN, 0), (0, 0)))
    ob, x3 = _attn_fwd(qn, kp, vp, bkt, rel_bias, sinks, wo, bo, x2)
    h3, act1, dx4, dx4b, sq = _mlp_fwd(x3, mlp_norm_g[1:2], wu1, wd1, target=loss_target[0])
    loss = lax.psum(jnp.sum(sq) * (0.5 / D), ("x", "y", "c"))

    dup1, dx3, dx3b, dg_mlp1 = _mlp_bwd(dx4, dx4b, x3, mlp_norm_g[1:2], act1, wu1, wd1)
    g_wdn1 = _wgrad(act1, dx4b, 512, D, "wgrad_down1")
    g_wup1 = _wgrad(h3, dup1, D, 512, "wgrad_up1")
    src_mlp1 = jnp.concatenate([_cols_to_blocks(g_wup1, FFB), g_wdn1.reshape(NDEV, FFB, D)], axis=1)
    dqs, dkp, dvp, d_sinks, d_rel, d_bo, land_mlp1 = _attn_bwd(dx3, dx3b, qn, kp, vp, bkt, bkt4, rel_bias, sinks, wo,
                                                             carry=_Scatter(src_mlp1))
    g_wo = _wgrad(ob, dx3b, D, 512, "wgrad_o")
    dqkv, dx2, dx2b, d_bqkv, d_attn_g, d_qg, d_kg = _attn_qkv_bwd(
        dqs, dkp[WIN:], dvp[WIN:], raw, x2, dx3, attn_g, wqkv, qg, kg, e16, e16t, e2, e2t, fold, fold2)
    g_wqkv = _wgrad(h2, dqkv, D, 640, "wgrad_qkv")
    src_attn = jnp.concatenate([_cols_to_blocks(g_wqkv, 160), g_wo.reshape(NDEV, 128, D)], axis=1)
    dup0, dx1, dx1b, dg_mlp0, land_attn = _mlp_bwd(dx2, dx2b, x1, mlp_norm_g[0:1], act0, wu0, wd0,
                                                   carry=_Scatter(src_attn))
    g_wdn0 = _wgrad(act0, dx2b, 512, D, "wgrad_down0")
    g_wup0 = _wgrad(h1, dup0, D, 512, "wgrad_up0")
    dc, d_bout, d_lng, d_lnb, d_dwb = _conv_mid_bwd(dx1, dx1b, cc, conv_ln_g, conv_ln_b, w_out)
    g_wout = _wgrad(sb, dx1b, D, 512, "wgrad_conv_out")
    src_mid = jnp.concatenate([g_wout.reshape(NDEV, 128, D), _cols_to_blocks(g_wup0, FFB),
                               g_wdn0.reshape(NDEV, FFB, D)], axis=1)
    dglu, d_dw, land_mid = _dwconv_bwd(dc, glu, dw32, carry=_Scatter(src_mid))
    du, grad_x, d_bin, d_cng = _conv_in_bwd(dglu, u, xs, dx1, conv_norm_g, w_in)
    g_win = _wgrad(h0, du, D, 512, "wgrad_conv_in")

    small_grads = {
        "conv_norm_g": d_cng, "conv_b_in": d_bin, "conv_dw": d_dw[:CONVW][None], "conv_dw_b": d_dwb,
        "conv_ln_g": d_lng, "conv_ln_b": d_lnb, "conv_b_out": d_bout, "attn_norm_g": d_attn_g, "b_qkv": d_bqkv,
        "q_norm_g": d_qg[:, :HD], "k_norm_g": d_kg[:, :HD], "sinks": d_sinks[:, :NH], "b_o": d_bo,
        "rel_bias": d_rel[:, :NH], "mlp_norm_g": jnp.concatenate([dg_mlp0, dg_mlp1], axis=0)}
    land_last, sred = _scatter_last(_cols_to_blocks(g_win, 256), _pack_small([small_grads[n] for n, _, _ in SMALL]))
    sg_full = _unpack_small(sred, [shp for _, shp, _ in SMALL])

    mpack, vpack = _pack_big(M_), _pack_big(V_)
    pieces = []
    for land, (r0, n), nm in ((land_last, RS_LAST, "adamw_conv_in"), (land_mid, RS_MID, "adamw_mid"),
                              (land_attn, RS_ATTN, "adamw_attn"), (land_mlp1, RS_MLP1, "adamw_mlp1")):
        pieces.append(_reduce_adamw(land, wpack[r0:r0 + n], mpack[r0:r0 + n], vpack[r0:r0 + n], nm))
    red, d_big, m_big, v_big = [jnp.concatenate([p[k] for p in pieces], axis=0) for k in range(4)]
    G = _unpack_big(red)
    for (n, shp, sh), gfull in zip(SMALL, sg_full):
        if sh:
            width = shp[-1] // NDEV
            G[n] = lax.dynamic_slice_in_dim(gfull, me * width, width, axis=gfull.ndim - 1)
        else:
            G[n] = gfull
    snames = [n for n, _, _ in SMALL]
    d_sm, m_sm, v_sm = _adamw(*[_pack_small([t[n] for n in snames]) for t in (W, G, M_, V_)], "adamw_small")
    sshapes = [W[n].shape for n in snames]
    outs = []
    for big, small in ((d_big, d_sm), (m_big, m_sm), (v_big, v_sm)):
        o = _unpack_big(big)
        o.update(dict(zip(snames, _unpack_small(small, sshapes))))
        outs.append(o)
    order = ["conv_norm_g", "conv_w_in", "conv_b_in", "conv_dw", "conv_dw_b", "conv_ln_g", "conv_ln_b", "conv_w_out",
             "conv_b_out", "attn_norm_g", "w_qkv", "b_qkv", "q_norm_g", "k_norm_g", "sinks", "w_o", "b_o", "rel_bias",
             "mlp_norm_g", "w_up", "w_down"]
    return (loss, grad_x[None], *[G[n] for n in order], *[outs[0][n] for n in order],
            *[outs[1][n] for n in order], *[outs[2][n] for n in order])
```

```python
import math

import numpy as np
import jax
import jax.numpy as jnp
from jax import lax
from jax.experimental import pallas as pl
from jax.experimental.pallas import tpu as pltpu

F32 = jnp.float32
BF16 = jnp.bfloat16

D = 1024
DFF = 4096
NH, NKV, HD = 16, 2, 64
NPAIR = NH // 2
KV = NKV * HD
QKV = D + 2 * KV
CONVW = 31
WIN = 128
NBKT = 32
EPS = 1e-6
NEG = -1e30
NDEV = 8
FFB = DFF // NDEV
HALO = 32

LR, B1, B2, AEPS, WD, STEP = 0.001, 0.9, 0.999, 1e-08, 0.01, 10

PACK = (("conv_w_in", 256), ("conv_w_out", 128), ("up0", 512), ("dn0", 512), ("w_qkv", 160), ("w_o", 128),
        ("up1", 512), ("dn1", 512))
PACK_OFF = {n: sum(r for _, r in PACK[:i]) for i, (n, _) in enumerate(PACK)}
PACK_ROWS = sum(r for _, r in PACK)
PACK_CH = 272
AG_FIRST = (0, 384)
AG_MLP0 = (384, 1024)
AG_ATTN = (1408, 288)
AG_MLP1 = (1696, 1024)
RS_MLP1 = (1696, 1024)
RS_ATTN = (1408, 288)
RS_MID = (256, 1152)
RS_LAST = (0, 256)
MESH = pl.DeviceIdType.MESH


def _cp(vmem_mb, n_axes=1):
    return pltpu.CompilerParams(dimension_semantics=("arbitrary",) * n_axes, vmem_limit_bytes=vmem_mb << 20)


def _const(shape):
    nd = len(shape)
    return pl.BlockSpec(shape, lambda *_: (0,) * nd, pipeline_mode=pl.Buffered(1))


def _acc(shape):
    nd = len(shape)
    return pl.BlockSpec(shape, lambda *_: (0,) * nd)


def _rows(tm, n):
    return pl.BlockSpec((tm, n), lambda i: (i, 0))


def _sds(shape, dtype):
    return jax.ShapeDtypeStruct(shape, dtype)


def _dot(a, b):
    return jnp.dot(a, b, preferred_element_type=F32)


def _dot_nt(a, b):
    return lax.dot_general(a, b, (((1,), (1,)), ((), ())), preferred_element_type=F32)


def _dot_tn(a, b):
    return lax.dot_general(a, b, (((0,), (0,)), ((), ())), preferred_element_type=F32)


def _dot_hi(x, e):
    x1 = x.astype(BF16)
    x2 = (x - x1.astype(F32)).astype(BF16)
    return _dot(x1, e) + _dot(x2, e)


def _rms(x):
    return lax.rsqrt(jnp.mean(x * x, axis=-1, keepdims=True) + EPS)


def _rms_bwd(dh, x, r, g):
    xh = x * r
    dxh = dh * g
    dx = r * (dxh - xh * jnp.mean(dxh * xh, axis=-1, keepdims=True))
    return dx, jnp.sum(dh * xh, axis=0, keepdims=True)


def _colsum(a):
    return jnp.sum(a, axis=0, keepdims=True)


def _first(ref, val):
    @pl.when(pl.program_id(0) == 0)
    def _():
        ref[...] = jnp.zeros_like(ref)
    ref[...] += val


def _first_cols(ref, vals):
    @pl.when(pl.program_id(0) == 0)
    def _():
        ref[...] = jnp.zeros_like(ref)
    for c0, val in vals:
        ref[:, c0:c0 + val.shape[1]] += val


def _conv_in_fwd(x, g, w, b):
    S = x.shape[0]
    tm = min(512, S)

    def body(x_ref, g_ref, w_ref, b_ref, h_ref, u_ref, glu_ref):
        xv = x_ref[...]
        hb = (xv * _rms(xv) * g_ref[...]).astype(BF16)
        h_ref[...] = hb
        u = _dot(hb, w_ref[...]) + b_ref[...]
        u_ref[...] = u
        glu_ref[...] = u[:, :D] * jax.nn.sigmoid(u[:, D:])

    return pl.pallas_call(
        body, name="conv_in_fwd", grid=(S // tm,),
        in_specs=[_rows(tm, D), _const((1, D)), _const((D, 2 * D)), _const((1, 2 * D))],
        out_specs=[_rows(tm, D), _rows(tm, 2 * D), _rows(tm, D)],
        out_shape=[_sds((S, D), BF16), _sds((S, 2 * D), F32), _sds((S, D), F32)],
        compiler_params=_cp(48))(x, g, w, b)


def _fill_ext(ext, halo_ref, cur_ref, tm, keep):
    ext[0:HALO, :] = jnp.where(keep, halo_ref[...], 0.0)
    ext[HALO:HALO + tm, :] = cur_ref[...]


def _shift_copies(dst, src, n):
    for s_ in range(1, 8):
        dst[s_ - 1, 0:n, :] = src[s_:s_ + n, :]


def _tap(src, sh, o, r0, c0):
    a, s_ = divmod(o, 8)
    ref = src if s_ == 0 else sh.at[s_ - 1]
    return ref[r0 + 8 * a:r0 + 8 * a + 32, c0:c0 + 512]


def _conv_mid_fwd(glu, x, dw, dwb, lng, lnb, w, b, carry=None):
    S = x.shape[0]
    tm = min(256, S)
    hb = tm // HALO

    def body(gl_ref, halo_ref, x_ref, dw_ref, dwb_ref, lng_ref, lnb_ref, w_ref, b_ref, c_ref, s_ref, xo_ref, ext, esh):
        i = pl.program_id(0)
        _fill_ext(ext, halo_ref, gl_ref, tm, i > 0)
        _shift_copies(esh, ext, tm + 24)
        for r0 in range(0, tm, 32):
            for c0 in range(0, D, 512):
                acc = jnp.zeros((32, 512), F32) + dwb_ref[:, c0:c0 + 512]
                for k in range(CONVW):
                    acc = acc + dw_ref[k:k + 1, c0:c0 + 512] * _tap(ext, esh, 2 + k, r0, c0)
                c_ref[r0:r0 + 32, c0:c0 + 512] = acc
        c = c_ref[...]
        mu = jnp.mean(c, axis=-1, keepdims=True)
        xc = c - mu
        y = xc * lax.rsqrt(jnp.mean(xc * xc, axis=-1, keepdims=True) + EPS) * lng_ref[...] + lnb_ref[...]
        sb = (y * jax.nn.sigmoid(y)).astype(BF16)
        s_ref[...] = sb
        xo_ref[...] = x_ref[...] + _dot(sb, w_ref[...]) + b_ref[...]

    return _carried_call(
        body, carry, name="conv_mid_fwd", grid=(S // tm,),
        in_specs=[_rows(tm, D), pl.BlockSpec((HALO, D), lambda i: (jnp.maximum(i * hb - 1, 0), 0)), _rows(tm, D),
                  _const((32, D)), _const((1, D)), _const((1, D)), _const((1, D)), _const((D, D)), _const((1, D))],
        out_specs=[_rows(tm, D), _rows(tm, D), _rows(tm, D)],
        out_shape=[_sds((S, D), F32), _sds((S, D), BF16), _sds((S, D), F32)],
        scratch_shapes=[pltpu.VMEM((tm + HALO, D), F32), pltpu.VMEM((7, tm + 24, D), F32)],
        compiler_params=_cp(48), args=[glu, glu, x, dw, dwb, lng, lnb, w, b])


def _mlp_fwd(x, g, wu, wd, target=None, carry=None):
    S = x.shape[0]
    tm = min(512, S)
    last = target is not None

    def body(*refs):
        if last:
            x_ref, g_ref, wu_ref, wd_ref, t_ref, h_ref, act_ref, dy_ref, dyb_ref, sq_ref, acc = refs
        else:
            x_ref, g_ref, wu_ref, wd_ref, h_ref, act_ref, xo_ref, acc = refs
        xv = x_ref[...]
        hb = (xv * _rms(xv) * g_ref[...]).astype(BF16)
        h_ref[...] = hb
        for j in range(NDEV):
            a = jnp.square(jnp.maximum(_dot(hb, wu_ref[j]), 0.0)).astype(BF16)
            act_ref[:, j * FFB:(j + 1) * FFB] = a
            if j == 0:
                acc[...] = _dot(a, wd_ref[j])
            else:
                acc[...] += _dot(a, wd_ref[j])
        y = xv + acc[...]
        if last:
            diff = y - t_ref[...]
            dy = diff * (1.0 / D)
            dy_ref[...] = dy
            dyb_ref[...] = dy.astype(BF16)
            _first(sq_ref, _colsum(diff * diff))
        else:
            xo_ref[...] = y

    in_specs = [_rows(tm, D), _const((1, D)), _const((NDEV, D, FFB)), _const((NDEV, FFB, D))]
    args = [x, g, wu, wd]
    out_specs = [_rows(tm, D), _rows(tm, DFF)]
    out_shape = [_sds((S, D), BF16), _sds((S, DFF), BF16)]
    if last:
        in_specs.append(_rows(tm, D))
        args.append(target)
        out_specs += [_rows(tm, D), _rows(tm, D), _acc((1, D))]
        out_shape += [_sds((S, D), F32), _sds((S, D), BF16), _sds((1, D), F32)]
    else:
        out_specs.append(_rows(tm, D))
        out_shape.append(_sds((S, D), F32))
    return _carried_call(
        body, carry, name="mlp_fwd_loss" if last else "mlp_fwd", grid=(S // tm,),
        in_specs=in_specs, out_specs=out_specs, out_shape=out_shape,
        scratch_shapes=[pltpu.VMEM((tm, D), F32)],
        compiler_params=_cp(52), args=args)


def _mlp_bwd(dy, dyb, x, g, act, wu, wd, carry=None):
    S = x.shape[0]
    tm = min(256, S)

    def body(dy_ref, dyb_ref, x_ref, g_ref, act_ref, wu_ref, wd_ref, dup_ref, dx_ref, dxb_ref, dg_ref, acc):
        db = dyb_ref[...]
        for j in range(NDEV):
            dact = _dot_nt(db, wd_ref[j])
            a = act_ref[:, j * FFB:(j + 1) * FFB].astype(F32)
            dup = (dact * (2.0 * jnp.sqrt(a))).astype(BF16)
            dup_ref[:, j * FFB:(j + 1) * FFB] = dup
            if j == 0:
                acc[...] = _dot_nt(dup, wu_ref[j])
            else:
                acc[...] += _dot_nt(dup, wu_ref[j])
        xv = x_ref[...]
        dxn, dg = _rms_bwd(acc[...], xv, _rms(xv), g_ref[...])
        dx = dy_ref[...] + dxn
        dx_ref[...] = dx
        dxb_ref[...] = dx.astype(BF16)
        _first(dg_ref, dg)

    return _carried_call(
        body, carry, name="mlp_bwd", grid=(S // tm,),
        in_specs=[_rows(tm, D), _rows(tm, D), _rows(tm, D), _const((1, D)), _rows(tm, DFF),
                  _const((NDEV, D, FFB)), _const((NDEV, FFB, D))],
        out_specs=[_rows(tm, DFF), _rows(tm, D), _rows(tm, D), _acc((1, D))],
        out_shape=[_sds((S, DFF), BF16), _sds((S, D), F32), _sds((S, D), BF16), _sds((1, D), F32)],
        scratch_shapes=[pltpu.VMEM((tm, D), F32)],
        compiler_params=_cp(52), args=[dy, dyb, x, g, act, wu, wd])


def _wgrad(a, b, tm, tn, name):
    S, M = a.shape
    N = b.shape[1]
    tk = min(4096, S)
    nk = S // tk

    def body(a_ref, b_ref, o_ref, acc):
        k = pl.program_id(2)

        @pl.when(k == 0)
        def _():
            acc[...] = jnp.zeros_like(acc)

        acc[...] += _dot_tn(a_ref[...], b_ref[...])

        @pl.when(k == nk - 1)
        def _():
            o_ref[...] = acc[...].astype(BF16)

    return pl.pallas_call(
        body, name=name, grid=(M // tm, N // tn, nk),
        in_specs=[pl.BlockSpec((tk, tm), lambda i, j, k: (k, i)), pl.BlockSpec((tk, tn), lambda i, j, k: (k, j))],
        out_specs=pl.BlockSpec((tm, tn), lambda i, j, k: (i, j)),
        out_shape=_sds((M, N), BF16),
        scratch_shapes=[pltpu.VMEM((tm, tn), F32)],
        compiler_params=_cp(48, 3))(a, b)


def _bucket_table():
    q = np.arange(WIN)[:, None]
    k = np.arange(2 * WIN)[None, :]
    dist = q + WIN - k
    n = np.maximum(dist, 0)
    max_exact = NBKT // 2
    large = max_exact + (np.log(np.maximum(n, 1).astype(np.float32) / max_exact)
                         / math.log(WIN / max_exact) * (NBKT - max_exact)).astype(np.int32)
    large = np.minimum(large, NBKT - 1)
    bkt = np.where(n < max_exact, n, large).astype(np.int32)
    return np.where((dist >= 0) & (dist < WIN), bkt, -1).astype(np.int32)


def _seg_mats():
    e16 = np.zeros((D, 128), np.float32)
    e16[np.arange(D), np.arange(D) // HD] = 1.0
    e2 = np.zeros((KV, 128), np.float32)
    e2[np.arange(KV), np.arange(KV) // HD] = 1.0
    fold = np.zeros((D, 128), np.float32)
    fold[np.arange(D), np.arange(D) % HD] = 1.0
    fold2 = np.zeros((KV, 128), np.float32)
    fold2[np.arange(KV), np.arange(KV) % HD] = 1.0
    return [jnp.asarray(m, BF16) for m in (e16, e16.T, e2, e2.T, fold, fold2)]


def _head_rms(t, e, et):
    r = lax.rsqrt(_dot_hi(t * t, e) * (1.0 / HD) + EPS)
    return _dot_hi(r, et)


def _attn_qkv_fwd(x, g, w, b, qg, kg, e16, e16t, e2, e2t):
    S = x.shape[0]
    tm = min(512, S)

    def body(x_ref, g_ref, w_ref, b_ref, qg_ref, kg_ref, e16_ref, e16t_ref, e2_ref, e2t_ref,
             h_ref, raw_ref, qn_ref, kn_ref, v_ref):
        xv = x_ref[...]
        hb = (xv * _rms(xv) * g_ref[...]).astype(BF16)
        h_ref[...] = hb
        raw = _dot(hb, w_ref[...]) + b_ref[...]
        raw_ref[...] = raw
        q = raw[:, :D]
        k = raw[:, D:D + KV]
        qn = (q * _head_rms(q, e16_ref[...], e16t_ref[...]) * qg_ref[...] * 0.125).astype(BF16)
        for bb in range(tm // WIN):
            for p in range(NPAIR):
                r = (bb * NPAIR + p) * WIN
                qn_ref[r:r + WIN, :] = qn[bb * WIN:(bb + 1) * WIN, p * 128:(p + 1) * 128]
        kn_ref[...] = (k * _head_rms(k, e2_ref[...], e2t_ref[...]) * kg_ref[...]).astype(BF16)
        v_ref[...] = raw[:, D + KV:].astype(BF16)

    return pl.pallas_call(
        body, name="attn_qkv_fwd", grid=(S // tm,),
        in_specs=[_rows(tm, D), _const((1, D)), _const((D, QKV)), _const((1, QKV)), _const((1, D)), _const((1, KV)),
                  _const((D, 128)), _const((128, D)), _const((KV, 128)), _const((128, KV))],
        out_specs=[_rows(tm, D), _rows(tm, QKV), _rows(tm * NPAIR, 128), _rows(tm, KV), _rows(tm, KV)],
        out_shape=[_sds((S, D), BF16), _sds((S, QKV), F32), _sds((S * NPAIR, 128), BF16), _sds((S, KV), BF16),
                   _sds((S, KV), BF16)],
        compiler_params=_cp(48))(x, g, w, b, qg, kg, e16, e16t, e2, e2t)


def _build_bias(bkt_ref, rb_ref, sk_ref, bias_sc, sk_sc):
    bkt = bkt_ref[...]
    row = lax.broadcasted_iota(jnp.int32, (2 * WIN, WIN), 0)
    for h in range(NH):
        p, e = h // 2, h % 2
        g, pp = p // 4, p % 4

        def add(bk, acc, h=h):
            return acc + jnp.where(bkt == bk, rb_ref[bk, h], 0.0)

        bias = lax.fori_loop(0, NBKT, add, jnp.where(bkt < 0, NEG, 0.0).astype(F32))
        bias_sc[0, g, e, :, pp * WIN:(pp + 1) * WIN] = bias
        bias_sc[1, g, e, :, pp * WIN:(pp + 1) * WIN] = jnp.where(row < WIN, NEG, bias)
        sk_sc[g, e, :, pp * WIN:(pp + 1) * WIN] = jnp.zeros((1, WIN), F32) + sk_ref[0, h]


def _stacks(f, mlo):
    r = pltpu.roll(f, HD, 1)
    z = jnp.zeros_like(f)
    return ((jnp.where(mlo, f, z).astype(BF16), jnp.where(mlo, z, r).astype(BF16)),
            (jnp.where(mlo, r, z).astype(BF16), jnp.where(mlo, z, f).astype(BF16)))


def _unstack(d, mlo):
    z = jnp.zeros_like(d[0][0])
    return (jnp.where(mlo, d[0][0], z) + pltpu.roll(jnp.where(mlo, z, d[0][1]), HD, 1)
            + pltpu.roll(jnp.where(mlo, d[1][0], z), HD, 1) + jnp.where(mlo, z, d[1][1]))


def _softmax_sink(l, sk):
    m = jnp.maximum(jnp.max(l, axis=0, keepdims=True), sk)
    ex = jnp.exp(l - m)
    es = jnp.exp(sk - m)
    inv = 1.0 / (jnp.sum(ex, axis=0, keepdims=True) + es)
    return ex * inv, es * inv


GROWS = 4 * WIN


def _attn_fwd(qn, kp, vp, bktt, rel_bias, sinks, wo, bo, x, carry=None):
    S = x.shape[0]
    tq = min(512, S)
    nblk = tq // WIN

    def body(q_ref, k_ref, v_ref, bkt_ref, rb_ref, sk_ref, wo_ref, bo_ref, x_ref, o_ref, xo_ref, bias_sc, sk_sc):
        i = pl.program_id(0)

        @pl.when(i == 0)
        def _():
            _build_bias(bkt_ref, rb_ref, sk_ref, bias_sc, sk_sc)

        mlo = lax.broadcasted_iota(jnp.int32, (2 * WIN, KV), 1) < HD

        def blk(bb, carry):
            r0 = pl.multiple_of(bb * WIN, WIN)
            g0 = pl.multiple_of(i * tq + bb * WIN, WIN)
            first = (g0 == 0).astype(jnp.int32)
            ks = _stacks(k_ref[pl.ds(g0, 2 * WIN), :].astype(F32), mlo)
            vs = _stacks(v_ref[pl.ds(g0, 2 * WIN), :].astype(F32), mlo)
            for g in range(NKV):
                qs = q_ref[pl.ds(pl.multiple_of(bb * (NPAIR * WIN) + g * GROWS, GROWS), GROWS), :]
                og = jnp.zeros((GROWS, 128), F32)
                for e in range(2):
                    l = _dot_nt(ks[g][e], qs) + bias_sc[first, g, e]
                    pr, _ = _softmax_sink(l, sk_sc[g, e])
                    og = og + _dot_tn(pr.astype(BF16), vs[g][e])
                for pp in range(4):
                    p = 4 * g + pp
                    o_ref[pl.ds(r0, WIN), p * 128:(p + 1) * 128] = og[pp * WIN:(pp + 1) * WIN].astype(BF16)
            return carry

        lax.fori_loop(0, nblk, blk, 0)
        xo_ref[...] = x_ref[...] + _dot(o_ref[...], wo_ref[...]) + bo_ref[...]

    smem = pl.BlockSpec(memory_space=pltpu.SMEM)
    return _carried_call(
        body, carry, name="attn_fwd", grid=(S // tq,),
        in_specs=[_rows(tq * NPAIR, 128), _const((S + WIN, KV)), _const((S + WIN, KV)), _const((2 * WIN, WIN)), smem, smem,
                  _const((D, D)), _const((1, D)), _rows(tq, D)],
        out_specs=[_rows(tq, D), _rows(tq, D)],
        out_shape=[_sds((S, D), BF16), _sds((S, D), F32)],
        scratch_shapes=[pltpu.VMEM((2, NKV, 2, 2 * WIN, GROWS), F32), pltpu.VMEM((NKV, 2, 1, GROWS), F32)],
        compiler_params=_cp(48), args=[qn, kp, vp, bktt, rel_bias, sinks, wo, bo, x])


def _attn_bwd(dx, dxb, qn, kp, vp, bktt, bktt4, rel_bias, sinks, wo, carry=None):
    S = dx.shape[0]
    tq = min(512, S)
    nblk = tq // WIN
    nsteps = S // tq

    def body(dx_ref, dxb_ref, q_ref, k_ref, v_ref, bkt_ref, bkt4_ref, rb_ref, sk_ref, wo_ref,
             dq_ref, dk_ref, dv_ref, dsk_ref, drb_ref, dbo_ref, bias_sc, sk_sc, dbias_sc, dsk_sc, do_sc):
        i = pl.program_id(0)

        @pl.when(i == 0)
        def _():
            _build_bias(bkt_ref, rb_ref, sk_ref, bias_sc, sk_sc)
            dbias_sc[...] = jnp.zeros_like(dbias_sc)
            dsk_sc[...] = jnp.zeros_like(dsk_sc)
            dk_ref[...] = jnp.zeros_like(dk_ref)
            dv_ref[...] = jnp.zeros_like(dv_ref)

        _first(dbo_ref, _colsum(dx_ref[...]))
        do = _dot_nt(dxb_ref[...], wo_ref[...]).astype(BF16)
        for bb in range(nblk):
            for p in range(NPAIR):
                r = (bb * NPAIR + p) * WIN
                do_sc[r:r + WIN, :] = do[bb * WIN:(bb + 1) * WIN, p * 128:(p + 1) * 128]

        mlo = lax.broadcasted_iota(jnp.int32, (2 * WIN, KV), 1) < HD

        def blk(bb, carry):
            g0 = pl.multiple_of(i * tq + bb * WIN, WIN)
            first = (g0 == 0).astype(jnp.int32)
            ks = _stacks(k_ref[pl.ds(g0, 2 * WIN), :].astype(F32), mlo)
            vs = _stacks(v_ref[pl.ds(g0, 2 * WIN), :].astype(F32), mlo)
            dks = [[None, None], [None, None]]
            dvs = [[None, None], [None, None]]
            for g in range(NKV):
                rows = pl.ds(pl.multiple_of(bb * (NPAIR * WIN) + g * GROWS, GROWS), GROWS)
                qs = q_ref[rows, :]
                dos = do_sc[rows, :]
                dqs = jnp.zeros((GROWS, 128), F32)
                for e in range(2):
                    l = _dot_nt(ks[g][e], qs) + bias_sc[first, g, e]
                    pr, ps = _softmax_sink(l, sk_sc[g, e])
                    dp = _dot_nt(vs[g][e], dos)
                    dr = jnp.sum(pr * dp, axis=0, keepdims=True)
                    dl = pr * (dp - dr)
                    dsk_sc[g, e] -= ps * dr
                    dbias_sc[g, e] += dl
                    dlb = dl.astype(BF16)
                    dqs = dqs + _dot_tn(dlb, ks[g][e])
                    dks[g][e] = _dot(dlb, qs)
                    dvs[g][e] = _dot(pr.astype(BF16), dos)
                dq_ref[rows, :] = dqs
            dk_ref[pl.ds(g0, 2 * WIN), :] += _unstack(dks, mlo)
            dv_ref[pl.ds(g0, 2 * WIN), :] += _unstack(dvs, mlo)
            return carry

        lax.fori_loop(0, nblk, blk, 0)

        @pl.when(i == nsteps - 1)
        def _():
            bkt4v = bkt4_ref[...]
            rid = lax.broadcasted_iota(jnp.int32, (NBKT, 128), 0)
            lid = lax.broadcasted_iota(jnp.int32, (NBKT, 128), 1)
            lid1 = lax.broadcasted_iota(jnp.int32, (1, 128), 1)
            dsk = jnp.zeros((1, 128), F32)
            for g in range(NKV):
                for e in range(2):
                    for pp in range(4):
                        h = 2 * (4 * g + pp) + e
                        t = jnp.sum(dsk_sc[g, e, :, pp * WIN:(pp + 1) * WIN], axis=1, keepdims=True)
                        dsk = dsk + jnp.where(lid1 == h, t, 0.0)
            dsk_ref[...] = dsk

            def per_bucket(bk, acc):
                mb = bkt4v == bk
                for g in range(NKV):
                    for e in range(2):
                        t = jnp.sum(jnp.where(mb, dbias_sc[g, e], 0.0), axis=0, keepdims=True)
                        for pp in range(4):
                            h = 2 * (4 * g + pp) + e
                            tt = jnp.sum(t[:, pp * WIN:(pp + 1) * WIN], axis=1, keepdims=True)
                            acc = acc + jnp.where((rid == bk) & (lid == h), tt, 0.0)
                return acc

            drb_ref[...] = lax.fori_loop(0, NBKT, per_bucket, jnp.zeros((NBKT, 128), F32))

    smem = pl.BlockSpec(memory_space=pltpu.SMEM)
    return _carried_call(
        body, carry, name="attn_bwd", grid=(nsteps,),
        in_specs=[_rows(tq, D), _rows(tq, D), _rows(tq * NPAIR, 128), _const((S + WIN, KV)), _const((S + WIN, KV)),
                  _const((2 * WIN, WIN)), _const((2 * WIN, GROWS)), smem, smem, _const((D, D))],
        out_specs=[_rows(tq * NPAIR, 128), _acc((S + WIN, KV)), _acc((S + WIN, KV)), _acc((1, 128)), _acc((NBKT, 128)),
                   _acc((1, D))],
        out_shape=[_sds((S * NPAIR, 128), F32), _sds((S + WIN, KV), F32), _sds((S + WIN, KV), F32), _sds((1, 128), F32),
                   _sds((NBKT, 128), F32), _sds((1, D), F32)],
        scratch_shapes=[pltpu.VMEM((2, NKV, 2, 2 * WIN, GROWS), F32), pltpu.VMEM((NKV, 2, 1, GROWS), F32),
                        pltpu.VMEM((NKV, 2, 2 * WIN, GROWS), F32), pltpu.VMEM((NKV, 2, 1, GROWS), F32),
                        pltpu.VMEM((tq * NPAIR, 128), BF16)],
        compiler_params=_cp(52), args=[dx, dxb, qn, kp, vp, bktt, bktt4, rel_bias, sinks, wo])


def _head_norm_bwd(dn, t, e, et, gt):
    r = _head_rms(t, e, et)
    th = t * r
    dth = dn * gt
    dt = r * (dth - th * _dot_hi(_dot_hi(dth * th, e) * (1.0 / HD), et))
    return dt, _colsum(dn * th)


def _attn_qkv_bwd(dqs, dk, dv, raw, x, dxo, g, w, qg, kg, e16, e16t, e2, e2t, fold, fold2):
    S = x.shape[0]
    tm = min(512, S)

    def body(dq_ref, dk_ref, dv_ref, raw_ref, x_ref, dxo_ref, g_ref, w_ref, qg_ref, kg_ref,
             e16_ref, e16t_ref, e2_ref, e2t_ref, fold_ref, fold2_ref,
             dqkv_ref, dx_ref, dxb_ref, db_ref, dg_ref, dqg_ref, dkg_ref, dq_sc):
        for bb in range(tm // WIN):
            for p in range(NPAIR):
                r = (bb * NPAIR + p) * WIN
                dq_sc[bb * WIN:(bb + 1) * WIN, p * 128:(p + 1) * 128] = dq_ref[r:r + WIN, :]
        dq, cq = _head_norm_bwd(dq_sc[...] * 0.125, raw_ref[:, :D], e16_ref[...], e16t_ref[...], qg_ref[...])
        dk_, ck = _head_norm_bwd(dk_ref[...], raw_ref[:, D:D + KV], e2_ref[...], e2t_ref[...], kg_ref[...])
        dv_ = dv_ref[...]
        _first(dqg_ref, _dot_hi(cq, fold_ref[...]))
        _first(dkg_ref, _dot_hi(ck, fold2_ref[...]))
        _first_cols(db_ref, [(0, _colsum(dq)), (D, _colsum(dk_)), (D + KV, _colsum(dv_))])
        dqb, dkb, dvb = dq.astype(BF16), dk_.astype(BF16), dv_.astype(BF16)
        dqkv_ref[:, :D] = dqb
        dqkv_ref[:, D:D + KV] = dkb
        dqkv_ref[:, D + KV:] = dvb
        dh = (_dot_nt(dqb, w_ref[:, :D]) + _dot_nt(dkb, w_ref[:, D:D + KV]) + _dot_nt(dvb, w_ref[:, D + KV:]))
        xv = x_ref[...]
        dxn, dg = _rms_bwd(dh, xv, _rms(xv), g_ref[...])
        dx = dxo_ref[...] + dxn
        dx_ref[...] = dx
        dxb_ref[...] = dx.astype(BF16)
        _first(dg_ref, dg)

    return pl.pallas_call(
        body, name="attn_qkv_bwd", grid=(S // tm,),
        in_specs=[_rows(tm * NPAIR, 128), _rows(tm, KV), _rows(tm, KV), _rows(tm, QKV), _rows(tm, D), _rows(tm, D),
                  _const((1, D)), _const((D, QKV)), _const((1, D)), _const((1, KV)),
                  _const((D, 128)), _const((128, D)), _const((KV, 128)), _const((128, KV)),
                  _const((D, 128)), _const((KV, 128))],
        out_specs=[_rows(tm, QKV), _rows(tm, D), _rows(tm, D), _acc((1, QKV)), _acc((1, D)), _acc((1, 128)), _acc((1, 128))],
        out_shape=[_sds((S, QKV), BF16), _sds((S, D), F32), _sds((S, D), BF16), _sds((1, QKV), F32), _sds((1, D), F32),
                   _sds((1, 128), F32), _sds((1, 128), F32)],
        scratch_shapes=[pltpu.VMEM((tm, D), F32)],
        compiler_params=_cp(48))(dqs, dk, dv, raw, x, dxo, g, w, qg, kg, e16, e16t, e2, e2t, fold, fold2)


def _conv_mid_bwd(dx, dxb, c, lng, lnb, w):
    S = dx.shape[0]
    tm = min(512, S)

    def body(dx_ref, dxb_ref, c_ref, lng_ref, lnb_ref, w_ref, dc_ref, dbo_ref, dlg_ref, dlb_ref, ddwb_ref):
        _first(dbo_ref, _colsum(dx_ref[...]))
        ds = _dot_nt(dxb_ref[...], w_ref[...])
        c = c_ref[...]
        xc = c - jnp.mean(c, axis=-1, keepdims=True)
        rstd = lax.rsqrt(jnp.mean(xc * xc, axis=-1, keepdims=True) + EPS)
        ch = xc * rstd
        y = ch * lng_ref[...] + lnb_ref[...]
        sg = jax.nn.sigmoid(y)
        dy = ds * (sg * (1.0 + y * (1.0 - sg)))
        _first(dlg_ref, _colsum(dy * ch))
        _first(dlb_ref, _colsum(dy))
        dch = dy * lng_ref[...]
        dc = rstd * (dch - jnp.mean(dch, axis=-1, keepdims=True) - ch * jnp.mean(dch * ch, axis=-1, keepdims=True))
        dc_ref[...] = dc
        _first(ddwb_ref, _colsum(dc))

    return pl.pallas_call(
        body, name="conv_mid_bwd", grid=(S // tm,),
        in_specs=[_rows(tm, D), _rows(tm, D), _rows(tm, D), _const((1, D)), _const((1, D)), _const((D, D))],
        out_specs=[_rows(tm, D), _acc((1, D)), _acc((1, D)), _acc((1, D)), _acc((1, D))],
        out_shape=[_sds((S, D), F32)] + [_sds((1, D), F32)] * 4,
        compiler_params=_cp(40))(dx, dxb, c, lng, lnb, w)


def _dwconv_bwd(dc, glu, dw, carry=None):
    S = dc.shape[0]
    tm = min(256, S)
    hb = tm // HALO
    nsteps = S // tm

    def body(dc_ref, nxt_ref, gl_ref, halo_ref, dw_ref, dgl_ref, ddw_ref, ext, dext, esh, dsh):
        i = pl.program_id(0)
        _fill_ext(ext, halo_ref, gl_ref, tm, i > 0)
        dext[0:tm, :] = dc_ref[...]
        dext[tm:tm + HALO, :] = jnp.where(i < nsteps - 1, nxt_ref[...], 0.0)
        _shift_copies(esh, ext, tm + 24)
        _shift_copies(dsh, dext, tm + 24)

        @pl.when(i == 0)
        def _():
            ddw_ref[...] = jnp.zeros_like(ddw_ref)

        for c0 in range(0, D, 512):
            for r0 in range(0, tm, 32):
                acc = jnp.zeros((32, 512), F32)
                for j in range(CONVW):
                    acc = acc + dw_ref[CONVW - 1 - j:CONVW - j, c0:c0 + 512] * _tap(dext, dsh, j, r0, c0)
                dgl_ref[r0:r0 + 32, c0:c0 + 512] = acc
            for k in range(CONVW):
                acc = jnp.zeros((32, 512), F32)
                for r0 in range(0, tm, 32):
                    acc = acc + dext[r0:r0 + 32, c0:c0 + 512] * _tap(ext, esh, 2 + k, r0, c0)
                ddw_ref[k:k + 1, c0:c0 + 512] += _colsum(acc)

    return _carried_call(
        body, carry, name="dwconv_bwd", grid=(nsteps,),
        in_specs=[_rows(tm, D), pl.BlockSpec((HALO, D), lambda i: (jnp.minimum((i + 1) * hb, S // HALO - 1), 0)),
                  _rows(tm, D), pl.BlockSpec((HALO, D), lambda i: (jnp.maximum(i * hb - 1, 0), 0)), _const((32, D))],
        out_specs=[_rows(tm, D), _acc((32, D))],
        out_shape=[_sds((S, D), F32), _sds((32, D), F32)],
        scratch_shapes=[pltpu.VMEM((tm + HALO, D), F32), pltpu.VMEM((tm + HALO, D), F32),
                        pltpu.VMEM((7, tm + 24, D), F32), pltpu.VMEM((7, tm + 24, D), F32)],
        compiler_params=_cp(48), args=[dc, dc, glu, glu, dw])


def _conv_in_bwd(dglu, u, x, dxo, g, w):
    S = x.shape[0]
    tm = min(512, S)

    def body(dgl_ref, u_ref, x_ref, dxo_ref, g_ref, w_ref, du_ref, dx_ref, db_ref, dg_ref):
        dgl = dgl_ref[...]
        a = u_ref[:, :D]
        sg = jax.nn.sigmoid(u_ref[:, D:])
        da = dgl * sg
        dgt = dgl * a * sg * (1.0 - sg)
        _first_cols(db_ref, [(0, _colsum(da)), (D, _colsum(dgt))])
        dab, dgb = da.astype(BF16), dgt.astype(BF16)
        du_ref[:, :D] = dab
        du_ref[:, D:] = dgb
        dh = _dot_nt(dab, w_ref[:, :D]) + _dot_nt(dgb, w_ref[:, D:])
        xv = x_ref[...]
        dxn, dg = _rms_bwd(dh, xv, _rms(xv), g_ref[...])
        dx_ref[...] = dxo_ref[...] + dxn
        _first(dg_ref, dg)

    return pl.pallas_call(
        body, name="conv_in_bwd", grid=(S // tm,),
        in_specs=[_rows(tm, D), _rows(tm, 2 * D), _rows(tm, D), _rows(tm, D), _const((1, D)), _const((D, 2 * D))],
        out_specs=[_rows(tm, 2 * D), _rows(tm, D), _acc((1, 2 * D)), _acc((1, D))],
        out_shape=[_sds((S, 2 * D), BF16), _sds((S, D), F32), _sds((1, 2 * D), F32), _sds((1, D), F32)],
        compiler_params=_cp(48))(dglu, u, x, dxo, g, w)


def _coords():
    return lax.axis_index("x"), lax.axis_index("y"), lax.axis_index("c")


def _split(refs, *counts):
    out, k = [], 0
    for n in counts:
        out.append(refs[k:k + n])
        k += n
    return out


def _carried_call(body, carry, *, name, grid, in_specs, out_specs, out_shape, scratch_shapes, compiler_params, args):
    if carry is None:
        return pl.pallas_call(body, name=name, grid=grid, in_specs=in_specs, out_specs=out_specs, out_shape=out_shape,
                              scratch_shapes=scratch_shapes, compiler_params=compiler_params)(*args)
    counts = (len(in_specs), len(carry.args), len(out_specs), len(carry.out_shape), len(scratch_shapes), len(carry.scratch))
    nsteps = grid[0]

    def wrapped(*refs):
        ins, cin, outs, cout, scr, cscr = _split(refs, *counts)
        i = pl.program_id(0)
        carry.top(i, nsteps, cin, cout, cscr)
        body(*ins, *outs, *scr)
        carry.bottom(i, nsteps, cin, cout, cscr)

    return pl.pallas_call(
        wrapped, name=name, grid=grid, in_specs=list(in_specs) + carry.in_specs,
        out_specs=list(out_specs) + carry.out_specs, out_shape=list(out_shape) + carry.out_shape,
        scratch_shapes=list(scratch_shapes) + carry.scratch, compiler_params=compiler_params)(*args, *carry.args)


class _Gather:
    def __init__(self, pack_bf, rows):
        self.r0, self.n = rows
        self.args = [pack_bf]
        self.in_specs = [pl.BlockSpec(memory_space=pl.ANY)]
        self.out_specs = [pl.BlockSpec(memory_space=pl.ANY)]
        self.out_shape = [_sds((NDEV, self.n, D), BF16)]
        self.scratch = [pltpu.SemaphoreType.DMA((7,)), pltpu.SemaphoreType.DMA((7,)), pltpu.SemaphoreType.DMA]

    def _copies(self, ins, outs, scr):
        src = ins[0].at[pl.ds(self.r0, self.n)]
        dst = outs[0]
        send, recv, lsem = scr
        x, y, c = _coords()
        me, sib = (x, y, c), (x, y, 1 - c)
        chips = [(1 - x, y), (x, 1 - y), (1 - x, 1 - y)]

        def cp(k, block, to, own=False):
            idx = 4 * block[0] + 2 * block[1] + block[2]
            return pltpu.make_async_remote_copy(
                src_ref=src if own else dst.at[idx], dst_ref=dst.at[idx], send_sem=send.at[k], recv_sem=recv.at[k],
                device_id=to, device_id_type=MESH)

        own = [cp(0, me, sib, True)] + [cp(1 + j, me, (*ch, c), True) for j, ch in enumerate(chips)]
        ici_in = [cp(1 + j, (*ch, c), me) for j, ch in enumerate(chips)]
        fwd = [cp(4 + j, (*ch, c), sib) for j, ch in enumerate(chips)]
        sib_in = [cp(0, sib, me)] + [cp(4 + j, (*ch, 1 - c), me) for j, ch in enumerate(chips)]
        local = pltpu.make_async_copy(src, dst.at[4 * x + 2 * y + c], lsem)
        return own, ici_in, fwd, sib_in, local

    def top(self, i, nsteps, ins, outs, scr):
        @pl.when(i == 0)
        def _():
            own, _, _, _, local = self._copies(ins, outs, scr)
            local.start()
            for cp in own:
                cp.start()

        @pl.when(i == nsteps // 2)
        def _():
            _, ici_in, fwd, _, _ = self._copies(ins, outs, scr)
            for a_, f_ in zip(ici_in, fwd):
                a_.wait_recv()
                f_.start()

    def bottom(self, i, nsteps, ins, outs, scr):
        @pl.when(i == nsteps - 1)
        def _():
            own, _, fwd, sib_in, local = self._copies(ins, outs, scr)
            for cp in sib_in:
                cp.wait_recv()
            for cp in own + fwd:
                cp.wait_send()
            local.wait()


class _Scatter:
    def __init__(self, src):
        self.args = [src]
        self.in_specs = [pl.BlockSpec(memory_space=pl.ANY)]
        self.out_specs = [pl.BlockSpec(memory_space=pl.ANY)]
        self.out_shape = [_sds(src.shape, BF16)]
        self.scratch = [pltpu.SemaphoreType.DMA((7,)), pltpu.SemaphoreType.DMA((7,)), pltpu.SemaphoreType.DMA]

    def _copies(self, ins, outs, scr):
        src, land = ins[0], outs[0]
        send, recv, lsem = scr
        x, y, c = _coords()
        me = 4 * x + 2 * y + c
        sends, recvs = [], []
        for k in range(1, NDEV):
            peer = (x ^ ((k >> 2) & 1), y ^ ((k >> 1) & 1), c ^ (k & 1))
            pidx = me ^ k
            sends.append(pltpu.make_async_remote_copy(
                src_ref=src.at[pidx], dst_ref=land.at[me], send_sem=send.at[k - 1], recv_sem=recv.at[k - 1],
                device_id=peer, device_id_type=MESH))
            recvs.append(pltpu.make_async_remote_copy(
                src_ref=src.at[pidx], dst_ref=land.at[pidx], send_sem=send.at[k - 1], recv_sem=recv.at[k - 1],
                device_id=peer, device_id_type=MESH))
        return sends, recvs, pltpu.make_async_copy(src.at[me], land.at[me], lsem)

    def top(self, i, nsteps, ins, outs, scr):
        @pl.when(i == 0)
        def _():
            sends, _, local = self._copies(ins, outs, scr)
            local.start()
            for cp in sends:
                cp.start()

    def bottom(self, i, nsteps, ins, outs, scr):
        @pl.when(i == nsteps - 1)
        def _():
            sends, recvs, local = self._copies(ins, outs, scr)
            for cp in recvs:
                cp.wait_recv()
            for cp in sends:
                cp.wait_send()
            local.wait()


def _gather_first(wpack, spack):
    R, RS_ = wpack.shape[0], spack.shape[0]
    r0, n = AG_FIRST

    def body(w_ref, s_ref, pb_ref, gw_ref, gs_ref, stage, send_sems, recv_sems, local_sems):
        x, y, c = _coords()
        me, sib = (x, y, c), (x, y, 1 - c)
        chips = [(1 - x, y), (x, 1 - y), (1 - x, 1 - y)]

        def cast(t, carry):
            r = pl.multiple_of(t * PACK_CH, PACK_CH)
            stage[pl.ds(r, PACK_CH), :] = w_ref[pl.ds(r, PACK_CH), :].astype(BF16)
            return carry

        lax.fori_loop(0, R // PACK_CH, cast, 0)

        def slot(px, py, pc):
            return 4 * px + 2 * py + pc

        def copies(k, block, to, own=False):
            idx = slot(*block)
            return [
                pltpu.make_async_remote_copy(
                    src_ref=stage.at[pl.ds(r0, n)] if own else gw_ref.at[idx], dst_ref=gw_ref.at[idx],
                    send_sem=send_sems.at[0, k], recv_sem=recv_sems.at[0, k], device_id=to, device_id_type=MESH),
                pltpu.make_async_remote_copy(
                    src_ref=s_ref if own else gs_ref.at[idx], dst_ref=gs_ref.at[idx],
                    send_sem=send_sems.at[1, k], recv_sem=recv_sems.at[1, k], device_id=to, device_id_type=MESH)]

        mine = pltpu.make_async_copy(stage.at[pl.ds(r0, n)], gw_ref.at[slot(*me)], local_sems.at[0])
        mine.start()
        whole = pltpu.make_async_copy(stage, pb_ref, local_sems.at[1])
        whole.start()
        gs_ref[slot(*me)] = s_ref[...]
        first = copies(0, me, sib, own=True)
        for j, chip in enumerate(chips):
            first += copies(1 + j, me, (*chip, c), own=True)
        for cp in first:
            cp.start()
        passed = []
        for j, chip in enumerate(chips):
            for cp in copies(1 + j, (*chip, c), me):
                cp.wait_recv()
            fwd = copies(4 + j, (*chip, c), sib)
            for cp in fwd:
                cp.start()
            passed += fwd
        for cp in copies(0, sib, me):
            cp.wait_recv()
        for j, chip in enumerate(chips):
            for cp in copies(4 + j, (*chip, 1 - c), me):
                cp.wait_recv()
        for cp in first + passed:
            cp.wait_send()
        mine.wait()
        whole.wait()

    vmem = pl.BlockSpec(memory_space=pltpu.VMEM)
    hbm = pl.BlockSpec(memory_space=pl.ANY)
    return pl.pallas_call(
        body, name="gather_first",
        in_specs=[vmem, vmem],
        out_specs=[hbm, hbm, vmem],
        out_shape=[_sds((R, D), BF16), _sds((NDEV, n, D), BF16), _sds((NDEV, RS_, 128), F32)],
        scratch_shapes=[pltpu.VMEM((R, D), BF16), pltpu.SemaphoreType.DMA((2, 7)), pltpu.SemaphoreType.DMA((2, 7)),
                        pltpu.SemaphoreType.DMA((2,))],
        compiler_params=pltpu.CompilerParams(vmem_limit_bytes=40 << 20))(wpack, spack)


def _scatter_last(gsrc, spack):
    n, RS_ = gsrc.shape[1], spack.shape[0]

    def body(g_ref, s_ref, land_ref, sred_ref, sland, send_sems, recv_sems, local_sem):
        x, y, c = _coords()
        me = 4 * x + 2 * y + c
        sends = []
        for k in range(1, NDEV):
            peer = (x ^ ((k >> 2) & 1), y ^ ((k >> 1) & 1), c ^ (k & 1))
            pidx = me ^ k
            cw = pltpu.make_async_remote_copy(
                src_ref=g_ref.at[pidx], dst_ref=land_ref.at[me], send_sem=send_sems.at[0, k - 1],
                recv_sem=recv_sems.at[0, k - 1], device_id=peer, device_id_type=MESH)
            cs = pltpu.make_async_remote_copy(
                src_ref=s_ref, dst_ref=sland.at[me], send_sem=send_sems.at[1, k - 1],
                recv_sem=recv_sems.at[1, k - 1], device_id=peer, device_id_type=MESH)
            cw.start()
            cs.start()
            sends += [cw, cs]
        mine = pltpu.make_async_copy(g_ref.at[me], land_ref.at[me], local_sem)
        mine.start()
        sland[me] = s_ref[...]
        for k in range(1, NDEV):
            peer = (x ^ ((k >> 2) & 1), y ^ ((k >> 1) & 1), c ^ (k & 1))
            pidx = me ^ k
            pltpu.make_async_remote_copy(
                src_ref=g_ref.at[pidx], dst_ref=land_ref.at[pidx], send_sem=send_sems.at[0, k - 1],
                recv_sem=recv_sems.at[0, k - 1], device_id=peer, device_id_type=MESH).wait_recv()
            pltpu.make_async_remote_copy(
                src_ref=s_ref, dst_ref=sland.at[pidx], send_sem=send_sems.at[1, k - 1],
                recv_sem=recv_sems.at[1, k - 1], device_id=peer, device_id_type=MESH).wait_recv()
        mine.wait()
        for cp in sends:
            cp.wait_send()
        acc = sland[0]
        for s_ in range(1, NDEV):
            acc = acc + sland[s_]
        sred_ref[...] = acc

    vmem = pl.BlockSpec(memory_space=pltpu.VMEM)
    hbm = pl.BlockSpec(memory_space=pl.ANY)
    return pl.pallas_call(
        body, name="scatter_last",
        in_specs=[hbm, vmem],
        out_specs=[hbm, vmem],
        out_shape=[_sds((NDEV, n, D), BF16), _sds((RS_, 128), F32)],
        scratch_shapes=[pltpu.VMEM((NDEV, RS_, 128), F32), pltpu.SemaphoreType.DMA((2, 7)),
                        pltpu.SemaphoreType.DMA((2, 7)), pltpu.SemaphoreType.DMA],
        compiler_params=pltpu.CompilerParams(vmem_limit_bytes=16 << 20))(gsrc, spack)


def _adam_math(w, g, m, v):
    nm = B1 * m + (1.0 - B1) * g
    nv = B2 * v + (1.0 - B2) * jnp.square(g)
    m_hat = nm / (1.0 - B1 ** STEP)
    v_hat = nv / (1.0 - B2 ** STEP)
    return -LR * (m_hat / (jnp.sqrt(v_hat) + AEPS) + WD * w), nm, nv


def _adamw(w, g, m, v, name):
    R, L = w.shape
    tr = next(c for c in (680, 512, 392, 256, 136, 64, 8) if R % c == 0)

    def body(w_ref, g_ref, m_ref, v_ref, d_ref, nm_ref, nv_ref):
        d_ref[...], nm_ref[...], nv_ref[...] = _adam_math(w_ref[...], g_ref[...], m_ref[...], v_ref[...])

    spec = pl.BlockSpec((tr, L), lambda i: (i, 0))
    return pl.pallas_call(
        body, name=name, grid=(R // tr,), in_specs=[spec] * 4, out_specs=[spec] * 3,
        out_shape=[_sds((R, L), F32)] * 3, compiler_params=_cp(48))(w, g, m, v)


def _reduce_adamw(land, w, m, v, name):
    n = land.shape[1]
    tr = next(c for c in (256, 192, 144, 128, 64, 32, 16) if n % c == 0)

    def body(l_ref, w_ref, m_ref, v_ref, g_ref, d_ref, nm_ref, nv_ref):
        g = l_ref[0].astype(F32)
        for s_ in range(1, NDEV):
            g = g + l_ref[s_].astype(F32)
        g_ref[...] = g
        d_ref[...], nm_ref[...], nv_ref[...] = _adam_math(w_ref[...], g, m_ref[...], v_ref[...])

    spec = pl.BlockSpec((tr, D), lambda i: (i, 0))
    return pl.pallas_call(
        body, name=name, grid=(n // tr,),
        in_specs=[pl.BlockSpec((NDEV, tr, D), lambda i: (0, i, 0))] + [spec] * 3, out_specs=[spec] * 4,
        out_shape=[_sds((n, D), F32)] * 4, compiler_params=_cp(48))(land, w, m, v)


BIG = ("conv_w_in", "conv_w_out", "w_qkv", "w_o", "w_up", "w_down")
SMALL = (("conv_norm_g", (1, D), False), ("conv_b_in", (1, 2 * D), False), ("conv_dw", (1, CONVW, D), True),
         ("conv_dw_b", (1, D), False), ("conv_ln_g", (1, D), False), ("conv_ln_b", (1, D), False),
         ("conv_b_out", (1, D), False), ("attn_norm_g", (1, D), True), ("b_qkv", (1, QKV), True),
         ("q_norm_g", (1, HD), False), ("k_norm_g", (1, HD), False), ("sinks", (1, NH), False),
         ("b_o", (1, D), True), ("rel_bias", (NBKT, NH), False), ("mlp_norm_g", (2, D), False))


def _lanes(a):
    flat = a.reshape(-1)
    n = flat.shape[0]
    rows = -(-n // 128)
    return jnp.pad(flat, (0, rows * 128 - n)).reshape(rows, 128)


def _pack_small(arrs):
    rows = jnp.concatenate([_lanes(a) for a in arrs], axis=0)
    pad = -rows.shape[0] % 8
    return jnp.pad(rows, ((0, pad), (0, 0)))


def _unpack_small(pack, shapes):
    out, r = [], 0
    for shp in shapes:
        n = int(np.prod(shp))
        rows = -(-n // 128)
        out.append(pack[r:r + rows].reshape(-1)[:n].reshape(shp))
        r += rows
    return out


def _pack_big(p):
    return jnp.concatenate([p["conv_w_in"].reshape(256, D), p["conv_w_out"].reshape(128, D),
                            p["w_up"][0].reshape(FFB, D), p["w_down"][0], p["w_qkv"].reshape(160, D),
                            p["w_o"].reshape(128, D), p["w_up"][1].reshape(FFB, D), p["w_down"][1]], axis=0)


def _unpack_big(pack):
    o = PACK_OFF
    part = lambda n, r: pack[o[n]:o[n] + r]
    return {"conv_w_in": part("conv_w_in", 256).reshape(1, D, 256), "conv_w_out": part("conv_w_out", 128).reshape(1, 128, D),
            "w_qkv": part("w_qkv", 160).reshape(1, D, 160), "w_o": part("w_o", 128).reshape(1, 128, D),
            "w_up": jnp.stack([part("up0", FFB).reshape(D, FFB), part("up1", FFB).reshape(D, FFB)]),
            "w_down": jnp.stack([part("dn0", FFB), part("dn1", FFB)])}


def _cols_to_blocks(a, n):
    M = a.shape[0]
    return a.reshape(M, NDEV, n).transpose(1, 0, 2).reshape(NDEV, M * n // D, D)


def kernel(x, conv_norm_g, conv_w_in, conv_b_in, conv_dw, conv_dw_b, conv_ln_g, conv_ln_b, conv_w_out, conv_b_out, attn_norm_g, w_qkv, b_qkv, q_norm_g, k_norm_g, sinks, w_o, b_o, rel_bias, mlp_norm_g, w_up, w_down, loss_target, m_conv_norm_g, m_conv_w_in, m_conv_b_in, m_conv_dw, m_conv_dw_b, m_conv_ln_g, m_conv_ln_b, m_conv_w_out, m_conv_b_out, m_attn_norm_g, m_w_qkv, m_b_qkv, m_q_norm_g, m_k_norm_g, m_sinks, m_w_o, m_b_o, m_rel_bias, m_mlp_norm_g, m_w_up, m_w_down, v_conv_norm_g, v_conv_w_in, v_conv_b_in, v_conv_dw, v_conv_dw_b, v_conv_ln_g, v_conv_ln_b, v_conv_w_out, v_conv_b_out, v_attn_norm_g, v_w_qkv, v_b_qkv, v_q_norm_g, v_k_norm_g, v_sinks, v_w_o, v_b_o, v_rel_bias, v_mlp_norm_g, v_w_up, v_w_down):
    names = [n for n, _, _ in SMALL] + list(BIG)
    loc = dict(locals())
    W = {n: loc[n] for n in names}
    M_ = {n: loc["m_" + n] for n in names}
    V_ = {n: loc["v_" + n] for n in names}
    me = 4 * lax.axis_index("x") + 2 * lax.axis_index("y") + lax.axis_index("c")
    xs = x[0]
    S = xs.shape[0]

    sharded = [n for n, _, sh in SMALL if sh]
    wpack = _pack_big(W)
    pack_bf, gw0, gs = _gather_first(wpack, _pack_small([W[n] for n in sharded]))
    w_in = gw0[:, 0:256].reshape(NDEV, D, 256).transpose(1, 0, 2).reshape(D, 2 * D)
    w_out = gw0[:, 256:384].reshape(D, D)
    shard_shapes = {"conv_dw": (CONVW, 128), "attn_norm_g": (1, 128), "b_qkv": (1, 160), "b_o": (1, 128)}
    parts = [_unpack_small(gs[j], [shard_shapes[n] for n in sharded]) for j in range(NDEV)]
    full = {n: jnp.concatenate([parts[j][i] for j in range(NDEV)], axis=-1) for i, n in enumerate(sharded)}
    dw32 = jnp.pad(full["conv_dw"], ((0, 1), (0, 0)))
    attn_g, bqkv, bo = full["attn_norm_g"], full["b_qkv"], full["b_o"]
    qg = jnp.tile(q_norm_g, (1, NH))
    kg = jnp.tile(k_norm_g, (1, NKV))
    e16, e16t, e2, e2t, fold, fold2 = _seg_mats()
    bkt = jnp.asarray(_bucket_table().T)
    bkt4 = jnp.asarray(np.tile(_bucket_table().T, (1, 4)))

    h0, u, glu = _conv_in_fwd(xs, conv_norm_g, w_in, conv_b_in)
    cc, sb, x1, gw1 = _conv_mid_fwd(glu, xs, dw32, conv_dw_b, conv_ln_g, conv_ln_b, w_out, conv_b_out,
                                    carry=_Gather(pack_bf, AG_MLP0))
    wu0, wd0 = gw1[:, 0:FFB].reshape(NDEV, D, FFB), gw1[:, FFB:2 * FFB]
    h1, act0, x2, gw2 = _mlp_fwd(x1, mlp_norm_g[0:1], wu0, wd0, carry=_Gather(pack_bf, AG_ATTN))
    wqkv = gw2[:, 0:160].reshape(NDEV, D, 160).transpose(1, 0, 2).reshape(D, QKV)
    wo = gw2[:, 160:288].reshape(D, D)
    h2, raw, qn, kn, vv = _attn_qkv_fwd(x2, attn_g, wqkv, bqkv, qg, kg, e16, e16t, e2, e2t)
    kp = jnp.pad(kn, ((WIN, 0), (0, 0)))
    vp = jnp.pad(vv, ((WIN, 0), (0, 0)))
    ob, x3, gw3 = _attn_fwd(qn, kp, vp, bkt, rel_bias, sinks, wo, bo, x2, carry=_Gather(pack_bf, AG_MLP1))
    wu1, wd1 = gw3[:, 0:FFB].reshape(NDEV, D, FFB), gw3[:, FFB:2 * FFB]
    h3, act1, dx4, dx4b, sq = _mlp_fwd(x3, mlp_norm_g[1:2], wu1, wd1, target=loss_target[0])
    loss = lax.psum(jnp.sum(sq) * (0.5 / D), ("x", "y", "c"))

    dup1, dx3, dx3b, dg_mlp1 = _mlp_bwd(dx4, dx4b, x3, mlp_norm_g[1:2], act1, wu1, wd1)
    g_wdn1 = _wgrad(act1, dx4b, 512, D, "wgrad_down1")
    g_wup1 = _wgrad(h3, dup1, D, 512, "wgrad_up1")
    src_mlp1 = jnp.concatenate([_cols_to_blocks(g_wup1, FFB), g_wdn1.reshape(NDEV, FFB, D)], axis=1)
    dqs, dkp, dvp, d_sinks, d_rel, d_bo, land_mlp1 = _attn_bwd(dx3, dx3b, qn, kp, vp, bkt, bkt4, rel_bias, sinks, wo,
                                                             carry=_Scatter(src_mlp1))
    g_wo = _wgrad(ob, dx3b, D, 512, "wgrad_o")
    dqkv, dx2, dx2b, d_bqkv, d_attn_g, d_qg, d_kg = _attn_qkv_bwd(
        dqs, dkp[WIN:], dvp[WIN:], raw, x2, dx3, attn_g, wqkv, qg, kg, e16, e16t, e2, e2t, fold, fold2)
    g_wqkv = _wgrad(h2, dqkv, D, 640, "wgrad_qkv")
    src_attn = jnp.concatenate([_cols_to_blocks(g_wqkv, 160), g_wo.reshape(NDEV, 128, D)], axis=1)
    dup0, dx1, dx1b, dg_mlp0, land_attn = _mlp_bwd(dx2, dx2b, x1, mlp_norm_g[0:1], act0, wu0, wd0,
                                                   carry=_Scatter(src_attn))
    g_wdn0 = _wgrad(act0, dx2b, 512, D, "wgrad_down0")
    g_wup0 = _wgrad(h1, dup0, D, 512, "wgrad_up0")
    dc, d_bout, d_lng, d_lnb, d_dwb = _conv_mid_bwd(dx1, dx1b, cc, conv_ln_g, conv_ln_b, w_out)
    g_wout = _wgrad(sb, dx1b, D, 512, "wgrad_conv_out")
    src_mid = jnp.concatenate([g_wout.reshape(NDEV, 128, D), _cols_to_blocks(g_wup0, FFB),
                               g_wdn0.reshape(NDEV, FFB, D)], axis=1)
    dglu, d_dw, land_mid = _dwconv_bwd(dc, glu, dw32, carry=_Scatter(src_mid))
    du, grad_x, d_bin, d_cng = _conv_in_bwd(dglu, u, xs, dx1, conv_norm_g, w_in)
    g_win = _wgrad(h0, du, D, 512, "wgrad_conv_in")

    small_grads = {
        "conv_norm_g": d_cng, "conv_b_in": d_bin, "conv_dw": d_dw[:CONVW][None], "conv_dw_b": d_dwb,
        "conv_ln_g": d_lng, "conv_ln_b": d_lnb, "conv_b_out": d_bout, "attn_norm_g": d_attn_g, "b_qkv": d_bqkv,
        "q_norm_g": d_qg[:, :HD], "k_norm_g": d_kg[:, :HD], "sinks": d_sinks[:, :NH], "b_o": d_bo,
        "rel_bias": d_rel[:, :NH], "mlp_norm_g": jnp.concatenate([dg_mlp0, dg_mlp1], axis=0)}
    land_last, sred = _scatter_last(_cols_to_blocks(g_win, 256), _pack_small([small_grads[n] for n, _, _ in SMALL]))
    sg_full = _unpack_small(sred, [shp for _, shp, _ in SMALL])

    mpack, vpack = _pack_big(M_), _pack_big(V_)
    pieces = []
    for land, (r0, n), nm in ((land_last, RS_LAST, "adamw_conv_in"), (land_mid, RS_MID, "adamw_mid"),
                              (land_attn, RS_ATTN, "adamw_attn"), (land_mlp1, RS_MLP1, "adamw_mlp1")):
        pieces.append(_reduce_adamw(land, wpack[r0:r0 + n], mpack[r0:r0 + n], vpack[r0:r0 + n], nm))
    red, d_big, m_big, v_big = [jnp.concatenate([p[k] for p in pieces], axis=0) for k in range(4)]
    G = _unpack_big(red)
    for (n, shp, sh), gfull in zip(SMALL, sg_full):
        if sh:
            width = shp[-1] // NDEV
            G[n] = lax.dynamic_slice_in_dim(gfull, me * width, width, axis=gfull.ndim - 1)
        else:
            G[n] = gfull
    snames = [n for n, _, _ in SMALL]
    d_sm, m_sm, v_sm = _adamw(*[_pack_small([t[n] for n in snames]) for t in (W, G, M_, V_)], "adamw_small")
    sshapes = [W[n].shape for n in snames]
    outs = []
    for big, small in ((d_big, d_sm), (m_big, m_sm), (v_big, v_sm)):
        o = _unpack_big(big)
        o.update(dict(zip(snames, _unpack_small(small, sshapes))))
        outs.append(o)
    order = ["conv_norm_g", "conv_w_in", "conv_b_in", "conv_dw", "conv_dw_b", "conv_ln_g", "conv_ln_b", "conv_w_out",
             "conv_b_out", "attn_norm_g", "w_qkv", "b_qkv", "q_norm_g", "k_norm_g", "sinks", "w_o", "b_o", "rel_bias",
             "mlp_norm_g", "w_up", "w_down"]
    return (loss, grad_x[None], *[G[n] for n in order], *[outs[0][n] for n in order],
            *[outs[1][n] for n in order], *[outs[2][n] for n in order])
```

```python
import math

import numpy as np
import jax
import jax.numpy as jnp
from jax import lax
from jax.experimental import pallas as pl
from jax.experimental.pallas import tpu as pltpu

F32 = jnp.float32
BF16 = jnp.bfloat16

D = 1024
DFF = 4096
NH, NKV, HD = 16, 2, 64
NPAIR = NH // 2
KV = NKV * HD
QKV = D + 2 * KV
CONVW = 31
WIN = 128
NBKT = 32
EPS = 1e-6
NEG = -1e30
NDEV = 8
FFB = DFF // NDEV
HALO = 32

LR, B1, B2, AEPS, WD, STEP = 0.001, 0.9, 0.999, 1e-08, 0.01, 10

PACK = (("conv_w_in", 256), ("conv_w_out", 128), ("up0", 512), ("dn0", 512), ("w_qkv", 160), ("w_o", 128),
        ("up1", 512), ("dn1", 512))
PACK_OFF = {n: sum(r for _, r in PACK[:i]) for i, (n, _) in enumerate(PACK)}
PACK_ROWS = sum(r for _, r in PACK)
PACK_CH = 272
AG_FIRST = (0, 384)
AG_MLP0 = (384, 1024)
AG_ATTN = (1408, 288)
AG_MLP1 = (1696, 1024)
RS_MLP1 = (1696, 1024)
RS_ATTN = (1408, 288)
RS_MID = (256, 1152)
RS_LAST = (0, 256)
MESH = pl.DeviceIdType.MESH


def _cp(vmem_mb, n_axes=1):
    return pltpu.CompilerParams(dimension_semantics=("arbitrary",) * n_axes, vmem_limit_bytes=vmem_mb << 20)


def _const(shape):
    nd = len(shape)
    return pl.BlockSpec(shape, lambda *_: (0,) * nd, pipeline_mode=pl.Buffered(1))


def _acc(shape):
    nd = len(shape)
    return pl.BlockSpec(shape, lambda *_: (0,) * nd)


def _rows(tm, n):
    return pl.BlockSpec((tm, n), lambda i: (i, 0))


def _sds(shape, dtype):
    return jax.ShapeDtypeStruct(shape, dtype)


def _dot(a, b):
    return jnp.dot(a, b, preferred_element_type=F32)


def _dot_nt(a, b):
    return lax.dot_general(a, b, (((1,), (1,)), ((), ())), preferred_element_type=F32)


def _dot_tn(a, b):
    return lax.dot_general(a, b, (((0,), (0,)), ((), ())), preferred_element_type=F32)


def _dot_hi(x, e):
    x1 = x.astype(BF16)
    x2 = (x - x1.astype(F32)).astype(BF16)
    return _dot(x1, e) + _dot(x2, e)


def _rms(x):
    return lax.rsqrt(jnp.mean(x * x, axis=-1, keepdims=True) + EPS)


def _rms_bwd(dh, x, r, g):
    xh = x * r
    dxh = dh * g
    dx = r * (dxh - xh * jnp.mean(dxh * xh, axis=-1, keepdims=True))
    return dx, jnp.sum(dh * xh, axis=0, keepdims=True)


def _colsum(a):
    return jnp.sum(a, axis=0, keepdims=True)


def _first(ref, val):
    @pl.when(pl.program_id(0) == 0)
    def _():
        ref[...] = jnp.zeros_like(ref)
    ref[...] += val


def _first_cols(ref, vals):
    @pl.when(pl.program_id(0) == 0)
    def _():
        ref[...] = jnp.zeros_like(ref)
    for c0, val in vals:
        ref[:, c0:c0 + val.shape[1]] += val


def _conv_in_fwd(x, g, w, b):
    S = x.shape[0]
    tm = min(512, S)

    def body(x_ref, g_ref, w_ref, b_ref, h_ref, u_ref, glu_ref):
        xv = x_ref[...]
        hb = (xv * _rms(xv) * g_ref[...]).astype(BF16)
        h_ref[...] = hb
        u = _dot(hb, w_ref[...]) + b_ref[...]
        u_ref[...] = u
        glu_ref[...] = u[:, :D] * jax.nn.sigmoid(u[:, D:])

    return pl.pallas_call(
        body, name="conv_in_fwd", grid=(S // tm,),
        in_specs=[_rows(tm, D), _const((1, D)), _const((D, 2 * D)), _const((1, 2 * D))],
        out_specs=[_rows(tm, D), _rows(tm, 2 * D), _rows(tm, D)],
        out_shape=[_sds((S, D), BF16), _sds((S, 2 * D), F32), _sds((S, D), F32)],
        compiler_params=_cp(48))(x, g, w, b)


def _fill_ext(ext, halo_ref, cur_ref, tm, keep):
    ext[0:HALO, :] = jnp.where(keep, halo_ref[...], 0.0)
    ext[HALO:HALO + tm, :] = cur_ref[...]


def _shift_copies(dst, src, n):
    for s_ in range(1, 8):
        dst[s_ - 1, 0:n, :] = src[s_:s_ + n, :]


def _tap(src, sh, o, r0, c0):
    a, s_ = divmod(o, 8)
    ref = src if s_ == 0 else sh.at[s_ - 1]
    return ref[r0 + 8 * a:r0 + 8 * a + 32, c0:c0 + 512]


def _conv_mid_fwd(glu, x, dw, dwb, lng, lnb, w, b, carry=None):
    S = x.shape[0]
    tm = min(256, S)
    hb = tm // HALO

    def body(gl_ref, halo_ref, x_ref, dw_ref, dwb_ref, lng_ref, lnb_ref, w_ref, b_ref, c_ref, s_ref, xo_ref, ext, esh):
        i = pl.program_id(0)
        _fill_ext(ext, halo_ref, gl_ref, tm, i > 0)
        _shift_copies(esh, ext, tm + 24)
        for r0 in range(0, tm, 32):
            for c0 in range(0, D, 512):
                acc = jnp.zeros((32, 512), F32) + dwb_ref[:, c0:c0 + 512]
                for k in range(CONVW):
                    acc = acc + dw_ref[k:k + 1, c0:c0 + 512] * _tap(ext, esh, 2 + k, r0, c0)
                c_ref[r0:r0 + 32, c0:c0 + 512] = acc
        c = c_ref[...]
        mu = jnp.mean(c, axis=-1, keepdims=True)
        xc = c - mu
        y = xc * lax.rsqrt(jnp.mean(xc * xc, axis=-1, keepdims=True) + EPS) * lng_ref[...] + lnb_ref[...]
        sb = (y * jax.nn.sigmoid(y)).astype(BF16)
        s_ref[...] = sb
        xo_ref[...] = x_ref[...] + _dot(sb, w_ref[...]) + b_ref[...]

    return _carried_call(
        body, carry, name="conv_mid_fwd", grid=(S // tm,),
        in_specs=[_rows(tm, D), pl.BlockSpec((HALO, D), lambda i: (jnp.maximum(i * hb - 1, 0), 0)), _rows(tm, D),
                  _const((32, D)), _const((1, D)), _const((1, D)), _const((1, D)), _const((D, D)), _const((1, D))],
        out_specs=[_rows(tm, D), _rows(tm, D), _rows(tm, D)],
        out_shape=[_sds((S, D), F32), _sds((S, D), BF16), _sds((S, D), F32)],
        scratch_shapes=[pltpu.VMEM((tm + HALO, D), F32), pltpu.VMEM((7, tm + 24, D), F32)],
        compiler_params=_cp(48), args=[glu, glu, x, dw, dwb, lng, lnb, w, b])


def _mlp_fwd(x, g, wu, wd, target=None, carry=None):
    S = x.shape[0]
    tm = min(512, S)
    last = target is not None

    def body(*refs):
        if last:
            x_ref, g_ref, wu_ref, wd_ref, t_ref, h_ref, act_ref, dy_ref, dyb_ref, sq_ref, acc = refs
        else:
            x_ref, g_ref, wu_ref, wd_ref, h_ref, act_ref, xo_ref, acc = refs
        xv = x_ref[...]
        hb = (xv * _rms(xv) * g_ref[...]).astype(BF16)
        h_ref[...] = hb
        for j in range(NDEV):
            a = jnp.square(jnp.maximum(_dot(hb, wu_ref[j]), 0.0)).astype(BF16)
            act_ref[:, j * FFB:(j + 1) * FFB] = a
            if j == 0:
                acc[...] = _dot(a, wd_ref[j])
            else:
                acc[...] += _dot(a, wd_ref[j])
        y = xv + acc[...]
        if last:
            diff = y - t_ref[...]
            dy = diff * (1.0 / D)
            dy_ref[...] = dy
            dyb_ref[...] = dy.astype(BF16)
            _first(sq_ref, _colsum(diff * diff))
        else:
            xo_ref[...] = y

    in_specs = [_rows(tm, D), _const((1, D)), _const((NDEV, D, FFB)), _const((NDEV, FFB, D))]
    args = [x, g, wu, wd]
    out_specs = [_rows(tm, D), _rows(tm, DFF)]
    out_shape = [_sds((S, D), BF16), _sds((S, DFF), BF16)]
    if last:
        in_specs.append(_rows(tm, D))
        args.append(target)
        out_specs += [_rows(tm, D), _rows(tm, D), _acc((1, D))]
        out_shape += [_sds((S, D), F32), _sds((S, D), BF16), _sds((1, D), F32)]
    else:
        out_specs.append(_rows(tm, D))
        out_shape.append(_sds((S, D), F32))
    return _carried_call(
        body, carry, name="mlp_fwd_loss" if last else "mlp_fwd", grid=(S // tm,),
        in_specs=in_specs, out_specs=out_specs, out_shape=out_shape,
        scratch_shapes=[pltpu.VMEM((tm, D), F32)],
        compiler_params=_cp(52), args=args)


def _mlp_bwd(dy, x, g, act, wu, wd, carry=None):
    S = x.shape[0]
    tm = min(512, S)

    def body(dy_ref, x_ref, g_ref, act_ref, wu_ref, wd_ref, dup_ref, dx_ref, dxb_ref, dg_ref, acc):
        db = dy_ref[...].astype(BF16)
        for j in range(NDEV):
            dact = _dot_nt(db, wd_ref[j])
            a = act_ref[:, j * FFB:(j + 1) * FFB].astype(F32)
            dup = (dact * (2.0 * jnp.sqrt(a))).astype(BF16)
            dup_ref[:, j * FFB:(j + 1) * FFB] = dup
            if j == 0:
                acc[...] = _dot_nt(dup, wu_ref[j])
            else:
                acc[...] += _dot_nt(dup, wu_ref[j])
        xv = x_ref[...]
        dxn, dg = _rms_bwd(acc[...], xv, _rms(xv), g_ref[...])
        dx = dy_ref[...] + dxn
        dx_ref[...] = dx
        dxb_ref[...] = dx.astype(BF16)
        _first(dg_ref, dg)

    return _carried_call(
        body, carry, name="mlp_bwd", grid=(S // tm,),
        in_specs=[_rows(tm, D), _rows(tm, D), _const((1, D)), _rows(tm, DFF),
                  _const((NDEV, D, FFB)), _const((NDEV, FFB, D))],
        out_specs=[_rows(tm, DFF), _rows(tm, D), _rows(tm, D), _acc((1, D))],
        out_shape=[_sds((S, DFF), BF16), _sds((S, D), F32), _sds((S, D), BF16), _sds((1, D), F32)],
        scratch_shapes=[pltpu.VMEM((tm, D), F32)],
        compiler_params=_cp(58), args=[dy, x, g, act, wu, wd])


def _wgrad(a, b, tm, tn, name):
    S, M = a.shape
    N = b.shape[1]
    tk = min(4096, S)
    nk = S // tk

    def body(a_ref, b_ref, o_ref, acc):
        k = pl.program_id(2)

        @pl.when(k == 0)
        def _():
            acc[...] = jnp.zeros_like(acc)

        acc[...] += _dot_tn(a_ref[...], b_ref[...])

        @pl.when(k == nk - 1)
        def _():
            o_ref[...] = acc[...].astype(BF16)

    return pl.pallas_call(
        body, name=name, grid=(M // tm, N // tn, nk),
        in_specs=[pl.BlockSpec((tk, tm), lambda i, j, k: (k, i)), pl.BlockSpec((tk, tn), lambda i, j, k: (k, j))],
        out_specs=pl.BlockSpec((tm, tn), lambda i, j, k: (i, j)),
        out_shape=_sds((M, N), BF16),
        scratch_shapes=[pltpu.VMEM((tm, tn), F32)],
        compiler_params=_cp(48, 3))(a, b)


def _bucket_table():
    q = np.arange(WIN)[:, None]
    k = np.arange(2 * WIN)[None, :]
    dist = q + WIN - k
    n = np.maximum(dist, 0)
    max_exact = NBKT // 2
    large = max_exact + (np.log(np.maximum(n, 1).astype(np.float32) / max_exact)
                         / math.log(WIN / max_exact) * (NBKT - max_exact)).astype(np.int32)
    large = np.minimum(large, NBKT - 1)
    bkt = np.where(n < max_exact, n, large).astype(np.int32)
    return np.where((dist >= 0) & (dist < WIN), bkt, -1).astype(np.int32)


def _seg_mats():
    e16 = np.zeros((D, 128), np.float32)
    e16[np.arange(D), np.arange(D) // HD] = 1.0
    e2 = np.zeros((KV, 128), np.float32)
    e2[np.arange(KV), np.arange(KV) // HD] = 1.0
    fold = np.zeros((D, 128), np.float32)
    fold[np.arange(D), np.arange(D) % HD] = 1.0
    fold2 = np.zeros((KV, 128), np.float32)
    fold2[np.arange(KV), np.arange(KV) % HD] = 1.0
    return [jnp.asarray(m, BF16) for m in (e16, e16.T, e2, e2.T, fold, fold2)]


def _head_rms(t, e, et):
    r = lax.rsqrt(_dot_hi(t * t, e) * (1.0 / HD) + EPS)
    return _dot_hi(r, et)


def _attn_qkv_fwd(x, g, w, b, qg, kg, e16, e16t, e2, e2t):
    S = x.shape[0]
    tm = min(512, S)

    def body(x_ref, g_ref, w_ref, b_ref, qg_ref, kg_ref, e16_ref, e16t_ref, e2_ref, e2t_ref,
             h_ref, raw_ref, qn_ref, kn_ref, v_ref):
        xv = x_ref[...]
        hb = (xv * _rms(xv) * g_ref[...]).astype(BF16)
        h_ref[...] = hb
        raw = _dot(hb, w_ref[...]) + b_ref[...]
        raw_ref[...] = raw
        q = raw[:, :D]
        k = raw[:, D:D + KV]
        qn = (q * _head_rms(q, e16_ref[...], e16t_ref[...]) * qg_ref[...] * 0.125).astype(BF16)
        for bb in range(tm // WIN):
            for p in range(NPAIR):
                r = (bb * NPAIR + p) * WIN
                qn_ref[r:r + WIN, :] = qn[bb * WIN:(bb + 1) * WIN, p * 128:(p + 1) * 128]
        kn_ref[...] = (k * _head_rms(k, e2_ref[...], e2t_ref[...]) * kg_ref[...]).astype(BF16)
        v_ref[...] = raw[:, D + KV:].astype(BF16)

    return pl.pallas_call(
        body, name="attn_qkv_fwd", grid=(S // tm,),
        in_specs=[_rows(tm, D), _const((1, D)), _const((D, QKV)), _const((1, QKV)), _const((1, D)), _const((1, KV)),
                  _const((D, 128)), _const((128, D)), _const((KV, 128)), _const((128, KV))],
        out_specs=[_rows(tm, D), _rows(tm, QKV), _rows(tm * NPAIR, 128), _rows(tm, KV), _rows(tm, KV)],
        out_shape=[_sds((S, D), BF16), _sds((S, QKV), F32), _sds((S * NPAIR, 128), BF16), _sds((S, KV), BF16),
                   _sds((S, KV), BF16)],
        compiler_params=_cp(48))(x, g, w, b, qg, kg, e16, e16t, e2, e2t)


def _build_bias(bkt_ref, rb_ref, sk_ref, bias_sc, sk_sc):
    bkt = bkt_ref[...]
    row = lax.broadcasted_iota(jnp.int32, (2 * WIN, WIN), 0)
    for h in range(NH):
        p, e = h // 2, h % 2
        g, pp = p // 4, p % 4

        def add(bk, acc, h=h):
            return acc + jnp.where(bkt == bk, rb_ref[bk, h], 0.0)

        bias = lax.fori_loop(0, NBKT, add, jnp.where(bkt < 0, NEG, 0.0).astype(F32))
        bias_sc[0, g, e, :, pp * WIN:(pp + 1) * WIN] = bias
        bias_sc[1, g, e, :, pp * WIN:(pp + 1) * WIN] = jnp.where(row < WIN, NEG, bias)
        sk_sc[g, e, :, pp * WIN:(pp + 1) * WIN] = jnp.zeros((1, WIN), F32) + sk_ref[0, h]


def _stacks(f, mlo):
    r = pltpu.roll(f, HD, 1)
    z = jnp.zeros_like(f)
    return ((jnp.where(mlo, f, z).astype(BF16), jnp.where(mlo, z, r).astype(BF16)),
            (jnp.where(mlo, r, z).astype(BF16), jnp.where(mlo, z, f).astype(BF16)))


def _unstack(d, mlo):
    z = jnp.zeros_like(d[0][0])
    return (jnp.where(mlo, d[0][0], z) + pltpu.roll(jnp.where(mlo, z, d[0][1]), HD, 1)
            + pltpu.roll(jnp.where(mlo, d[1][0], z), HD, 1) + jnp.where(mlo, z, d[1][1]))


def _softmax_sink(l, sk):
    m = jnp.maximum(jnp.max(l, axis=0, keepdims=True), sk)
    ex = jnp.exp(l - m)
    es = jnp.exp(sk - m)
    inv = 1.0 / (jnp.sum(ex, axis=0, keepdims=True) + es)
    return ex * inv, es * inv


GROWS = 4 * WIN


def _attn_fwd(qn, kp, vp, bktt, rel_bias, sinks, wo, bo, x, carry=None):
    S = x.shape[0]
    tq = min(512, S)
    nblk = tq // WIN

    def body(q_ref, k_ref, v_ref, bkt_ref, rb_ref, sk_ref, wo_ref, bo_ref, x_ref, o_ref, xo_ref, bias_sc, sk_sc):
        i = pl.program_id(0)

        @pl.when(i == 0)
        def _():
            _build_bias(bkt_ref, rb_ref, sk_ref, bias_sc, sk_sc)

        mlo = lax.broadcasted_iota(jnp.int32, (2 * WIN, KV), 1) < HD

        def blk(bb, carry):
            r0 = pl.multiple_of(bb * WIN, WIN)
            g0 = pl.multiple_of(i * tq + bb * WIN, WIN)
            first = (g0 == 0).astype(jnp.int32)
            ks = _stacks(k_ref[pl.ds(g0, 2 * WIN), :].astype(F32), mlo)
            vs = _stacks(v_ref[pl.ds(g0, 2 * WIN), :].astype(F32), mlo)
            for g in range(NKV):
                qs = q_ref[pl.ds(pl.multiple_of(bb * (NPAIR * WIN) + g * GROWS, GROWS), GROWS), :]
                og = jnp.zeros((GROWS, 128), F32)
                for e in range(2):
                    l = _dot_nt(ks[g][e], qs) + bias_sc[first, g, e]
                    pr, _ = _softmax_sink(l, sk_sc[g, e])
                    og = og + _dot_tn(pr.astype(BF16), vs[g][e])
                for pp in range(4):
                    p = 4 * g + pp
                    o_ref[pl.ds(r0, WIN), p * 128:(p + 1) * 128] = og[pp * WIN:(pp + 1) * WIN].astype(BF16)
            return carry

        lax.fori_loop(0, nblk, blk, 0)
        xo_ref[...] = x_ref[...] + _dot(o_ref[...], wo_ref[...]) + bo_ref[...]

    smem = pl.BlockSpec(memory_space=pltpu.SMEM)
    return _carried_call(
        body, carry, name="attn_fwd", grid=(S // tq,),
        in_specs=[_rows(tq * NPAIR, 128), _const((S + WIN, KV)), _const((S + WIN, KV)), _const((2 * WIN, WIN)), smem, smem,
                  _const((D, D)), _const((1, D)), _rows(tq, D)],
        out_specs=[_rows(tq, D), _rows(tq, D)],
        out_shape=[_sds((S, D), BF16), _sds((S, D), F32)],
        scratch_shapes=[pltpu.VMEM((2, NKV, 2, 2 * WIN, GROWS), F32), pltpu.VMEM((NKV, 2, 1, GROWS), F32)],
        compiler_params=_cp(48), args=[qn, kp, vp, bktt, rel_bias, sinks, wo, bo, x])


def _attn_bwd(dx, dxb, qn, kp, vp, bktt, bktt4, rel_bias, sinks, wo, carry=None):
    S = dx.shape[0]
    tq = min(512, S)
    nblk = tq // WIN
    nsteps = S // tq

    def body(dx_ref, dxb_ref, q_ref, k_ref, v_ref, bkt_ref, bkt4_ref, rb_ref, sk_ref, wo_ref,
             dq_ref, dk_ref, dv_ref, dsk_ref, drb_ref, dbo_ref, bias_sc, sk_sc, dbias_sc, dsk_sc, do_sc):
        i = pl.program_id(0)

        @pl.when(i == 0)
        def _():
            _build_bias(bkt_ref, rb_ref, sk_ref, bias_sc, sk_sc)
            dbias_sc[...] = jnp.zeros_like(dbias_sc)
            dsk_sc[...] = jnp.zeros_like(dsk_sc)
            dk_ref[...] = jnp.zeros_like(dk_ref)
            dv_ref[...] = jnp.zeros_like(dv_ref)

        _first(dbo_ref, _colsum(dx_ref[...]))
        do = _dot_nt(dxb_ref[...], wo_ref[...]).astype(BF16)
        for bb in range(nblk):
            for p in range(NPAIR):
                r = (bb * NPAIR + p) * WIN
                do_sc[r:r + WIN, :] = do[bb * WIN:(bb + 1) * WIN, p * 128:(p + 1) * 128]

        mlo = lax.broadcasted_iota(jnp.int32, (2 * WIN, KV), 1) < HD

        def blk(bb, carry):
            g0 = pl.multiple_of(i * tq + bb * WIN, WIN)
            first = (g0 == 0).astype(jnp.int32)
            ks = _stacks(k_ref[pl.ds(g0, 2 * WIN), :].astype(F32), mlo)
            vs = _stacks(v_ref[pl.ds(g0, 2 * WIN), :].astype(F32), mlo)
            dks = [[None, None], [None, None]]
            dvs = [[None, None], [None, None]]
            for g in range(NKV):
                rows = pl.ds(pl.multiple_of(bb * (NPAIR * WIN) + g * GROWS, GROWS), GROWS)
                qs = q_ref[rows, :]
                dos = do_sc[rows, :]
                dqs = jnp.zeros((GROWS, 128), F32)
                for e in range(2):
                    l = _dot_nt(ks[g][e], qs) + bias_sc[first, g, e]
                    pr, ps = _softmax_sink(l, sk_sc[g, e])
                    dp = _dot_nt(vs[g][e], dos)
                    dr = jnp.sum(pr * dp, axis=0, keepdims=True)
                    dl = pr * (dp - dr)
                    dsk_sc[g, e] -= ps * dr
                    dbias_sc[g, e] += dl
                    dlb = dl.astype(BF16)
                    dqs = dqs + _dot_tn(dlb, ks[g][e])
                    dks[g][e] = _dot(dlb, qs)
                    dvs[g][e] = _dot(pr.astype(BF16), dos)
                dq_ref[rows, :] = dqs
            dk_ref[pl.ds(g0, 2 * WIN), :] += _unstack(dks, mlo)
            dv_ref[pl.ds(g0, 2 * WIN), :] += _unstack(dvs, mlo)
            return carry

        lax.fori_loop(0, nblk, blk, 0)

        @pl.when(i == nsteps - 1)
        def _():
            bkt4v = bkt4_ref[...]
            rid = lax.broadcasted_iota(jnp.int32, (NBKT, 128), 0)
            lid = lax.broadcasted_iota(jnp.int32, (NBKT, 128), 1)
            lid1 = lax.broadcasted_iota(jnp.int32, (1, 128), 1)
            dsk = jnp.zeros((1, 128), F32)
            for g in range(NKV):
                for e in range(2):
                    for pp in range(4):
                        h = 2 * (4 * g + pp) + e
                        t = jnp.sum(dsk_sc[g, e, :, pp * WIN:(pp + 1) * WIN], axis=1, keepdims=True)
                        dsk = dsk + jnp.where(lid1 == h, t, 0.0)
            dsk_ref[...] = dsk

            def per_bucket(bk, acc):
                mb = bkt4v == bk
                for g in range(NKV):
                    for e in range(2):
                        t = jnp.sum(jnp.where(mb, dbias_sc[g, e], 0.0), axis=0, keepdims=True)
                        for pp in range(4):
                            h = 2 * (4 * g + pp) + e
                            tt = jnp.sum(t[:, pp * WIN:(pp + 1) * WIN], axis=1, keepdims=True)
                            acc = acc + jnp.where((rid == bk) & (lid == h), tt, 0.0)
                return acc

            drb_ref[...] = lax.fori_loop(0, NBKT, per_bucket, jnp.zeros((NBKT, 128), F32))

    smem = pl.BlockSpec(memory_space=pltpu.SMEM)
    return _carried_call(
        body, carry, name="attn_bwd", grid=(nsteps,),
        in_specs=[_rows(tq, D), _rows(tq, D), _rows(tq * NPAIR, 128), _const((S + WIN, KV)), _const((S + WIN, KV)),
                  _const((2 * WIN, WIN)), _const((2 * WIN, GROWS)), smem, smem, _const((D, D))],
        out_specs=[_rows(tq * NPAIR, 128), _acc((S + WIN, KV)), _acc((S + WIN, KV)), _acc((1, 128)), _acc((NBKT, 128)),
                   _acc((1, D))],
        out_shape=[_sds((S * NPAIR, 128), F32), _sds((S + WIN, KV), F32), _sds((S + WIN, KV), F32), _sds((1, 128), F32),
                   _sds((NBKT, 128), F32), _sds((1, D), F32)],
        scratch_shapes=[pltpu.VMEM((2, NKV, 2, 2 * WIN, GROWS), F32), pltpu.VMEM((NKV, 2, 1, GROWS), F32),
                        pltpu.VMEM((NKV, 2, 2 * WIN, GROWS), F32), pltpu.VMEM((NKV, 2, 1, GROWS), F32),
                        pltpu.VMEM((tq * NPAIR, 128), BF16)],
        compiler_params=_cp(52), args=[dx, dxb, qn, kp, vp, bktt, bktt4, rel_bias, sinks, wo])


def _head_norm_bwd(dn, t, e, et, gt):
    r = _head_rms(t, e, et)
    th = t * r
    dth = dn * gt
    dt = r * (dth - th * _dot_hi(_dot_hi(dth * th, e) * (1.0 / HD), et))
    return dt, _colsum(dn * th)


def _attn_qkv_bwd(dqs, dk, dv, raw, x, dxo, g, w, qg, kg, e16, e16t, e2, e2t, fold, fold2):
    S = x.shape[0]
    tm = min(512, S)

    def body(dq_ref, dk_ref, dv_ref, raw_ref, x_ref, dxo_ref, g_ref, w_ref, qg_ref, kg_ref,
             e16_ref, e16t_ref, e2_ref, e2t_ref, fold_ref, fold2_ref,
             dqkv_ref, dx_ref, dxb_ref, db_ref, dg_ref, dqg_ref, dkg_ref, dq_sc):
        for bb in range(tm // WIN):
            for p in range(NPAIR):
                r = (bb * NPAIR + p) * WIN
                dq_sc[bb * WIN:(bb + 1) * WIN, p * 128:(p + 1) * 128] = dq_ref[r:r + WIN, :]
        dq, cq = _head_norm_bwd(dq_sc[...] * 0.125, raw_ref[:, :D], e16_ref[...], e16t_ref[...], qg_ref[...])
        dk_, ck = _head_norm_bwd(dk_ref[...], raw_ref[:, D:D + KV], e2_ref[...], e2t_ref[...], kg_ref[...])
        dv_ = dv_ref[...]
        _first(dqg_ref, _dot_hi(cq, fold_ref[...]))
        _first(dkg_ref, _dot_hi(ck, fold2_ref[...]))
        _first_cols(db_ref, [(0, _colsum(dq)), (D, _colsum(dk_)), (D + KV, _colsum(dv_))])
        dqb, dkb, dvb = dq.astype(BF16), dk_.astype(BF16), dv_.astype(BF16)
        dqkv_ref[:, :D] = dqb
        dqkv_ref[:, D:D + KV] = dkb
        dqkv_ref[:, D + KV:] = dvb
        dh = (_dot_nt(dqb, w_ref[:, :D]) + _dot_nt(dkb, w_ref[:, D:D + KV]) + _dot_nt(dvb, w_ref[:, D + KV:]))
        xv = x_ref[...]
        dxn, dg = _rms_bwd(dh, xv, _rms(xv), g_ref[...])
        dx = dxo_ref[...] + dxn
        dx_ref[...] = dx
        dxb_ref[...] = dx.astype(BF16)
        _first(dg_ref, dg)

    return pl.pallas_call(
        body, name="attn_qkv_bwd", grid=(S // tm,),
        in_specs=[_rows(tm * NPAIR, 128), _rows(tm, KV), _rows(tm, KV), _rows(tm, QKV), _rows(tm, D), _rows(tm, D),
                  _const((1, D)), _const((D, QKV)), _const((1, D)), _const((1, KV)),
                  _const((D, 128)), _const((128, D)), _const((KV, 128)), _const((128, KV)),
                  _const((D, 128)), _const((KV, 128))],
        out_specs=[_rows(tm, QKV), _rows(tm, D), _rows(tm, D), _acc((1, QKV)), _acc((1, D)), _acc((1, 128)), _acc((1, 128))],
        out_shape=[_sds((S, QKV), BF16), _sds((S, D), F32), _sds((S, D), BF16), _sds((1, QKV), F32), _sds((1, D), F32),
                   _sds((1, 128), F32), _sds((1, 128), F32)],
        scratch_shapes=[pltpu.VMEM((tm, D), F32)],
        compiler_params=_cp(48))(dqs, dk, dv, raw, x, dxo, g, w, qg, kg, e16, e16t, e2, e2t, fold, fold2)


def _conv_mid_bwd(dx, dxb, c, lng, lnb, w):
    S = dx.shape[0]
    tm = min(512, S)

    def body(dx_ref, dxb_ref, c_ref, lng_ref, lnb_ref, w_ref, dc_ref, dbo_ref, dlg_ref, dlb_ref, ddwb_ref):
        _first(dbo_ref, _colsum(dx_ref[...]))
        ds = _dot_nt(dxb_ref[...], w_ref[...])
        c = c_ref[...]
        xc = c - jnp.mean(c, axis=-1, keepdims=True)
        rstd = lax.rsqrt(jnp.mean(xc * xc, axis=-1, keepdims=True) + EPS)
        ch = xc * rstd
        y = ch * lng_ref[...] + lnb_ref[...]
        sg = jax.nn.sigmoid(y)
        dy = ds * (sg * (1.0 + y * (1.0 - sg)))
        _first(dlg_ref, _colsum(dy * ch))
        _first(dlb_ref, _colsum(dy))
        dch = dy * lng_ref[...]
        dc = rstd * (dch - jnp.mean(dch, axis=-1, keepdims=True) - ch * jnp.mean(dch * ch, axis=-1, keepdims=True))
        dc_ref[...] = dc
        _first(ddwb_ref, _colsum(dc))

    return pl.pallas_call(
        body, name="conv_mid_bwd", grid=(S // tm,),
        in_specs=[_rows(tm, D), _rows(tm, D), _rows(tm, D), _const((1, D)), _const((1, D)), _const((D, D))],
        out_specs=[_rows(tm, D), _acc((1, D)), _acc((1, D)), _acc((1, D)), _acc((1, D))],
        out_shape=[_sds((S, D), F32)] + [_sds((1, D), F32)] * 4,
        compiler_params=_cp(40))(dx, dxb, c, lng, lnb, w)


def _dwconv_bwd(dc, glu, dw, carry=None):
    S = dc.shape[0]
    tm = min(256, S)
    hb = tm // HALO
    nsteps = S // tm

    def body(dc_ref, nxt_ref, gl_ref, halo_ref, dw_ref, dgl_ref, ddw_ref, ext, dext, esh, dsh):
        i = pl.program_id(0)
        _fill_ext(ext, halo_ref, gl_ref, tm, i > 0)
        dext[0:tm, :] = dc_ref[...]
        dext[tm:tm + HALO, :] = jnp.where(i < nsteps - 1, nxt_ref[...], 0.0)
        _shift_copies(esh, ext, tm + 24)
        _shift_copies(dsh, dext, tm + 24)

        @pl.when(i == 0)
        def _():
            ddw_ref[...] = jnp.zeros_like(ddw_ref)

        for c0 in range(0, D, 512):
            for r0 in range(0, tm, 32):
                acc = jnp.zeros((32, 512), F32)
                for j in range(CONVW):
                    acc = acc + dw_ref[CONVW - 1 - j:CONVW - j, c0:c0 + 512] * _tap(dext, dsh, j, r0, c0)
                dgl_ref[r0:r0 + 32, c0:c0 + 512] = acc
            for k in range(CONVW):
                acc = jnp.zeros((32, 512), F32)
                for r0 in range(0, tm, 32):
                    acc = acc + dext[r0:r0 + 32, c0:c0 + 512] * _tap(ext, esh, 2 + k, r0, c0)
                ddw_ref[k:k + 1, c0:c0 + 512] += _colsum(acc)

    return _carried_call(
        body, carry, name="dwconv_bwd", grid=(nsteps,),
        in_specs=[_rows(tm, D), pl.BlockSpec((HALO, D), lambda i: (jnp.minimum((i + 1) * hb, S // HALO - 1), 0)),
                  _rows(tm, D), pl.BlockSpec((HALO, D), lambda i: (jnp.maximum(i * hb - 1, 0), 0)), _const((32, D))],
        out_specs=[_rows(tm, D), _acc((32, D))],
        out_shape=[_sds((S, D), F32), _sds((32, D), F32)],
        scratch_shapes=[pltpu.VMEM((tm + HALO, D), F32), pltpu.VMEM((tm + HALO, D), F32),
                        pltpu.VMEM((7, tm + 24, D), F32), pltpu.VMEM((7, tm + 24, D), F32)],
        compiler_params=_cp(48), args=[dc, dc, glu, glu, dw])


def _conv_in_bwd(dglu, u, x, dxo, g, w):
    S = x.shape[0]
    tm = min(512, S)

    def body(dgl_ref, u_ref, x_ref, dxo_ref, g_ref, w_ref, du_ref, dx_ref, db_ref, dg_ref):
        dgl = dgl_ref[...]
        a = u_ref[:, :D]
        sg = jax.nn.sigmoid(u_ref[:, D:])
        da = dgl * sg
        dgt = dgl * a * sg * (1.0 - sg)
        _first_cols(db_ref, [(0, _colsum(da)), (D, _colsum(dgt))])
        dab, dgb = da.astype(BF16), dgt.astype(BF16)
        du_ref[:, :D] = dab
        du_ref[:, D:] = dgb
        dh = _dot_nt(dab, w_ref[:, :D]) + _dot_nt(dgb, w_ref[:, D:])
        xv = x_ref[...]
        dxn, dg = _rms_bwd(dh, xv, _rms(xv), g_ref[...])
        dx_ref[...] = dxo_ref[...] + dxn
        _first(dg_ref, dg)

    return pl.pallas_call(
        body, name="conv_in_bwd", grid=(S // tm,),
        in_specs=[_rows(tm, D), _rows(tm, 2 * D), _rows(tm, D), _rows(tm, D), _const((1, D)), _const((D, 2 * D))],
        out_specs=[_rows(tm, 2 * D), _rows(tm, D), _acc((1, 2 * D)), _acc((1, D))],
        out_shape=[_sds((S, 2 * D), BF16), _sds((S, D), F32), _sds((1, 2 * D), F32), _sds((1, D), F32)],
        compiler_params=_cp(48))(dglu, u, x, dxo, g, w)


def _coords():
    return lax.axis_index("x"), lax.axis_index("y"), lax.axis_index("c")


def _split(refs, *counts):
    out, k = [], 0
    for n in counts:
        out.append(refs[k:k + n])
        k += n
    return out


def _carried_call(body, carry, *, name, grid, in_specs, out_specs, out_shape, scratch_shapes, compiler_params, args):
    if carry is None:
        return pl.pallas_call(body, name=name, grid=grid, in_specs=in_specs, out_specs=out_specs, out_shape=out_shape,
                              scratch_shapes=scratch_shapes, compiler_params=compiler_params)(*args)
    counts = (len(in_specs), len(carry.args), len(out_specs), len(carry.out_shape), len(scratch_shapes), len(carry.scratch))
    nsteps = grid[0]

    def wrapped(*refs):
        ins, cin, outs, cout, scr, cscr = _split(refs, *counts)
        i = pl.program_id(0)
        carry.top(i, nsteps, cin, cout, cscr)
        body(*ins, *outs, *scr)
        carry.bottom(i, nsteps, cin, cout, cscr)

    return pl.pallas_call(
        wrapped, name=name, grid=grid, in_specs=list(in_specs) + carry.in_specs,
        out_specs=list(out_specs) + carry.out_specs, out_shape=list(out_shape) + carry.out_shape,
        scratch_shapes=list(scratch_shapes) + carry.scratch, compiler_params=compiler_params)(*args, *carry.args)


class _Gather:
    def __init__(self, pack_bf, rows):
        self.r0, self.n = rows
        self.args = [pack_bf]
        self.in_specs = [pl.BlockSpec(memory_space=pl.ANY)]
        self.out_specs = [pl.BlockSpec(memory_space=pl.ANY)]
        self.out_shape = [_sds((NDEV, self.n, D), BF16)]
        self.scratch = [pltpu.SemaphoreType.DMA((7,)), pltpu.SemaphoreType.DMA((7,)), pltpu.SemaphoreType.DMA]

    def _copies(self, ins, outs, scr):
        src = ins[0].at[pl.ds(self.r0, self.n)]
        dst = outs[0]
        send, recv, lsem = scr
        x, y, c = _coords()
        me, sib = (x, y, c), (x, y, 1 - c)
        chips = [(1 - x, y), (x, 1 - y), (1 - x, 1 - y)]

        def cp(k, block, to, own=False):
            idx = 4 * block[0] + 2 * block[1] + block[2]
            return pltpu.make_async_remote_copy(
                src_ref=src if own else dst.at[idx], dst_ref=dst.at[idx], send_sem=send.at[k], recv_sem=recv.at[k],
                device_id=to, device_id_type=MESH)

        own = [cp(0, me, sib, True)] + [cp(1 + j, me, (*ch, c), True) for j, ch in enumerate(chips)]
        ici_in = [cp(1 + j, (*ch, c), me) for j, ch in enumerate(chips)]
        fwd = [cp(4 + j, (*ch, c), sib) for j, ch in enumerate(chips)]
        sib_in = [cp(0, sib, me)] + [cp(4 + j, (*ch, 1 - c), me) for j, ch in enumerate(chips)]
        local = pltpu.make_async_copy(src, dst.at[4 * x + 2 * y + c], lsem)
        return own, ici_in, fwd, sib_in, local

    def top(self, i, nsteps, ins, outs, scr):
        @pl.when(i == 0)
        def _():
            own, _, _, _, local = self._copies(ins, outs, scr)
            local.start()
            for cp in own:
                cp.start()

        @pl.when(i == (3 * nsteps) // 4)
        def _():
            _, ici_in, fwd, _, _ = self._copies(ins, outs, scr)
            for a_, f_ in zip(ici_in, fwd):
                a_.wait_recv()
                f_.start()

    def bottom(self, i, nsteps, ins, outs, scr):
        @pl.when(i == nsteps - 1)
        def _():
            own, _, fwd, sib_in, local = self._copies(ins, outs, scr)
            for cp in sib_in:
                cp.wait_recv()
            for cp in own + fwd:
                cp.wait_send()
            local.wait()


class _Scatter:
    def __init__(self, src):
        self.args = [src]
        self.in_specs = [pl.BlockSpec(memory_space=pl.ANY)]
        self.out_specs = [pl.BlockSpec(memory_space=pl.ANY)]
        self.out_shape = [_sds(src.shape, BF16)]
        self.scratch = [pltpu.SemaphoreType.DMA((7,)), pltpu.SemaphoreType.DMA((7,)), pltpu.SemaphoreType.DMA]

    def _copies(self, ins, outs, scr):
        src, land = ins[0], outs[0]
        send, recv, lsem = scr
        x, y, c = _coords()
        me = 4 * x + 2 * y + c
        sends, recvs = [], []
        for k in range(1, NDEV):
            peer = (x ^ ((k >> 2) & 1), y ^ ((k >> 1) & 1), c ^ (k & 1))
            pidx = me ^ k
            sends.append(pltpu.make_async_remote_copy(
                src_ref=src.at[pidx], dst_ref=land.at[me], send_sem=send.at[k - 1], recv_sem=recv.at[k - 1],
                device_id=peer, device_id_type=MESH))
            recvs.append(pltpu.make_async_remote_copy(
                src_ref=src.at[pidx], dst_ref=land.at[pidx], send_sem=send.at[k - 1], recv_sem=recv.at[k - 1],
                device_id=peer, device_id_type=MESH))
        return sends, recvs, pltpu.make_async_copy(src.at[me], land.at[me], lsem)

    def top(self, i, nsteps, ins, outs, scr):
        @pl.when(i == 0)
        def _():
            sends, _, local = self._copies(ins, outs, scr)
            local.start()
            for cp in sends:
                cp.start()

    def bottom(self, i, nsteps, ins, outs, scr):
        @pl.when(i == nsteps - 1)
        def _():
            sends, recvs, local = self._copies(ins, outs, scr)
            for cp in recvs:
                cp.wait_recv()
            for cp in sends:
                cp.wait_send()
            local.wait()


def _gather_first(wpack, spack):
    R, RS_ = wpack.shape[0], spack.shape[0]
    r0, n = AG_FIRST

    def body(w_ref, s_ref, pb_ref, gw_ref, gs_ref, stage, send_sems, recv_sems, local_sems):
        x, y, c = _coords()
        me, sib = (x, y, c), (x, y, 1 - c)
        chips = [(1 - x, y), (x, 1 - y), (1 - x, 1 - y)]

        def cast(t, carry):
            r = pl.multiple_of(t * PACK_CH, PACK_CH)
            stage[pl.ds(r, PACK_CH), :] = w_ref[pl.ds(r, PACK_CH), :].astype(BF16)
            return carry

        lax.fori_loop(0, R // PACK_CH, cast, 0)

        def slot(px, py, pc):
            return 4 * px + 2 * py + pc

        def copies(k, block, to, own=False):
            idx = slot(*block)
            return [
                pltpu.make_async_remote_copy(
                    src_ref=stage.at[pl.ds(r0, n)] if own else gw_ref.at[idx], dst_ref=gw_ref.at[idx],
                    send_sem=send_sems.at[0, k], recv_sem=recv_sems.at[0, k], device_id=to, device_id_type=MESH),
                pltpu.make_async_remote_copy(
                    src_ref=s_ref if own else gs_ref.at[idx], dst_ref=gs_ref.at[idx],
                    send_sem=send_sems.at[1, k], recv_sem=recv_sems.at[1, k], device_id=to, device_id_type=MESH)]

        mine = pltpu.make_async_copy(stage.at[pl.ds(r0, n)], gw_ref.at[slot(*me)], local_sems.at[0])
        mine.start()
        whole = pltpu.make_async_copy(stage, pb_ref, local_sems.at[1])
        whole.start()
        gs_ref[slot(*me)] = s_ref[...]
        first = copies(0, me, sib, own=True)
        for j, chip in enumerate(chips):
            first += copies(1 + j, me, (*chip, c), own=True)
        for cp in first:
            cp.start()
        passed = []
        for j, chip in enumerate(chips):
            for cp in copies(1 + j, (*chip, c), me):
                cp.wait_recv()
            fwd = copies(4 + j, (*chip, c), sib)
            for cp in fwd:
                cp.start()
            passed += fwd
        for cp in copies(0, sib, me):
            cp.wait_recv()
        for j, chip in enumerate(chips):
            for cp in copies(4 + j, (*chip, 1 - c), me):
                cp.wait_recv()
        for cp in first + passed:
            cp.wait_send()
        mine.wait()
        whole.wait()

    vmem = pl.BlockSpec(memory_space=pltpu.VMEM)
    hbm = pl.BlockSpec(memory_space=pl.ANY)
    return pl.pallas_call(
        body, name="gather_first",
        in_specs=[vmem, vmem],
        out_specs=[hbm, hbm, vmem],
        out_shape=[_sds((R, D), BF16), _sds((NDEV, n, D), BF16), _sds((NDEV, RS_, 128), F32)],
        scratch_shapes=[pltpu.VMEM((R, D), BF16), pltpu.SemaphoreType.DMA((2, 7)), pltpu.SemaphoreType.DMA((2, 7)),
                        pltpu.SemaphoreType.DMA((2,))],
        compiler_params=pltpu.CompilerParams(vmem_limit_bytes=40 << 20))(wpack, spack)


def _scatter_last(gsrc, spack):
    n, RS_ = gsrc.shape[1], spack.shape[0]

    def body(g_ref, s_ref, land_ref, sred_ref, sland, send_sems, recv_sems, local_sem):
        x, y, c = _coords()
        me = 4 * x + 2 * y + c
        sends = []
        for k in range(1, NDEV):
            peer = (x ^ ((k >> 2) & 1), y ^ ((k >> 1) & 1), c ^ (k & 1))
            pidx = me ^ k
            cw = pltpu.make_async_remote_copy(
                src_ref=g_ref.at[pidx], dst_ref=land_ref.at[me], send_sem=send_sems.at[0, k - 1],
                recv_sem=recv_sems.at[0, k - 1], device_id=peer, device_id_type=MESH)
            cs = pltpu.make_async_remote_copy(
                src_ref=s_ref, dst_ref=sland.at[me], send_sem=send_sems.at[1, k - 1],
                recv_sem=recv_sems.at[1, k - 1], device_id=peer, device_id_type=MESH)
            cw.start()
            cs.start()
            sends += [cw, cs]
        mine = pltpu.make_async_copy(g_ref.at[me], land_ref.at[me], local_sem)
        mine.start()
        sland[me] = s_ref[...]
        for k in range(1, NDEV):
            peer = (x ^ ((k >> 2) & 1), y ^ ((k >> 1) & 1), c ^ (k & 1))
            pidx = me ^ k
            pltpu.make_async_remote_copy(
                src_ref=g_ref.at[pidx], dst_ref=land_ref.at[pidx], send_sem=send_sems.at[0, k - 1],
                recv_sem=recv_sems.at[0, k - 1], device_id=peer, device_id_type=MESH).wait_recv()
            pltpu.make_async_remote_copy(
                src_ref=s_ref, dst_ref=sland.at[pidx], send_sem=send_sems.at[1, k - 1],
                recv_sem=recv_sems.at[1, k - 1], device_id=peer, device_id_type=MESH).wait_recv()
        mine.wait()
        for cp in sends:
            cp.wait_send()
        acc = sland[0]
        for s_ in range(1, NDEV):
            acc = acc + sland[s_]
        sred_ref[...] = acc

    vmem = pl.BlockSpec(memory_space=pltpu.VMEM)
    hbm = pl.BlockSpec(memory_space=pl.ANY)
    return pl.pallas_call(
        body, name="scatter_last",
        in_specs=[hbm, vmem],
        out_specs=[hbm, vmem],
        out_shape=[_sds((NDEV, n, D), BF16), _sds((RS_, 128), F32)],
        scratch_shapes=[pltpu.VMEM((NDEV, RS_, 128), F32), pltpu.SemaphoreType.DMA((2, 7)),
                        pltpu.SemaphoreType.DMA((2, 7)), pltpu.SemaphoreType.DMA],
        compiler_params=pltpu.CompilerParams(vmem_limit_bytes=16 << 20))(gsrc, spack)


def _adam_math(w, g, m, v):
    nm = B1 * m + (1.0 - B1) * g
    nv = B2 * v + (1.0 - B2) * jnp.square(g)
    m_hat = nm / (1.0 - B1 ** STEP)
    v_hat = nv / (1.0 - B2 ** STEP)
    return -LR * (m_hat / (jnp.sqrt(v_hat) + AEPS) + WD * w), nm, nv


def _adamw(w, g, m, v, name):
    R, L = w.shape
    tr = next(c for c in (680, 512, 392, 256, 136, 64, 8) if R % c == 0)

    def body(w_ref, g_ref, m_ref, v_ref, d_ref, nm_ref, nv_ref):
        d_ref[...], nm_ref[...], nv_ref[...] = _adam_math(w_ref[...], g_ref[...], m_ref[...], v_ref[...])

    spec = pl.BlockSpec((tr, L), lambda i: (i, 0))
    return pl.pallas_call(
        body, name=name, grid=(R // tr,), in_specs=[spec] * 4, out_specs=[spec] * 3,
        out_shape=[_sds((R, L), F32)] * 3, compiler_params=_cp(48))(w, g, m, v)


def _reduce_adamw(land, w, m, v, name):
    n = land.shape[1]
    tr = next(c for c in (256, 192, 144, 128, 64, 32, 16) if n % c == 0)

    def body(l_ref, w_ref, m_ref, v_ref, g_ref, d_ref, nm_ref, nv_ref):
        g = l_ref[0].astype(F32)
        for s_ in range(1, NDEV):
            g = g + l_ref[s_].astype(F32)
        g_ref[...] = g
        d_ref[...], nm_ref[...], nv_ref[...] = _adam_math(w_ref[...], g, m_ref[...], v_ref[...])

    spec = pl.BlockSpec((tr, D), lambda i: (i, 0))
    return pl.pallas_call(
        body, name=name, grid=(n // tr,),
        in_specs=[pl.BlockSpec((NDEV, tr, D), lambda i: (0, i, 0))] + [spec] * 3, out_specs=[spec] * 4,
        out_shape=[_sds((n, D), F32)] * 4, compiler_params=_cp(48))(land, w, m, v)


BIG = ("conv_w_in", "conv_w_out", "w_qkv", "w_o", "w_up", "w_down")
SMALL = (("conv_norm_g", (1, D), False), ("conv_b_in", (1, 2 * D), False), ("conv_dw", (1, CONVW, D), True),
         ("conv_dw_b", (1, D), False), ("conv_ln_g", (1, D), False), ("conv_ln_b", (1, D), False),
         ("conv_b_out", (1, D), False), ("attn_norm_g", (1, D), True), ("b_qkv", (1, QKV), True),
         ("q_norm_g", (1, HD), False), ("k_norm_g", (1, HD), False), ("sinks", (1, NH), False),
         ("b_o", (1, D), True), ("rel_bias", (NBKT, NH), False), ("mlp_norm_g", (2, D), False))


def _lanes(a):
    flat = a.reshape(-1)
    n = flat.shape[0]
    rows = -(-n // 128)
    return jnp.pad(flat, (0, rows * 128 - n)).reshape(rows, 128)


def _pack_small(arrs):
    rows = jnp.concatenate([_lanes(a) for a in arrs], axis=0)
    pad = -rows.shape[0] % 8
    return jnp.pad(rows, ((0, pad), (0, 0)))


def _unpack_small(pack, shapes):
    out, r = [], 0
    for shp in shapes:
        n = int(np.prod(shp))
        rows = -(-n // 128)
        out.append(pack[r:r + rows].reshape(-1)[:n].reshape(shp))
        r += rows
    return out


def _pack_big(p):
    return jnp.concatenate([p["conv_w_in"].reshape(256, D), p["conv_w_out"].reshape(128, D),
                            p["w_up"][0].reshape(FFB, D), p["w_down"][0], p["w_qkv"].reshape(160, D),
                            p["w_o"].reshape(128, D), p["w_up"][1].reshape(FFB, D), p["w_down"][1]], axis=0)


def _unpack_big(pack):
    o = PACK_OFF
    part = lambda n, r: pack[o[n]:o[n] + r]
    return {"conv_w_in": part("conv_w_in", 256).reshape(1, D, 256), "conv_w_out": part("conv_w_out", 128).reshape(1, 128, D),
            "w_qkv": part("w_qkv", 160).reshape(1, D, 160), "w_o": part("w_o", 128).reshape(1, 128, D),
            "w_up": jnp.stack([part("up0", FFB).reshape(D, FFB), part("up1", FFB).reshape(D, FFB)]),
            "w_down": jnp.stack([part("dn0", FFB), part("dn1", FFB)])}


def _cols_to_blocks(a, n):
    M = a.shape[0]
    return a.reshape(M, NDEV, n).transpose(1, 0, 2).reshape(NDEV, M * n // D, D)


def kernel(x, conv_norm_g, conv_w_in, conv_b_in, conv_dw, conv_dw_b, conv_ln_g, conv_ln_b, conv_w_out, conv_b_out, attn_norm_g, w_qkv, b_qkv, q_norm_g, k_norm_g, sinks, w_o, b_o, rel_bias, mlp_norm_g, w_up, w_down, loss_target, m_conv_norm_g, m_conv_w_in, m_conv_b_in, m_conv_dw, m_conv_dw_b, m_conv_ln_g, m_conv_ln_b, m_conv_w_out, m_conv_b_out, m_attn_norm_g, m_w_qkv, m_b_qkv, m_q_norm_g, m_k_norm_g, m_sinks, m_w_o, m_b_o, m_rel_bias, m_mlp_norm_g, m_w_up, m_w_down, v_conv_norm_g, v_conv_w_in, v_conv_b_in, v_conv_dw, v_conv_dw_b, v_conv_ln_g, v_conv_ln_b, v_conv_w_out, v_conv_b_out, v_attn_norm_g, v_w_qkv, v_b_qkv, v_q_norm_g, v_k_norm_g, v_sinks, v_w_o, v_b_o, v_rel_bias, v_mlp_norm_g, v_w_up, v_w_down):
    names = [n for n, _, _ in SMALL] + list(BIG)
    loc = dict(locals())
    W = {n: loc[n] for n in names}
    M_ = {n: loc["m_" + n] for n in names}
    V_ = {n: loc["v_" + n] for n in names}
    me = 4 * lax.axis_index("x") + 2 * lax.axis_index("y") + lax.axis_index("c")
    xs = x[0]
    S = xs.shape[0]

    sharded = [n for n, _, sh in SMALL if sh]
    wpack = _pack_big(W)
    pack_bf, gw0, gs = _gather_first(wpack, _pack_small([W[n] for n in sharded]))
    w_in = gw0[:, 0:256].reshape(NDEV, D, 256).transpose(1, 0, 2).reshape(D, 2 * D)
    w_out = gw0[:, 256:384].reshape(D, D)
    shard_shapes = {"conv_dw": (CONVW, 128), "attn_norm_g": (1, 128), "b_qkv": (1, 160), "b_o": (1, 128)}
    parts = [_unpack_small(gs[j], [shard_shapes[n] for n in sharded]) for j in range(NDEV)]
    full = {n: jnp.concatenate([parts[j][i] for j in range(NDEV)], axis=-1) for i, n in enumerate(sharded)}
    dw32 = jnp.pad(full["conv_dw"], ((0, 1), (0, 0)))
    attn_g, bqkv, bo = full["attn_norm_g"], full["b_qkv"], full["b_o"]
    qg = jnp.tile(q_norm_g, (1, NH))
    kg = jnp.tile(k_norm_g, (1, NKV))
    e16, e16t, e2, e2t, fold, fold2 = _seg_mats()
    bkt = jnp.asarray(_bucket_table().T)
    bkt4 = jnp.asarray(np.tile(_bucket_table().T, (1, 4)))

    h0, u, glu = _conv_in_fwd(xs, conv_norm_g, w_in, conv_b_in)
    cc, sb, x1, gw1 = _conv_mid_fwd(glu, xs, dw32, conv_dw_b, conv_ln_g, conv_ln_b, w_out, conv_b_out,
                                    carry=_Gather(pack_bf, AG_MLP0))
    wu0, wd0 = gw1[:, 0:FFB].reshape(NDEV, D, FFB), gw1[:, FFB:2 * FFB]
    h1, act0, x2, gw2 = _mlp_fwd(x1, mlp_norm_g[0:1], wu0, wd0, carry=_Gather(pack_bf, AG_ATTN))
    wqkv = gw2[:, 0:160].reshape(NDEV, D, 160).transpose(1, 0, 2).reshape(D, QKV)
    wo = gw2[:, 160:288].reshape(D, D)
    h2, raw, qn, kn, vv = _attn_qkv_fwd(x2, attn_g, wqkv, bqkv, qg, kg, e16, e16t, e2, e2t)
    kp = jnp.pad(kn, ((WIN, 0), (0, 0)))
    vp = jnp.pad(vv, ((WIN, 0), (0, 0)))
    ob, x3, gw3 = _attn_fwd(qn, kp, vp, bkt, rel_bias, sinks, wo, bo, x2, carry=_Gather(pack_bf, AG_MLP1))
    wu1, wd1 = gw3[:, 0:FFB].reshape(NDEV, D, FFB), gw3[:, FFB:2 * FFB]
    h3, act1, dx4, dx4b, sq = _mlp_fwd(x3, mlp_norm_g[1:2], wu1, wd1, target=loss_target[0])
    loss = lax.psum(jnp.sum(sq) * (0.5 / D), ("x", "y", "c"))

    dup1, dx3, dx3b, dg_mlp1 = _mlp_bwd(dx4, x3, mlp_norm_g[1:2], act1, wu1, wd1)
    g_wdn1 = _wgrad(act1, dx4b, 512, D, "wgrad_down1")
    g_wup1 = _wgrad(h3, dup1, D, 512, "wgrad_up1")
    src_mlp1 = jnp.concatenate([_cols_to_blocks(g_wup1, FFB), g_wdn1.reshape(NDEV, FFB, D)], axis=1)
    dqs, dkp, dvp, d_sinks, d_rel, d_bo, land_mlp1 = _attn_bwd(dx3, dx3b, qn, kp, vp, bkt, bkt4, rel_bias, sinks, wo,
                                                             carry=_Scatter(src_mlp1))
    g_wo = _wgrad(ob, dx3b, D, 512, "wgrad_o")
    dqkv, dx2, dx2b, d_bqkv, d_attn_g, d_qg, d_kg = _attn_qkv_bwd(
        dqs, dkp[WIN:], dvp[WIN:], raw, x2, dx3, attn_g, wqkv, qg, kg, e16, e16t, e2, e2t, fold, fold2)
    g_wqkv = _wgrad(h2, dqkv, D, 640, "wgrad_qkv")
    src_attn = jnp.concatenate([_cols_to_blocks(g_wqkv, 160), g_wo.reshape(NDEV, 128, D)], axis=1)
    dup0, dx1, dx1b, dg_mlp0, land_attn = _mlp_bwd(dx2, x1, mlp_norm_g[0:1], act0, wu0, wd0,
                                                   carry=_Scatter(src_attn))
    g_wdn0 = _wgrad(act0, dx2b, 512, D, "wgrad_down0")
    g_wup0 = _wgrad(h1, dup0, D, 512, "wgrad_up0")
    dc, d_bout, d_lng, d_lnb, d_dwb = _conv_mid_bwd(dx1, dx1b, cc, conv_ln_g, conv_ln_b, w_out)
    g_wout = _wgrad(sb, dx1b, D, 512, "wgrad_conv_out")
    src_mid = jnp.concatenate([g_wout.reshape(NDEV, 128, D), _cols_to_blocks(g_wup0, FFB),
                               g_wdn0.reshape(NDEV, FFB, D)], axis=1)
    dglu, d_dw, land_mid = _dwconv_bwd(dc, glu, dw32, carry=_Scatter(src_mid))
    du, grad_x, d_bin, d_cng = _conv_in_bwd(dglu, u, xs, dx1, conv_norm_g, w_in)
    g_win = _wgrad(h0, du, D, 512, "wgrad_conv_in")

    small_grads = {
        "conv_norm_g": d_cng, "conv_b_in": d_bin, "conv_dw": d_dw[:CONVW][None], "conv_dw_b": d_dwb,
        "conv_ln_g": d_lng, "conv_ln_b": d_lnb, "conv_b_out": d_bout, "attn_norm_g": d_attn_g, "b_qkv": d_bqkv,
        "q_norm_g": d_qg[:, :HD], "k_norm_g": d_kg[:, :HD], "sinks": d_sinks[:, :NH], "b_o": d_bo,
        "rel_bias": d_rel[:, :NH], "mlp_norm_g": jnp.concatenate([dg_mlp0, dg_mlp1], axis=0)}
    land_last, sred = _scatter_last(_cols_to_blocks(g_win, 256), _pack_small([small_grads[n] for n, _, _ in SMALL]))
    sg_full = _unpack_small(sred, [shp for _, shp, _ in SMALL])

    mpack, vpack = _pack_big(M_), _pack_big(V_)
    pieces = []
    for land, (r0, n), nm in ((land_last, RS_LAST, "adamw_conv_in"), (land_mid, RS_MID, "adamw_mid"),
                              (land_attn, RS_ATTN, "adamw_attn"), (land_mlp1, RS_MLP1, "adamw_mlp1")):
        pieces.append(_reduce_adamw(land, wpack[r0:r0 + n], mpack[r0:r0 + n], vpack[r0:r0 + n], nm))
    red, d_big, m_big, v_big = [jnp.concatenate([p[k] for p in pieces], axis=0) for k in range(4)]
    G = _unpack_big(red)
    for (n, shp, sh), gfull in zip(SMALL, sg_full):
        if sh:
            width = shp[-1] // NDEV
            G[n] = lax.dynamic_slice_in_dim(gfull, me * width, width, axis=gfull.ndim - 1)
        else:
            G[n] = gfull
    snames = [n for n, _, _ in SMALL]
    d_sm, m_sm, v_sm = _adamw(*[_pack_small([t[n] for n in snames]) for t in (W, G, M_, V_)], "adamw_small")
    sshapes = [W[n].shape for n in snames]
    outs = []
    for big, small in ((d_big, d_sm), (m_big, m_sm), (v_big, v_sm)):
        o = _unpack_big(big)
        o.update(dict(zip(snames, _unpack_small(small, sshapes))))
        outs.append(o)
    order = ["conv_norm_g", "conv_w_in", "conv_b_in", "conv_dw", "conv_dw_b", "conv_ln_g", "conv_ln_b", "conv_w_out",
             "conv_b_out", "attn_norm_g", "w_qkv", "b_qkv", "q_norm_g", "k_norm_g", "sinks", "w_o", "b_o", "rel_bias",
             "mlp_norm_g", "w_up", "w_down"]
    return (loss, grad_x[None], *[G[n] for n in order], *[outs[0][n] for n in order],
            *[outs[1][n] for n in order], *[outs[2][n] for n in order])
```

```python
import math

import numpy as np
import jax
import jax.numpy as jnp
from jax import lax
from jax.experimental import pallas as pl
from jax.experimental.pallas import tpu as pltpu

F32 = jnp.float32
BF16 = jnp.bfloat16

D = 1024
DFF = 4096
NH, NKV, HD = 16, 2, 64
NPAIR = NH // 2
KV = NKV * HD
QKV = D + 2 * KV
CONVW = 31
WIN = 128
NBKT = 32
EPS = 1e-6
NEG = -1e30
NDEV = 8
FFB = DFF // NDEV
HALO = 32

LR, B1, B2, AEPS, WD, STEP = 0.001, 0.9, 0.999, 1e-08, 0.01, 10

MESH = pl.DeviceIdType.MESH


def _cp(vmem_mb, n_axes=1):
    return pltpu.CompilerParams(dimension_semantics=("arbitrary",) * n_axes, vmem_limit_bytes=vmem_mb << 20)


def _const(shape):
    nd = len(shape)
    return pl.BlockSpec(shape, lambda *_: (0,) * nd, pipeline_mode=pl.Buffered(1))


def _acc(shape):
    nd = len(shape)
    return pl.BlockSpec(shape, lambda *_: (0,) * nd)


def _rows(tm, n):
    return pl.BlockSpec((tm, n), lambda i: (i, 0))


def _sds(shape, dtype):
    return jax.ShapeDtypeStruct(shape, dtype)


def _dot(a, b):
    return jnp.dot(a, b, preferred_element_type=F32)


def _dot_nt(a, b):
    return lax.dot_general(a, b, (((1,), (1,)), ((), ())), preferred_element_type=F32)


def _dot_tn(a, b):
    return lax.dot_general(a, b, (((0,), (0,)), ((), ())), preferred_element_type=F32)


def _dot_hi(x, e):
    x1 = x.astype(BF16)
    x2 = (x - x1.astype(F32)).astype(BF16)
    return _dot(x1, e) + _dot(x2, e)


def _rms(x):
    return lax.rsqrt(jnp.mean(x * x, axis=-1, keepdims=True) + EPS)


def _rms_bwd(dh, x, r, g):
    xh = x * r
    dxh = dh * g
    dx = r * (dxh - xh * jnp.mean(dxh * xh, axis=-1, keepdims=True))
    return dx, jnp.sum(dh * xh, axis=0, keepdims=True)


def _colsum(a):
    return jnp.sum(a, axis=0, keepdims=True)


def _first(ref, val):
    @pl.when(pl.program_id(0) == 0)
    def _():
        ref[...] = jnp.zeros_like(ref)
    ref[...] += val


def _first_cols(ref, vals):
    @pl.when(pl.program_id(0) == 0)
    def _():
        ref[...] = jnp.zeros_like(ref)
    for c0, val in vals:
        ref[:, c0:c0 + val.shape[1]] += val


def _conv_in_fwd(x, g, w, b):
    S = x.shape[0]
    tm = min(512, S)

    def body(x_ref, g_ref, w_ref, b_ref, h_ref, u_ref, glu_ref):
        xv = x_ref[...]
        hb = (xv * _rms(xv) * g_ref[...]).astype(BF16)
        h_ref[...] = hb
        u = _dot(hb, w_ref[...]) + b_ref[...]
        u_ref[...] = u
        glu_ref[...] = u[:, :D] * jax.nn.sigmoid(u[:, D:])

    return pl.pallas_call(
        body, name="conv_in_fwd", grid=(S // tm,),
        in_specs=[_rows(tm, D), _const((1, D)), _const((D, 2 * D)), _const((1, 2 * D))],
        out_specs=[_rows(tm, D), _rows(tm, 2 * D), _rows(tm, D)],
        out_shape=[_sds((S, D), BF16), _sds((S, 2 * D), F32), _sds((S, D), F32)],
        compiler_params=_cp(48))(x, g, w, b)


def _fill_ext(ext, halo_ref, cur_ref, tm, keep):
    ext[0:HALO, :] = jnp.where(keep, halo_ref[...], 0.0)
    ext[HALO:HALO + tm, :] = cur_ref[...]


def _shift_copies(dst, src, n):
    for s_ in range(1, 8):
        dst[s_ - 1, 0:n, :] = src[s_:s_ + n, :]


def _tap(src, sh, o, r0, c0):
    a, s_ = divmod(o, 8)
    ref = src if s_ == 0 else sh.at[s_ - 1]
    return ref[r0 + 8 * a:r0 + 8 * a + 32, c0:c0 + 512]


def _conv_mid_fwd(glu, x, dw, dwb, lng, lnb, w, b, carry=None):
    S = x.shape[0]
    tm = min(256, S)
    hb = tm // HALO

    def body(gl_ref, halo_ref, x_ref, dw_ref, dwb_ref, lng_ref, lnb_ref, w_ref, b_ref, c_ref, s_ref, xo_ref, ext, esh):
        i = pl.program_id(0)
        _fill_ext(ext, halo_ref, gl_ref, tm, i > 0)
        _shift_copies(esh, ext, tm + 24)
        for r0 in range(0, tm, 32):
            for c0 in range(0, D, 512):
                acc = jnp.zeros((32, 512), F32) + dwb_ref[:, c0:c0 + 512]
                for k in range(CONVW):
                    acc = acc + dw_ref[k:k + 1, c0:c0 + 512] * _tap(ext, esh, 2 + k, r0, c0)
                c_ref[r0:r0 + 32, c0:c0 + 512] = acc
        c = c_ref[...]
        mu = jnp.mean(c, axis=-1, keepdims=True)
        xc = c - mu
        y = xc * lax.rsqrt(jnp.mean(xc * xc, axis=-1, keepdims=True) + EPS) * lng_ref[...] + lnb_ref[...]
        sb = (y * jax.nn.sigmoid(y)).astype(BF16)
        s_ref[...] = sb
        xo_ref[...] = x_ref[...] + _dot(sb, w_ref[...]) + b_ref[...]

    return _carried_call(
        body, carry, name="conv_mid_fwd", grid=(S // tm,),
        in_specs=[_rows(tm, D), pl.BlockSpec((HALO, D), lambda i: (jnp.maximum(i * hb - 1, 0), 0)), _rows(tm, D),
                  _const((32, D)), _const((1, D)), _const((1, D)), _const((1, D)), _const((D, D)), _const((1, D))],
        out_specs=[_rows(tm, D), _rows(tm, D), _rows(tm, D)],
        out_shape=[_sds((S, D), F32), _sds((S, D), BF16), _sds((S, D), F32)],
        scratch_shapes=[pltpu.VMEM((tm + HALO, D), F32), pltpu.VMEM((7, tm + 24, D), F32)],
        compiler_params=_cp(48), args=[glu, glu, x, dw, dwb, lng, lnb, w, b])


def _mlp_fwd(x, g, wu, wd, target=None, carry=None):
    S = x.shape[0]
    tm = min(512, S)
    last = target is not None

    def body(*refs):
        if last:
            x_ref, g_ref, wu_ref, wd_ref, t_ref, h_ref, act_ref, dy_ref, dyb_ref, sq_ref, acc = refs
        else:
            x_ref, g_ref, wu_ref, wd_ref, h_ref, act_ref, xo_ref, acc = refs
        xv = x_ref[...]
        hb = (xv * _rms(xv) * g_ref[...]).astype(BF16)
        h_ref[...] = hb
        for j in range(NDEV):
            up = _dot(hb[:, :FFB], wu_ref[j, :, :FFB]) + _dot(hb[:, FFB:], wu_ref[j, :, FFB:])
            a = jnp.square(jnp.maximum(up, 0.0)).astype(BF16)
            act_ref[:, j * FFB:(j + 1) * FFB] = a
            if j == 0:
                acc[...] = _dot(a, wd_ref[j])
            else:
                acc[...] += _dot(a, wd_ref[j])
        y = xv + acc[...]
        if last:
            diff = y - t_ref[...]
            dy = diff * (1.0 / D)
            dy_ref[...] = dy
            dyb_ref[...] = dy.astype(BF16)
            _first(sq_ref, _colsum(diff * diff))
        else:
            xo_ref[...] = y

    in_specs = [_rows(tm, D), _const((1, D)), _const((NDEV, FFB, D)), _const((NDEV, FFB, D))]
    args = [x, g, wu, wd]
    out_specs = [_rows(tm, D), _rows(tm, DFF)]
    out_shape = [_sds((S, D), BF16), _sds((S, DFF), BF16)]
    if last:
        in_specs.append(_rows(tm, D))
        args.append(target)
        out_specs += [_rows(tm, D), _rows(tm, D), _acc((1, D))]
        out_shape += [_sds((S, D), F32), _sds((S, D), BF16), _sds((1, D), F32)]
    else:
        out_specs.append(_rows(tm, D))
        out_shape.append(_sds((S, D), F32))
    return _carried_call(
        body, carry, name="mlp_fwd_loss" if last else "mlp_fwd", grid=(S // tm,),
        in_specs=in_specs, out_specs=out_specs, out_shape=out_shape,
        scratch_shapes=[pltpu.VMEM((tm, D), F32)],
        compiler_params=_cp(52), args=args)


def _mlp_bwd(dy, x, g, act, wu, wd, carry=None):
    S = x.shape[0]
    tm = min(512, S)

    def body(dy_ref, x_ref, g_ref, act_ref, wu_ref, wd_ref, dup_ref, dx_ref, dxb_ref, dg_ref, acc):
        db = dy_ref[...].astype(BF16)
        for j in range(NDEV):
            dact = _dot_nt(db, wd_ref[j])
            a = act_ref[:, j * FFB:(j + 1) * FFB].astype(F32)
            dup = (dact * (2.0 * jnp.sqrt(a))).astype(BF16)
            dup_ref[:, j * FFB:(j + 1) * FFB] = dup
            for h_ in range(2):
                part = _dot_nt(dup, wu_ref[j, :, h_ * FFB:(h_ + 1) * FFB])
                if j == 0:
                    acc[:, h_ * FFB:(h_ + 1) * FFB] = part
                else:
                    acc[:, h_ * FFB:(h_ + 1) * FFB] += part
        xv = x_ref[...]
        dxn, dg = _rms_bwd(acc[...], xv, _rms(xv), g_ref[...])
        dx = dy_ref[...] + dxn
        dx_ref[...] = dx
        dxb_ref[...] = dx.astype(BF16)
        _first(dg_ref, dg)

    return _carried_call(
        body, carry, name="mlp_bwd", grid=(S // tm,),
        in_specs=[_rows(tm, D), _rows(tm, D), _const((1, D)), _rows(tm, DFF),
                  _const((NDEV, FFB, D)), _const((NDEV, FFB, D))],
        out_specs=[_rows(tm, DFF), _rows(tm, D), _rows(tm, D), _acc((1, D))],
        out_shape=[_sds((S, DFF), BF16), _sds((S, D), F32), _sds((S, D), BF16), _sds((1, D), F32)],
        scratch_shapes=[pltpu.VMEM((tm, D), F32)],
        compiler_params=_cp(58), args=[dy, x, g, act, wu, wd])


def _wgrad(a, b, tm, tn, name):
    S, M = a.shape
    N = b.shape[1]
    tk = min(4096, S)
    nk = S // tk

    def body(a_ref, b_ref, o_ref, acc):
        k = pl.program_id(2)

        @pl.when(k == 0)
        def _():
            acc[...] = jnp.zeros_like(acc)

        acc[...] += _dot_tn(a_ref[...], b_ref[...])

        @pl.when(k == nk - 1)
        def _():
            o_ref[...] = acc[...].astype(BF16)

    return pl.pallas_call(
        body, name=name, grid=(M // tm, N // tn, nk),
        in_specs=[pl.BlockSpec((tk, tm), lambda i, j, k: (k, i)), pl.BlockSpec((tk, tn), lambda i, j, k: (k, j))],
        out_specs=pl.BlockSpec((tm, tn), lambda i, j, k: (i, j)),
        out_shape=_sds((M, N), BF16),
        scratch_shapes=[pltpu.VMEM((tm, tn), F32)],
        compiler_params=_cp(48, 3))(a, b)


def _wgrad_split(a, b, nsplit, name):
    S, M = a.shape
    N = b.shape[1]
    wb = N // NDEV
    tm, tn = M // nsplit, D
    nb = tn // wb
    tk = min(4096, S)
    nk = S // tk

    def body(a_ref, b_ref, o_ref, acc):
        k = pl.program_id(2)

        @pl.when(k == 0)
        def _():
            acc[...] = jnp.zeros_like(acc)

        acc[...] += _dot_tn(a_ref[...], b_ref[...])

        @pl.when(k == nk - 1)
        def _():
            for jj in range(nb):
                o_ref[jj] = acc[:, jj * wb:(jj + 1) * wb].astype(BF16)

    return pl.pallas_call(
        body, name=name, grid=(nsplit, N // tn, nk),
        in_specs=[pl.BlockSpec((tk, tm), lambda i, j, k: (k, i)), pl.BlockSpec((tk, tn), lambda i, j, k: (k, j))],
        out_specs=pl.BlockSpec((nb, tm, wb), lambda i, j, k: (j, 0, i)),
        out_shape=_sds((NDEV, tm, nsplit * wb), BF16),
        scratch_shapes=[pltpu.VMEM((tm, tn), F32)],
        compiler_params=_cp(48, 3))(a, b)


def _bucket_table():
    q = np.arange(WIN)[:, None]
    k = np.arange(2 * WIN)[None, :]
    dist = q + WIN - k
    n = np.maximum(dist, 0)
    max_exact = NBKT // 2
    large = max_exact + (np.log(np.maximum(n, 1).astype(np.float32) / max_exact)
                         / math.log(WIN / max_exact) * (NBKT - max_exact)).astype(np.int32)
    large = np.minimum(large, NBKT - 1)
    bkt = np.where(n < max_exact, n, large).astype(np.int32)
    return np.where((dist >= 0) & (dist < WIN), bkt, -1).astype(np.int32)


def _seg_mats():
    e16 = np.zeros((D, 128), np.float32)
    e16[np.arange(D), np.arange(D) // HD] = 1.0
    e2 = np.zeros((KV, 128), np.float32)
    e2[np.arange(KV), np.arange(KV) // HD] = 1.0
    fold = np.zeros((D, 128), np.float32)
    fold[np.arange(D), np.arange(D) % HD] = 1.0
    fold2 = np.zeros((KV, 128), np.float32)
    fold2[np.arange(KV), np.arange(KV) % HD] = 1.0
    return [jnp.asarray(m, BF16) for m in (e16, e16.T, e2, e2.T, fold, fold2)]


def _head_rms(t, e, et):
    r = lax.rsqrt(_dot_hi(t * t, e) * (1.0 / HD) + EPS)
    return _dot_hi(r, et)


def _attn_qkv_fwd(x, g, w, b, qg, kg, e16, e16t, e2, e2t):
    S = x.shape[0]
    tm = min(512, S)

    def body(x_ref, g_ref, w_ref, b_ref, qg_ref, kg_ref, e16_ref, e16t_ref, e2_ref, e2t_ref,
             h_ref, raw_ref, qn_ref, kn_ref, v_ref):
        xv = x_ref[...]
        hb = (xv * _rms(xv) * g_ref[...]).astype(BF16)
        h_ref[...] = hb
        raw = _dot(hb, w_ref[...]) + b_ref[...]
        raw_ref[...] = raw
        q = raw[:, :D]
        k = raw[:, D:D + KV]
        qn = (q * _head_rms(q, e16_ref[...], e16t_ref[...]) * qg_ref[...] * 0.125).astype(BF16)
        for bb in range(tm // WIN):
            for p in range(NPAIR):
                r = (bb * NPAIR + p) * WIN
                qn_ref[r:r + WIN, :] = qn[bb * WIN:(bb + 1) * WIN, p * 128:(p + 1) * 128]
        kn_ref[...] = (k * _head_rms(k, e2_ref[...], e2t_ref[...]) * kg_ref[...]).astype(BF16)
        v_ref[...] = raw[:, D + KV:].astype(BF16)

    return pl.pallas_call(
        body, name="attn_qkv_fwd", grid=(S // tm,),
        in_specs=[_rows(tm, D), _const((1, D)), _const((D, QKV)), _const((1, QKV)), _const((1, D)), _const((1, KV)),
                  _const((D, 128)), _const((128, D)), _const((KV, 128)), _const((128, KV))],
        out_specs=[_rows(tm, D), _rows(tm, QKV), _rows(tm * NPAIR, 128), _rows(tm, KV), _rows(tm, KV)],
        out_shape=[_sds((S, D), BF16), _sds((S, QKV), F32), _sds((S * NPAIR, 128), BF16), _sds((S, KV), BF16),
                   _sds((S, KV), BF16)],
        compiler_params=_cp(48))(x, g, w, b, qg, kg, e16, e16t, e2, e2t)


def _build_bias(bkt_ref, rb_ref, sk_ref, bias_sc, sk_sc):
    bkt = bkt_ref[...]
    row = lax.broadcasted_iota(jnp.int32, (2 * WIN, WIN), 0)
    for h in range(NH):
        p, e = h // 2, h % 2
        g, pp = p // 4, p % 4

        def add(bk, acc, h=h):
            return acc + jnp.where(bkt == bk, rb_ref[bk, h], 0.0)

        bias = lax.fori_loop(0, NBKT, add, jnp.where(bkt < 0, NEG, 0.0).astype(F32))
        bias_sc[0, g, e, :, pp * WIN:(pp + 1) * WIN] = bias
        bias_sc[1, g, e, :, pp * WIN:(pp + 1) * WIN] = jnp.where(row < WIN, NEG, bias)
        sk_sc[g, e, :, pp * WIN:(pp + 1) * WIN] = jnp.zeros((1, WIN), F32) + sk_ref[0, h]


def _stacks(f, mlo):
    r = pltpu.roll(f, HD, 1)
    z = jnp.zeros_like(f)
    return ((jnp.where(mlo, f, z).astype(BF16), jnp.where(mlo, z, r).astype(BF16)),
            (jnp.where(mlo, r, z).astype(BF16), jnp.where(mlo, z, f).astype(BF16)))


def _unstack(d, mlo):
    z = jnp.zeros_like(d[0][0])
    return (jnp.where(mlo, d[0][0], z) + pltpu.roll(jnp.where(mlo, z, d[0][1]), HD, 1)
            + pltpu.roll(jnp.where(mlo, d[1][0], z), HD, 1) + jnp.where(mlo, z, d[1][1]))


def _softmax_sink(l, sk):
    m = jnp.maximum(jnp.max(l, axis=0, keepdims=True), sk)
    ex = jnp.exp(l - m)
    es = jnp.exp(sk - m)
    inv = 1.0 / (jnp.sum(ex, axis=0, keepdims=True) + es)
    return ex * inv, es * inv


GROWS = 4 * WIN


def _attn_fwd(qn, kp, vp, bktt, rel_bias, sinks, wo, bo, x, carry=None):
    S = x.shape[0]
    tq = min(512, S)
    nblk = tq // WIN

    def body(q_ref, k_ref, v_ref, bkt_ref, rb_ref, sk_ref, wo_ref, bo_ref, x_ref, o_ref, xo_ref, bias_sc, sk_sc):
        i = pl.program_id(0)

        @pl.when(i == 0)
        def _():
            _build_bias(bkt_ref, rb_ref, sk_ref, bias_sc, sk_sc)

        mlo = lax.broadcasted_iota(jnp.int32, (2 * WIN, KV), 1) < HD

        def blk(bb, carry):
            r0 = pl.multiple_of(bb * WIN, WIN)
            g0 = pl.multiple_of(i * tq + bb * WIN, WIN)
            first = (g0 == 0).astype(jnp.int32)
            ks = _stacks(k_ref[pl.ds(g0, 2 * WIN), :].astype(F32), mlo)
            vs = _stacks(v_ref[pl.ds(g0, 2 * WIN), :].astype(F32), mlo)
            for g in range(NKV):
                qs = q_ref[pl.ds(pl.multiple_of(bb * (NPAIR * WIN) + g * GROWS, GROWS), GROWS), :]
                og = jnp.zeros((GROWS, 128), F32)
                for e in range(2):
                    l = _dot_nt(ks[g][e], qs) + bias_sc[first, g, e]
                    pr, _ = _softmax_sink(l, sk_sc[g, e])
                    og = og + _dot_tn(pr.astype(BF16), vs[g][e])
                for pp in range(4):
                    p = 4 * g + pp
                    o_ref[pl.ds(r0, WIN), p * 128:(p + 1) * 128] = og[pp * WIN:(pp + 1) * WIN].astype(BF16)
            return carry

        lax.fori_loop(0, nblk, blk, 0)
        xo_ref[...] = x_ref[...] + _dot(o_ref[...], wo_ref[...]) + bo_ref[...]

    smem = pl.BlockSpec(memory_space=pltpu.SMEM)
    return _carried_call(
        body, carry, name="attn_fwd", grid=(S // tq,),
        in_specs=[_rows(tq * NPAIR, 128), _const((S + WIN, KV)), _const((S + WIN, KV)), _const((2 * WIN, WIN)), smem, smem,
                  _const((D, D)), _const((1, D)), _rows(tq, D)],
        out_specs=[_rows(tq, D), _rows(tq, D)],
        out_shape=[_sds((S, D), BF16), _sds((S, D), F32)],
        scratch_shapes=[pltpu.VMEM((2, NKV, 2, 2 * WIN, GROWS), F32), pltpu.VMEM((NKV, 2, 1, GROWS), F32)],
        compiler_params=_cp(48), args=[qn, kp, vp, bktt, rel_bias, sinks, wo, bo, x])


def _attn_bwd(dx, dxb, qn, kp, vp, bktt, bktt4, rel_bias, sinks, wo, carry=None):
    S = dx.shape[0]
    tq = min(512, S)
    nblk = tq // WIN
    nsteps = S // tq

    def body(dx_ref, dxb_ref, q_ref, k_ref, v_ref, bkt_ref, bkt4_ref, rb_ref, sk_ref, wo_ref,
             dq_ref, dk_ref, dv_ref, dsk_ref, drb_ref, dbo_ref, bias_sc, sk_sc, dbias_sc, dsk_sc, do_sc):
        i = pl.program_id(0)

        @pl.when(i == 0)
        def _():
            _build_bias(bkt_ref, rb_ref, sk_ref, bias_sc, sk_sc)
            dbias_sc[...] = jnp.zeros_like(dbias_sc)
            dsk_sc[...] = jnp.zeros_like(dsk_sc)
            dk_ref[...] = jnp.zeros_like(dk_ref)
            dv_ref[...] = jnp.zeros_like(dv_ref)

        _first(dbo_ref, _colsum(dx_ref[...]))
        do = _dot_nt(dxb_ref[...], wo_ref[...]).astype(BF16)
        for bb in range(nblk):
            for p in range(NPAIR):
                r = (bb * NPAIR + p) * WIN
                do_sc[r:r + WIN, :] = do[bb * WIN:(bb + 1) * WIN, p * 128:(p + 1) * 128]

        mlo = lax.broadcasted_iota(jnp.int32, (2 * WIN, KV), 1) < HD

        def blk(bb, carry):
            g0 = pl.multiple_of(i * tq + bb * WIN, WIN)
            first = (g0 == 0).astype(jnp.int32)
            ks = _stacks(k_ref[pl.ds(g0, 2 * WIN), :].astype(F32), mlo)
            vs = _stacks(v_ref[pl.ds(g0, 2 * WIN), :].astype(F32), mlo)
            dks = [[None, None], [None, None]]
            dvs = [[None, None], [None, None]]
            for g in range(NKV):
                rows = pl.ds(pl.multiple_of(bb * (NPAIR * WIN) + g * GROWS, GROWS), GROWS)
                qs = q_ref[rows, :]
                dos = do_sc[rows, :]
                dqs = jnp.zeros((GROWS, 128), F32)
                for e in range(2):
                    l = _dot_nt(ks[g][e], qs) + bias_sc[first, g, e]
                    pr, ps = _softmax_sink(l, sk_sc[g, e])
                    dp = _dot_nt(vs[g][e], dos)
                    dr = jnp.sum(pr * dp, axis=0, keepdims=True)
                    dl = pr * (dp - dr)
                    dsk_sc[g, e] -= ps * dr
                    dbias_sc[g, e] += dl
                    dlb = dl.astype(BF16)
                    dqs = dqs + _dot_tn(dlb, ks[g][e])
                    dks[g][e] = _dot(dlb, qs)
                    dvs[g][e] = _dot(pr.astype(BF16), dos)
                dq_ref[rows, :] = dqs
            dk_ref[pl.ds(g0, 2 * WIN), :] += _unstack(dks, mlo)
            dv_ref[pl.ds(g0, 2 * WIN), :] += _unstack(dvs, mlo)
            return carry

        lax.fori_loop(0, nblk, blk, 0)

        @pl.when(i == nsteps - 1)
        def _():
            bkt4v = bkt4_ref[...]
            rid = lax.broadcasted_iota(jnp.int32, (NBKT, 128), 0)
            lid = lax.broadcasted_iota(jnp.int32, (NBKT, 128), 1)
            lid1 = lax.broadcasted_iota(jnp.int32, (1, 128), 1)
            dsk = jnp.zeros((1, 128), F32)
            for g in range(NKV):
                for e in range(2):
                    for pp in range(4):
                        h = 2 * (4 * g + pp) + e
                        t = jnp.sum(dsk_sc[g, e, :, pp * WIN:(pp + 1) * WIN], axis=1, keepdims=True)
                        dsk = dsk + jnp.where(lid1 == h, t, 0.0)
            dsk_ref[...] = dsk

            def per_bucket(bk, acc):
                mb = bkt4v == bk
                for g in range(NKV):
                    for e in range(2):
                        t = jnp.sum(jnp.where(mb, dbias_sc[g, e], 0.0), axis=0, keepdims=True)
                        for pp in range(4):
                            h = 2 * (4 * g + pp) + e
                            tt = jnp.sum(t[:, pp * WIN:(pp + 1) * WIN], axis=1, keepdims=True)
                            acc = acc + jnp.where((rid == bk) & (lid == h), tt, 0.0)
                return acc

            drb_ref[...] = lax.fori_loop(0, NBKT, per_bucket, jnp.zeros((NBKT, 128), F32))

    smem = pl.BlockSpec(memory_space=pltpu.SMEM)
    return _carried_call(
        body, carry, name="attn_bwd", grid=(nsteps,),
        in_specs=[_rows(tq, D), _rows(tq, D), _rows(tq * NPAIR, 128), _const((S + WIN, KV)), _const((S + WIN, KV)),
                  _const((2 * WIN, WIN)), _const((2 * WIN, GROWS)), smem, smem, _const((D, D))],
        out_specs=[_rows(tq * NPAIR, 128), _acc((S + WIN, KV)), _acc((S + WIN, KV)), _acc((1, 128)), _acc((NBKT, 128)),
                   _acc((1, D))],
        out_shape=[_sds((S * NPAIR, 128), F32), _sds((S + WIN, KV), F32), _sds((S + WIN, KV), F32), _sds((1, 128), F32),
                   _sds((NBKT, 128), F32), _sds((1, D), F32)],
        scratch_shapes=[pltpu.VMEM((2, NKV, 2, 2 * WIN, GROWS), F32), pltpu.VMEM((NKV, 2, 1, GROWS), F32),
                        pltpu.VMEM((NKV, 2, 2 * WIN, GROWS), F32), pltpu.VMEM((NKV, 2, 1, GROWS), F32),
                        pltpu.VMEM((tq * NPAIR, 128), BF16)],
        compiler_params=_cp(52), args=[dx, dxb, qn, kp, vp, bktt, bktt4, rel_bias, sinks, wo])


def _head_norm_bwd(dn, t, e, et, gt):
    r = _head_rms(t, e, et)
    th = t * r
    dth = dn * gt
    dt = r * (dth - th * _dot_hi(_dot_hi(dth * th, e) * (1.0 / HD), et))
    return dt, _colsum(dn * th)


def _attn_qkv_bwd(dqs, dk, dv, raw, x, dxo, g, w, qg, kg, e16, e16t, e2, e2t, fold, fold2):
    S = x.shape[0]
    tm = min(512, S)

    def body(dq_ref, dk_ref, dv_ref, raw_ref, x_ref, dxo_ref, g_ref, w_ref, qg_ref, kg_ref,
             e16_ref, e16t_ref, e2_ref, e2t_ref, fold_ref, fold2_ref,
             dqkv_ref, dx_ref, dxb_ref, db_ref, dg_ref, dqg_ref, dkg_ref, dq_sc):
        for bb in range(tm // WIN):
            for p in range(NPAIR):
                r = (bb * NPAIR + p) * WIN
                dq_sc[bb * WIN:(bb + 1) * WIN, p * 128:(p + 1) * 128] = dq_ref[r:r + WIN, :]
        dq, cq = _head_norm_bwd(dq_sc[...] * 0.125, raw_ref[:, :D], e16_ref[...], e16t_ref[...], qg_ref[...])
        dk_, ck = _head_norm_bwd(dk_ref[...], raw_ref[:, D:D + KV], e2_ref[...], e2t_ref[...], kg_ref[...])
        dv_ = dv_ref[...]
        _first(dqg_ref, _dot_hi(cq, fold_ref[...]))
        _first(dkg_ref, _dot_hi(ck, fold2_ref[...]))
        _first_cols(db_ref, [(0, _colsum(dq)), (D, _colsum(dk_)), (D + KV, _colsum(dv_))])
        dqb, dkb, dvb = dq.astype(BF16), dk_.astype(BF16), dv_.astype(BF16)
        dqkv_ref[:, :D] = dqb
        dqkv_ref[:, D:D + KV] = dkb
        dqkv_ref[:, D + KV:] = dvb
        dh = (_dot_nt(dqb, w_ref[:, :D]) + _dot_nt(dkb, w_ref[:, D:D + KV]) + _dot_nt(dvb, w_ref[:, D + KV:]))
        xv = x_ref[...]
        dxn, dg = _rms_bwd(dh, xv, _rms(xv), g_ref[...])
        dx = dxo_ref[...] + dxn
        dx_ref[...] = dx
        dxb_ref[...] = dx.astype(BF16)
        _first(dg_ref, dg)

    return pl.pallas_call(
        body, name="attn_qkv_bwd", grid=(S // tm,),
        in_specs=[_rows(tm * NPAIR, 128), _rows(tm, KV), _rows(tm, KV), _rows(tm, QKV), _rows(tm, D), _rows(tm, D),
                  _const((1, D)), _const((D, QKV)), _const((1, D)), _const((1, KV)),
                  _const((D, 128)), _const((128, D)), _const((KV, 128)), _const((128, KV)),
                  _const((D, 128)), _const((KV, 128))],
        out_specs=[_rows(tm, QKV), _rows(tm, D), _rows(tm, D), _acc((1, QKV)), _acc((1, D)), _acc((1, 128)), _acc((1, 128))],
        out_shape=[_sds((S, QKV), BF16), _sds((S, D), F32), _sds((S, D), BF16), _sds((1, QKV), F32), _sds((1, D), F32),
                   _sds((1, 128), F32), _sds((1, 128), F32)],
        scratch_shapes=[pltpu.VMEM((tm, D), F32)],
        compiler_params=_cp(48))(dqs, dk, dv, raw, x, dxo, g, w, qg, kg, e16, e16t, e2, e2t, fold, fold2)


def _conv_mid_bwd(dx, dxb, c, lng, lnb, w):
    S = dx.shape[0]
    tm = min(512, S)

    def body(dx_ref, dxb_ref, c_ref, lng_ref, lnb_ref, w_ref, dc_ref, dbo_ref, dlg_ref, dlb_ref, ddwb_ref):
        _first(dbo_ref, _colsum(dx_ref[...]))
        ds = _dot_nt(dxb_ref[...], w_ref[...])
        c = c_ref[...]
        xc = c - jnp.mean(c, axis=-1, keepdims=True)
        rstd = lax.rsqrt(jnp.mean(xc * xc, axis=-1, keepdims=True) + EPS)
        ch = xc * rstd
        y = ch * lng_ref[...] + lnb_ref[...]
        sg = jax.nn.sigmoid(y)
        dy = ds * (sg * (1.0 + y * (1.0 - sg)))
        _first(dlg_ref, _colsum(dy * ch))
        _first(dlb_ref, _colsum(dy))
        dch = dy * lng_ref[...]
        dc = rstd * (dch - jnp.mean(dch, axis=-1, keepdims=True) - ch * jnp.mean(dch * ch, axis=-1, keepdims=True))
        dc_ref[...] = dc
        _first(ddwb_ref, _colsum(dc))

    return pl.pallas_call(
        body, name="conv_mid_bwd", grid=(S // tm,),
        in_specs=[_rows(tm, D), _rows(tm, D), _rows(tm, D), _const((1, D)), _const((1, D)), _const((D, D))],
        out_specs=[_rows(tm, D), _acc((1, D)), _acc((1, D)), _acc((1, D)), _acc((1, D))],
        out_shape=[_sds((S, D), F32)] + [_sds((1, D), F32)] * 4,
        compiler_params=_cp(40))(dx, dxb, c, lng, lnb, w)


def _dwconv_bwd(dc, glu, dw, carry=None):
    S = dc.shape[0]
    tm = min(256, S)
    hb = tm // HALO
    nsteps = S // tm

    def body(dc_ref, nxt_ref, gl_ref, halo_ref, dw_ref, dgl_ref, ddw_ref, ext, dext, esh, dsh):
        i = pl.program_id(0)
        _fill_ext(ext, halo_ref, gl_ref, tm, i > 0)
        dext[0:tm, :] = dc_ref[...]
        dext[tm:tm + HALO, :] = jnp.where(i < nsteps - 1, nxt_ref[...], 0.0)
        _shift_copies(esh, ext, tm + 24)
        _shift_copies(dsh, dext, tm + 24)

        @pl.when(i == 0)
        def _():
            ddw_ref[...] = jnp.zeros_like(ddw_ref)

        for c0 in range(0, D, 512):
            for r0 in range(0, tm, 32):
                acc = jnp.zeros((32, 512), F32)
                for j in range(CONVW):
                    acc = acc + dw_ref[CONVW - 1 - j:CONVW - j, c0:c0 + 512] * _tap(dext, dsh, j, r0, c0)
                dgl_ref[r0:r0 + 32, c0:c0 + 512] = acc
            for k in range(CONVW):
                acc = jnp.zeros((32, 512), F32)
                for r0 in range(0, tm, 32):
                    acc = acc + dext[r0:r0 + 32, c0:c0 + 512] * _tap(ext, esh, 2 + k, r0, c0)
                ddw_ref[k:k + 1, c0:c0 + 512] += _colsum(acc)

    return _carried_call(
        body, carry, name="dwconv_bwd", grid=(nsteps,),
        in_specs=[_rows(tm, D), pl.BlockSpec((HALO, D), lambda i: (jnp.minimum((i + 1) * hb, S // HALO - 1), 0)),
                  _rows(tm, D), pl.BlockSpec((HALO, D), lambda i: (jnp.maximum(i * hb - 1, 0), 0)), _const((32, D))],
        out_specs=[_rows(tm, D), _acc((32, D))],
        out_shape=[_sds((S, D), F32), _sds((32, D), F32)],
        scratch_shapes=[pltpu.VMEM((tm + HALO, D), F32), pltpu.VMEM((tm + HALO, D), F32),
                        pltpu.VMEM((7, tm + 24, D), F32), pltpu.VMEM((7, tm + 24, D), F32)],
        compiler_params=_cp(48), args=[dc, dc, glu, glu, dw])


def _conv_in_bwd(dglu, u, x, dxo, g, w):
    S = x.shape[0]
    tm = min(512, S)

    def body(dgl_ref, u_ref, x_ref, dxo_ref, g_ref, w_ref, du_ref, dx_ref, db_ref, dg_ref):
        dgl = dgl_ref[...]
        a = u_ref[:, :D]
        sg = jax.nn.sigmoid(u_ref[:, D:])
        da = dgl * sg
        dgt = dgl * a * sg * (1.0 - sg)
        _first_cols(db_ref, [(0, _colsum(da)), (D, _colsum(dgt))])
        dab, dgb = da.astype(BF16), dgt.astype(BF16)
        du_ref[:, :D] = dab
        du_ref[:, D:] = dgb
        dh = _dot_nt(dab, w_ref[:, :D]) + _dot_nt(dgb, w_ref[:, D:])
        xv = x_ref[...]
        dxn, dg = _rms_bwd(dh, xv, _rms(xv), g_ref[...])
        dx_ref[...] = dxo_ref[...] + dxn
        _first(dg_ref, dg)

    return pl.pallas_call(
        body, name="conv_in_bwd", grid=(S // tm,),
        in_specs=[_rows(tm, D), _rows(tm, 2 * D), _rows(tm, D), _rows(tm, D), _const((1, D)), _const((D, 2 * D))],
        out_specs=[_rows(tm, 2 * D), _rows(tm, D), _acc((1, 2 * D)), _acc((1, D))],
        out_shape=[_sds((S, 2 * D), BF16), _sds((S, D), F32), _sds((1, 2 * D), F32), _sds((1, D), F32)],
        compiler_params=_cp(48))(dglu, u, x, dxo, g, w)


def _coords():
    return lax.axis_index("x"), lax.axis_index("y"), lax.axis_index("c")


def _split(refs, *counts):
    out, k = [], 0
    for n in counts:
        out.append(refs[k:k + n])
        k += n
    return out


def _carried_call(body, carry, *, name, grid, in_specs, out_specs, out_shape, scratch_shapes, compiler_params, args):
    if carry is None:
        return pl.pallas_call(body, name=name, grid=grid, in_specs=in_specs, out_specs=out_specs, out_shape=out_shape,
                              scratch_shapes=scratch_shapes, compiler_params=compiler_params)(*args)
    counts = (len(in_specs), len(carry.args), len(out_specs), len(carry.out_shape), len(scratch_shapes), len(carry.scratch))
    nsteps = grid[0]

    def wrapped(*refs):
        ins, cin, outs, cout, scr, cscr = _split(refs, *counts)
        i = pl.program_id(0)
        carry.top(i, nsteps, cin, cout, cscr)
        body(*ins, *outs, *scr)
        carry.bottom(i, nsteps, cin, cout, cscr)

    return pl.pallas_call(
        wrapped, name=name, grid=grid, in_specs=list(in_specs) + carry.in_specs,
        out_specs=list(out_specs) + carry.out_specs, out_shape=list(out_shape) + carry.out_shape,
        scratch_shapes=list(scratch_shapes) + carry.scratch, compiler_params=compiler_params)(*args, *carry.args)


def _routes():
    x, y, c = _coords()
    return (x, y, c), (x, y, 1 - c), [(1 - x, y), (x, 1 - y), (1 - x, 1 - y)]


def _gather_copies(srcs, dsts, send, recv):
    me, sib, chips = _routes()
    c = me[2]

    def cp(a, k, block, to, own=False):
        idx = 4 * block[0] + 2 * block[1] + block[2]
        return pltpu.make_async_remote_copy(
            src_ref=srcs[a] if own else dsts[a].at[idx], dst_ref=dsts[a].at[idx], send_sem=send.at[a, k],
            recv_sem=recv.at[a, k], device_id=to, device_id_type=MESH)

    own, ici_in, fwd, sib_in = [], [], [], []
    for a in range(len(srcs)):
        own += [cp(a, 0, me, sib, True)] + [cp(a, 1 + j, me, (*ch, c), True) for j, ch in enumerate(chips)]
        ici_in += [cp(a, 1 + j, (*ch, c), me) for j, ch in enumerate(chips)]
        fwd += [cp(a, 4 + j, (*ch, c), sib) for j, ch in enumerate(chips)]
        sib_in += [cp(a, 0, sib, me)] + [cp(a, 4 + j, (*ch, 1 - c), me) for j, ch in enumerate(chips)]
    return own, ici_in, fwd, sib_in


class _Gather:
    def __init__(self, srcs):
        self.idx = [i for _, i in srcs]
        self.args = [a for a, _ in srcs]
        n = len(srcs)
        self.in_specs = [pl.BlockSpec(memory_space=pl.ANY)] * n
        self.out_specs = [pl.BlockSpec(memory_space=pl.ANY)] * n
        self.out_shape = [_sds((NDEV,) + (a.shape if i is None else a.shape[1:]), BF16) for a, i in srcs]
        self.scratch = [pltpu.SemaphoreType.DMA((n, 7)), pltpu.SemaphoreType.DMA((n, 7)), pltpu.SemaphoreType.DMA((n,))]

    def _copies(self, ins, outs, scr):
        send, recv, lsem = scr
        srcs = [r if i is None else r.at[i] for r, i in zip(ins, self.idx)]
        me = _routes()[0]
        slot = 4 * me[0] + 2 * me[1] + me[2]
        local = [pltpu.make_async_copy(s_, d_.at[slot], lsem.at[a]) for a, (s_, d_) in enumerate(zip(srcs, outs))]
        return _gather_copies(srcs, outs, send, recv) + (local,)

    def top(self, i, nsteps, ins, outs, scr):
        @pl.when(i == 0)
        def _():
            own, _, _, _, local = self._copies(ins, outs, scr)
            for cp in local + own:
                cp.start()

        @pl.when(i == (3 * nsteps) // 4)
        def _():
            _, ici_in, fwd, _, _ = self._copies(ins, outs, scr)
            for a_, f_ in zip(ici_in, fwd):
                a_.wait_recv()
                f_.start()

    def bottom(self, i, nsteps, ins, outs, scr):
        @pl.when(i == nsteps - 1)
        def _():
            own, _, fwd, sib_in, local = self._copies(ins, outs, scr)
            for cp in sib_in:
                cp.wait_recv()
            for cp in own + fwd:
                cp.wait_send()
            for cp in local:
                cp.wait()


def _scatter_copies(srcs, lands, send, recv):
    x, y, c = _coords()
    me = 4 * x + 2 * y + c
    sends, recvs = [], []
    for a in range(len(srcs)):
        for k in range(1, NDEV):
            peer = (x ^ ((k >> 2) & 1), y ^ ((k >> 1) & 1), c ^ (k & 1))
            pidx = me ^ k
            sends.append(pltpu.make_async_remote_copy(
                src_ref=srcs[a].at[pidx], dst_ref=lands[a].at[me], send_sem=send.at[a, k - 1],
                recv_sem=recv.at[a, k - 1], device_id=peer, device_id_type=MESH))
            recvs.append(pltpu.make_async_remote_copy(
                src_ref=srcs[a].at[pidx], dst_ref=lands[a].at[pidx], send_sem=send.at[a, k - 1],
                recv_sem=recv.at[a, k - 1], device_id=peer, device_id_type=MESH))
    return sends, recvs


class _Scatter:
    def __init__(self, srcs):
        n = len(srcs)
        self.args = list(srcs)
        self.in_specs = [pl.BlockSpec(memory_space=pl.ANY)] * n
        self.out_specs = [pl.BlockSpec(memory_space=pl.ANY)] * n
        self.out_shape = [_sds(a.shape, BF16) for a in srcs]
        self.scratch = [pltpu.SemaphoreType.DMA((n, 7)), pltpu.SemaphoreType.DMA((n, 7)), pltpu.SemaphoreType.DMA((n,))]

    def _copies(self, ins, outs, scr):
        send, recv, lsem = scr
        x, y, c = _coords()
        me = 4 * x + 2 * y + c
        local = [pltpu.make_async_copy(s_.at[me], d_.at[me], lsem.at[a]) for a, (s_, d_) in enumerate(zip(ins, outs))]
        return _scatter_copies(ins, outs, send, recv) + (local,)

    def top(self, i, nsteps, ins, outs, scr):
        @pl.when(i == 0)
        def _():
            sends, _, local = self._copies(ins, outs, scr)
            for cp in local + sends:
                cp.start()

    def bottom(self, i, nsteps, ins, outs, scr):
        @pl.when(i == nsteps - 1)
        def _():
            sends, recvs, local = self._copies(ins, outs, scr)
            for cp in recvs:
                cp.wait_recv()
            for cp in sends:
                cp.wait_send()
            for cp in local:
                cp.wait()


def _gather_first(w_in, w_out, w_qkv, w_o, w_up, w_down, spack):
    RS_ = spack.shape[0]

    def body(win_ref, wout_ref, wqkv_ref, wo_ref, wup_ref, wdn_ref, s_ref,
             qkv_b, wo_b, up_b, dn_b, gin_ref, gout_ref, gs_ref,
             st_in, st_out, st_qkv, st_wo, st_up, st_dn, send, recv, lsem):
        for q in range(4):
            st_in[:, q * 256:(q + 1) * 256] = win_ref[q * 256:(q + 1) * 256, :].astype(BF16)
        st_out[...] = wout_ref[...].astype(BF16)
        st_qkv[...] = wqkv_ref[...].astype(BF16)
        st_wo[...] = wo_ref[...].astype(BF16)
        for l in range(2):
            for h_ in range(2):
                st_up[l, :, h_ * FFB:(h_ + 1) * FFB] = wup_ref[l, h_ * FFB:(h_ + 1) * FFB, :].astype(BF16)
            st_dn[l] = wdn_ref[l].astype(BF16)
        me = _routes()[0]
        slot = 4 * me[0] + 2 * me[1] + me[2]
        srcs, dsts = [st_in, st_out, s_ref], [gin_ref, gout_ref, gs_ref]
        local = [pltpu.make_async_copy(s_, d_, lsem.at[a]) for a, (s_, d_) in enumerate(
            [(st_qkv, qkv_b), (st_wo, wo_b), (st_up, up_b), (st_dn, dn_b), (st_in, gin_ref.at[slot]),
             (st_out, gout_ref.at[slot])])]
        for cp in local:
            cp.start()
        gs_ref[slot] = s_ref[...]
        own, ici_in, fwd, sib_in = _gather_copies(srcs, dsts, send, recv)
        for cp in own:
            cp.start()
        for a_, f_ in zip(ici_in, fwd):
            a_.wait_recv()
            f_.start()
        for cp in sib_in:
            cp.wait_recv()
        for cp in own + fwd:
            cp.wait_send()
        for cp in local:
            cp.wait()

    vmem = pl.BlockSpec(memory_space=pltpu.VMEM)
    hbm = pl.BlockSpec(memory_space=pl.ANY)
    return pl.pallas_call(
        body, name="gather_first",
        in_specs=[vmem] * 7,
        out_specs=[hbm] * 6 + [vmem],
        out_shape=[_sds((160, D), BF16), _sds((128, D), BF16), _sds((2, FFB, D), BF16), _sds((2, FFB, D), BF16),
                   _sds((NDEV, 256, D), BF16), _sds((NDEV, 128, D), BF16), _sds((NDEV, RS_, 128), F32)],
        scratch_shapes=[pltpu.VMEM((256, D), BF16), pltpu.VMEM((128, D), BF16), pltpu.VMEM((160, D), BF16),
                        pltpu.VMEM((128, D), BF16), pltpu.VMEM((2, FFB, D), BF16), pltpu.VMEM((2, FFB, D), BF16),
                        pltpu.SemaphoreType.DMA((3, 7)), pltpu.SemaphoreType.DMA((3, 7)), pltpu.SemaphoreType.DMA((6,))],
        compiler_params=pltpu.CompilerParams(vmem_limit_bytes=40 << 20))(w_in, w_out, w_qkv, w_o, w_up, w_down, spack)


def _scatter_last(gsrc, spack):
    n, RS_ = gsrc.shape[1], spack.shape[0]

    def body(g_ref, s_ref, land_ref, sred_ref, sland, send, recv, lsem):
        x, y, c = _coords()
        me = 4 * x + 2 * y + c
        small_out, small_in = [], []
        for k in range(1, NDEV):
            peer = (x ^ ((k >> 2) & 1), y ^ ((k >> 1) & 1), c ^ (k & 1))
            pidx = me ^ k
            small_out.append(pltpu.make_async_remote_copy(
                src_ref=s_ref, dst_ref=sland.at[me], send_sem=send.at[1, k - 1], recv_sem=recv.at[1, k - 1],
                device_id=peer, device_id_type=MESH))
            small_in.append(pltpu.make_async_remote_copy(
                src_ref=s_ref, dst_ref=sland.at[pidx], send_sem=send.at[1, k - 1], recv_sem=recv.at[1, k - 1],
                device_id=peer, device_id_type=MESH))
        big_out, big_in = _scatter_copies([g_ref], [land_ref], send, recv)
        mine = pltpu.make_async_copy(g_ref.at[me], land_ref.at[me], lsem)
        mine.start()
        for cp in big_out + small_out:
            cp.start()
        sland[me] = s_ref[...]
        for cp in big_in + small_in:
            cp.wait_recv()
        for cp in big_out + small_out:
            cp.wait_send()
        mine.wait()
        acc = sland[0]
        for s_ in range(1, NDEV):
            acc = acc + sland[s_]
        sred_ref[...] = acc

    vmem = pl.BlockSpec(memory_space=pltpu.VMEM)
    hbm = pl.BlockSpec(memory_space=pl.ANY)
    return pl.pallas_call(
        body, name="scatter_last",
        in_specs=[hbm, vmem],
        out_specs=[hbm, vmem],
        out_shape=[_sds((NDEV, n, D), BF16), _sds((RS_, 128), F32)],
        scratch_shapes=[pltpu.VMEM((NDEV, RS_, 128), F32), pltpu.SemaphoreType.DMA((2, 7)),
                        pltpu.SemaphoreType.DMA((2, 7)), pltpu.SemaphoreType.DMA],
        compiler_params=pltpu.CompilerParams(vmem_limit_bytes=16 << 20))(gsrc, spack)


def _adam_math(w, g, m, v):
    nm = B1 * m + (1.0 - B1) * g
    nv = B2 * v + (1.0 - B2) * jnp.square(g)
    m_hat = nm / (1.0 - B1 ** STEP)
    v_hat = nv / (1.0 - B2 ** STEP)
    return -LR * (m_hat / (jnp.sqrt(v_hat) + AEPS) + WD * w), nm, nv


def _adamw(w, g, m, v, name):
    R, L = w.shape
    tr = next(c for c in (680, 512, 392, 256, 136, 64, 8) if R % c == 0)

    def body(w_ref, g_ref, m_ref, v_ref, d_ref, nm_ref, nv_ref):
        d_ref[...], nm_ref[...], nv_ref[...] = _adam_math(w_ref[...], g_ref[...], m_ref[...], v_ref[...])

    spec = pl.BlockSpec((tr, L), lambda i: (i, 0))
    return pl.pallas_call(
        body, name=name, grid=(R // tr,), in_specs=[spec] * 4, out_specs=[spec] * 3,
        out_shape=[_sds((R, L), F32)] * 3, compiler_params=_cp(48))(w, g, m, v)


def _reduce_adamw(lands, w, m, v, nsplit, name):
    L = len(lands)
    R = lands[0].shape[1]
    wl = D // nsplit
    tr = next(c for c in (256, 160, 128) if R % c == 0)
    nr = R // tr

    def body(*refs):
        l_refs, (w_ref, m_ref, v_ref), (g_ref, d_ref, nm_ref, nv_ref) = _split(refs, L, 3, 4)
        for l in range(L):
            @pl.when(pl.program_id(0) == l)
            def _(l=l):
                g = l_refs[l][0].astype(F32)
                for s_ in range(1, NDEV):
                    g = g + l_refs[l][s_].astype(F32)
                g_ref[...] = g
                d_ref[...], nm_ref[...], nv_ref[...] = _adam_math(w_ref[...], g, m_ref[...], v_ref[...])

    def land_spec(l):
        return pl.BlockSpec((NDEV, tr, wl), lambda ll, h, i: (0, jnp.where(ll == l, i, 0), jnp.where(ll == l, h, 0)))

    spec = pl.BlockSpec((None, tr, wl), lambda ll, h, i: (ll, h * nr + i, 0))
    return pl.pallas_call(
        body, name=name, grid=(L, nsplit, nr),
        in_specs=[land_spec(l) for l in range(L)] + [spec] * 3, out_specs=[spec] * 4,
        out_shape=[_sds(w.shape, F32)] * 4, compiler_params=_cp(48, 3))(*lands, w, m, v)


BIG = ("conv_w_in", "conv_w_out", "w_qkv", "w_o", "w_up", "w_down")
SMALL = (("conv_norm_g", (1, D), False), ("conv_b_in", (1, 2 * D), False), ("conv_dw", (1, CONVW, D), True),
         ("conv_dw_b", (1, D), False), ("conv_ln_g", (1, D), False), ("conv_ln_b", (1, D), False),
         ("conv_b_out", (1, D), False), ("attn_norm_g", (1, D), True), ("b_qkv", (1, QKV), True),
         ("q_norm_g", (1, HD), False), ("k_norm_g", (1, HD), False), ("sinks", (1, NH), False),
         ("b_o", (1, D), True), ("rel_bias", (NBKT, NH), False), ("mlp_norm_g", (2, D), False))


def _small_rows(n):
    return -(-n // 1024) * 8


def _lanes(a):
    flat = a.reshape(-1)
    n = flat.shape[0]
    rows = _small_rows(n)
    return jnp.pad(flat, (0, rows * 128 - n)).reshape(rows, 128)


def _pack_small(arrs):
    return jnp.concatenate([_lanes(a) for a in arrs], axis=0)


def _unpack_small(pack, shapes):
    out, r = [], 0
    for shp in shapes:
        n = int(np.prod(shp))
        rows = _small_rows(n)
        out.append(pack[r:r + rows].reshape(-1)[:n].reshape(shp))
        r += rows
    return out


def _cols_to_blocks(a, n):
    M = a.shape[0]
    return a.reshape(M, NDEV, n).transpose(1, 0, 2).reshape(NDEV, M * n // D, D)


def kernel(x, conv_norm_g, conv_w_in, conv_b_in, conv_dw, conv_dw_b, conv_ln_g, conv_ln_b, conv_w_out, conv_b_out, attn_norm_g, w_qkv, b_qkv, q_norm_g, k_norm_g, sinks, w_o, b_o, rel_bias, mlp_norm_g, w_up, w_down, loss_target, m_conv_norm_g, m_conv_w_in, m_conv_b_in, m_conv_dw, m_conv_dw_b, m_conv_ln_g, m_conv_ln_b, m_conv_w_out, m_conv_b_out, m_attn_norm_g, m_w_qkv, m_b_qkv, m_q_norm_g, m_k_norm_g, m_sinks, m_w_o, m_b_o, m_rel_bias, m_mlp_norm_g, m_w_up, m_w_down, v_conv_norm_g, v_conv_w_in, v_conv_b_in, v_conv_dw, v_conv_dw_b, v_conv_ln_g, v_conv_ln_b, v_conv_w_out, v_conv_b_out, v_attn_norm_g, v_w_qkv, v_b_qkv, v_q_norm_g, v_k_norm_g, v_sinks, v_w_o, v_b_o, v_rel_bias, v_mlp_norm_g, v_w_up, v_w_down):
    names = [n for n, _, _ in SMALL] + list(BIG)
    loc = dict(locals())
    W = {n: loc[n] for n in names}
    M_ = {n: loc["m_" + n] for n in names}
    V_ = {n: loc["v_" + n] for n in names}
    me = 4 * lax.axis_index("x") + 2 * lax.axis_index("y") + lax.axis_index("c")
    xs = x[0]
    S = xs.shape[0]

    sharded = [n for n, _, sh in SMALL if sh]
    qkv_b, wo_b, up_b, dn_b, g_win, g_wout, gs = _gather_first(
        conv_w_in[0], conv_w_out[0], w_qkv.reshape(160, D), w_o[0], w_up, w_down, _pack_small([W[n] for n in sharded]))
    w_in = g_win.reshape(NDEV, 256, 4, 256).transpose(2, 1, 0, 3).reshape(D, 2 * D)
    w_out = g_wout.reshape(D, D)
    shard_shapes = {"conv_dw": (CONVW, 128), "attn_norm_g": (1, 128), "b_qkv": (1, 160), "b_o": (1, 128)}
    parts = [_unpack_small(gs[j], [shard_shapes[n] for n in sharded]) for j in range(NDEV)]
    full = {n: jnp.concatenate([parts[j][i] for j in range(NDEV)], axis=-1) for i, n in enumerate(sharded)}
    dw32 = jnp.pad(full["conv_dw"], ((0, 1), (0, 0)))
    attn_g, bqkv, bo = full["attn_norm_g"], full["b_qkv"], full["b_o"]
    qg = jnp.tile(q_norm_g, (1, NH))
    kg = jnp.tile(k_norm_g, (1, NKV))
    e16, e16t, e2, e2t, fold, fold2 = _seg_mats()
    bkt = jnp.asarray(_bucket_table().T)
    bkt4 = jnp.asarray(np.tile(_bucket_table().T, (1, 4)))

    h0, u, glu = _conv_in_fwd(xs, conv_norm_g, w_in, conv_b_in)
    cc, sb, x1, wu0, wd0 = _conv_mid_fwd(glu, xs, dw32, conv_dw_b, conv_ln_g, conv_ln_b, w_out, conv_b_out,
                                         carry=_Gather([(up_b, 0), (dn_b, 0)]))
    h1, act0, x2, g_qkv, g_wo = _mlp_fwd(x1, mlp_norm_g[0:1], wu0, wd0, carry=_Gather([(qkv_b, None), (wo_b, None)]))
    wqkv = g_qkv.reshape(NDEV, D, 160).transpose(1, 0, 2).reshape(D, QKV)
    wo = g_wo.reshape(D, D)
    h2, raw, qn, kn, vv = _attn_qkv_fwd(x2, attn_g, wqkv, bqkv, qg, kg, e16, e16t, e2, e2t)
    kp = jnp.pad(kn, ((WIN, 0), (0, 0)))
    vp = jnp.pad(vv, ((WIN, 0), (0, 0)))
    ob, x3, wu1, wd1 = _attn_fwd(qn, kp, vp, bkt, rel_bias, sinks, wo, bo, x2, carry=_Gather([(up_b, 1), (dn_b, 1)]))
    h3, act1, dx4, dx4b, sq = _mlp_fwd(x3, mlp_norm_g[1:2], wu1, wd1, target=loss_target[0])
    loss = lax.psum(jnp.sum(sq) * (0.5 / D), ("x", "y", "c"))

    dup1, dx3, dx3b, dg_mlp1 = _mlp_bwd(dx4, x3, mlp_norm_g[1:2], act1, wu1, wd1)
    g_dn1 = _wgrad(act1, dx4b, 512, D, "wgrad_down1").reshape(NDEV, FFB, D)
    g_up1 = _wgrad_split(h3, dup1, 2, "wgrad_up1")
    dqs, dkp, dvp, d_sinks, d_rel, d_bo, l_up1, l_dn1 = _attn_bwd(
        dx3, dx3b, qn, kp, vp, bkt, bkt4, rel_bias, sinks, wo, carry=_Scatter([g_up1, g_dn1]))
    g_wo = _wgrad(ob, dx3b, D, 512, "wgrad_o").reshape(NDEV, 128, D)
    dqkv, dx2, dx2b, d_bqkv, d_attn_g, d_qg, d_kg = _attn_qkv_bwd(
        dqs, dkp[WIN:], dvp[WIN:], raw, x2, dx3, attn_g, wqkv, qg, kg, e16, e16t, e2, e2t, fold, fold2)
    g_wqkv = _cols_to_blocks(_wgrad(h2, dqkv, D, 640, "wgrad_qkv"), 160)
    dup0, dx1, dx1b, dg_mlp0, l_qkv, l_wo = _mlp_bwd(dx2, x1, mlp_norm_g[0:1], act0, wu0, wd0,
                                                     carry=_Scatter([g_wqkv, g_wo]))
    g_dn0 = _wgrad(act0, dx2b, 512, D, "wgrad_down0").reshape(NDEV, FFB, D)
    g_up0 = _wgrad_split(h1, dup0, 2, "wgrad_up0")
    dc, d_bout, d_lng, d_lnb, d_dwb = _conv_mid_bwd(dx1, dx1b, cc, conv_ln_g, conv_ln_b, w_out)
    g_wout = _wgrad(sb, dx1b, D, 512, "wgrad_conv_out").reshape(NDEV, 128, D)
    dglu, d_dw, l_wout, l_up0, l_dn0 = _dwconv_bwd(dc, glu, dw32, carry=_Scatter([g_wout, g_up0, g_dn0]))
    du, grad_x, d_bin, d_cng = _conv_in_bwd(dglu, u, xs, dx1, conv_norm_g, w_in)
    g_win = _wgrad_split(h0, du, 4, "wgrad_conv_in")

    small_grads = {
        "conv_norm_g": d_cng, "conv_b_in": d_bin, "conv_dw": d_dw[:CONVW][None], "conv_dw_b": d_dwb,
        "conv_ln_g": d_lng, "conv_ln_b": d_lnb, "conv_b_out": d_bout, "attn_norm_g": d_attn_g, "b_qkv": d_bqkv,
        "q_norm_g": d_qg[:, :HD], "k_norm_g": d_kg[:, :HD], "sinks": d_sinks[:, :NH], "b_o": d_bo,
        "rel_bias": d_rel[:, :NH],
        "mlp_norm_g": jnp.concatenate([dg_mlp0.reshape(8, 128), dg_mlp1.reshape(8, 128)], axis=0)}
    l_win, sred = _scatter_last(g_win, _pack_small([small_grads[n] for n, _, _ in SMALL]))
    sg_full = _unpack_small(sred, [shp for _, shp, _ in SMALL])

    big = {}
    for n, lands, nsplit in (("conv_w_in", [l_win], 4), ("conv_w_out", [l_wout], 1), ("w_o", [l_wo], 1),
                             ("w_up", [l_up0, l_up1], 2), ("w_down", [l_dn0, l_dn1], 1)):
        big[n] = _reduce_adamw(lands, W[n], M_[n], V_[n], nsplit, "adamw_" + n)
    flat = lambda t: t.reshape(1, 160, D)
    big["w_qkv"] = [t.reshape(1, D, 160) for t in
                    _reduce_adamw([l_qkv], flat(w_qkv), flat(m_w_qkv), flat(v_w_qkv), 1, "adamw_w_qkv")]
    G = {n: big[n][0] for n in BIG}
    for (n, shp, sh), gfull in zip(SMALL, sg_full):
        if sh:
            width = shp[-1] // NDEV
            G[n] = lax.dynamic_slice_in_dim(gfull, me * width, width, axis=gfull.ndim - 1)
        else:
            G[n] = gfull
    snames = [n for n, _, _ in SMALL]
    small = _adamw(*[_pack_small([t[n] for n in snames]) for t in (W, G, M_, V_)], "adamw_small")
    sshapes = [W[n].shape for n in snames]
    outs = []
    for k in range(3):
        o = {n: big[n][k + 1] for n in BIG}
        o.update(dict(zip(snames, _unpack_small(small[k], sshapes))))
        outs.append(o)
    order = ["conv_norm_g", "conv_w_in", "conv_b_in", "conv_dw", "conv_dw_b", "conv_ln_g", "conv_ln_b", "conv_w_out",
             "conv_b_out", "attn_norm_g", "w_qkv", "b_qkv", "q_norm_g", "k_norm_g", "sinks", "w_o", "b_o", "rel_bias",
             "mlp_norm_g", "w_up", "w_down"]
    return (loss, grad_x[None], *[G[n] for n in order], *[outs[0][n] for n in order],
            *[outs[1][n] for n in order], *[outs[2][n] for n in order])
```

```python
import math

import numpy as np
import jax
import jax.numpy as jnp
from jax import lax
from jax.experimental import pallas as pl
from jax.experimental.pallas import tpu as pltpu

F32 = jnp.float32
BF16 = jnp.bfloat16

D = 1024
DFF = 4096
NH, NKV, HD = 16, 2, 64
NPAIR = NH // 2
KV = NKV * HD
QKV = D + 2 * KV
CONVW = 31
WIN = 128
NBKT = 32
EPS = 1e-6
NEG = -1e30
NDEV = 8
FFB = DFF // NDEV
HALO = 32

LR, B1, B2, AEPS, WD, STEP = 0.001, 0.9, 0.999, 1e-08, 0.01, 10

MESH = pl.DeviceIdType.MESH


def _cp(vmem_mb, n_axes=1):
    return pltpu.CompilerParams(dimension_semantics=("arbitrary",) * n_axes, vmem_limit_bytes=vmem_mb << 20)


def _const(shape):
    nd = len(shape)
    return pl.BlockSpec(shape, lambda *_: (0,) * nd, pipeline_mode=pl.Buffered(1))


def _acc(shape):
    nd = len(shape)
    return pl.BlockSpec(shape, lambda *_: (0,) * nd)


def _rows(tm, n):
    return pl.BlockSpec((tm, n), lambda i: (i, 0))


def _sds(shape, dtype):
    return jax.ShapeDtypeStruct(shape, dtype)


def _dot(a, b):
    return jnp.dot(a, b, preferred_element_type=F32)


def _dot_nt(a, b):
    return lax.dot_general(a, b, (((1,), (1,)), ((), ())), preferred_element_type=F32)


def _dot_tn(a, b):
    return lax.dot_general(a, b, (((0,), (0,)), ((), ())), preferred_element_type=F32)


def _dot_hi(x, e):
    x1 = x.astype(BF16)
    x2 = (x - x1.astype(F32)).astype(BF16)
    return _dot(x1, e) + _dot(x2, e)


def _rms(x):
    return lax.rsqrt(jnp.mean(x * x, axis=-1, keepdims=True) + EPS)


def _rms_bwd(dh, x, r, g):
    xh = x * r
    dxh = dh * g
    dx = r * (dxh - xh * jnp.mean(dxh * xh, axis=-1, keepdims=True))
    return dx, jnp.sum(dh * xh, axis=0, keepdims=True)


def _colsum(a):
    return jnp.sum(a, axis=0, keepdims=True)


def _first(ref, val):
    @pl.when(pl.program_id(0) == 0)
    def _():
        ref[...] = jnp.zeros_like(ref)
    ref[...] += val


def _first_cols(ref, vals):
    @pl.when(pl.program_id(0) == 0)
    def _():
        ref[...] = jnp.zeros_like(ref)
    for c0, val in vals:
        ref[:, c0:c0 + val.shape[1]] += val


def _conv_in_fwd(x, g, w, b):
    S = x.shape[0]
    tm = min(512, S)

    def body(x_ref, g_ref, w_ref, b_ref, h_ref, u_ref, glu_ref):
        xv = x_ref[...]
        hb = (xv * _rms(xv) * g_ref[...]).astype(BF16)
        h_ref[...] = hb
        u = _dot(hb, w_ref[...]) + b_ref[...]
        u_ref[...] = u
        glu_ref[...] = u[:, :D] * jax.nn.sigmoid(u[:, D:])

    return pl.pallas_call(
        body, name="conv_in_fwd", grid=(S // tm,),
        in_specs=[_rows(tm, D), _const((1, D)), _const((D, 2 * D)), _const((1, 2 * D))],
        out_specs=[_rows(tm, D), _rows(tm, 2 * D), _rows(tm, D)],
        out_shape=[_sds((S, D), BF16), _sds((S, 2 * D), F32), _sds((S, D), F32)],
        compiler_params=_cp(48))(x, g, w, b)


def _fill_ext(ext, halo_ref, cur_ref, tm, keep):
    ext[0:HALO, :] = jnp.where(keep, halo_ref[...], 0.0)
    ext[HALO:HALO + tm, :] = cur_ref[...]


def _shift_copies(dst, src, n):
    for s_ in range(1, 8):
        dst[s_ - 1, 0:n, :] = src[s_:s_ + n, :]


def _tap(src, sh, o, r0, c0):
    a, s_ = divmod(o, 8)
    ref = src if s_ == 0 else sh.at[s_ - 1]
    return ref[r0 + 8 * a:r0 + 8 * a + 32, c0:c0 + 512]


def _conv_mid_fwd(glu, x, dw, dwb, lng, lnb, w, b, carry=None):
    S = x.shape[0]
    tm = min(256, S)
    hb = tm // HALO

    def body(gl_ref, halo_ref, x_ref, dw_ref, dwb_ref, lng_ref, lnb_ref, w_ref, b_ref, c_ref, s_ref, xo_ref, ext, esh):
        i = pl.program_id(0)
        _fill_ext(ext, halo_ref, gl_ref, tm, i > 0)
        _shift_copies(esh, ext, tm + 24)
        for r0 in range(0, tm, 32):
            for c0 in range(0, D, 512):
                acc = jnp.zeros((32, 512), F32) + dwb_ref[:, c0:c0 + 512]
                for k in range(CONVW):
                    acc = acc + dw_ref[k:k + 1, c0:c0 + 512] * _tap(ext, esh, 2 + k, r0, c0)
                c_ref[r0:r0 + 32, c0:c0 + 512] = acc
        c = c_ref[...]
        mu = jnp.mean(c, axis=-1, keepdims=True)
        xc = c - mu
        y = xc * lax.rsqrt(jnp.mean(xc * xc, axis=-1, keepdims=True) + EPS) * lng_ref[...] + lnb_ref[...]
        sb = (y * jax.nn.sigmoid(y)).astype(BF16)
        s_ref[...] = sb
        xo_ref[...] = x_ref[...] + _dot(sb, w_ref[...]) + b_ref[...]

    return _carried_call(
        body, carry, name="conv_mid_fwd", grid=(S // tm,),
        in_specs=[_rows(tm, D), pl.BlockSpec((HALO, D), lambda i: (jnp.maximum(i * hb - 1, 0), 0)), _rows(tm, D),
                  _const((32, D)), _const((1, D)), _const((1, D)), _const((1, D)), _const((D, D)), _const((1, D))],
        out_specs=[_rows(tm, D), _rows(tm, D), _rows(tm, D)],
        out_shape=[_sds((S, D), F32), _sds((S, D), BF16), _sds((S, D), F32)],
        scratch_shapes=[pltpu.VMEM((tm + HALO, D), F32), pltpu.VMEM((7, tm + 24, D), F32)],
        compiler_params=_cp(48), args=[glu, glu, x, dw, dwb, lng, lnb, w, b])


def _mlp_fwd(x, g, wu, wd, target=None, carry=None):
    S = x.shape[0]
    tm = min(512, S)
    last = target is not None

    def body(*refs):
        if last:
            x_ref, g_ref, wu_ref, wd_ref, t_ref, h_ref, act_ref, dy_ref, dyb_ref, sq_ref, acc = refs
        else:
            x_ref, g_ref, wu_ref, wd_ref, h_ref, act_ref, xo_ref, acc = refs
        xv = x_ref[...]
        hb = (xv * _rms(xv) * g_ref[...]).astype(BF16)
        h_ref[...] = hb
        for j in range(NDEV):
            up = _dot(hb[:, :FFB], wu_ref[j, :, :FFB]) + _dot(hb[:, FFB:], wu_ref[j, :, FFB:])
            a = jnp.square(jnp.maximum(up, 0.0)).astype(BF16)
            act_ref[:, j * FFB:(j + 1) * FFB] = a
            if j == 0:
                acc[...] = _dot(a, wd_ref[j])
            else:
                acc[...] += _dot(a, wd_ref[j])
        y = xv + acc[...]
        if last:
            diff = y - t_ref[...]
            dy = diff * (1.0 / D)
            dy_ref[...] = dy
            dyb_ref[...] = dy.astype(BF16)
            _first(sq_ref, _colsum(diff * diff))
        else:
            xo_ref[...] = y

    in_specs = [_rows(tm, D), _const((1, D)), _const((NDEV, FFB, D)), _const((NDEV, FFB, D))]
    args = [x, g, wu, wd]
    out_specs = [_rows(tm, D), _rows(tm, DFF)]
    out_shape = [_sds((S, D), BF16), _sds((S, DFF), BF16)]
    if last:
        in_specs.append(_rows(tm, D))
        args.append(target)
        out_specs += [_rows(tm, D), _rows(tm, D), _acc((1, D))]
        out_shape += [_sds((S, D), F32), _sds((S, D), BF16), _sds((1, D), F32)]
    else:
        out_specs.append(_rows(tm, D))
        out_shape.append(_sds((S, D), F32))
    return _carried_call(
        body, carry, name="mlp_fwd_loss" if last else "mlp_fwd", grid=(S // tm,),
        in_specs=in_specs, out_specs=out_specs, out_shape=out_shape,
        scratch_shapes=[pltpu.VMEM((tm, D), F32)],
        compiler_params=_cp(52), args=args)


def _mlp_bwd(dy, x, g, act, wu, wd, carry=None):
    S = x.shape[0]
    tm = min(512, S)

    def body(dy_ref, x_ref, g_ref, act_ref, wu_ref, wd_ref, dup_ref, dx_ref, dxb_ref, dg_ref, acc):
        db = dy_ref[...].astype(BF16)
        for j in range(NDEV):
            dact = _dot_nt(db, wd_ref[j])
            a = act_ref[:, j * FFB:(j + 1) * FFB].astype(F32)
            dup = (dact * (2.0 * jnp.sqrt(a))).astype(BF16)
            dup_ref[:, j * FFB:(j + 1) * FFB] = dup
            for h_ in range(2):
                part = _dot_nt(dup, wu_ref[j, :, h_ * FFB:(h_ + 1) * FFB])
                if j == 0:
                    acc[:, h_ * FFB:(h_ + 1) * FFB] = part
                else:
                    acc[:, h_ * FFB:(h_ + 1) * FFB] += part
        xv = x_ref[...]
        dxn, dg = _rms_bwd(acc[...], xv, _rms(xv), g_ref[...])
        dx = dy_ref[...] + dxn
        dx_ref[...] = dx
        dxb_ref[...] = dx.astype(BF16)
        _first(dg_ref, dg)

    return _carried_call(
        body, carry, name="mlp_bwd", grid=(S // tm,),
        in_specs=[_rows(tm, D), _rows(tm, D), _const((1, D)), _rows(tm, DFF),
                  _const((NDEV, FFB, D)), _const((NDEV, FFB, D))],
        out_specs=[_rows(tm, DFF), _rows(tm, D), _rows(tm, D), _acc((1, D))],
        out_shape=[_sds((S, DFF), BF16), _sds((S, D), F32), _sds((S, D), BF16), _sds((1, D), F32)],
        scratch_shapes=[pltpu.VMEM((tm, D), F32)],
        compiler_params=_cp(58), args=[dy, x, g, act, wu, wd])


def _wgrad(a, b, tm, tn, name):
    S, M = a.shape
    N = b.shape[1]
    tk = min(4096, S)
    nk = S // tk

    def body(a_ref, b_ref, o_ref, acc):
        k = pl.program_id(2)

        @pl.when(k == 0)
        def _():
            acc[...] = jnp.zeros_like(acc)

        acc[...] += _dot_tn(a_ref[...], b_ref[...])

        @pl.when(k == nk - 1)
        def _():
            o_ref[...] = acc[...].astype(BF16)

    return pl.pallas_call(
        body, name=name, grid=(M // tm, N // tn, nk),
        in_specs=[pl.BlockSpec((tk, tm), lambda i, j, k: (k, i)), pl.BlockSpec((tk, tn), lambda i, j, k: (k, j))],
        out_specs=pl.BlockSpec((tm, tn), lambda i, j, k: (i, j)),
        out_shape=_sds((M, N), BF16),
        scratch_shapes=[pltpu.VMEM((tm, tn), F32)],
        compiler_params=_cp(48, 3))(a, b)


def _wgrad_split(a, b, nsplit, name):
    S, M = a.shape
    N = b.shape[1]
    wb = N // NDEV
    tm, tn = M // nsplit, D
    nb = tn // wb
    tk = min(4096, S)
    nk = S // tk

    def body(a_ref, b_ref, o_ref, acc):
        k = pl.program_id(2)

        @pl.when(k == 0)
        def _():
            acc[...] = jnp.zeros_like(acc)

        acc[...] += _dot_tn(a_ref[...], b_ref[...])

        @pl.when(k == nk - 1)
        def _():
            for jj in range(nb):
                o_ref[jj] = acc[:, jj * wb:(jj + 1) * wb].astype(BF16)

    return pl.pallas_call(
        body, name=name, grid=(nsplit, N // tn, nk),
        in_specs=[pl.BlockSpec((tk, tm), lambda i, j, k: (k, i)), pl.BlockSpec((tk, tn), lambda i, j, k: (k, j))],
        out_specs=pl.BlockSpec((nb, tm, wb), lambda i, j, k: (j, 0, i)),
        out_shape=_sds((NDEV, tm, nsplit * wb), BF16),
        scratch_shapes=[pltpu.VMEM((tm, tn), F32)],
        compiler_params=_cp(48, 3))(a, b)


def _bucket_table():
    q = np.arange(WIN)[:, None]
    k = np.arange(2 * WIN)[None, :]
    dist = q + WIN - k
    n = np.maximum(dist, 0)
    max_exact = NBKT // 2
    large = max_exact + (np.log(np.maximum(n, 1).astype(np.float32) / max_exact)
                         / math.log(WIN / max_exact) * (NBKT - max_exact)).astype(np.int32)
    large = np.minimum(large, NBKT - 1)
    bkt = np.where(n < max_exact, n, large).astype(np.int32)
    return np.where((dist >= 0) & (dist < WIN), bkt, -1).astype(np.int32)


def _seg_mats():
    e16 = np.zeros((D, 128), np.float32)
    e16[np.arange(D), np.arange(D) // HD] = 1.0
    e2 = np.zeros((KV, 128), np.float32)
    e2[np.arange(KV), np.arange(KV) // HD] = 1.0
    fold = np.zeros((D, 128), np.float32)
    fold[np.arange(D), np.arange(D) % HD] = 1.0
    fold2 = np.zeros((KV, 128), np.float32)
    fold2[np.arange(KV), np.arange(KV) % HD] = 1.0
    return [jnp.asarray(m, BF16) for m in (e16, e16.T, e2, e2.T, fold, fold2)]


def _head_rms(t, e, et):
    r = lax.rsqrt(_dot_hi(t * t, e) * (1.0 / HD) + EPS)
    return _dot_hi(r, et)


def _attn_qkv_fwd(x, g, w, b, qg, kg, e16, e16t, e2, e2t):
    S = x.shape[0]
    tm = min(512, S)

    def body(x_ref, g_ref, w_ref, b_ref, qg_ref, kg_ref, e16_ref, e16t_ref, e2_ref, e2t_ref,
             h_ref, raw_ref, qn_ref, kn_ref, v_ref):
        xv = x_ref[...]
        hb = (xv * _rms(xv) * g_ref[...]).astype(BF16)
        h_ref[...] = hb
        raw = _dot(hb, w_ref[...]) + b_ref[...]
        raw_ref[...] = raw
        q = raw[:, :D]
        k = raw[:, D:D + KV]
        qn = (q * _head_rms(q, e16_ref[...], e16t_ref[...]) * qg_ref[...] * 0.125).astype(BF16)
        for bb in range(tm // WIN):
            for p in range(NPAIR):
                r = (bb * NPAIR + p) * WIN
                qn_ref[r:r + WIN, :] = qn[bb * WIN:(bb + 1) * WIN, p * 128:(p + 1) * 128]
        kn_ref[...] = (k * _head_rms(k, e2_ref[...], e2t_ref[...]) * kg_ref[...]).astype(BF16)
        v_ref[...] = raw[:, D + KV:].astype(BF16)

    return pl.pallas_call(
        body, name="attn_qkv_fwd", grid=(S // tm,),
        in_specs=[_rows(tm, D), _const((1, D)), _const((D, QKV)), _const((1, QKV)), _const((1, D)), _const((1, KV)),
                  _const((D, 128)), _const((128, D)), _const((KV, 128)), _const((128, KV))],
        out_specs=[_rows(tm, D), _rows(tm, QKV), _rows(tm * NPAIR, 128), _rows(tm, KV), _rows(tm, KV)],
        out_shape=[_sds((S, D), BF16), _sds((S, QKV), F32), _sds((S * NPAIR, 128), BF16), _sds((S, KV), BF16),
                   _sds((S, KV), BF16)],
        compiler_params=_cp(48))(x, g, w, b, qg, kg, e16, e16t, e2, e2t)


def _build_bias(bkt_ref, rb_ref, sk_ref, bias_sc, sk_sc):
    bkt = bkt_ref[...]
    row = lax.broadcasted_iota(jnp.int32, (2 * WIN, WIN), 0)
    for h in range(NH):
        p, e = h // 2, h % 2
        g, pp = p // 4, p % 4

        def add(bk, acc, h=h):
            return acc + jnp.where(bkt == bk, rb_ref[bk, h], 0.0)

        bias = lax.fori_loop(0, NBKT, add, jnp.where(bkt < 0, NEG, 0.0).astype(F32))
        bias_sc[0, g, e, :, pp * WIN:(pp + 1) * WIN] = bias
        bias_sc[1, g, e, :, pp * WIN:(pp + 1) * WIN] = jnp.where(row < WIN, NEG, bias)
        sk_sc[g, e, :, pp * WIN:(pp + 1) * WIN] = jnp.zeros((1, WIN), F32) + sk_ref[0, h]


def _stacks(f, mlo):
    r = pltpu.roll(f, HD, 1)
    z = jnp.zeros_like(f)
    return ((jnp.where(mlo, f, z).astype(BF16), jnp.where(mlo, z, r).astype(BF16)),
            (jnp.where(mlo, r, z).astype(BF16), jnp.where(mlo, z, f).astype(BF16)))


def _unstack(d, mlo):
    z = jnp.zeros_like(d[0][0])
    return (jnp.where(mlo, d[0][0], z) + pltpu.roll(jnp.where(mlo, z, d[0][1]), HD, 1)
            + pltpu.roll(jnp.where(mlo, d[1][0], z), HD, 1) + jnp.where(mlo, z, d[1][1]))


def _softmax_sink(l, sk):
    m = jnp.maximum(jnp.max(l, axis=0, keepdims=True), sk)
    ex = jnp.exp(l - m)
    es = jnp.exp(sk - m)
    inv = 1.0 / (jnp.sum(ex, axis=0, keepdims=True) + es)
    return ex * inv, es * inv


GROWS = 4 * WIN


def _attn_fwd(qn, kp, vp, bktt, rel_bias, sinks, wo, bo, x, carry=None):
    S = x.shape[0]
    tq = min(512, S)
    nblk = tq // WIN

    def body(q_ref, k_ref, v_ref, bkt_ref, rb_ref, sk_ref, wo_ref, bo_ref, x_ref, o_ref, xo_ref, bias_sc, sk_sc):
        i = pl.program_id(0)

        @pl.when(i == 0)
        def _():
            _build_bias(bkt_ref, rb_ref, sk_ref, bias_sc, sk_sc)

        mlo = lax.broadcasted_iota(jnp.int32, (2 * WIN, KV), 1) < HD

        def blk(bb, carry):
            r0 = pl.multiple_of(bb * WIN, WIN)
            g0 = pl.multiple_of(i * tq + bb * WIN, WIN)
            first = (g0 == 0).astype(jnp.int32)
            ks = _stacks(k_ref[pl.ds(g0, 2 * WIN), :].astype(F32), mlo)
            vs = _stacks(v_ref[pl.ds(g0, 2 * WIN), :].astype(F32), mlo)
            for g in range(NKV):
                qs = q_ref[pl.ds(pl.multiple_of(bb * (NPAIR * WIN) + g * GROWS, GROWS), GROWS), :]
                og = jnp.zeros((GROWS, 128), F32)
                for e in range(2):
                    l = _dot_nt(ks[g][e], qs) + bias_sc[first, g, e]
                    pr, _ = _softmax_sink(l, sk_sc[g, e])
                    og = og + _dot_tn(pr.astype(BF16), vs[g][e])
                for pp in range(4):
                    p = 4 * g + pp
                    o_ref[pl.ds(r0, WIN), p * 128:(p + 1) * 128] = og[pp * WIN:(pp + 1) * WIN].astype(BF16)
            return carry

        lax.fori_loop(0, nblk, blk, 0)
        xo_ref[...] = x_ref[...] + _dot(o_ref[...], wo_ref[...]) + bo_ref[...]

    smem = pl.BlockSpec(memory_space=pltpu.SMEM)
    return _carried_call(
        body, carry, name="attn_fwd", grid=(S // tq,),
        in_specs=[_rows(tq * NPAIR, 128), _const((S + WIN, KV)), _const((S + WIN, KV)), _const((2 * WIN, WIN)), smem, smem,
                  _const((D, D)), _const((1, D)), _rows(tq, D)],
        out_specs=[_rows(tq, D), _rows(tq, D)],
        out_shape=[_sds((S, D), BF16), _sds((S, D), F32)],
        scratch_shapes=[pltpu.VMEM((2, NKV, 2, 2 * WIN, GROWS), F32), pltpu.VMEM((NKV, 2, 1, GROWS), F32)],
        compiler_params=_cp(48), args=[qn, kp, vp, bktt, rel_bias, sinks, wo, bo, x])


def _attn_bwd(dx, dxb, qn, kp, vp, bktt, bktt4, rel_bias, sinks, wo, carry=None):
    S = dx.shape[0]
    tq = min(512, S)
    nblk = tq // WIN
    nsteps = S // tq

    def body(dx_ref, dxb_ref, q_ref, k_ref, v_ref, bkt_ref, bkt4_ref, rb_ref, sk_ref, wo_ref,
             dq_ref, dk_ref, dv_ref, dsk_ref, drb_ref, dbo_ref, bias_sc, sk_sc, dbias_sc, dsk_sc, do_sc):
        i = pl.program_id(0)

        @pl.when(i == 0)
        def _():
            _build_bias(bkt_ref, rb_ref, sk_ref, bias_sc, sk_sc)
            dbias_sc[...] = jnp.zeros_like(dbias_sc)
            dsk_sc[...] = jnp.zeros_like(dsk_sc)
            dk_ref[...] = jnp.zeros_like(dk_ref)
            dv_ref[...] = jnp.zeros_like(dv_ref)

        _first(dbo_ref, _colsum(dx_ref[...]))
        do = _dot_nt(dxb_ref[...], wo_ref[...]).astype(BF16)
        for bb in range(nblk):
            for p in range(NPAIR):
                r = (bb * NPAIR + p) * WIN
                do_sc[r:r + WIN, :] = do[bb * WIN:(bb + 1) * WIN, p * 128:(p + 1) * 128]

        mlo = lax.broadcasted_iota(jnp.int32, (2 * WIN, KV), 1) < HD

        def blk(bb, carry):
            g0 = pl.multiple_of(i * tq + bb * WIN, WIN)
            first = (g0 == 0).astype(jnp.int32)
            ks = _stacks(k_ref[pl.ds(g0, 2 * WIN), :].astype(F32), mlo)
            vs = _stacks(v_ref[pl.ds(g0, 2 * WIN), :].astype(F32), mlo)
            dks = [[None, None], [None, None]]
            dvs = [[None, None], [None, None]]
            for g in range(NKV):
                rows = pl.ds(pl.multiple_of(bb * (NPAIR * WIN) + g * GROWS, GROWS), GROWS)
                qs = q_ref[rows, :]
                dos = do_sc[rows, :]
                dqs = jnp.zeros((GROWS, 128), F32)
                for e in range(2):
                    l = _dot_nt(ks[g][e], qs) + bias_sc[first, g, e]
                    pr, ps = _softmax_sink(l, sk_sc[g, e])
                    dp = _dot_nt(vs[g][e], dos)
                    dr = jnp.sum(pr * dp, axis=0, keepdims=True)
                    dl = pr * (dp - dr)
                    dsk_sc[g, e] -= ps * dr
                    dbias_sc[g, e] += dl
                    dlb = dl.astype(BF16)
                    dqs = dqs + _dot_tn(dlb, ks[g][e])
                    dks[g][e] = _dot(dlb, qs)
                    dvs[g][e] = _dot(pr.astype(BF16), dos)
                dq_ref[rows, :] = dqs
            dk_ref[pl.ds(g0, 2 * WIN), :] += _unstack(dks, mlo)
            dv_ref[pl.ds(g0, 2 * WIN), :] += _unstack(dvs, mlo)
            return carry

        lax.fori_loop(0, nblk, blk, 0)

        @pl.when(i == nsteps - 1)
        def _():
            bkt4v = bkt4_ref[...]
            rid = lax.broadcasted_iota(jnp.int32, (NBKT, 128), 0)
            lid = lax.broadcasted_iota(jnp.int32, (NBKT, 128), 1)
            lid1 = lax.broadcasted_iota(jnp.int32, (1, 128), 1)
            dsk = jnp.zeros((1, 128), F32)
            for g in range(NKV):
                for e in range(2):
                    for pp in range(4):
                        h = 2 * (4 * g + pp) + e
                        t = jnp.sum(dsk_sc[g, e, :, pp * WIN:(pp + 1) * WIN], axis=1, keepdims=True)
                        dsk = dsk + jnp.where(lid1 == h, t, 0.0)
            dsk_ref[...] = dsk

            def per_bucket(bk, acc):
                mb = bkt4v == bk
                for g in range(NKV):
                    for e in range(2):
                        t = jnp.sum(jnp.where(mb, dbias_sc[g, e], 0.0), axis=0, keepdims=True)
                        for pp in range(4):
                            h = 2 * (4 * g + pp) + e
                            tt = jnp.sum(t[:, pp * WIN:(pp + 1) * WIN], axis=1, keepdims=True)
                            acc = acc + jnp.where((rid == bk) & (lid == h), tt, 0.0)
                return acc

            drb_ref[...] = lax.fori_loop(0, NBKT, per_bucket, jnp.zeros((NBKT, 128), F32))

    smem = pl.BlockSpec(memory_space=pltpu.SMEM)
    return _carried_call(
        body, carry, name="attn_bwd", grid=(nsteps,),
        in_specs=[_rows(tq, D), _rows(tq, D), _rows(tq * NPAIR, 128), _const((S + WIN, KV)), _const((S + WIN, KV)),
                  _const((2 * WIN, WIN)), _const((2 * WIN, GROWS)), smem, smem, _const((D, D))],
        out_specs=[_rows(tq * NPAIR, 128), _acc((S + WIN, KV)), _acc((S + WIN, KV)), _acc((1, 128)), _acc((NBKT, 128)),
                   _acc((1, D))],
        out_shape=[_sds((S * NPAIR, 128), F32), _sds((S + WIN, KV), F32), _sds((S + WIN, KV), F32), _sds((1, 128), F32),
                   _sds((NBKT, 128), F32), _sds((1, D), F32)],
        scratch_shapes=[pltpu.VMEM((2, NKV, 2, 2 * WIN, GROWS), F32), pltpu.VMEM((NKV, 2, 1, GROWS), F32),
                        pltpu.VMEM((NKV, 2, 2 * WIN, GROWS), F32), pltpu.VMEM((NKV, 2, 1, GROWS), F32),
                        pltpu.VMEM((tq * NPAIR, 128), BF16)],
        compiler_params=_cp(52), args=[dx, dxb, qn, kp, vp, bktt, bktt4, rel_bias, sinks, wo])


def _head_norm_bwd(dn, t, e, et, gt):
    r = _head_rms(t, e, et)
    th = t * r
    dth = dn * gt
    dt = r * (dth - th * _dot_hi(_dot_hi(dth * th, e) * (1.0 / HD), et))
    return dt, _colsum(dn * th)


def _attn_qkv_bwd(dqs, dk, dv, raw, x, dxo, g, w, qg, kg, e16, e16t, e2, e2t, fold, fold2):
    S = x.shape[0]
    tm = min(512, S)

    def body(dq_ref, dk_ref, dv_ref, raw_ref, x_ref, dxo_ref, g_ref, w_ref, qg_ref, kg_ref,
             e16_ref, e16t_ref, e2_ref, e2t_ref, fold_ref, fold2_ref,
             dqkv_ref, dx_ref, dxb_ref, db_ref, dg_ref, dqg_ref, dkg_ref, dq_sc):
        for bb in range(tm // WIN):
            for p in range(NPAIR):
                r = (bb * NPAIR + p) * WIN
                dq_sc[bb * WIN:(bb + 1) * WIN, p * 128:(p + 1) * 128] = dq_ref[r:r + WIN, :]
        dq, cq = _head_norm_bwd(dq_sc[...] * 0.125, raw_ref[:, :D], e16_ref[...], e16t_ref[...], qg_ref[...])
        dk_, ck = _head_norm_bwd(dk_ref[...], raw_ref[:, D:D + KV], e2_ref[...], e2t_ref[...], kg_ref[...])
        dv_ = dv_ref[...]
        _first(dqg_ref, _dot_hi(cq, fold_ref[...]))
        _first(dkg_ref, _dot_hi(ck, fold2_ref[...]))
        _first_cols(db_ref, [(0, _colsum(dq)), (D, _colsum(dk_)), (D + KV, _colsum(dv_))])
        dqb, dkb, dvb = dq.astype(BF16), dk_.astype(BF16), dv_.astype(BF16)
        dqkv_ref[:, :D] = dqb
        dqkv_ref[:, D:D + KV] = dkb
        dqkv_ref[:, D + KV:] = dvb
        dh = (_dot_nt(dqb, w_ref[:, :D]) + _dot_nt(dkb, w_ref[:, D:D + KV]) + _dot_nt(dvb, w_ref[:, D + KV:]))
        xv = x_ref[...]
        dxn, dg = _rms_bwd(dh, xv, _rms(xv), g_ref[...])
        dx = dxo_ref[...] + dxn
        dx_ref[...] = dx
        dxb_ref[...] = dx.astype(BF16)
        _first(dg_ref, dg)

    return pl.pallas_call(
        body, name="attn_qkv_bwd", grid=(S // tm,),
        in_specs=[_rows(tm * NPAIR, 128), _rows(tm, KV), _rows(tm, KV), _rows(tm, QKV), _rows(tm, D), _rows(tm, D),
                  _const((1, D)), _const((D, QKV)), _const((1, D)), _const((1, KV)),
                  _const((D, 128)), _const((128, D)), _const((KV, 128)), _const((128, KV)),
                  _const((D, 128)), _const((KV, 128))],
        out_specs=[_rows(tm, QKV), _rows(tm, D), _rows(tm, D), _acc((1, QKV)), _acc((1, D)), _acc((1, 128)), _acc((1, 128))],
        out_shape=[_sds((S, QKV), BF16), _sds((S, D), F32), _sds((S, D), BF16), _sds((1, QKV), F32), _sds((1, D), F32),
                   _sds((1, 128), F32), _sds((1, 128), F32)],
        scratch_shapes=[pltpu.VMEM((tm, D), F32)],
        compiler_params=_cp(48))(dqs, dk, dv, raw, x, dxo, g, w, qg, kg, e16, e16t, e2, e2t, fold, fold2)


def _conv_mid_bwd(dx, dxb, c, lng, lnb, w):
    S = dx.shape[0]
    tm = min(512, S)

    def body(dx_ref, dxb_ref, c_ref, lng_ref, lnb_ref, w_ref, dc_ref, dbo_ref, dlg_ref, dlb_ref, ddwb_ref):
        _first(dbo_ref, _colsum(dx_ref[...]))
        ds = _dot_nt(dxb_ref[...], w_ref[...])
        c = c_ref[...]
        xc = c - jnp.mean(c, axis=-1, keepdims=True)
        rstd = lax.rsqrt(jnp.mean(xc * xc, axis=-1, keepdims=True) + EPS)
        ch = xc * rstd
        y = ch * lng_ref[...] + lnb_ref[...]
        sg = jax.nn.sigmoid(y)
        dy = ds * (sg * (1.0 + y * (1.0 - sg)))
        _first(dlg_ref, _colsum(dy * ch))
        _first(dlb_ref, _colsum(dy))
        dch = dy * lng_ref[...]
        dc = rstd * (dch - jnp.mean(dch, axis=-1, keepdims=True) - ch * jnp.mean(dch * ch, axis=-1, keepdims=True))
        dc_ref[...] = dc
        _first(ddwb_ref, _colsum(dc))

    return pl.pallas_call(
        body, name="conv_mid_bwd", grid=(S // tm,),
        in_specs=[_rows(tm, D), _rows(tm, D), _rows(tm, D), _const((1, D)), _const((1, D)), _const((D, D))],
        out_specs=[_rows(tm, D), _acc((1, D)), _acc((1, D)), _acc((1, D)), _acc((1, D))],
        out_shape=[_sds((S, D), F32)] + [_sds((1, D), F32)] * 4,
        compiler_params=_cp(40))(dx, dxb, c, lng, lnb, w)


def _dwconv_bwd(dc, glu, dw, carry=None):
    S = dc.shape[0]
    tm = min(256, S)
    hb = tm // HALO
    nsteps = S // tm

    def body(dc_ref, nxt_ref, gl_ref, halo_ref, dw_ref, dgl_ref, ddw_ref, ext, dext, esh, dsh):
        i = pl.program_id(0)
        _fill_ext(ext, halo_ref, gl_ref, tm, i > 0)
        dext[0:tm, :] = dc_ref[...]
        dext[tm:tm + HALO, :] = jnp.where(i < nsteps - 1, nxt_ref[...], 0.0)
        _shift_copies(esh, ext, tm + 24)
        _shift_copies(dsh, dext, tm + 24)

        @pl.when(i == 0)
        def _():
            ddw_ref[...] = jnp.zeros_like(ddw_ref)

        for c0 in range(0, D, 512):
            for r0 in range(0, tm, 32):
                acc = jnp.zeros((32, 512), F32)
                for j in range(CONVW):
                    acc = acc + dw_ref[CONVW - 1 - j:CONVW - j, c0:c0 + 512] * _tap(dext, dsh, j, r0, c0)
                dgl_ref[r0:r0 + 32, c0:c0 + 512] = acc
            for k in range(CONVW):
                acc = jnp.zeros((32, 512), F32)
                for r0 in range(0, tm, 32):
                    acc = acc + dext[r0:r0 + 32, c0:c0 + 512] * _tap(ext, esh, 2 + k, r0, c0)
                ddw_ref[k:k + 1, c0:c0 + 512] += _colsum(acc)

    return _carried_call(
        body, carry, name="dwconv_bwd", grid=(nsteps,),
        in_specs=[_rows(tm, D), pl.BlockSpec((HALO, D), lambda i: (jnp.minimum((i + 1) * hb, S // HALO - 1), 0)),
                  _rows(tm, D), pl.BlockSpec((HALO, D), lambda i: (jnp.maximum(i * hb - 1, 0), 0)), _const((32, D))],
        out_specs=[_rows(tm, D), _acc((32, D))],
        out_shape=[_sds((S, D), F32), _sds((32, D), F32)],
        scratch_shapes=[pltpu.VMEM((tm + HALO, D), F32), pltpu.VMEM((tm + HALO, D), F32),
                        pltpu.VMEM((7, tm + 24, D), F32), pltpu.VMEM((7, tm + 24, D), F32)],
        compiler_params=_cp(48), args=[dc, dc, glu, glu, dw])


def _conv_in_bwd(dglu, u, x, dxo, g, w):
    S = x.shape[0]
    tm = min(512, S)

    def body(dgl_ref, u_ref, x_ref, dxo_ref, g_ref, w_ref, du_ref, dx_ref, db_ref, dg_ref):
        dgl = dgl_ref[...]
        a = u_ref[:, :D]
        sg = jax.nn.sigmoid(u_ref[:, D:])
        da = dgl * sg
        dgt = dgl * a * sg * (1.0 - sg)
        _first_cols(db_ref, [(0, _colsum(da)), (D, _colsum(dgt))])
        dab, dgb = da.astype(BF16), dgt.astype(BF16)
        du_ref[:, :D] = dab
        du_ref[:, D:] = dgb
        dh = _dot_nt(dab, w_ref[:, :D]) + _dot_nt(dgb, w_ref[:, D:])
        xv = x_ref[...]
        dxn, dg = _rms_bwd(dh, xv, _rms(xv), g_ref[...])
        dx_ref[...] = dxo_ref[...] + dxn
        _first(dg_ref, dg)

    return pl.pallas_call(
        body, name="conv_in_bwd", grid=(S // tm,),
        in_specs=[_rows(tm, D), _rows(tm, 2 * D), _rows(tm, D), _rows(tm, D), _const((1, D)), _const((D, 2 * D))],
        out_specs=[_rows(tm, 2 * D), _rows(tm, D), _acc((1, 2 * D)), _acc((1, D))],
        out_shape=[_sds((S, 2 * D), BF16), _sds((S, D), F32), _sds((1, 2 * D), F32), _sds((1, D), F32)],
        compiler_params=_cp(48))(dglu, u, x, dxo, g, w)


def _coords():
    return lax.axis_index("x"), lax.axis_index("y"), lax.axis_index("c")


def _split(refs, *counts):
    out, k = [], 0
    for n in counts:
        out.append(refs[k:k + n])
        k += n
    return out


def _carried_call(body, carry, *, name, grid, in_specs, out_specs, out_shape, scratch_shapes, compiler_params, args):
    if carry is None:
        return pl.pallas_call(body, name=name, grid=grid, in_specs=in_specs, out_specs=out_specs, out_shape=out_shape,
                              scratch_shapes=scratch_shapes, compiler_params=compiler_params)(*args)
    counts = (len(in_specs), len(carry.args), len(out_specs), len(carry.out_shape), len(scratch_shapes), len(carry.scratch))
    nsteps = grid[0]

    def wrapped(*refs):
        ins, cin, outs, cout, scr, cscr = _split(refs, *counts)
        i = pl.program_id(0)
        carry.top(i, nsteps, cin, cout, cscr)
        body(*ins, *outs, *scr)
        carry.bottom(i, nsteps, cin, cout, cscr)

    return pl.pallas_call(
        wrapped, name=name, grid=grid, in_specs=list(in_specs) + carry.in_specs,
        out_specs=list(out_specs) + carry.out_specs, out_shape=list(out_shape) + carry.out_shape,
        scratch_shapes=list(scratch_shapes) + carry.scratch, compiler_params=compiler_params)(*args, *carry.args)


def _routes():
    x, y, c = _coords()
    return (x, y, c), (x, y, 1 - c), [(1 - x, y), (x, 1 - y), (1 - x, 1 - y)]


def _gather_copies(srcs, dsts, send, recv):
    me, sib, chips = _routes()
    c = me[2]

    def cp(a, k, block, to, own=False):
        idx = 4 * block[0] + 2 * block[1] + block[2]
        return pltpu.make_async_remote_copy(
            src_ref=srcs[a] if own else dsts[a].at[idx], dst_ref=dsts[a].at[idx], send_sem=send.at[a, k],
            recv_sem=recv.at[a, k], device_id=to, device_id_type=MESH)

    own, ici_in, fwd, sib_in = [], [], [], []
    for a in range(len(srcs)):
        own += [cp(a, 0, me, sib, True)] + [cp(a, 1 + j, me, (*ch, c), True) for j, ch in enumerate(chips)]
        ici_in += [cp(a, 1 + j, (*ch, c), me) for j, ch in enumerate(chips)]
        fwd += [cp(a, 4 + j, (*ch, c), sib) for j, ch in enumerate(chips)]
        sib_in += [cp(a, 0, sib, me)] + [cp(a, 4 + j, (*ch, 1 - c), me) for j, ch in enumerate(chips)]
    return own, ici_in, fwd, sib_in


class _Gather:
    def __init__(self, srcs):
        self.idx = [i for _, i in srcs]
        self.args = [a for a, _ in srcs]
        n = len(srcs)
        self.in_specs = [pl.BlockSpec(memory_space=pl.ANY)] * n
        self.out_specs = [pl.BlockSpec(memory_space=pl.ANY)] * n
        self.out_shape = [_sds((NDEV,) + (a.shape if i is None else a.shape[1:]), BF16) for a, i in srcs]
        self.scratch = [pltpu.SemaphoreType.DMA((n, 7)), pltpu.SemaphoreType.DMA((n, 7)), pltpu.SemaphoreType.DMA((n,))]

    def _copies(self, ins, outs, scr):
        send, recv, lsem = scr
        srcs = [r if i is None else r.at[i] for r, i in zip(ins, self.idx)]
        me = _routes()[0]
        slot = 4 * me[0] + 2 * me[1] + me[2]
        local = [pltpu.make_async_copy(s_, d_.at[slot], lsem.at[a]) for a, (s_, d_) in enumerate(zip(srcs, outs))]
        return _gather_copies(srcs, outs, send, recv) + (local,)

    def top(self, i, nsteps, ins, outs, scr):
        @pl.when(i == 0)
        def _():
            own, _, _, _, local = self._copies(ins, outs, scr)
            for cp in local + own:
                cp.start()

        @pl.when(i == (3 * nsteps) // 4)
        def _():
            _, ici_in, fwd, _, _ = self._copies(ins, outs, scr)
            for a_, f_ in zip(ici_in, fwd):
                a_.wait_recv()
                f_.start()

    def bottom(self, i, nsteps, ins, outs, scr):
        @pl.when(i == nsteps - 1)
        def _():
            own, _, fwd, sib_in, local = self._copies(ins, outs, scr)
            for cp in sib_in:
                cp.wait_recv()
            for cp in own + fwd:
                cp.wait_send()
            for cp in local:
                cp.wait()


def _scatter_copies(srcs, lands, send, recv):
    x, y, c = _coords()
    me = 4 * x + 2 * y + c
    sends, recvs = [], []
    for a in range(len(srcs)):
        for k in range(1, NDEV):
            peer = (x ^ ((k >> 2) & 1), y ^ ((k >> 1) & 1), c ^ (k & 1))
            pidx = me ^ k
            sends.append(pltpu.make_async_remote_copy(
                src_ref=srcs[a].at[pidx], dst_ref=lands[a].at[me], send_sem=send.at[a, k - 1],
                recv_sem=recv.at[a, k - 1], device_id=peer, device_id_type=MESH))
            recvs.append(pltpu.make_async_remote_copy(
                src_ref=srcs[a].at[pidx], dst_ref=lands[a].at[pidx], send_sem=send.at[a, k - 1],
                recv_sem=recv.at[a, k - 1], device_id=peer, device_id_type=MESH))
    return sends, recvs


class _Scatter:
    def __init__(self, srcs):
        n = len(srcs)
        self.args = list(srcs)
        self.in_specs = [pl.BlockSpec(memory_space=pl.ANY)] * n
        self.out_specs = [pl.BlockSpec(memory_space=pl.ANY)] * n
        self.out_shape = [_sds(a.shape, BF16) for a in srcs]
        self.scratch = [pltpu.SemaphoreType.DMA((n, 7)), pltpu.SemaphoreType.DMA((n, 7)), pltpu.SemaphoreType.DMA((n,))]

    def _copies(self, ins, outs, scr):
        send, recv, lsem = scr
        x, y, c = _coords()
        me = 4 * x + 2 * y + c
        local = [pltpu.make_async_copy(s_.at[me], d_.at[me], lsem.at[a]) for a, (s_, d_) in enumerate(zip(ins, outs))]
        return _scatter_copies(ins, outs, send, recv) + (local,)

    def top(self, i, nsteps, ins, outs, scr):
        @pl.when(i == 0)
        def _():
            sends, _, local = self._copies(ins, outs, scr)
            for cp in local + sends:
                cp.start()

    def bottom(self, i, nsteps, ins, outs, scr):
        @pl.when(i == nsteps - 1)
        def _():
            sends, recvs, local = self._copies(ins, outs, scr)
            for cp in recvs:
                cp.wait_recv()
            for cp in sends:
                cp.wait_send()
            for cp in local:
                cp.wait()


def _gather_first(w_in, w_out, w_qkv, w_o, w_up, w_down, smalls):
    ns = len(smalls)

    def body(*refs):
        (win_ref, wout_ref, wqkv_ref, wo_ref, wup_ref, wdn_ref), s_refs, (qkv_b, wo_b, up_b, dn_b, gin_ref, gout_ref), \
            gs_refs, (st_in, st_out, st_qkv, st_wo, st_up, st_dn, send, recv, lsem) = _split(refs, 6, ns, 6, ns, 9)
        for q in range(4):
            st_in[:, q * 256:(q + 1) * 256] = win_ref[q * 256:(q + 1) * 256, :].astype(BF16)
        st_out[...] = wout_ref[...].astype(BF16)
        st_qkv[...] = wqkv_ref[...].astype(BF16)
        st_wo[...] = wo_ref[...].astype(BF16)
        for l in range(2):
            for h_ in range(2):
                st_up[l, :, h_ * FFB:(h_ + 1) * FFB] = wup_ref[l, h_ * FFB:(h_ + 1) * FFB, :].astype(BF16)
            st_dn[l] = wdn_ref[l].astype(BF16)
        me = _routes()[0]
        slot = 4 * me[0] + 2 * me[1] + me[2]
        srcs, dsts = [st_in, st_out, *s_refs], [gin_ref, gout_ref, *gs_refs]
        local = [pltpu.make_async_copy(s_, d_, lsem.at[a]) for a, (s_, d_) in enumerate(
            [(st_qkv, qkv_b), (st_wo, wo_b), (st_up, up_b), (st_dn, dn_b), (st_in, gin_ref.at[slot]),
             (st_out, gout_ref.at[slot])])]
        for cp in local:
            cp.start()
        for s_ref, gs_ref in zip(s_refs, gs_refs):
            gs_ref[slot] = s_ref[...]
        own, ici_in, fwd, sib_in = _gather_copies(srcs, dsts, send, recv)
        for cp in own:
            cp.start()
        for a_, f_ in zip(ici_in, fwd):
            a_.wait_recv()
            f_.start()
        for cp in sib_in:
            cp.wait_recv()
        for cp in own + fwd:
            cp.wait_send()
        for cp in local:
            cp.wait()

    vmem = pl.BlockSpec(memory_space=pltpu.VMEM)
    hbm = pl.BlockSpec(memory_space=pl.ANY)
    return pl.pallas_call(
        body, name="gather_first",
        in_specs=[vmem] * (6 + ns),
        out_specs=[hbm] * 6 + [vmem] * ns,
        out_shape=[_sds((160, D), BF16), _sds((128, D), BF16), _sds((2, FFB, D), BF16), _sds((2, FFB, D), BF16),
                   _sds((NDEV, 256, D), BF16), _sds((NDEV, 128, D), BF16)] + [_sds((NDEV,) + a.shape, F32) for a in smalls],
        scratch_shapes=[pltpu.VMEM((256, D), BF16), pltpu.VMEM((128, D), BF16), pltpu.VMEM((160, D), BF16),
                        pltpu.VMEM((128, D), BF16), pltpu.VMEM((2, FFB, D), BF16), pltpu.VMEM((2, FFB, D), BF16),
                        pltpu.SemaphoreType.DMA((2 + ns, 7)), pltpu.SemaphoreType.DMA((2 + ns, 7)),
                        pltpu.SemaphoreType.DMA((6,))],
        compiler_params=pltpu.CompilerParams(vmem_limit_bytes=40 << 20))(w_in, w_out, w_qkv, w_o, w_up, w_down, *smalls)


SMALL_ROW = {"conv_norm_g": 0, "conv_b_in": 1, "conv_dw_b": 3, "conv_ln_g": 4, "conv_ln_b": 5, "conv_b_out": 6,
             "attn_norm_g": 7, "b_qkv": 8, "q_norm_g": 10, "k_norm_g": 11, "sinks": 12, "b_o": 13, "mlp_norm_g": 14,
             "conv_dw": 16, "rel_bias": 48}
SMALL_ROWS = 80


def _scatter_last(gsrc, small):
    n = gsrc.shape[1]
    names = list(small)
    flat = []
    for nm in names:
        flat += list(small[nm]) if isinstance(small[nm], (tuple, list)) else [small[nm]]

    def body(*refs):
        (g_ref,), s_refs, (land_ref, sred_ref), (pk, sland, send, recv, lsem) = _split(refs, 1, len(flat), 2, 5)
        x, y, c = _coords()
        me = 4 * x + 2 * y + c
        pk[...] = jnp.zeros_like(pk)
        k = 0
        for nm in names:
            cnt = len(small[nm]) if isinstance(small[nm], (tuple, list)) else 1
            r = SMALL_ROW[nm]
            for ref in s_refs[k:k + cnt]:
                rows, lanes = ref.shape
                for c0 in range(0, lanes, D):
                    w_ = min(D, lanes - c0)
                    pk[r:r + rows, 0:w_] = ref[:, c0:c0 + w_]
                    r += rows
            k += cnt
        small_out, small_in = [], []
        for k in range(1, NDEV):
            peer = (x ^ ((k >> 2) & 1), y ^ ((k >> 1) & 1), c ^ (k & 1))
            pidx = me ^ k
            small_out.append(pltpu.make_async_remote_copy(
                src_ref=pk, dst_ref=sland.at[me], send_sem=send.at[1, k - 1], recv_sem=recv.at[1, k - 1],
                device_id=peer, device_id_type=MESH))
            small_in.append(pltpu.make_async_remote_copy(
                src_ref=pk, dst_ref=sland.at[pidx], send_sem=send.at[1, k - 1], recv_sem=recv.at[1, k - 1],
                device_id=peer, device_id_type=MESH))
        big_out, big_in = _scatter_copies([g_ref], [land_ref], send, recv)
        mine = pltpu.make_async_copy(g_ref.at[me], land_ref.at[me], lsem)
        mine.start()
        for cp in big_out + small_out:
            cp.start()
        sland[me] = pk[...]
        for cp in big_in + small_in:
            cp.wait_recv()
        for cp in big_out + small_out:
            cp.wait_send()
        mine.wait()
        acc = sland[0]
        for s_ in range(1, NDEV):
            acc = acc + sland[s_]
        sred_ref[...] = acc

    vmem = pl.BlockSpec(memory_space=pltpu.VMEM)
    hbm = pl.BlockSpec(memory_space=pl.ANY)
    return pl.pallas_call(
        body, name="scatter_last",
        in_specs=[hbm] + [vmem] * len(flat),
        out_specs=[hbm, vmem],
        out_shape=[_sds((NDEV, n, D), BF16), _sds((SMALL_ROWS, D), F32)],
        scratch_shapes=[pltpu.VMEM((SMALL_ROWS, D), F32), pltpu.VMEM((NDEV, SMALL_ROWS, D), F32),
                        pltpu.SemaphoreType.DMA((2, 7)), pltpu.SemaphoreType.DMA((2, 7)), pltpu.SemaphoreType.DMA],
        compiler_params=pltpu.CompilerParams(vmem_limit_bytes=24 << 20))(gsrc, *flat)


def _adam_math(w, g, m, v):
    nm = B1 * m + (1.0 - B1) * g
    nv = B2 * v + (1.0 - B2) * jnp.square(g)
    m_hat = nm / (1.0 - B1 ** STEP)
    v_hat = nv / (1.0 - B2 ** STEP)
    return -LR * (m_hat / (jnp.sqrt(v_hat) + AEPS) + WD * w), nm, nv


def _adamw_small(sred, g_bqkv, W, M_, V_):
    names = list(W)
    nn = len(names)

    def body(*refs):
        (sred_ref, gq_ref), w_refs, m_refs, v_refs, o_refs = _split(refs, 2, nn, nn, nn, 4 * nn)
        x, y, c = _coords()
        me = 4 * x + 2 * y + c
        for i, nm in enumerate(names):
            w_ref, m_ref, v_ref = w_refs[i], m_refs[i], v_refs[i]
            g_ref, d_ref, nm_ref, nv_ref = o_refs[4 * i:4 * i + 4]
            rows, lanes = w_ref.shape
            r = SMALL_ROW[nm]
            for c0 in range(0, lanes, D):
                w_ = min(D, lanes - c0)
                cs = slice(c0, c0 + w_)
                if nm == "b_qkv":
                    g = gq_ref[...]
                elif nm in LANE_SHARDED:
                    g = jnp.zeros((rows, w_), F32)
                    for j in range(NDEV):
                        g = g + jnp.where(me == j, sred_ref[r:r + rows, j * w_:(j + 1) * w_], 0.0)
                else:
                    g = sred_ref[r:r + rows, 0:w_]
                    r += rows
                g_ref[:, cs] = g
                d_ref[:, cs], nm_ref[:, cs], nv_ref[:, cs] = _adam_math(w_ref[:, cs], g, m_ref[:, cs], v_ref[:, cs])

    vmem = pl.BlockSpec(memory_space=pltpu.VMEM)
    ws, ms, vs = ([t[nm] for nm in names] for t in (W, M_, V_))
    outs = pl.pallas_call(
        body, name="adamw_small", in_specs=[vmem] * (2 + 3 * nn), out_specs=[vmem] * (4 * nn),
        out_shape=[_sds(W[nm].shape, F32) for nm in names for _ in range(4)],
        compiler_params=pltpu.CompilerParams(vmem_limit_bytes=16 << 20))(sred, g_bqkv, *ws, *ms, *vs)
    return {nm: outs[4 * i:4 * i + 4] for i, nm in enumerate(names)}


def _reduce_adamw(lands, w, m, v, nsplit, name):
    L = len(lands)
    R = lands[0].shape[1]
    wl = D // nsplit
    tr = next(c for c in (256, 160, 128) if R % c == 0)
    nr = R // tr

    def body(*refs):
        l_refs, (w_ref, m_ref, v_ref), (g_ref, d_ref, nm_ref, nv_ref) = _split(refs, L, 3, 4)
        for l in range(L):
            @pl.when(pl.program_id(0) == l)
            def _(l=l):
                g = l_refs[l][0].astype(F32)
                for s_ in range(1, NDEV):
                    g = g + l_refs[l][s_].astype(F32)
                g_ref[...] = g
                d_ref[...], nm_ref[...], nv_ref[...] = _adam_math(w_ref[...], g, m_ref[...], v_ref[...])

    def land_spec(l):
        return pl.BlockSpec((NDEV, tr, wl), lambda ll, h, i: (0, jnp.where(ll == l, i, 0), jnp.where(ll == l, h, 0)))

    spec = pl.BlockSpec((None, tr, wl), lambda ll, h, i: (ll, h * nr + i, 0))
    return pl.pallas_call(
        body, name=name, grid=(L, nsplit, nr),
        in_specs=[land_spec(l) for l in range(L)] + [spec] * 3, out_specs=[spec] * 4,
        out_shape=[_sds(w.shape, F32)] * 4, compiler_params=_cp(48, 3))(*lands, w, m, v)


BIG = ("conv_w_in", "conv_w_out", "w_qkv", "w_o", "w_up", "w_down")
SMALL = ("conv_norm_g", "conv_b_in", "conv_dw", "conv_dw_b", "conv_ln_g", "conv_ln_b", "conv_b_out", "attn_norm_g",
         "b_qkv", "q_norm_g", "k_norm_g", "sinks", "b_o", "rel_bias", "mlp_norm_g")
LANE_SHARDED = ("conv_dw", "attn_norm_g", "b_o")


def _cols_to_blocks(a, n):
    M = a.shape[0]
    return a.reshape(M, NDEV, n).transpose(1, 0, 2).reshape(NDEV, M * n // D, D)


def kernel(x, conv_norm_g, conv_w_in, conv_b_in, conv_dw, conv_dw_b, conv_ln_g, conv_ln_b, conv_w_out, conv_b_out, attn_norm_g, w_qkv, b_qkv, q_norm_g, k_norm_g, sinks, w_o, b_o, rel_bias, mlp_norm_g, w_up, w_down, loss_target, m_conv_norm_g, m_conv_w_in, m_conv_b_in, m_conv_dw, m_conv_dw_b, m_conv_ln_g, m_conv_ln_b, m_conv_w_out, m_conv_b_out, m_attn_norm_g, m_w_qkv, m_b_qkv, m_q_norm_g, m_k_norm_g, m_sinks, m_w_o, m_b_o, m_rel_bias, m_mlp_norm_g, m_w_up, m_w_down, v_conv_norm_g, v_conv_w_in, v_conv_b_in, v_conv_dw, v_conv_dw_b, v_conv_ln_g, v_conv_ln_b, v_conv_w_out, v_conv_b_out, v_attn_norm_g, v_w_qkv, v_b_qkv, v_q_norm_g, v_k_norm_g, v_sinks, v_w_o, v_b_o, v_rel_bias, v_mlp_norm_g, v_w_up, v_w_down):
    names = list(SMALL) + list(BIG)
    loc = dict(locals())
    W = {n: loc[n] for n in names}
    M_ = {n: loc["m_" + n] for n in names}
    V_ = {n: loc["v_" + n] for n in names}
    me = 4 * lax.axis_index("x") + 2 * lax.axis_index("y") + lax.axis_index("c")
    xs = x[0]

    qkv_b, wo_b, up_b, dn_b, g_win, g_wout, g_dw, g_ang, g_bq, g_bo = _gather_first(
        conv_w_in[0], conv_w_out[0], w_qkv.reshape(160, D), w_o[0], w_up, w_down,
        [conv_dw[0], attn_norm_g, b_qkv, b_o])
    w_in = g_win.reshape(NDEV, 256, 4, 256).transpose(2, 1, 0, 3).reshape(D, 2 * D)
    w_out = g_wout.reshape(D, D)
    dw32 = jnp.pad(g_dw.transpose(1, 0, 2).reshape(CONVW, D), ((0, 1), (0, 0)))
    attn_g, bqkv, bo = g_ang.reshape(1, D), g_bq.reshape(1, QKV), g_bo.reshape(1, D)
    qg = jnp.tile(q_norm_g, (1, NH))
    kg = jnp.tile(k_norm_g, (1, NKV))
    e16, e16t, e2, e2t, fold, fold2 = _seg_mats()
    bkt = jnp.asarray(_bucket_table().T)
    bkt4 = jnp.asarray(np.tile(_bucket_table().T, (1, 4)))

    h0, u, glu = _conv_in_fwd(xs, conv_norm_g, w_in, conv_b_in)
    cc, sb, x1, wu0, wd0 = _conv_mid_fwd(glu, xs, dw32, conv_dw_b, conv_ln_g, conv_ln_b, w_out, conv_b_out,
                                         carry=_Gather([(up_b, 0), (dn_b, 0)]))
    h1, act0, x2, g_qkv, g_wo = _mlp_fwd(x1, mlp_norm_g[0:1], wu0, wd0, carry=_Gather([(qkv_b, None), (wo_b, None)]))
    wqkv = g_qkv.reshape(NDEV, D, 160).transpose(1, 0, 2).reshape(D, QKV)
    wo = g_wo.reshape(D, D)
    h2, raw, qn, kn, vv = _attn_qkv_fwd(x2, attn_g, wqkv, bqkv, qg, kg, e16, e16t, e2, e2t)
    kp = jnp.pad(kn, ((WIN, 0), (0, 0)))
    vp = jnp.pad(vv, ((WIN, 0), (0, 0)))
    ob, x3, wu1, wd1 = _attn_fwd(qn, kp, vp, bkt, rel_bias, sinks, wo, bo, x2, carry=_Gather([(up_b, 1), (dn_b, 1)]))
    h3, act1, dx4, dx4b, sq = _mlp_fwd(x3, mlp_norm_g[1:2], wu1, wd1, target=loss_target[0])
    loss = lax.psum(jnp.sum(sq) * (0.5 / D), ("x", "y", "c"))

    dup1, dx3, dx3b, dg_mlp1 = _mlp_bwd(dx4, x3, mlp_norm_g[1:2], act1, wu1, wd1)
    g_dn1 = _wgrad(act1, dx4b, 512, D, "wgrad_down1").reshape(NDEV, FFB, D)
    g_up1 = _wgrad_split(h3, dup1, 2, "wgrad_up1")
    dqs, dkp, dvp, d_sinks, d_rel, d_bo, l_up1, l_dn1 = _attn_bwd(
        dx3, dx3b, qn, kp, vp, bkt, bkt4, rel_bias, sinks, wo, carry=_Scatter([g_up1, g_dn1]))
    g_wo = _wgrad(ob, dx3b, D, 512, "wgrad_o").reshape(NDEV, 128, D)
    dqkv, dx2, dx2b, d_bqkv, d_attn_g, d_qg, d_kg = _attn_qkv_bwd(
        dqs, dkp[WIN:], dvp[WIN:], raw, x2, dx3, attn_g, wqkv, qg, kg, e16, e16t, e2, e2t, fold, fold2)
    g_wqkv = _cols_to_blocks(_wgrad(h2, dqkv, D, 640, "wgrad_qkv"), 160)
    dup0, dx1, dx1b, dg_mlp0, l_qkv, l_wo = _mlp_bwd(dx2, x1, mlp_norm_g[0:1], act0, wu0, wd0,
                                                     carry=_Scatter([g_wqkv, g_wo]))
    g_dn0 = _wgrad(act0, dx2b, 512, D, "wgrad_down0").reshape(NDEV, FFB, D)
    g_up0 = _wgrad_split(h1, dup0, 2, "wgrad_up0")
    dc, d_bout, d_lng, d_lnb, d_dwb = _conv_mid_bwd(dx1, dx1b, cc, conv_ln_g, conv_ln_b, w_out)
    g_wout = _wgrad(sb, dx1b, D, 512, "wgrad_conv_out").reshape(NDEV, 128, D)
    dglu, d_dw, l_wout, l_up0, l_dn0 = _dwconv_bwd(dc, glu, dw32, carry=_Scatter([g_wout, g_up0, g_dn0]))
    du, grad_x, d_bin, d_cng = _conv_in_bwd(dglu, u, xs, dx1, conv_norm_g, w_in)
    g_win = _wgrad_split(h0, du, 4, "wgrad_conv_in")

    l_win, sred = _scatter_last(g_win, {
        "conv_norm_g": d_cng, "conv_b_in": d_bin, "conv_dw": d_dw, "conv_dw_b": d_dwb, "conv_ln_g": d_lng,
        "conv_ln_b": d_lnb, "conv_b_out": d_bout, "attn_norm_g": d_attn_g, "b_qkv": d_bqkv, "q_norm_g": d_qg,
        "k_norm_g": d_kg, "sinks": d_sinks, "b_o": d_bo, "rel_bias": d_rel, "mlp_norm_g": (dg_mlp0, dg_mlp1)})

    big = {}
    for n, lands, nsplit in (("conv_w_in", [l_win], 4), ("conv_w_out", [l_wout], 1), ("w_o", [l_wo], 1),
                             ("w_up", [l_up0, l_up1], 2), ("w_down", [l_dn0, l_dn1], 1)):
        big[n] = _reduce_adamw(lands, W[n], M_[n], V_[n], nsplit, "adamw_" + n)
    flat = lambda t: t.reshape(1, 160, D)
    big["w_qkv"] = [t.reshape(1, D, 160) for t in
                    _reduce_adamw([l_qkv], flat(w_qkv), flat(m_w_qkv), flat(v_w_qkv), 1, "adamw_w_qkv")]
    r = SMALL_ROW["b_qkv"]
    g_bqkv = lax.dynamic_slice_in_dim(jnp.concatenate([sred[r:r + 1], sred[r + 1:r + 2, :QKV - D]], axis=1),
                                      me * 160, 160, axis=1)
    two_d = lambda t: {n: (t[n][0] if n == "conv_dw" else t[n]) for n in SMALL}
    small = _adamw_small(sred, g_bqkv, two_d(W), two_d(M_), two_d(V_))
    small["conv_dw"] = [t[None] for t in small["conv_dw"]]
    outs = [{**{n: big[n][k] for n in BIG}, **{n: small[n][k] for n in SMALL}} for k in range(4)]
    G, outs = outs[0], outs[1:]
    order = ["conv_norm_g", "conv_w_in", "conv_b_in", "conv_dw", "conv_dw_b", "conv_ln_g", "conv_ln_b", "conv_w_out",
             "conv_b_out", "attn_norm_g", "w_qkv", "b_qkv", "q_norm_g", "k_norm_g", "sinks", "w_o", "b_o", "rel_bias",
             "mlp_norm_g", "w_up", "w_down"]
    return (loss, grad_x[None], *[G[n] for n in order], *[outs[0][n] for n in order],
            *[outs[1][n] for n in order], *[outs[2][n] for n in order])
```

```python
import math

import numpy as np
import jax
import jax.numpy as jnp
from jax import lax
from jax.experimental import pallas as pl
from jax.experimental.pallas import tpu as pltpu

F32 = jnp.float32
BF16 = jnp.bfloat16

D = 1024
DFF = 4096
NH, NKV, HD = 16, 2, 64
NPAIR = NH // 2
KV = NKV * HD
QKV = D + 2 * KV
CONVW = 31
WIN = 128
NBKT = 32
EPS = 1e-6
NEG = -1e30
NDEV = 8
FFB = DFF // NDEV
HALO = 32

LR, B1, B2, AEPS, WD, STEP = 0.001, 0.9, 0.999, 1e-08, 0.01, 10

MESH = pl.DeviceIdType.MESH


def _cp(vmem_mb, n_axes=1):
    return pltpu.CompilerParams(dimension_semantics=("arbitrary",) * n_axes, vmem_limit_bytes=vmem_mb << 20)


def _const(shape):
    nd = len(shape)
    return pl.BlockSpec(shape, lambda *_: (0,) * nd, pipeline_mode=pl.Buffered(1))


def _acc(shape):
    nd = len(shape)
    return pl.BlockSpec(shape, lambda *_: (0,) * nd)


def _rows(tm, n):
    return pl.BlockSpec((tm, n), lambda i: (i, 0))


def _sds(shape, dtype):
    return jax.ShapeDtypeStruct(shape, dtype)


def _dot(a, b):
    return jnp.dot(a, b, preferred_element_type=F32)


def _dot_nt(a, b):
    return lax.dot_general(a, b, (((1,), (1,)), ((), ())), preferred_element_type=F32)


def _dot_tn(a, b):
    return lax.dot_general(a, b, (((0,), (0,)), ((), ())), preferred_element_type=F32)


def _dot_hi(x, e):
    x1 = x.astype(BF16)
    x2 = (x - x1.astype(F32)).astype(BF16)
    return _dot(x1, e) + _dot(x2, e)


def _rms(x):
    return lax.rsqrt(jnp.mean(x * x, axis=-1, keepdims=True) + EPS)


def _rms_bwd(dh, x, r, g):
    xh = x * r
    dxh = dh * g
    dx = r * (dxh - xh * jnp.mean(dxh * xh, axis=-1, keepdims=True))
    return dx, jnp.sum(dh * xh, axis=0, keepdims=True)


def _colsum(a):
    return jnp.sum(a, axis=0, keepdims=True)


def _first(ref, val):
    @pl.when(pl.program_id(0) == 0)
    def _():
        ref[...] = jnp.zeros_like(ref)
    ref[...] += val


def _first_cols(ref, vals):
    @pl.when(pl.program_id(0) == 0)
    def _():
        ref[...] = jnp.zeros_like(ref)
    for c0, val in vals:
        ref[:, c0:c0 + val.shape[1]] += val


def _conv_in_fwd(x, g, w, b):
    S = x.shape[0]
    tm = min(512, S)

    def body(x_ref, g_ref, w_ref, b_ref, h_ref, u_ref, glu_ref):
        xv = x_ref[...]
        hb = (xv * _rms(xv) * g_ref[...]).astype(BF16)
        h_ref[...] = hb
        u = _dot(hb, w_ref[...]) + b_ref[...]
        u_ref[...] = u.astype(BF16)
        glu_ref[...] = u[:, :D] * jax.nn.sigmoid(u[:, D:])

    return pl.pallas_call(
        body, name="conv_in_fwd", grid=(S // tm,),
        in_specs=[_rows(tm, D), _const((1, D)), _const((D, 2 * D)), _const((1, 2 * D))],
        out_specs=[_rows(tm, D), _rows(tm, 2 * D), _rows(tm, D)],
        out_shape=[_sds((S, D), BF16), _sds((S, 2 * D), BF16), _sds((S, D), F32)],
        compiler_params=_cp(48))(x, g, w, b)


def _fill_ext(ext, halo_ref, cur_ref, tm, keep):
    ext[0:HALO, :] = jnp.where(keep, halo_ref[...], 0.0)
    ext[HALO:HALO + tm, :] = cur_ref[...]


def _shift_copies(dst, src, n):
    for s_ in range(1, 8):
        dst[s_ - 1, 0:n, :] = src[s_:s_ + n, :]


def _tap(src, sh, o, r0, c0):
    a, s_ = divmod(o, 8)
    ref = src if s_ == 0 else sh.at[s_ - 1]
    return ref[r0 + 8 * a:r0 + 8 * a + 32, c0:c0 + 512]


def _conv_mid_fwd(glu, x, dw, dwb, lng, lnb, w, b, carry=None):
    S = x.shape[0]
    tm = min(256, S)
    hb = tm // HALO

    def body(gl_ref, halo_ref, x_ref, dw_ref, dwb_ref, lng_ref, lnb_ref, w_ref, b_ref, c_ref, s_ref, xo_ref, ext, esh):
        i = pl.program_id(0)
        _fill_ext(ext, halo_ref, gl_ref, tm, i > 0)
        _shift_copies(esh, ext, tm + 24)
        for r0 in range(0, tm, 32):
            for c0 in range(0, D, 512):
                acc = jnp.zeros((32, 512), F32) + dwb_ref[:, c0:c0 + 512]
                for k in range(CONVW):
                    acc = acc + dw_ref[k:k + 1, c0:c0 + 512] * _tap(ext, esh, 2 + k, r0, c0)
                c_ref[r0:r0 + 32, c0:c0 + 512] = acc
        c = c_ref[...]
        mu = jnp.mean(c, axis=-1, keepdims=True)
        xc = c - mu
        y = xc * lax.rsqrt(jnp.mean(xc * xc, axis=-1, keepdims=True) + EPS) * lng_ref[...] + lnb_ref[...]
        sb = (y * jax.nn.sigmoid(y)).astype(BF16)
        s_ref[...] = sb
        xo_ref[...] = x_ref[...] + _dot(sb, w_ref[...]) + b_ref[...]

    return _carried_call(
        body, carry, name="conv_mid_fwd", grid=(S // tm,),
        in_specs=[_rows(tm, D), pl.BlockSpec((HALO, D), lambda i: (jnp.maximum(i * hb - 1, 0), 0)), _rows(tm, D),
                  _const((32, D)), _const((1, D)), _const((1, D)), _const((1, D)), _const((D, D)), _const((1, D))],
        out_specs=[_rows(tm, D), _rows(tm, D), _rows(tm, D)],
        out_shape=[_sds((S, D), F32), _sds((S, D), BF16), _sds((S, D), F32)],
        scratch_shapes=[pltpu.VMEM((tm + HALO, D), F32), pltpu.VMEM((7, tm + 24, D), F32)],
        compiler_params=_cp(48), args=[glu, glu, x, dw, dwb, lng, lnb, w, b])


def _mlp_fwd(x, g, wu, wd, target=None, carry=None):
    S = x.shape[0]
    tm = min(512, S)
    last = target is not None

    def body(*refs):
        if last:
            x_ref, g_ref, wu_ref, wd_ref, t_ref, h_ref, act_ref, dy_ref, dyb_ref, sq_ref, acc = refs
        else:
            x_ref, g_ref, wu_ref, wd_ref, h_ref, act_ref, xo_ref, acc = refs
        xv = x_ref[...]
        hb = (xv * _rms(xv) * g_ref[...]).astype(BF16)
        h_ref[...] = hb
        for j in range(NDEV):
            up = _dot(hb[:, :FFB], wu_ref[j, :, :FFB]) + _dot(hb[:, FFB:], wu_ref[j, :, FFB:])
            a = jnp.square(jnp.maximum(up, 0.0)).astype(BF16)
            act_ref[:, j * FFB:(j + 1) * FFB] = a
            if j == 0:
                acc[...] = _dot(a, wd_ref[j])
            else:
                acc[...] += _dot(a, wd_ref[j])
        y = xv + acc[...]
        if last:
            diff = y - t_ref[...]
            dy = diff * (1.0 / D)
            dy_ref[...] = dy
            dyb_ref[...] = dy.astype(BF16)
            _first(sq_ref, _colsum(diff * diff))
        else:
            xo_ref[...] = y

    in_specs = [_rows(tm, D), _const((1, D)), _const((NDEV, FFB, D)), _const((NDEV, FFB, D))]
    args = [x, g, wu, wd]
    out_specs = [_rows(tm, D), _rows(tm, DFF)]
    out_shape = [_sds((S, D), BF16), _sds((S, DFF), BF16)]
    if last:
        in_specs.append(_rows(tm, D))
        args.append(target)
        out_specs += [_rows(tm, D), _rows(tm, D), _acc((1, D))]
        out_shape += [_sds((S, D), F32), _sds((S, D), BF16), _sds((1, D), F32)]
    else:
        out_specs.append(_rows(tm, D))
        out_shape.append(_sds((S, D), F32))
    return _carried_call(
        body, carry, name="mlp_fwd_loss" if last else "mlp_fwd", grid=(S // tm,),
        in_specs=in_specs, out_specs=out_specs, out_shape=out_shape,
        scratch_shapes=[pltpu.VMEM((tm, D), F32)],
        compiler_params=_cp(52), args=args)


def _mlp_bwd(dy, x, g, act, wu, wd, carry=None):
    S = x.shape[0]
    tm = min(512, S)

    def body(dy_ref, x_ref, g_ref, act_ref, wu_ref, wd_ref, dup_ref, dx_ref, dxb_ref, dg_ref, acc):
        db = dy_ref[...].astype(BF16)
        for j in range(NDEV):
            dact = _dot_nt(db, wd_ref[j])
            a = act_ref[:, j * FFB:(j + 1) * FFB].astype(F32)
            dup = (dact * (2.0 * jnp.sqrt(a))).astype(BF16)
            dup_ref[:, j * FFB:(j + 1) * FFB] = dup
            for h_ in range(2):
                part = _dot_nt(dup, wu_ref[j, :, h_ * FFB:(h_ + 1) * FFB])
                if j == 0:
                    acc[:, h_ * FFB:(h_ + 1) * FFB] = part
                else:
                    acc[:, h_ * FFB:(h_ + 1) * FFB] += part
        xv = x_ref[...]
        dxn, dg = _rms_bwd(acc[...], xv, _rms(xv), g_ref[...])
        dx = dy_ref[...] + dxn
        dx_ref[...] = dx
        dxb_ref[...] = dx.astype(BF16)
        _first(dg_ref, dg)

    return _carried_call(
        body, carry, name="mlp_bwd", grid=(S // tm,),
        in_specs=[_rows(tm, D), _rows(tm, D), _const((1, D)), _rows(tm, DFF),
                  _const((NDEV, FFB, D)), _const((NDEV, FFB, D))],
        out_specs=[_rows(tm, DFF), _rows(tm, D), _rows(tm, D), _acc((1, D))],
        out_shape=[_sds((S, DFF), BF16), _sds((S, D), F32), _sds((S, D), BF16), _sds((1, D), F32)],
        scratch_shapes=[pltpu.VMEM((tm, D), F32)],
        compiler_params=_cp(58), args=[dy, x, g, act, wu, wd])


def _wgrad(a, b, tm, tn, name):
    S, M = a.shape
    N = b.shape[1]
    tk = min(4096, S)
    nk = S // tk

    def body(a_ref, b_ref, o_ref, acc):
        k = pl.program_id(2)

        @pl.when(k == 0)
        def _():
            acc[...] = jnp.zeros_like(acc)

        acc[...] += _dot_tn(a_ref[...], b_ref[...])

        @pl.when(k == nk - 1)
        def _():
            o_ref[...] = acc[...].astype(BF16)

    return pl.pallas_call(
        body, name=name, grid=(M // tm, N // tn, nk),
        in_specs=[pl.BlockSpec((tk, tm), lambda i, j, k: (k, i)), pl.BlockSpec((tk, tn), lambda i, j, k: (k, j))],
        out_specs=pl.BlockSpec((tm, tn), lambda i, j, k: (i, j)),
        out_shape=_sds((M, N), BF16),
        scratch_shapes=[pltpu.VMEM((tm, tn), F32)],
        compiler_params=_cp(48, 3))(a, b)


def _wgrad_split(a, b, nsplit, name):
    S, M = a.shape
    N = b.shape[1]
    wb = N // NDEV
    tm, tn = M // nsplit, D
    nb = tn // wb
    tk = min(4096, S)
    nk = S // tk

    def body(a_ref, b_ref, o_ref, acc):
        k = pl.program_id(2)

        @pl.when(k == 0)
        def _():
            acc[...] = jnp.zeros_like(acc)

        acc[...] += _dot_tn(a_ref[...], b_ref[...])

        @pl.when(k == nk - 1)
        def _():
            for jj in range(nb):
                o_ref[jj] = acc[:, jj * wb:(jj + 1) * wb].astype(BF16)

    return pl.pallas_call(
        body, name=name, grid=(nsplit, N // tn, nk),
        in_specs=[pl.BlockSpec((tk, tm), lambda i, j, k: (k, i)), pl.BlockSpec((tk, tn), lambda i, j, k: (k, j))],
        out_specs=pl.BlockSpec((nb, tm, wb), lambda i, j, k: (j, 0, i)),
        out_shape=_sds((NDEV, tm, nsplit * wb), BF16),
        scratch_shapes=[pltpu.VMEM((tm, tn), F32)],
        compiler_params=_cp(48, 3))(a, b)


def _bucket_table():
    q = np.arange(WIN)[:, None]
    k = np.arange(2 * WIN)[None, :]
    dist = q + WIN - k
    n = np.maximum(dist, 0)
    max_exact = NBKT // 2
    large = max_exact + (np.log(np.maximum(n, 1).astype(np.float32) / max_exact)
                         / math.log(WIN / max_exact) * (NBKT - max_exact)).astype(np.int32)
    large = np.minimum(large, NBKT - 1)
    bkt = np.where(n < max_exact, n, large).astype(np.int32)
    return np.where((dist >= 0) & (dist < WIN), bkt, -1).astype(np.int32)


def _seg_mats():
    e16 = np.zeros((D, 128), np.float32)
    e16[np.arange(D), np.arange(D) // HD] = 1.0
    e2 = np.zeros((KV, 128), np.float32)
    e2[np.arange(KV), np.arange(KV) // HD] = 1.0
    fold = np.zeros((D, 128), np.float32)
    fold[np.arange(D), np.arange(D) % HD] = 1.0
    fold2 = np.zeros((KV, 128), np.float32)
    fold2[np.arange(KV), np.arange(KV) % HD] = 1.0
    return [jnp.asarray(m, BF16) for m in (e16, e16.T, e2, e2.T, fold, fold2)]


def _head_rms(t, e, et):
    r = lax.rsqrt(_dot_hi(t * t, e) * (1.0 / HD) + EPS)
    return _dot_hi(r, et)


def _attn_qkv_fwd(x, g, w, b, qg, kg, e16, e16t, e2, e2t):
    S = x.shape[0]
    tm = min(512, S)

    def body(x_ref, g_ref, w_ref, b_ref, qg_ref, kg_ref, e16_ref, e16t_ref, e2_ref, e2t_ref,
             h_ref, raw_ref, qn_ref, kn_ref, v_ref):
        xv = x_ref[...]
        hb = (xv * _rms(xv) * g_ref[...]).astype(BF16)
        h_ref[...] = hb
        raw = _dot(hb, w_ref[...]) + b_ref[...]
        raw_ref[...] = raw
        q = raw[:, :D]
        k = raw[:, D:D + KV]
        qn = (q * _head_rms(q, e16_ref[...], e16t_ref[...]) * qg_ref[...] * 0.125).astype(BF16)
        for bb in range(tm // WIN):
            for p in range(NPAIR):
                r = (bb * NPAIR + p) * WIN
                qn_ref[r:r + WIN, :] = qn[bb * WIN:(bb + 1) * WIN, p * 128:(p + 1) * 128]
        kn_ref[...] = (k * _head_rms(k, e2_ref[...], e2t_ref[...]) * kg_ref[...]).astype(BF16)
        v_ref[...] = raw[:, D + KV:].astype(BF16)

    return pl.pallas_call(
        body, name="attn_qkv_fwd", grid=(S // tm,),
        in_specs=[_rows(tm, D), _const((1, D)), _const((D, QKV)), _const((1, QKV)), _const((1, D)), _const((1, KV)),
                  _const((D, 128)), _const((128, D)), _const((KV, 128)), _const((128, KV))],
        out_specs=[_rows(tm, D), _rows(tm, QKV), _rows(tm * NPAIR, 128), _rows(tm, KV), _rows(tm, KV)],
        out_shape=[_sds((S, D), BF16), _sds((S, QKV), F32), _sds((S * NPAIR, 128), BF16), _sds((S, KV), BF16),
                   _sds((S, KV), BF16)],
        compiler_params=_cp(48))(x, g, w, b, qg, kg, e16, e16t, e2, e2t)


def _build_bias(bkt_ref, rb_ref, sk_ref, bias_sc, sk_sc):
    bkt = bkt_ref[...]
    row = lax.broadcasted_iota(jnp.int32, (2 * WIN, WIN), 0)
    for h in range(NH):
        p, e = h // 2, h % 2
        g, pp = p // 4, p % 4

        def add(bk, acc, h=h):
            return acc + jnp.where(bkt == bk, rb_ref[bk, h], 0.0)

        bias = lax.fori_loop(0, NBKT, add, jnp.where(bkt < 0, NEG, 0.0).astype(F32))
        bias_sc[0, g, e, :, pp * WIN:(pp + 1) * WIN] = bias
        bias_sc[1, g, e, :, pp * WIN:(pp + 1) * WIN] = jnp.where(row < WIN, NEG, bias)
        sk_sc[g, e, :, pp * WIN:(pp + 1) * WIN] = jnp.zeros((1, WIN), F32) + sk_ref[0, h]


def _band(ref, p0, g0):
    return jnp.concatenate([ref[pl.ds(p0, WIN), :].astype(F32), ref[pl.ds(g0, WIN), :].astype(F32)], axis=0)


def _stacks(f, mlo):
    r = pltpu.roll(f, HD, 1)
    z = jnp.zeros_like(f)
    return ((jnp.where(mlo, f, z).astype(BF16), jnp.where(mlo, z, r).astype(BF16)),
            (jnp.where(mlo, r, z).astype(BF16), jnp.where(mlo, z, f).astype(BF16)))


def _unstack(d, mlo):
    z = jnp.zeros_like(d[0][0])
    return (jnp.where(mlo, d[0][0], z) + pltpu.roll(jnp.where(mlo, z, d[0][1]), HD, 1)
            + pltpu.roll(jnp.where(mlo, d[1][0], z), HD, 1) + jnp.where(mlo, z, d[1][1]))


def _softmax_sink(l, sk):
    m = jnp.maximum(jnp.max(l, axis=0, keepdims=True), sk)
    ex = jnp.exp(l - m)
    es = jnp.exp(sk - m)
    inv = 1.0 / (jnp.sum(ex, axis=0, keepdims=True) + es)
    return ex * inv, es * inv


GROWS = 4 * WIN


def _attn_fwd(qn, kn, vn, bktt, rel_bias, sinks, wo, bo, x, carry=None):
    S = x.shape[0]
    tq = min(512, S)
    nblk = tq // WIN

    def body(q_ref, k_ref, v_ref, bkt_ref, rb_ref, sk_ref, wo_ref, bo_ref, x_ref, o_ref, xo_ref, bias_sc, sk_sc):
        i = pl.program_id(0)

        @pl.when(i == 0)
        def _():
            _build_bias(bkt_ref, rb_ref, sk_ref, bias_sc, sk_sc)

        mlo = lax.broadcasted_iota(jnp.int32, (2 * WIN, KV), 1) < HD

        def blk(bb, carry):
            r0 = pl.multiple_of(bb * WIN, WIN)
            g0 = pl.multiple_of(i * tq + bb * WIN, WIN)
            first = (g0 == 0).astype(jnp.int32)
            p0 = pl.multiple_of(jnp.maximum(g0 - WIN, 0), WIN)
            ks = _stacks(_band(k_ref, p0, g0), mlo)
            vs = _stacks(_band(v_ref, p0, g0), mlo)
            for g in range(NKV):
                qs = q_ref[pl.ds(pl.multiple_of(bb * (NPAIR * WIN) + g * GROWS, GROWS), GROWS), :]
                og = jnp.zeros((GROWS, 128), F32)
                for e in range(2):
                    l = _dot_nt(ks[g][e], qs) + bias_sc[first, g, e]
                    pr, _ = _softmax_sink(l, sk_sc[g, e])
                    og = og + _dot_tn(pr.astype(BF16), vs[g][e])
                for pp in range(4):
                    p = 4 * g + pp
                    o_ref[pl.ds(r0, WIN), p * 128:(p + 1) * 128] = og[pp * WIN:(pp + 1) * WIN].astype(BF16)
            return carry

        lax.fori_loop(0, nblk, blk, 0)
        xo_ref[...] = x_ref[...] + _dot(o_ref[...], wo_ref[...]) + bo_ref[...]

    smem = pl.BlockSpec(memory_space=pltpu.SMEM)
    return _carried_call(
        body, carry, name="attn_fwd", grid=(S // tq,),
        in_specs=[_rows(tq * NPAIR, 128), _const((S, KV)), _const((S, KV)), _const((2 * WIN, WIN)), smem, smem,
                  _const((D, D)), _const((1, D)), _rows(tq, D)],
        out_specs=[_rows(tq, D), _rows(tq, D)],
        out_shape=[_sds((S, D), BF16), _sds((S, D), F32)],
        scratch_shapes=[pltpu.VMEM((2, NKV, 2, 2 * WIN, GROWS), F32), pltpu.VMEM((NKV, 2, 1, GROWS), F32)],
        compiler_params=_cp(48), args=[qn, kn, vn, bktt, rel_bias, sinks, wo, bo, x])


def _attn_bwd(dx, dxb, qn, kn, vn, bktt, bktt4, rel_bias, sinks, wo, carry=None):
    S = dx.shape[0]
    tq = min(512, S)
    nblk = tq // WIN
    nsteps = S // tq

    def body(dx_ref, dxb_ref, q_ref, k_ref, v_ref, bkt_ref, bkt4_ref, rb_ref, sk_ref, wo_ref,
             dq_ref, dk_ref, dv_ref, dsk_ref, drb_ref, dbo_ref, bias_sc, sk_sc, dbias_sc, dsk_sc, do_sc):
        i = pl.program_id(0)

        @pl.when(i == 0)
        def _():
            _build_bias(bkt_ref, rb_ref, sk_ref, bias_sc, sk_sc)
            dbias_sc[...] = jnp.zeros_like(dbias_sc)
            dsk_sc[...] = jnp.zeros_like(dsk_sc)
            dk_ref[...] = jnp.zeros_like(dk_ref)
            dv_ref[...] = jnp.zeros_like(dv_ref)

        _first(dbo_ref, _colsum(dx_ref[...]))
        do = _dot_nt(dxb_ref[...], wo_ref[...]).astype(BF16)
        for bb in range(nblk):
            for p in range(NPAIR):
                r = (bb * NPAIR + p) * WIN
                do_sc[r:r + WIN, :] = do[bb * WIN:(bb + 1) * WIN, p * 128:(p + 1) * 128]

        mlo = lax.broadcasted_iota(jnp.int32, (2 * WIN, KV), 1) < HD

        def blk(bb, carry):
            g0 = pl.multiple_of(i * tq + bb * WIN, WIN)
            first = (g0 == 0).astype(jnp.int32)
            p0 = pl.multiple_of(jnp.maximum(g0 - WIN, 0), WIN)
            ks = _stacks(_band(k_ref, p0, g0), mlo)
            vs = _stacks(_band(v_ref, p0, g0), mlo)
            dks = [[None, None], [None, None]]
            dvs = [[None, None], [None, None]]
            for g in range(NKV):
                rows = pl.ds(pl.multiple_of(bb * (NPAIR * WIN) + g * GROWS, GROWS), GROWS)
                qs = q_ref[rows, :]
                dos = do_sc[rows, :]
                dqs = jnp.zeros((GROWS, 128), F32)
                for e in range(2):
                    l = _dot_nt(ks[g][e], qs) + bias_sc[first, g, e]
                    pr, ps = _softmax_sink(l, sk_sc[g, e])
                    dp = _dot_nt(vs[g][e], dos)
                    dr = jnp.sum(pr * dp, axis=0, keepdims=True)
                    dl = pr * (dp - dr)
                    dsk_sc[g, e] -= ps * dr
                    dbias_sc[g, e] += dl
                    dlb = dl.astype(BF16)
                    dqs = dqs + _dot_tn(dlb, ks[g][e])
                    dks[g][e] = _dot(dlb, qs)
                    dvs[g][e] = _dot(pr.astype(BF16), dos)
                dq_ref[rows, :] = dqs
            for ref, d in ((dk_ref, _unstack(dks, mlo)), (dv_ref, _unstack(dvs, mlo))):
                ref[pl.ds(p0, WIN), :] += d[:WIN]
                ref[pl.ds(g0, WIN), :] += d[WIN:]
            return carry

        lax.fori_loop(0, nblk, blk, 0)

        @pl.when(i == nsteps - 1)
        def _():
            bkt4v = bkt4_ref[...]
            rid = lax.broadcasted_iota(jnp.int32, (NBKT, 128), 0)
            lid = lax.broadcasted_iota(jnp.int32, (NBKT, 128), 1)
            lid1 = lax.broadcasted_iota(jnp.int32, (1, 128), 1)
            dsk = jnp.zeros((1, 128), F32)
            for g in range(NKV):
                for e in range(2):
                    for pp in range(4):
                        h = 2 * (4 * g + pp) + e
                        t = jnp.sum(dsk_sc[g, e, :, pp * WIN:(pp + 1) * WIN], axis=1, keepdims=True)
                        dsk = dsk + jnp.where(lid1 == h, t, 0.0)
            dsk_ref[...] = dsk

            def per_bucket(bk, acc):
                mb = bkt4v == bk
                for g in range(NKV):
                    for e in range(2):
                        t = jnp.sum(jnp.where(mb, dbias_sc[g, e], 0.0), axis=0, keepdims=True)
                        for pp in range(4):
                            h = 2 * (4 * g + pp) + e
                            tt = jnp.sum(t[:, pp * WIN:(pp + 1) * WIN], axis=1, keepdims=True)
                            acc = acc + jnp.where((rid == bk) & (lid == h), tt, 0.0)
                return acc

            drb_ref[...] = lax.fori_loop(0, NBKT, per_bucket, jnp.zeros((NBKT, 128), F32))

    smem = pl.BlockSpec(memory_space=pltpu.SMEM)
    return _carried_call(
        body, carry, name="attn_bwd", grid=(nsteps,),
        in_specs=[_rows(tq, D), _rows(tq, D), _rows(tq * NPAIR, 128), _const((S, KV)), _const((S, KV)),
                  _const((2 * WIN, WIN)), _const((2 * WIN, GROWS)), smem, smem, _const((D, D))],
        out_specs=[_rows(tq * NPAIR, 128), _acc((S, KV)), _acc((S, KV)), _acc((1, 128)), _acc((NBKT, 128)),
                   _acc((1, D))],
        out_shape=[_sds((S * NPAIR, 128), F32), _sds((S, KV), F32), _sds((S, KV), F32), _sds((1, 128), F32),
                   _sds((NBKT, 128), F32), _sds((1, D), F32)],
        scratch_shapes=[pltpu.VMEM((2, NKV, 2, 2 * WIN, GROWS), F32), pltpu.VMEM((NKV, 2, 1, GROWS), F32),
                        pltpu.VMEM((NKV, 2, 2 * WIN, GROWS), F32), pltpu.VMEM((NKV, 2, 1, GROWS), F32),
                        pltpu.VMEM((tq * NPAIR, 128), BF16)],
        compiler_params=_cp(52), args=[dx, dxb, qn, kn, vn, bktt, bktt4, rel_bias, sinks, wo])


def _head_norm_bwd(dn, t, e, et, gt):
    r = _head_rms(t, e, et)
    th = t * r
    dth = dn * gt
    dt = r * (dth - th * _dot_hi(_dot_hi(dth * th, e) * (1.0 / HD), et))
    return dt, _colsum(dn * th)


def _attn_qkv_bwd(dqs, dk, dv, raw, x, dxo, g, w, qg, kg, e16, e16t, e2, e2t, fold, fold2):
    S = x.shape[0]
    tm = min(512, S)

    def body(dq_ref, dk_ref, dv_ref, raw_ref, x_ref, dxo_ref, g_ref, w_ref, qg_ref, kg_ref,
             e16_ref, e16t_ref, e2_ref, e2t_ref, fold_ref, fold2_ref,
             dqkv_ref, dx_ref, dxb_ref, db_ref, dg_ref, dqg_ref, dkg_ref, dq_sc):
        for bb in range(tm // WIN):
            for p in range(NPAIR):
                r = (bb * NPAIR + p) * WIN
                dq_sc[bb * WIN:(bb + 1) * WIN, p * 128:(p + 1) * 128] = dq_ref[r:r + WIN, :]
        dq, cq = _head_norm_bwd(dq_sc[...] * 0.125, raw_ref[:, :D], e16_ref[...], e16t_ref[...], qg_ref[...])
        dk_, ck = _head_norm_bwd(dk_ref[...], raw_ref[:, D:D + KV], e2_ref[...], e2t_ref[...], kg_ref[...])
        dv_ = dv_ref[...]
        _first(dqg_ref, _dot_hi(cq, fold_ref[...]))
        _first(dkg_ref, _dot_hi(ck, fold2_ref[...]))
        _first_cols(db_ref, [(0, _colsum(dq)), (D, _colsum(dk_)), (D + KV, _colsum(dv_))])
        dqb, dkb, dvb = dq.astype(BF16), dk_.astype(BF16), dv_.astype(BF16)
        dqkv_ref[:, :D] = dqb
        dqkv_ref[:, D:D + KV] = dkb
        dqkv_ref[:, D + KV:] = dvb
        dh = (_dot_nt(dqb, w_ref[:, :D]) + _dot_nt(dkb, w_ref[:, D:D + KV]) + _dot_nt(dvb, w_ref[:, D + KV:]))
        xv = x_ref[...]
        dxn, dg = _rms_bwd(dh, xv, _rms(xv), g_ref[...])
        dx = dxo_ref[...] + dxn
        dx_ref[...] = dx
        dxb_ref[...] = dx.astype(BF16)
        _first(dg_ref, dg)

    return pl.pallas_call(
        body, name="attn_qkv_bwd", grid=(S // tm,),
        in_specs=[_rows(tm * NPAIR, 128), _rows(tm, KV), _rows(tm, KV), _rows(tm, QKV), _rows(tm, D), _rows(tm, D),
                  _const((1, D)), _const((D, QKV)), _const((1, D)), _const((1, KV)),
                  _const((D, 128)), _const((128, D)), _const((KV, 128)), _const((128, KV)),
                  _const((D, 128)), _const((KV, 128))],
        out_specs=[_rows(tm, QKV), _rows(tm, D), _rows(tm, D), _acc((1, QKV)), _acc((1, D)), _acc((1, 128)), _acc((1, 128))],
        out_shape=[_sds((S, QKV), BF16), _sds((S, D), F32), _sds((S, D), BF16), _sds((1, QKV), F32), _sds((1, D), F32),
                   _sds((1, 128), F32), _sds((1, 128), F32)],
        scratch_shapes=[pltpu.VMEM((tm, D), F32)],
        compiler_params=_cp(48))(dqs, dk, dv, raw, x, dxo, g, w, qg, kg, e16, e16t, e2, e2t, fold, fold2)


def _conv_mid_bwd(dx, dxb, c, lng, lnb, w):
    S = dx.shape[0]
    tm = min(512, S)

    def body(dx_ref, dxb_ref, c_ref, lng_ref, lnb_ref, w_ref, dc_ref, dbo_ref, dlg_ref, dlb_ref, ddwb_ref):
        _first(dbo_ref, _colsum(dx_ref[...]))
        ds = _dot_nt(dxb_ref[...], w_ref[...])
        c = c_ref[...]
        xc = c - jnp.mean(c, axis=-1, keepdims=True)
        rstd = lax.rsqrt(jnp.mean(xc * xc, axis=-1, keepdims=True) + EPS)
        ch = xc * rstd
        y = ch * lng_ref[...] + lnb_ref[...]
        sg = jax.nn.sigmoid(y)
        dy = ds * (sg * (1.0 + y * (1.0 - sg)))
        _first(dlg_ref, _colsum(dy * ch))
        _first(dlb_ref, _colsum(dy))
        dch = dy * lng_ref[...]
        dc = rstd * (dch - jnp.mean(dch, axis=-1, keepdims=True) - ch * jnp.mean(dch * ch, axis=-1, keepdims=True))
        dc_ref[...] = dc
        _first(ddwb_ref, _colsum(dc))

    return pl.pallas_call(
        body, name="conv_mid_bwd", grid=(S // tm,),
        in_specs=[_rows(tm, D), _rows(tm, D), _rows(tm, D), _const((1, D)), _const((1, D)), _const((D, D))],
        out_specs=[_rows(tm, D), _acc((1, D)), _acc((1, D)), _acc((1, D)), _acc((1, D))],
        out_shape=[_sds((S, D), F32)] + [_sds((1, D), F32)] * 4,
        compiler_params=_cp(40))(dx, dxb, c, lng, lnb, w)


def _dwconv_bwd(dc, glu, dw, carry=None):
    S = dc.shape[0]
    tm = min(256, S)
    hb = tm // HALO
    nsteps = S // tm

    def body(dc_ref, nxt_ref, gl_ref, halo_ref, dw_ref, dgl_ref, ddw_ref, ext, dext, esh, dsh):
        i = pl.program_id(0)
        _fill_ext(ext, halo_ref, gl_ref, tm, i > 0)
        dext[0:tm, :] = dc_ref[...]
        dext[tm:tm + HALO, :] = jnp.where(i < nsteps - 1, nxt_ref[...], 0.0)
        _shift_copies(esh, ext, tm + 24)
        _shift_copies(dsh, dext, tm + 24)

        @pl.when(i == 0)
        def _():
            ddw_ref[...] = jnp.zeros_like(ddw_ref)

        for c0 in range(0, D, 512):
            for r0 in range(0, tm, 32):
                acc = jnp.zeros((32, 512), F32)
                for j in range(CONVW):
                    acc = acc + dw_ref[CONVW - 1 - j:CONVW - j, c0:c0 + 512] * _tap(dext, dsh, j, r0, c0)
                dgl_ref[r0:r0 + 32, c0:c0 + 512] = acc
            for k in range(CONVW):
                acc = jnp.zeros((32, 512), F32)
                for r0 in range(0, tm, 32):
                    acc = acc + dext[r0:r0 + 32, c0:c0 + 512] * _tap(ext, esh, 2 + k, r0, c0)
                ddw_ref[k:k + 1, c0:c0 + 512] += _colsum(acc)

    return _carried_call(
        body, carry, name="dwconv_bwd", grid=(nsteps,),
        in_specs=[_rows(tm, D), pl.BlockSpec((HALO, D), lambda i: (jnp.minimum((i + 1) * hb, S // HALO - 1), 0)),
                  _rows(tm, D), pl.BlockSpec((HALO, D), lambda i: (jnp.maximum(i * hb - 1, 0), 0)), _const((32, D))],
        out_specs=[_rows(tm, D), _acc((32, D))],
        out_shape=[_sds((S, D), F32), _sds((32, D), F32)],
        scratch_shapes=[pltpu.VMEM((tm + HALO, D), F32), pltpu.VMEM((tm + HALO, D), F32),
                        pltpu.VMEM((7, tm + 24, D), F32), pltpu.VMEM((7, tm + 24, D), F32)],
        compiler_params=_cp(48), args=[dc, dc, glu, glu, dw])


def _conv_in_bwd(dglu, u, x, dxo, g, w):
    S = x.shape[0]
    tm = min(512, S)

    def body(dgl_ref, u_ref, x_ref, dxo_ref, g_ref, w_ref, du_ref, dx_ref, db_ref, dg_ref):
        dgl = dgl_ref[...]
        a = u_ref[:, :D].astype(F32)
        sg = jax.nn.sigmoid(u_ref[:, D:].astype(F32))
        da = dgl * sg
        dgt = dgl * a * sg * (1.0 - sg)
        _first_cols(db_ref, [(0, _colsum(da)), (D, _colsum(dgt))])
        dab, dgb = da.astype(BF16), dgt.astype(BF16)
        du_ref[:, :D] = dab
        du_ref[:, D:] = dgb
        dh = _dot_nt(dab, w_ref[:, :D]) + _dot_nt(dgb, w_ref[:, D:])
        xv = x_ref[...]
        dxn, dg = _rms_bwd(dh, xv, _rms(xv), g_ref[...])
        dx_ref[...] = dxo_ref[...] + dxn
        _first(dg_ref, dg)

    return pl.pallas_call(
        body, name="conv_in_bwd", grid=(S // tm,),
        in_specs=[_rows(tm, D), _rows(tm, 2 * D), _rows(tm, D), _rows(tm, D), _const((1, D)), _const((D, 2 * D))],
        out_specs=[_rows(tm, 2 * D), _rows(tm, D), _acc((1, 2 * D)), _acc((1, D))],
        out_shape=[_sds((S, 2 * D), BF16), _sds((S, D), F32), _sds((1, 2 * D), F32), _sds((1, D), F32)],
        compiler_params=_cp(48))(dglu, u, x, dxo, g, w)


def _coords():
    return lax.axis_index("x"), lax.axis_index("y"), lax.axis_index("c")


def _split(refs, *counts):
    out, k = [], 0
    for n in counts:
        out.append(refs[k:k + n])
        k += n
    return out


def _carried_call(body, carry, *, name, grid, in_specs, out_specs, out_shape, scratch_shapes, compiler_params, args):
    if carry is None:
        return pl.pallas_call(body, name=name, grid=grid, in_specs=in_specs, out_specs=out_specs, out_shape=out_shape,
                              scratch_shapes=scratch_shapes, compiler_params=compiler_params)(*args)
    counts = (len(in_specs), len(carry.args), len(out_specs), len(carry.out_shape), len(scratch_shapes), len(carry.scratch))
    nsteps = grid[0]

    def wrapped(*refs):
        ins, cin, outs, cout, scr, cscr = _split(refs, *counts)
        i = pl.program_id(0)
        carry.top(i, nsteps, cin, cout, cscr)
        body(*ins, *outs, *scr)
        carry.bottom(i, nsteps, cin, cout, cscr)

    return pl.pallas_call(
        wrapped, name=name, grid=grid, in_specs=list(in_specs) + carry.in_specs,
        out_specs=list(out_specs) + carry.out_specs, out_shape=list(out_shape) + carry.out_shape,
        scratch_shapes=list(scratch_shapes) + carry.scratch, compiler_params=compiler_params)(*args, *carry.args)


def _routes():
    x, y, c = _coords()
    return (x, y, c), (x, y, 1 - c), [(1 - x, y), (x, 1 - y), (1 - x, 1 - y)]


def _gather_copies(srcs, dsts, send, recv):
    me, sib, chips = _routes()
    c = me[2]

    def cp(a, k, block, to, own=False):
        idx = 4 * block[0] + 2 * block[1] + block[2]
        return pltpu.make_async_remote_copy(
            src_ref=srcs[a] if own else dsts[a].at[idx], dst_ref=dsts[a].at[idx], send_sem=send.at[a, k],
            recv_sem=recv.at[a, k], device_id=to, device_id_type=MESH)

    own, ici_in, fwd, sib_in = [], [], [], []
    for a in range(len(srcs)):
        own += [cp(a, 0, me, sib, True)] + [cp(a, 1 + j, me, (*ch, c), True) for j, ch in enumerate(chips)]
        ici_in += [cp(a, 1 + j, (*ch, c), me) for j, ch in enumerate(chips)]
        fwd += [cp(a, 4 + j, (*ch, c), sib) for j, ch in enumerate(chips)]
        sib_in += [cp(a, 0, sib, me)] + [cp(a, 4 + j, (*ch, 1 - c), me) for j, ch in enumerate(chips)]
    return own, ici_in, fwd, sib_in


class _Gather:
    def __init__(self, srcs):
        self.idx = [i for _, i in srcs]
        self.args = [a for a, _ in srcs]
        n = len(srcs)
        self.in_specs = [pl.BlockSpec(memory_space=pl.ANY)] * n
        self.out_specs = [pl.BlockSpec(memory_space=pl.ANY)] * n
        self.out_shape = [_sds((NDEV,) + (a.shape if i is None else a.shape[1:]), BF16) for a, i in srcs]
        self.scratch = [pltpu.SemaphoreType.DMA((n, 7)), pltpu.SemaphoreType.DMA((n, 7)), pltpu.SemaphoreType.DMA((n,))]

    def _copies(self, ins, outs, scr):
        send, recv, lsem = scr
        srcs = [r if i is None else r.at[i] for r, i in zip(ins, self.idx)]
        me = _routes()[0]
        slot = 4 * me[0] + 2 * me[1] + me[2]
        local = [pltpu.make_async_copy(s_, d_.at[slot], lsem.at[a]) for a, (s_, d_) in enumerate(zip(srcs, outs))]
        return _gather_copies(srcs, outs, send, recv) + (local,)

    def top(self, i, nsteps, ins, outs, scr):
        @pl.when(i == 0)
        def _():
            own, _, _, _, local = self._copies(ins, outs, scr)
            for cp in local + own:
                cp.start()

        @pl.when(i == (3 * nsteps) // 4)
        def _():
            _, ici_in, fwd, _, _ = self._copies(ins, outs, scr)
            for a_, f_ in zip(ici_in, fwd):
                a_.wait_recv()
                f_.start()

    def bottom(self, i, nsteps, ins, outs, scr):
        @pl.when(i == nsteps - 1)
        def _():
            own, _, fwd, sib_in, local = self._copies(ins, outs, scr)
            for cp in sib_in:
                cp.wait_recv()
            for cp in own + fwd:
                cp.wait_send()
            for cp in local:
                cp.wait()


def _scatter_copies(srcs, lands, send, recv):
    x, y, c = _coords()
    me = 4 * x + 2 * y + c
    sends, recvs = [], []
    for a in range(len(srcs)):
        for k in range(1, NDEV):
            peer = (x ^ ((k >> 2) & 1), y ^ ((k >> 1) & 1), c ^ (k & 1))
            pidx = me ^ k
            sends.append(pltpu.make_async_remote_copy(
                src_ref=srcs[a].at[pidx], dst_ref=lands[a].at[me], send_sem=send.at[a, k - 1],
                recv_sem=recv.at[a, k - 1], device_id=peer, device_id_type=MESH))
            recvs.append(pltpu.make_async_remote_copy(
                src_ref=srcs[a].at[pidx], dst_ref=lands[a].at[pidx], send_sem=send.at[a, k - 1],
                recv_sem=recv.at[a, k - 1], device_id=peer, device_id_type=MESH))
    return sends, recvs


class _Scatter:
    def __init__(self, srcs):
        n = len(srcs)
        self.args = list(srcs)
        self.in_specs = [pl.BlockSpec(memory_space=pl.ANY)] * n
        self.out_specs = [pl.BlockSpec(memory_space=pl.ANY)] * n
        self.out_shape = [_sds(a.shape, BF16) for a in srcs]
        self.scratch = [pltpu.SemaphoreType.DMA((n, 7)), pltpu.SemaphoreType.DMA((n, 7)), pltpu.SemaphoreType.DMA((n,))]

    def _copies(self, ins, outs, scr):
        send, recv, lsem = scr
        x, y, c = _coords()
        me = 4 * x + 2 * y + c
        local = [pltpu.make_async_copy(s_.at[me], d_.at[me], lsem.at[a]) for a, (s_, d_) in enumerate(zip(ins, outs))]
        return _scatter_copies(ins, outs, send, recv) + (local,)

    def top(self, i, nsteps, ins, outs, scr):
        @pl.when(i == 0)
        def _():
            sends, _, local = self._copies(ins, outs, scr)
            for cp in local + sends:
                cp.start()

    def bottom(self, i, nsteps, ins, outs, scr):
        @pl.when(i == nsteps - 1)
        def _():
            sends, recvs, local = self._copies(ins, outs, scr)
            for cp in recvs:
                cp.wait_recv()
            for cp in sends:
                cp.wait_send()
            for cp in local:
                cp.wait()


def _gather_first(w_in, w_out, w_qkv, w_o, w_up, w_down, smalls):
    ns = len(smalls)

    def body(*refs):
        (win_ref, wout_ref, wqkv_ref, wo_ref, wup_ref, wdn_ref), s_refs, (qkv_b, wo_b, up_b, dn_b, gin_ref, gout_ref), \
            gs_refs, (st_in, st_out, st_qkv, st_wo, st_up, st_dn, send, recv, lsem) = _split(refs, 6, ns, 6, ns, 9)
        for q in range(4):
            st_in[:, q * 256:(q + 1) * 256] = win_ref[q * 256:(q + 1) * 256, :].astype(BF16)
        st_out[...] = wout_ref[...].astype(BF16)
        st_qkv[...] = wqkv_ref[...].astype(BF16)
        st_wo[...] = wo_ref[...].astype(BF16)
        for l in range(2):
            for h_ in range(2):
                st_up[l, :, h_ * FFB:(h_ + 1) * FFB] = wup_ref[l, h_ * FFB:(h_ + 1) * FFB, :].astype(BF16)
            st_dn[l] = wdn_ref[l].astype(BF16)
        me = _routes()[0]
        slot = 4 * me[0] + 2 * me[1] + me[2]
        srcs, dsts = [st_in, st_out, *s_refs], [gin_ref, gout_ref, *gs_refs]
        local = [pltpu.make_async_copy(s_, d_, lsem.at[a]) for a, (s_, d_) in enumerate(
            [(st_qkv, qkv_b), (st_wo, wo_b), (st_up, up_b), (st_dn, dn_b), (st_in, gin_ref.at[slot]),
             (st_out, gout_ref.at[slot])])]
        for cp in local:
            cp.start()
        for s_ref, gs_ref in zip(s_refs, gs_refs):
            gs_ref[slot] = s_ref[...]
        own, ici_in, fwd, sib_in = _gather_copies(srcs, dsts, send, recv)
        for cp in own:
            cp.start()
        for a_, f_ in zip(ici_in, fwd):
            a_.wait_recv()
            f_.start()
        for cp in sib_in:
            cp.wait_recv()
        for cp in own + fwd:
            cp.wait_send()
        for cp in local:
            cp.wait()

    vmem = pl.BlockSpec(memory_space=pltpu.VMEM)
    hbm = pl.BlockSpec(memory_space=pl.ANY)
    return pl.pallas_call(
        body, name="gather_first",
        in_specs=[vmem] * (6 + ns),
        out_specs=[hbm] * 6 + [vmem] * ns,
        out_shape=[_sds((160, D), BF16), _sds((128, D), BF16), _sds((2, FFB, D), BF16), _sds((2, FFB, D), BF16),
                   _sds((NDEV, 256, D), BF16), _sds((NDEV, 128, D), BF16)] + [_sds((NDEV,) + a.shape, F32) for a in smalls],
        scratch_shapes=[pltpu.VMEM((256, D), BF16), pltpu.VMEM((128, D), BF16), pltpu.VMEM((160, D), BF16),
                        pltpu.VMEM((128, D), BF16), pltpu.VMEM((2, FFB, D), BF16), pltpu.VMEM((2, FFB, D), BF16),
                        pltpu.SemaphoreType.DMA((2 + ns, 7)), pltpu.SemaphoreType.DMA((2 + ns, 7)),
                        pltpu.SemaphoreType.DMA((6,))],
        compiler_params=pltpu.CompilerParams(vmem_limit_bytes=40 << 20))(w_in, w_out, w_qkv, w_o, w_up, w_down, *smalls)


SMALL_ROW = {"conv_norm_g": 0, "conv_b_in": 1, "conv_dw_b": 3, "conv_ln_g": 4, "conv_ln_b": 5, "conv_b_out": 6,
             "attn_norm_g": 7, "b_qkv": 8, "q_norm_g": 10, "k_norm_g": 11, "sinks": 12, "b_o": 13, "mlp_norm_g": 14,
             "conv_dw": 16}
SMALL_ROWS = 48


def _scatter_last(gsrc, small, d_rel):
    n = gsrc.shape[1]
    names = list(small)
    flat = []
    for nm in names:
        flat += list(small[nm]) if isinstance(small[nm], (tuple, list)) else [small[nm]]

    def body(*refs):
        (g_ref,), s_refs, (rel_ref,), (land_ref, sred_ref, rred_ref), (pk, sland, rland, send, recv, lsem) = _split(
            refs, 1, len(flat), 1, 3, 6)
        x, y, c = _coords()
        me = 4 * x + 2 * y + c
        pk[...] = jnp.zeros_like(pk)
        k = 0
        for nm in names:
            cnt = len(small[nm]) if isinstance(small[nm], (tuple, list)) else 1
            r = SMALL_ROW[nm]
            for ref in s_refs[k:k + cnt]:
                rows, lanes = ref.shape
                for c0 in range(0, lanes, D):
                    w_ = min(D, lanes - c0)
                    pk[r:r + rows, 0:w_] = ref[:, c0:c0 + w_]
                    r += rows
            k += cnt
        small_out, small_in = [], []
        for k in range(1, NDEV):
            peer = (x ^ ((k >> 2) & 1), y ^ ((k >> 1) & 1), c ^ (k & 1))
            pidx = me ^ k
            for a, (src, dst) in ((1, (pk, sland)), (2, (rel_ref, rland))):
                small_out.append(pltpu.make_async_remote_copy(
                    src_ref=src, dst_ref=dst.at[me], send_sem=send.at[a, k - 1], recv_sem=recv.at[a, k - 1],
                    device_id=peer, device_id_type=MESH))
                small_in.append(pltpu.make_async_remote_copy(
                    src_ref=src, dst_ref=dst.at[pidx], send_sem=send.at[a, k - 1], recv_sem=recv.at[a, k - 1],
                    device_id=peer, device_id_type=MESH))
        big_out, big_in = _scatter_copies([g_ref], [land_ref], send, recv)
        mine = pltpu.make_async_copy(g_ref.at[me], land_ref.at[me], lsem)
        mine.start()
        for cp in big_out + small_out:
            cp.start()
        sland[me] = pk[...]
        rland[me] = rel_ref[...]
        for cp in big_in + small_in:
            cp.wait_recv()
        for cp in big_out + small_out:
            cp.wait_send()
        mine.wait()
        for land, red in ((sland, sred_ref), (rland, rred_ref)):
            acc = land[0]
            for s_ in range(1, NDEV):
                acc = acc + land[s_]
            red[...] = acc

    vmem = pl.BlockSpec(memory_space=pltpu.VMEM)
    hbm = pl.BlockSpec(memory_space=pl.ANY)
    return pl.pallas_call(
        body, name="scatter_last",
        in_specs=[hbm] + [vmem] * (len(flat) + 1),
        out_specs=[hbm, vmem, vmem],
        out_shape=[_sds((NDEV, n, D), BF16), _sds((SMALL_ROWS, D), F32), _sds((NBKT, 128), F32)],
        scratch_shapes=[pltpu.VMEM((SMALL_ROWS, D), F32), pltpu.VMEM((NDEV, SMALL_ROWS, D), F32),
                        pltpu.VMEM((NDEV, NBKT, 128), F32),
                        pltpu.SemaphoreType.DMA((3, 7)), pltpu.SemaphoreType.DMA((3, 7)), pltpu.SemaphoreType.DMA],
        compiler_params=pltpu.CompilerParams(vmem_limit_bytes=24 << 20))(gsrc, *flat, d_rel)


def _adam_math(w, g, m, v):
    nm = B1 * m + (1.0 - B1) * g
    nv = B2 * v + (1.0 - B2) * jnp.square(g)
    m_hat = nm / (1.0 - B1 ** STEP)
    v_hat = nv / (1.0 - B2 ** STEP)
    return -LR * (m_hat / (jnp.sqrt(v_hat) + AEPS) + WD * w), nm, nv


def _adamw_small(sred, rred, g_bqkv, W, M_, V_):
    names = list(W)
    nn = len(names)

    def body(*refs):
        (sred_ref, rred_ref, gq_ref), w_refs, m_refs, v_refs, o_refs = _split(refs, 3, nn, nn, nn, 4 * nn)
        x, y, c = _coords()
        me = 4 * x + 2 * y + c
        for i, nm in enumerate(names):
            w_ref, m_ref, v_ref = w_refs[i], m_refs[i], v_refs[i]
            g_ref, d_ref, nm_ref, nv_ref = o_refs[4 * i:4 * i + 4]
            rows, lanes = w_ref.shape
            r = SMALL_ROW.get(nm)
            for c0 in range(0, lanes, D):
                w_ = min(D, lanes - c0)
                cs = slice(c0, c0 + w_)
                if nm == "b_qkv":
                    g = gq_ref[...]
                elif nm == "rel_bias":
                    g = rred_ref[:, 0:w_]
                elif nm in LANE_SHARDED:
                    g = jnp.zeros((rows, w_), F32)
                    for j in range(NDEV):
                        g = g + jnp.where(me == j, sred_ref[r:r + rows, j * w_:(j + 1) * w_], 0.0)
                else:
                    g = sred_ref[r:r + rows, 0:w_]
                    r += rows
                g_ref[:, cs] = g
                d_ref[:, cs], nm_ref[:, cs], nv_ref[:, cs] = _adam_math(w_ref[:, cs], g, m_ref[:, cs], v_ref[:, cs])

    vmem = pl.BlockSpec(memory_space=pltpu.VMEM)
    ws, ms, vs = ([t[nm] for nm in names] for t in (W, M_, V_))
    outs = pl.pallas_call(
        body, name="adamw_small", in_specs=[vmem] * (3 + 3 * nn), out_specs=[vmem] * (4 * nn),
        out_shape=[_sds(W[nm].shape, F32) for nm in names for _ in range(4)],
        compiler_params=pltpu.CompilerParams(vmem_limit_bytes=16 << 20))(sred, rred, g_bqkv, *ws, *ms, *vs)
    return {nm: outs[4 * i:4 * i + 4] for i, nm in enumerate(names)}


def _reduce_adamw(lands, w, m, v, nsplit, name):
    L = len(lands)
    R = lands[0].shape[1]
    wl = D // nsplit
    tr = next(c for c in (256, 160, 128) if R % c == 0)
    nr = R // tr

    def body(*refs):
        l_refs, (w_ref, m_ref, v_ref), (g_ref, d_ref, nm_ref, nv_ref) = _split(refs, L, 3, 4)
        for l in range(L):
            @pl.when(pl.program_id(0) == l)
            def _(l=l):
                g = l_refs[l][0].astype(F32)
                for s_ in range(1, NDEV):
                    g = g + l_refs[l][s_].astype(F32)
                g_ref[...] = g
                d_ref[...], nm_ref[...], nv_ref[...] = _adam_math(w_ref[...], g, m_ref[...], v_ref[...])

    def land_spec(l):
        return pl.BlockSpec((NDEV, tr, wl), lambda ll, h, i: (0, jnp.where(ll == l, i, 0), jnp.where(ll == l, h, 0)))

    spec = pl.BlockSpec((None, tr, wl), lambda ll, h, i: (ll, h * nr + i, 0))
    return pl.pallas_call(
        body, name=name, grid=(L, nsplit, nr),
        in_specs=[land_spec(l) for l in range(L)] + [spec] * 3, out_specs=[spec] * 4,
        out_shape=[_sds(w.shape, F32)] * 4, compiler_params=_cp(48, 3))(*lands, w, m, v)


BIG = ("conv_w_in", "conv_w_out", "w_qkv", "w_o", "w_up", "w_down")
SMALL = ("conv_norm_g", "conv_b_in", "conv_dw", "conv_dw_b", "conv_ln_g", "conv_ln_b", "conv_b_out", "attn_norm_g",
         "b_qkv", "q_norm_g", "k_norm_g", "sinks", "b_o", "rel_bias", "mlp_norm_g")
LANE_SHARDED = ("conv_dw", "attn_norm_g", "b_o")


def _cols_to_blocks(a, n):
    M = a.shape[0]
    return a.reshape(M, NDEV, n).transpose(1, 0, 2).reshape(NDEV, M * n // D, D)


def kernel(x, conv_norm_g, conv_w_in, conv_b_in, conv_dw, conv_dw_b, conv_ln_g, conv_ln_b, conv_w_out, conv_b_out, attn_norm_g, w_qkv, b_qkv, q_norm_g, k_norm_g, sinks, w_o, b_o, rel_bias, mlp_norm_g, w_up, w_down, loss_target, m_conv_norm_g, m_conv_w_in, m_conv_b_in, m_conv_dw, m_conv_dw_b, m_conv_ln_g, m_conv_ln_b, m_conv_w_out, m_conv_b_out, m_attn_norm_g, m_w_qkv, m_b_qkv, m_q_norm_g, m_k_norm_g, m_sinks, m_w_o, m_b_o, m_rel_bias, m_mlp_norm_g, m_w_up, m_w_down, v_conv_norm_g, v_conv_w_in, v_conv_b_in, v_conv_dw, v_conv_dw_b, v_conv_ln_g, v_conv_ln_b, v_conv_w_out, v_conv_b_out, v_attn_norm_g, v_w_qkv, v_b_qkv, v_q_norm_g, v_k_norm_g, v_sinks, v_w_o, v_b_o, v_rel_bias, v_mlp_norm_g, v_w_up, v_w_down):
    names = list(SMALL) + list(BIG)
    loc = dict(locals())
    W = {n: loc[n] for n in names}
    M_ = {n: loc["m_" + n] for n in names}
    V_ = {n: loc["v_" + n] for n in names}
    me = 4 * lax.axis_index("x") + 2 * lax.axis_index("y") + lax.axis_index("c")
    xs = x[0]

    qkv_b, wo_b, up_b, dn_b, g_win, g_wout, g_dw, g_ang, g_bq, g_bo = _gather_first(
        conv_w_in[0], conv_w_out[0], w_qkv.reshape(160, D), w_o[0], w_up, w_down,
        [conv_dw[0], attn_norm_g, b_qkv, b_o])
    w_in = g_win.reshape(NDEV, 256, 4, 256).transpose(2, 1, 0, 3).reshape(D, 2 * D)
    w_out = g_wout.reshape(D, D)
    dw32 = jnp.pad(g_dw.transpose(1, 0, 2).reshape(CONVW, D), ((0, 1), (0, 0)))
    attn_g, bqkv, bo = g_ang.reshape(1, D), g_bq.reshape(1, QKV), g_bo.reshape(1, D)
    qg = jnp.tile(q_norm_g, (1, NH))
    kg = jnp.tile(k_norm_g, (1, NKV))
    e16, e16t, e2, e2t, fold, fold2 = _seg_mats()
    bkt = jnp.asarray(_bucket_table().T)
    bkt4 = jnp.asarray(np.tile(_bucket_table().T, (1, 4)))

    h0, u, glu = _conv_in_fwd(xs, conv_norm_g, w_in, conv_b_in)
    cc, sb, x1, wu0, wd0 = _conv_mid_fwd(glu, xs, dw32, conv_dw_b, conv_ln_g, conv_ln_b, w_out, conv_b_out,
                                         carry=_Gather([(up_b, 0), (dn_b, 0)]))
    h1, act0, x2, g_qkv, g_wo = _mlp_fwd(x1, mlp_norm_g[0:1], wu0, wd0, carry=_Gather([(qkv_b, None), (wo_b, None)]))
    wqkv = g_qkv.reshape(NDEV, D, 160).transpose(1, 0, 2).reshape(D, QKV)
    wo = g_wo.reshape(D, D)
    h2, raw, qn, kn, vv = _attn_qkv_fwd(x2, attn_g, wqkv, bqkv, qg, kg, e16, e16t, e2, e2t)
    ob, x3, wu1, wd1 = _attn_fwd(qn, kn, vv, bkt, rel_bias, sinks, wo, bo, x2, carry=_Gather([(up_b, 1), (dn_b, 1)]))
    h3, act1, dx4, dx4b, sq = _mlp_fwd(x3, mlp_norm_g[1:2], wu1, wd1, target=loss_target[0])
    loss = lax.psum(jnp.sum(sq) * (0.5 / D), ("x", "y", "c"))

    dup1, dx3, dx3b, dg_mlp1 = _mlp_bwd(dx4, x3, mlp_norm_g[1:2], act1, wu1, wd1)
    g_dn1 = _wgrad(act1, dx4b, 512, D, "wgrad_down1").reshape(NDEV, FFB, D)
    g_up1 = _wgrad_split(h3, dup1, 2, "wgrad_up1")
    dqs, dk, dv, d_sinks, d_rel, d_bo, l_up1, l_dn1 = _attn_bwd(
        dx3, dx3b, qn, kn, vv, bkt, bkt4, rel_bias, sinks, wo, carry=_Scatter([g_up1, g_dn1]))
    g_wo = _wgrad(ob, dx3b, D, 512, "wgrad_o").reshape(NDEV, 128, D)
    dqkv, dx2, dx2b, d_bqkv, d_attn_g, d_qg, d_kg = _attn_qkv_bwd(
        dqs, dk, dv, raw, x2, dx3, attn_g, wqkv, qg, kg, e16, e16t, e2, e2t, fold, fold2)
    g_wqkv = _cols_to_blocks(_wgrad(h2, dqkv, D, 640, "wgrad_qkv"), 160)
    dup0, dx1, dx1b, dg_mlp0, l_qkv, l_wo = _mlp_bwd(dx2, x1, mlp_norm_g[0:1], act0, wu0, wd0,
                                                     carry=_Scatter([g_wqkv, g_wo]))
    g_dn0 = _wgrad(act0, dx2b, 512, D, "wgrad_down0").reshape(NDEV, FFB, D)
    g_up0 = _wgrad_split(h1, dup0, 2, "wgrad_up0")
    dc, d_bout, d_lng, d_lnb, d_dwb = _conv_mid_bwd(dx1, dx1b, cc, conv_ln_g, conv_ln_b, w_out)
    g_wout = _wgrad(sb, dx1b, D, 512, "wgrad_conv_out").reshape(NDEV, 128, D)
    dglu, d_dw, l_wout, l_up0, l_dn0 = _dwconv_bwd(dc, glu, dw32, carry=_Scatter([g_wout, g_up0, g_dn0]))
    du, grad_x, d_bin, d_cng = _conv_in_bwd(dglu, u, xs, dx1, conv_norm_g, w_in)
    g_win = _wgrad_split(h0, du, 4, "wgrad_conv_in")

    l_win, sred, rred = _scatter_last(g_win, {
        "conv_norm_g": d_cng, "conv_b_in": d_bin, "conv_dw": d_dw, "conv_dw_b": d_dwb, "conv_ln_g": d_lng,
        "conv_ln_b": d_lnb, "conv_b_out": d_bout, "attn_norm_g": d_attn_g, "b_qkv": d_bqkv, "q_norm_g": d_qg,
        "k_norm_g": d_kg, "sinks": d_sinks, "b_o": d_bo, "mlp_norm_g": (dg_mlp0, dg_mlp1)}, d_rel)

    big = {}
    for n, lands, nsplit in (("conv_w_in", [l_win], 4), ("conv_w_out", [l_wout], 1), ("w_o", [l_wo], 1),
                             ("w_up", [l_up0, l_up1], 2), ("w_down", [l_dn0, l_dn1], 1)):
        big[n] = _reduce_adamw(lands, W[n], M_[n], V_[n], nsplit, "adamw_" + n)
    flat = lambda t: t.reshape(1, 160, D)
    big["w_qkv"] = [t.reshape(1, D, 160) for t in
                    _reduce_adamw([l_qkv], flat(w_qkv), flat(m_w_qkv), flat(v_w_qkv), 1, "adamw_w_qkv")]
    r = SMALL_ROW["b_qkv"]
    g_bqkv = lax.dynamic_slice_in_dim(jnp.concatenate([sred[r:r + 1], sred[r + 1:r + 2, :QKV - D]], axis=1),
                                      me * 160, 160, axis=1)
    two_d = lambda t: {n: (t[n][0] if n == "conv_dw" else t[n]) for n in SMALL}
    small = _adamw_small(sred, rred, g_bqkv, two_d(W), two_d(M_), two_d(V_))
    small["conv_dw"] = [t[None] for t in small["conv_dw"]]
    outs = [{**{n: big[n][k] for n in BIG}, **{n: small[n][k] for n in SMALL}} for k in range(4)]
    G, outs = outs[0], outs[1:]
    order = ["conv_norm_g", "conv_w_in", "conv_b_in", "conv_dw", "conv_dw_b", "conv_ln_g", "conv_ln_b", "conv_w_out",
             "conv_b_out", "attn_norm_g", "w_qkv", "b_qkv", "q_norm_g", "k_norm_g", "sinks", "w_o", "b_o", "rel_bias",
             "mlp_norm_g", "w_up", "w_down"]
    return (loss, grad_x[None], *[G[n] for n in order], *[outs[0][n] for n in order],
            *[outs[1][n] for n in order], *[outs[2][n] for n in order])
```

```python
import math

import numpy as np
import jax
import jax.numpy as jnp
from jax import lax
from jax.experimental import pallas as pl
from jax.experimental.pallas import tpu as pltpu

F32 = jnp.float32
BF16 = jnp.bfloat16

D = 1024
DFF = 4096
NH, NKV, HD = 16, 2, 64
NPAIR = NH // 2
KV = NKV * HD
QKV = D + 2 * KV
CONVW = 31
WIN = 128
NBKT = 32
EPS = 1e-6
NEG = -1e30
NDEV = 8
FFB = DFF // NDEV
HALO = 32

LR, B1, B2, AEPS, WD, STEP = 0.001, 0.9, 0.999, 1e-08, 0.01, 10

MESH = pl.DeviceIdType.MESH


def _cp(vmem_mb, n_axes=1):
    return pltpu.CompilerParams(dimension_semantics=("arbitrary",) * n_axes, vmem_limit_bytes=vmem_mb << 20)


def _const(shape):
    nd = len(shape)
    return pl.BlockSpec(shape, lambda *_: (0,) * nd, pipeline_mode=pl.Buffered(1))


def _acc(shape):
    nd = len(shape)
    return pl.BlockSpec(shape, lambda *_: (0,) * nd)


def _rows(tm, n):
    return pl.BlockSpec((tm, n), lambda i: (i, 0))


def _sds(shape, dtype):
    return jax.ShapeDtypeStruct(shape, dtype)


def _dot(a, b):
    return jnp.dot(a, b, preferred_element_type=F32)


def _dot_nt(a, b):
    return lax.dot_general(a, b, (((1,), (1,)), ((), ())), preferred_element_type=F32)


def _dot_tn(a, b):
    return lax.dot_general(a, b, (((0,), (0,)), ((), ())), preferred_element_type=F32)


def _dot_hi(x, e):
    x1 = x.astype(BF16)
    x2 = (x - x1.astype(F32)).astype(BF16)
    return _dot(x1, e) + _dot(x2, e)


def _rms(x):
    return lax.rsqrt(jnp.mean(x * x, axis=-1, keepdims=True) + EPS)


def _rms_bwd(dh, x, r, g):
    xh = x * r
    dxh = dh * g
    dx = r * (dxh - xh * jnp.mean(dxh * xh, axis=-1, keepdims=True))
    return dx, jnp.sum(dh * xh, axis=0, keepdims=True)


def _colsum(a):
    return jnp.sum(a, axis=0, keepdims=True)


def _first(ref, val):
    @pl.when(pl.program_id(0) == 0)
    def _():
        ref[...] = jnp.zeros_like(ref)
    ref[...] += val


def _first_cols(ref, vals):
    @pl.when(pl.program_id(0) == 0)
    def _():
        ref[...] = jnp.zeros_like(ref)
    for c0, val in vals:
        ref[:, c0:c0 + val.shape[1]] += val


QW = 256


def _conv_in_fwd(x, g, gw, b, carry=None):
    S = x.shape[0]
    tm = min(512, S)

    def body(x_ref, g_ref, w_ref, b_ref, h_ref, u_ref, glu_ref):
        xv = x_ref[...]
        hb = (xv * _rms(xv) * g_ref[...]).astype(BF16)
        h_ref[...] = hb
        us = []
        for j in range(NDEV):
            uj = b_ref[:, j * QW:(j + 1) * QW] + _dot(hb[:, 0:QW], w_ref[j, :, 0:QW])
            for q in range(1, 4):
                uj = uj + _dot(hb[:, q * QW:(q + 1) * QW], w_ref[j, :, q * QW:(q + 1) * QW])
            u_ref[:, j * QW:(j + 1) * QW] = uj.astype(BF16)
            us.append(uj)
        for j in range(NDEV // 2):
            glu_ref[:, j * QW:(j + 1) * QW] = us[j] * jax.nn.sigmoid(us[j + NDEV // 2])

    return _carried_call(
        body, carry, name="conv_in_fwd", grid=(S // tm,),
        in_specs=[_rows(tm, D), _const((1, D)), _const((NDEV, QW, D)), _const((1, 2 * D))],
        out_specs=[_rows(tm, D), _rows(tm, 2 * D), _rows(tm, D)],
        out_shape=[_sds((S, D), BF16), _sds((S, 2 * D), BF16), _sds((S, D), F32)],
        scratch_shapes=[], compiler_params=_cp(48), args=[x, g, gw, b])


def _fill_ext(ext, halo_ref, cur_ref, tm, keep):
    ext[0:HALO, :] = jnp.where(keep, halo_ref[...], 0.0)
    ext[HALO:HALO + tm, :] = cur_ref[...]


def _shift_copies(dst, src, n):
    for s_ in range(1, 8):
        dst[s_ - 1, 0:n, :] = src[s_:s_ + n, :]


def _tap(src, sh, o, r0, c0):
    a, s_ = divmod(o, 8)
    ref = src if s_ == 0 else sh.at[s_ - 1]
    return ref[r0 + 8 * a:r0 + 8 * a + 32, c0:c0 + 512]


def _conv_mid_fwd(glu, x, dw, dwb, lng, lnb, w, b, carry=None):
    S = x.shape[0]
    tm = min(256, S)
    hb = tm // HALO

    def body(gl_ref, halo_ref, x_ref, dw_ref, dwb_ref, lng_ref, lnb_ref, w_ref, b_ref, c_ref, s_ref, xo_ref, ext, esh):
        i = pl.program_id(0)
        _fill_ext(ext, halo_ref, gl_ref, tm, i > 0)
        _shift_copies(esh, ext, tm + 24)
        for r0 in range(0, tm, 32):
            for c0 in range(0, D, 512):
                acc = jnp.zeros((32, 512), F32) + dwb_ref[:, c0:c0 + 512]
                for k in range(CONVW):
                    acc = acc + dw_ref[k:k + 1, c0:c0 + 512] * _tap(ext, esh, 2 + k, r0, c0)
                c_ref[r0:r0 + 32, c0:c0 + 512] = acc
        c = c_ref[...]
        mu = jnp.mean(c, axis=-1, keepdims=True)
        xc = c - mu
        y = xc * lax.rsqrt(jnp.mean(xc * xc, axis=-1, keepdims=True) + EPS) * lng_ref[...] + lnb_ref[...]
        sb = (y * jax.nn.sigmoid(y)).astype(BF16)
        s_ref[...] = sb
        xo_ref[...] = x_ref[...] + _dot(sb, w_ref[...]) + b_ref[...]

    return _carried_call(
        body, carry, name="conv_mid_fwd", grid=(S // tm,),
        in_specs=[_rows(tm, D), pl.BlockSpec((HALO, D), lambda i: (jnp.maximum(i * hb - 1, 0), 0)), _rows(tm, D),
                  _const((32, D)), _const((1, D)), _const((1, D)), _const((1, D)), _const((D, D)), _const((1, D))],
        out_specs=[_rows(tm, D), _rows(tm, D), _rows(tm, D)],
        out_shape=[_sds((S, D), F32), _sds((S, D), BF16), _sds((S, D), F32)],
        scratch_shapes=[pltpu.VMEM((tm + HALO, D), F32), pltpu.VMEM((7, tm + 24, D), F32)],
        compiler_params=_cp(48), args=[glu, glu, x, dw, dwb, lng, lnb, w, b])


def _mlp_fwd(x, g, wu, wd, target=None, carry=None):
    S = x.shape[0]
    tm = min(512, S)
    last = target is not None

    def body(*refs):
        if last:
            x_ref, g_ref, wu_ref, wd_ref, t_ref, h_ref, act_ref, dy_ref, dyb_ref, sq_ref, acc = refs
        else:
            x_ref, g_ref, wu_ref, wd_ref, h_ref, act_ref, xo_ref, acc = refs
        xv = x_ref[...]
        hb = (xv * _rms(xv) * g_ref[...]).astype(BF16)
        h_ref[...] = hb
        for j in range(NDEV):
            up = _dot(hb[:, :FFB], wu_ref[j, :, :FFB]) + _dot(hb[:, FFB:], wu_ref[j, :, FFB:])
            a = jnp.square(jnp.maximum(up, 0.0)).astype(BF16)
            act_ref[:, j * FFB:(j + 1) * FFB] = a
            if j == 0:
                acc[...] = _dot(a, wd_ref[j])
            else:
                acc[...] += _dot(a, wd_ref[j])
        y = xv + acc[...]
        if last:
            diff = y - t_ref[...]
            dy = diff * (1.0 / D)
            dy_ref[...] = dy
            dyb_ref[...] = dy.astype(BF16)
            _first(sq_ref, _colsum(diff * diff))
        else:
            xo_ref[...] = y

    in_specs = [_rows(tm, D), _const((1, D)), _const((NDEV, FFB, D)), _const((NDEV, FFB, D))]
    args = [x, g, wu, wd]
    out_specs = [_rows(tm, D), _rows(tm, DFF)]
    out_shape = [_sds((S, D), BF16), _sds((S, DFF), BF16)]
    if last:
        in_specs.append(_rows(tm, D))
        args.append(target)
        out_specs += [_rows(tm, D), _rows(tm, D), _acc((1, D))]
        out_shape += [_sds((S, D), F32), _sds((S, D), BF16), _sds((1, D), F32)]
    else:
        out_specs.append(_rows(tm, D))
        out_shape.append(_sds((S, D), F32))
    return _carried_call(
        body, carry, name="mlp_fwd_loss" if last else "mlp_fwd", grid=(S // tm,),
        in_specs=in_specs, out_specs=out_specs, out_shape=out_shape,
        scratch_shapes=[pltpu.VMEM((tm, D), F32)],
        compiler_params=_cp(52), args=args)


def _mlp_bwd(dy, x, g, act, wu, wd, carry=None):
    S = x.shape[0]
    tm = min(512, S)

    def body(dy_ref, x_ref, g_ref, act_ref, wu_ref, wd_ref, dup_ref, dx_ref, dxb_ref, dg_ref, acc):
        db = dy_ref[...].astype(BF16)
        for j in range(NDEV):
            dact = _dot_nt(db, wd_ref[j])
            a = act_ref[:, j * FFB:(j + 1) * FFB].astype(F32)
            dup = (dact * (2.0 * jnp.sqrt(a))).astype(BF16)
            dup_ref[:, j * FFB:(j + 1) * FFB] = dup
            for h_ in range(2):
                part = _dot_nt(dup, wu_ref[j, :, h_ * FFB:(h_ + 1) * FFB])
                if j == 0:
                    acc[:, h_ * FFB:(h_ + 1) * FFB] = part
                else:
                    acc[:, h_ * FFB:(h_ + 1) * FFB] += part
        xv = x_ref[...]
        dxn, dg = _rms_bwd(acc[...], xv, _rms(xv), g_ref[...])
        dx = dy_ref[...] + dxn
        dx_ref[...] = dx
        dxb_ref[...] = dx.astype(BF16)
        _first(dg_ref, dg)

    return _carried_call(
        body, carry, name="mlp_bwd", grid=(S // tm,),
        in_specs=[_rows(tm, D), _rows(tm, D), _const((1, D)), _rows(tm, DFF),
                  _const((NDEV, FFB, D)), _const((NDEV, FFB, D))],
        out_specs=[_rows(tm, DFF), _rows(tm, D), _rows(tm, D), _acc((1, D))],
        out_shape=[_sds((S, DFF), BF16), _sds((S, D), F32), _sds((S, D), BF16), _sds((1, D), F32)],
        scratch_shapes=[pltpu.VMEM((tm, D), F32)],
        compiler_params=_cp(58), args=[dy, x, g, act, wu, wd])


def _wgrad(a, b, tm, tn, name):
    S, M = a.shape
    N = b.shape[1]
    tk = min(4096, S)
    nk = S // tk

    def body(a_ref, b_ref, o_ref, acc):
        k = pl.program_id(2)

        @pl.when(k == 0)
        def _():
            acc[...] = jnp.zeros_like(acc)

        acc[...] += _dot_tn(a_ref[...], b_ref[...])

        @pl.when(k == nk - 1)
        def _():
            o_ref[...] = acc[...].astype(BF16)

    return pl.pallas_call(
        body, name=name, grid=(M // tm, N // tn, nk),
        in_specs=[pl.BlockSpec((tk, tm), lambda i, j, k: (k, i)), pl.BlockSpec((tk, tn), lambda i, j, k: (k, j))],
        out_specs=pl.BlockSpec((tm, tn), lambda i, j, k: (i, j)),
        out_shape=_sds((M, N), BF16),
        scratch_shapes=[pltpu.VMEM((tm, tn), F32)],
        compiler_params=_cp(48, 3))(a, b)


def _wgrad_split(a, b, nsplit, name):
    S, M = a.shape
    N = b.shape[1]
    wb = N // NDEV
    tm, tn = M // nsplit, D
    nb = tn // wb
    tk = min(4096, S)
    nk = S // tk

    def body(a_ref, b_ref, o_ref, acc):
        k = pl.program_id(2)

        @pl.when(k == 0)
        def _():
            acc[...] = jnp.zeros_like(acc)

        acc[...] += _dot_tn(a_ref[...], b_ref[...])

        @pl.when(k == nk - 1)
        def _():
            for jj in range(nb):
                o_ref[jj] = acc[:, jj * wb:(jj + 1) * wb].astype(BF16)

    return pl.pallas_call(
        body, name=name, grid=(nsplit, N // tn, nk),
        in_specs=[pl.BlockSpec((tk, tm), lambda i, j, k: (k, i)), pl.BlockSpec((tk, tn), lambda i, j, k: (k, j))],
        out_specs=pl.BlockSpec((nb, tm, wb), lambda i, j, k: (j, 0, i)),
        out_shape=_sds((NDEV, tm, nsplit * wb), BF16),
        scratch_shapes=[pltpu.VMEM((tm, tn), F32)],
        compiler_params=_cp(48, 3))(a, b)


def _bucket_table():
    q = np.arange(WIN)[:, None]
    k = np.arange(2 * WIN)[None, :]
    dist = q + WIN - k
    n = np.maximum(dist, 0)
    max_exact = NBKT // 2
    large = max_exact + (np.log(np.maximum(n, 1).astype(np.float32) / max_exact)
                         / math.log(WIN / max_exact) * (NBKT - max_exact)).astype(np.int32)
    large = np.minimum(large, NBKT - 1)
    bkt = np.where(n < max_exact, n, large).astype(np.int32)
    return np.where((dist >= 0) & (dist < WIN), bkt, -1).astype(np.int32)


def _seg_mats():
    e16 = np.zeros((D, 128), np.float32)
    e16[np.arange(D), np.arange(D) // HD] = 1.0
    e2 = np.zeros((KV, 128), np.float32)
    e2[np.arange(KV), np.arange(KV) // HD] = 1.0
    fold = np.zeros((D, 128), np.float32)
    fold[np.arange(D), np.arange(D) % HD] = 1.0
    fold2 = np.zeros((KV, 128), np.float32)
    fold2[np.arange(KV), np.arange(KV) % HD] = 1.0
    return [jnp.asarray(m, BF16) for m in (e16, e16.T, e2, e2.T, fold, fold2)]


def _head_rms(t, e, et):
    r = lax.rsqrt(_dot_hi(t * t, e) * (1.0 / HD) + EPS)
    return _dot_hi(r, et)


def _attn_qkv_fwd(x, g, w, b, qg, kg, e16, e16t, e2, e2t):
    S = x.shape[0]
    tm = min(512, S)

    def body(x_ref, g_ref, w_ref, b_ref, qg_ref, kg_ref, e16_ref, e16t_ref, e2_ref, e2t_ref,
             h_ref, raw_ref, qn_ref, kn_ref, v_ref):
        xv = x_ref[...]
        hb = (xv * _rms(xv) * g_ref[...]).astype(BF16)
        h_ref[...] = hb
        raw = _dot(hb, w_ref[...]) + b_ref[...]
        raw_ref[...] = raw
        q = raw[:, :D]
        k = raw[:, D:D + KV]
        qn = (q * _head_rms(q, e16_ref[...], e16t_ref[...]) * qg_ref[...] * 0.125).astype(BF16)
        for bb in range(tm // WIN):
            for p in range(NPAIR):
                r = (bb * NPAIR + p) * WIN
                qn_ref[r:r + WIN, :] = qn[bb * WIN:(bb + 1) * WIN, p * 128:(p + 1) * 128]
        kn_ref[...] = (k * _head_rms(k, e2_ref[...], e2t_ref[...]) * kg_ref[...]).astype(BF16)
        v_ref[...] = raw[:, D + KV:].astype(BF16)

    return pl.pallas_call(
        body, name="attn_qkv_fwd", grid=(S // tm,),
        in_specs=[_rows(tm, D), _const((1, D)), _const((D, QKV)), _const((1, QKV)), _const((1, D)), _const((1, KV)),
                  _const((D, 128)), _const((128, D)), _const((KV, 128)), _const((128, KV))],
        out_specs=[_rows(tm, D), _rows(tm, QKV), _rows(tm * NPAIR, 128), _rows(tm, KV), _rows(tm, KV)],
        out_shape=[_sds((S, D), BF16), _sds((S, QKV), F32), _sds((S * NPAIR, 128), BF16), _sds((S, KV), BF16),
                   _sds((S, KV), BF16)],
        compiler_params=_cp(48))(x, g, w, b, qg, kg, e16, e16t, e2, e2t)


def _build_bias(bkt_ref, rb_ref, sk_ref, bias_sc, sk_sc):
    bkt = bkt_ref[...]
    row = lax.broadcasted_iota(jnp.int32, (2 * WIN, WIN), 0)
    for h in range(NH):
        p, e = h // 2, h % 2
        g, pp = p // 4, p % 4

        def add(bk, acc, h=h):
            return acc + jnp.where(bkt == bk, rb_ref[bk, h], 0.0)

        bias = lax.fori_loop(0, NBKT, add, jnp.where(bkt < 0, NEG, 0.0).astype(F32))
        bias_sc[0, g, e, :, pp * WIN:(pp + 1) * WIN] = bias
        bias_sc[1, g, e, :, pp * WIN:(pp + 1) * WIN] = jnp.where(row < WIN, NEG, bias)
        sk_sc[g, e, :, pp * WIN:(pp + 1) * WIN] = jnp.zeros((1, WIN), F32) + sk_ref[0, h]


def _band(ref, p0, g0):
    return jnp.concatenate([ref[pl.ds(p0, WIN), :].astype(F32), ref[pl.ds(g0, WIN), :].astype(F32)], axis=0)


def _stacks(f, mlo):
    r = pltpu.roll(f, HD, 1)
    z = jnp.zeros_like(f)
    return ((jnp.where(mlo, f, z).astype(BF16), jnp.where(mlo, z, r).astype(BF16)),
            (jnp.where(mlo, r, z).astype(BF16), jnp.where(mlo, z, f).astype(BF16)))


def _unstack(d, mlo):
    z = jnp.zeros_like(d[0][0])
    return (jnp.where(mlo, d[0][0], z) + pltpu.roll(jnp.where(mlo, z, d[0][1]), HD, 1)
            + pltpu.roll(jnp.where(mlo, d[1][0], z), HD, 1) + jnp.where(mlo, z, d[1][1]))


def _softmax_sink(l, sk):
    m = jnp.maximum(jnp.max(l, axis=0, keepdims=True), sk)
    ex = jnp.exp(l - m)
    es = jnp.exp(sk - m)
    inv = 1.0 / (jnp.sum(ex, axis=0, keepdims=True) + es)
    return ex * inv, es * inv


GROWS = 4 * WIN


def _attn_fwd(qn, kn, vn, bktt, rel_bias, sinks, wo, bo, x, carry=None):
    S = x.shape[0]
    tq = min(512, S)
    nblk = tq // WIN

    def body(q_ref, k_ref, v_ref, bkt_ref, rb_ref, sk_ref, wo_ref, bo_ref, x_ref, o_ref, xo_ref, bias_sc, sk_sc):
        i = pl.program_id(0)

        @pl.when(i == 0)
        def _():
            _build_bias(bkt_ref, rb_ref, sk_ref, bias_sc, sk_sc)

        mlo = lax.broadcasted_iota(jnp.int32, (2 * WIN, KV), 1) < HD

        def blk(bb, carry):
            r0 = pl.multiple_of(bb * WIN, WIN)
            g0 = pl.multiple_of(i * tq + bb * WIN, WIN)
            first = (g0 == 0).astype(jnp.int32)
            p0 = pl.multiple_of(jnp.maximum(g0 - WIN, 0), WIN)
            ks = _stacks(_band(k_ref, p0, g0), mlo)
            vs = _stacks(_band(v_ref, p0, g0), mlo)
            for g in range(NKV):
                qs = q_ref[pl.ds(pl.multiple_of(bb * (NPAIR * WIN) + g * GROWS, GROWS), GROWS), :]
                og = jnp.zeros((GROWS, 128), F32)
                for e in range(2):
                    l = _dot_nt(ks[g][e], qs) + bias_sc[first, g, e]
                    pr, _ = _softmax_sink(l, sk_sc[g, e])
                    og = og + _dot_tn(pr.astype(BF16), vs[g][e])
                for pp in range(4):
                    p = 4 * g + pp
                    o_ref[pl.ds(r0, WIN), p * 128:(p + 1) * 128] = og[pp * WIN:(pp + 1) * WIN].astype(BF16)
            return carry

        lax.fori_loop(0, nblk, blk, 0)
        xo_ref[...] = x_ref[...] + _dot(o_ref[...], wo_ref[...]) + bo_ref[...]

    smem = pl.BlockSpec(memory_space=pltpu.SMEM)
    return _carried_call(
        body, carry, name="attn_fwd", grid=(S // tq,),
        in_specs=[_rows(tq * NPAIR, 128), _const((S, KV)), _const((S, KV)), _const((2 * WIN, WIN)), smem, smem,
                  _const((D, D)), _const((1, D)), _rows(tq, D)],
        out_specs=[_rows(tq, D), _rows(tq, D)],
        out_shape=[_sds((S, D), BF16), _sds((S, D), F32)],
        scratch_shapes=[pltpu.VMEM((2, NKV, 2, 2 * WIN, GROWS), F32), pltpu.VMEM((NKV, 2, 1, GROWS), F32)],
        compiler_params=_cp(48), args=[qn, kn, vn, bktt, rel_bias, sinks, wo, bo, x])


def _attn_bwd(dx, dxb, qn, kn, vn, bktt, bktt4, rel_bias, sinks, wo, carry=None):
    S = dx.shape[0]
    tq = min(512, S)
    nblk = tq // WIN
    nsteps = S // tq

    def body(dx_ref, dxb_ref, q_ref, k_ref, v_ref, bkt_ref, bkt4_ref, rb_ref, sk_ref, wo_ref,
             dq_ref, dk_ref, dv_ref, dsk_ref, drb_ref, dbo_ref, bias_sc, sk_sc, dbias_sc, dsk_sc, do_sc):
        i = pl.program_id(0)

        @pl.when(i == 0)
        def _():
            _build_bias(bkt_ref, rb_ref, sk_ref, bias_sc, sk_sc)
            dbias_sc[...] = jnp.zeros_like(dbias_sc)
            dsk_sc[...] = jnp.zeros_like(dsk_sc)
            dk_ref[...] = jnp.zeros_like(dk_ref)
            dv_ref[...] = jnp.zeros_like(dv_ref)

        _first(dbo_ref, _colsum(dx_ref[...]))
        do = _dot_nt(dxb_ref[...], wo_ref[...]).astype(BF16)
        for bb in range(nblk):
            for p in range(NPAIR):
                r = (bb * NPAIR + p) * WIN
                do_sc[r:r + WIN, :] = do[bb * WIN:(bb + 1) * WIN, p * 128:(p + 1) * 128]

        mlo = lax.broadcasted_iota(jnp.int32, (2 * WIN, KV), 1) < HD

        def blk(bb, carry):
            g0 = pl.multiple_of(i * tq + bb * WIN, WIN)
            first = (g0 == 0).astype(jnp.int32)
            p0 = pl.multiple_of(jnp.maximum(g0 - WIN, 0), WIN)
            ks = _stacks(_band(k_ref, p0, g0), mlo)
            vs = _stacks(_band(v_ref, p0, g0), mlo)
            dks = [[None, None], [None, None]]
            dvs = [[None, None], [None, None]]
            for g in range(NKV):
                rows = pl.ds(pl.multiple_of(bb * (NPAIR * WIN) + g * GROWS, GROWS), GROWS)
                qs = q_ref[rows, :]
                dos = do_sc[rows, :]
                dqs = jnp.zeros((GROWS, 128), F32)
                for e in range(2):
                    l = _dot_nt(ks[g][e], qs) + bias_sc[first, g, e]
                    pr, ps = _softmax_sink(l, sk_sc[g, e])
                    dp = _dot_nt(vs[g][e], dos)
                    dr = jnp.sum(pr * dp, axis=0, keepdims=True)
                    dl = pr * (dp - dr)
                    dsk_sc[g, e] -= ps * dr
                    dbias_sc[g, e] += dl
                    dlb = dl.astype(BF16)
                    dqs = dqs + _dot_tn(dlb, ks[g][e])
                    dks[g][e] = _dot(dlb, qs)
                    dvs[g][e] = _dot(pr.astype(BF16), dos)
                dq_ref[rows, :] = dqs
            for ref, d in ((dk_ref, _unstack(dks, mlo)), (dv_ref, _unstack(dvs, mlo))):
                ref[pl.ds(p0, WIN), :] += d[:WIN]
                ref[pl.ds(g0, WIN), :] += d[WIN:]
            return carry

        lax.fori_loop(0, nblk, blk, 0)

        @pl.when(i == nsteps - 1)
        def _():
            bkt4v = bkt4_ref[...]
            rid = lax.broadcasted_iota(jnp.int32, (NBKT, 128), 0)
            lid = lax.broadcasted_iota(jnp.int32, (NBKT, 128), 1)
            lid1 = lax.broadcasted_iota(jnp.int32, (1, 128), 1)
            dsk = jnp.zeros((1, 128), F32)
            for g in range(NKV):
                for e in range(2):
                    for pp in range(4):
                        h = 2 * (4 * g + pp) + e
                        t = jnp.sum(dsk_sc[g, e, :, pp * WIN:(pp + 1) * WIN], axis=1, keepdims=True)
                        dsk = dsk + jnp.where(lid1 == h, t, 0.0)
            dsk_ref[...] = dsk

            def per_bucket(bk, acc):
                mb = bkt4v == bk
                for g in range(NKV):
                    for e in range(2):
                        t = jnp.sum(jnp.where(mb, dbias_sc[g, e], 0.0), axis=0, keepdims=True)
                        for pp in range(4):
                            h = 2 * (4 * g + pp) + e
                            tt = jnp.sum(t[:, pp * WIN:(pp + 1) * WIN], axis=1, keepdims=True)
                            acc = acc + jnp.where((rid == bk) & (lid == h), tt, 0.0)
                return acc

            drb_ref[...] = lax.fori_loop(0, NBKT, per_bucket, jnp.zeros((NBKT, 128), F32))

    smem = pl.BlockSpec(memory_space=pltpu.SMEM)
    return _carried_call(
        body, carry, name="attn_bwd", grid=(nsteps,),
        in_specs=[_rows(tq, D), _rows(tq, D), _rows(tq * NPAIR, 128), _const((S, KV)), _const((S, KV)),
                  _const((2 * WIN, WIN)), _const((2 * WIN, GROWS)), smem, smem, _const((D, D))],
        out_specs=[_rows(tq * NPAIR, 128), _acc((S, KV)), _acc((S, KV)), _acc((1, 128)), _acc((NBKT, 128)),
                   _acc((1, D))],
        out_shape=[_sds((S * NPAIR, 128), F32), _sds((S, KV), F32), _sds((S, KV), F32), _sds((1, 128), F32),
                   _sds((NBKT, 128), F32), _sds((1, D), F32)],
        scratch_shapes=[pltpu.VMEM((2, NKV, 2, 2 * WIN, GROWS), F32), pltpu.VMEM((NKV, 2, 1, GROWS), F32),
                        pltpu.VMEM((NKV, 2, 2 * WIN, GROWS), F32), pltpu.VMEM((NKV, 2, 1, GROWS), F32),
                        pltpu.VMEM((tq * NPAIR, 128), BF16)],
        compiler_params=_cp(52), args=[dx, dxb, qn, kn, vn, bktt, bktt4, rel_bias, sinks, wo])


def _head_norm_bwd(dn, t, e, et, gt):
    r = _head_rms(t, e, et)
    th = t * r
    dth = dn * gt
    dt = r * (dth - th * _dot_hi(_dot_hi(dth * th, e) * (1.0 / HD), et))
    return dt, _colsum(dn * th)


def _attn_qkv_bwd(dqs, dk, dv, raw, x, dxo, g, w, qg, kg, e16, e16t, e2, e2t, fold, fold2):
    S = x.shape[0]
    tm = min(512, S)

    def body(dq_ref, dk_ref, dv_ref, raw_ref, x_ref, dxo_ref, g_ref, w_ref, qg_ref, kg_ref,
             e16_ref, e16t_ref, e2_ref, e2t_ref, fold_ref, fold2_ref,
             dqkv_ref, dx_ref, dxb_ref, db_ref, dg_ref, dqg_ref, dkg_ref, dq_sc):
        for bb in range(tm // WIN):
            for p in range(NPAIR):
                r = (bb * NPAIR + p) * WIN
                dq_sc[bb * WIN:(bb + 1) * WIN, p * 128:(p + 1) * 128] = dq_ref[r:r + WIN, :]
        dq, cq = _head_norm_bwd(dq_sc[...] * 0.125, raw_ref[:, :D], e16_ref[...], e16t_ref[...], qg_ref[...])
        dk_, ck = _head_norm_bwd(dk_ref[...], raw_ref[:, D:D + KV], e2_ref[...], e2t_ref[...], kg_ref[...])
        dv_ = dv_ref[...]
        _first(dqg_ref, _dot_hi(cq, fold_ref[...]))
        _first(dkg_ref, _dot_hi(ck, fold2_ref[...]))
        _first_cols(db_ref, [(0, _colsum(dq)), (D, _colsum(dk_)), (D + KV, _colsum(dv_))])
        dqb, dkb, dvb = dq.astype(BF16), dk_.astype(BF16), dv_.astype(BF16)
        dqkv_ref[:, :D] = dqb
        dqkv_ref[:, D:D + KV] = dkb
        dqkv_ref[:, D + KV:] = dvb
        dh = (_dot_nt(dqb, w_ref[:, :D]) + _dot_nt(dkb, w_ref[:, D:D + KV]) + _dot_nt(dvb, w_ref[:, D + KV:]))
        xv = x_ref[...]
        dxn, dg = _rms_bwd(dh, xv, _rms(xv), g_ref[...])
        dx = dxo_ref[...] + dxn
        dx_ref[...] = dx
        dxb_ref[...] = dx.astype(BF16)
        _first(dg_ref, dg)

    return pl.pallas_call(
        body, name="attn_qkv_bwd", grid=(S // tm,),
        in_specs=[_rows(tm * NPAIR, 128), _rows(tm, KV), _rows(tm, KV), _rows(tm, QKV), _rows(tm, D), _rows(tm, D),
                  _const((1, D)), _const((D, QKV)), _const((1, D)), _const((1, KV)),
                  _const((D, 128)), _const((128, D)), _const((KV, 128)), _const((128, KV)),
                  _const((D, 128)), _const((KV, 128))],
        out_specs=[_rows(tm, QKV), _rows(tm, D), _rows(tm, D), _acc((1, QKV)), _acc((1, D)), _acc((1, 128)), _acc((1, 128))],
        out_shape=[_sds((S, QKV), BF16), _sds((S, D), F32), _sds((S, D), BF16), _sds((1, QKV), F32), _sds((1, D), F32),
                   _sds((1, 128), F32), _sds((1, 128), F32)],
        scratch_shapes=[pltpu.VMEM((tm, D), F32)],
        compiler_params=_cp(48))(dqs, dk, dv, raw, x, dxo, g, w, qg, kg, e16, e16t, e2, e2t, fold, fold2)


def _conv_mid_bwd(dx, dxb, c, lng, lnb, w):
    S = dx.shape[0]
    tm = min(512, S)

    def body(dx_ref, dxb_ref, c_ref, lng_ref, lnb_ref, w_ref, dc_ref, dbo_ref, dlg_ref, dlb_ref, ddwb_ref):
        _first(dbo_ref, _colsum(dx_ref[...]))
        ds = _dot_nt(dxb_ref[...], w_ref[...])
        c = c_ref[...]
        xc = c - jnp.mean(c, axis=-1, keepdims=True)
        rstd = lax.rsqrt(jnp.mean(xc * xc, axis=-1, keepdims=True) + EPS)
        ch = xc * rstd
        y = ch * lng_ref[...] + lnb_ref[...]
        sg = jax.nn.sigmoid(y)
        dy = ds * (sg * (1.0 + y * (1.0 - sg)))
        _first(dlg_ref, _colsum(dy * ch))
        _first(dlb_ref, _colsum(dy))
        dch = dy * lng_ref[...]
        dc = rstd * (dch - jnp.mean(dch, axis=-1, keepdims=True) - ch * jnp.mean(dch * ch, axis=-1, keepdims=True))
        dc_ref[...] = dc
        _first(ddwb_ref, _colsum(dc))

    return pl.pallas_call(
        body, name="conv_mid_bwd", grid=(S // tm,),
        in_specs=[_rows(tm, D), _rows(tm, D), _rows(tm, D), _const((1, D)), _const((1, D)), _const((D, D))],
        out_specs=[_rows(tm, D), _acc((1, D)), _acc((1, D)), _acc((1, D)), _acc((1, D))],
        out_shape=[_sds((S, D), F32)] + [_sds((1, D), F32)] * 4,
        compiler_params=_cp(40))(dx, dxb, c, lng, lnb, w)


def _dwconv_bwd(dc, glu, dw, carry=None):
    S = dc.shape[0]
    tm = min(256, S)
    hb = tm // HALO
    nsteps = S // tm

    def body(dc_ref, nxt_ref, gl_ref, halo_ref, dw_ref, dgl_ref, ddw_ref, ext, dext, esh, dsh):
        i = pl.program_id(0)
        _fill_ext(ext, halo_ref, gl_ref, tm, i > 0)
        dext[0:tm, :] = dc_ref[...]
        dext[tm:tm + HALO, :] = jnp.where(i < nsteps - 1, nxt_ref[...], 0.0)
        _shift_copies(esh, ext, tm + 24)
        _shift_copies(dsh, dext, tm + 24)

        @pl.when(i == 0)
        def _():
            ddw_ref[...] = jnp.zeros_like(ddw_ref)

        for c0 in range(0, D, 512):
            for r0 in range(0, tm, 32):
                acc = jnp.zeros((32, 512), F32)
                for j in range(CONVW):
                    acc = acc + dw_ref[CONVW - 1 - j:CONVW - j, c0:c0 + 512] * _tap(dext, dsh, j, r0, c0)
                dgl_ref[r0:r0 + 32, c0:c0 + 512] = acc
            for k in range(CONVW):
                acc = jnp.zeros((32, 512), F32)
                for r0 in range(0, tm, 32):
                    acc = acc + dext[r0:r0 + 32, c0:c0 + 512] * _tap(ext, esh, 2 + k, r0, c0)
                ddw_ref[k:k + 1, c0:c0 + 512] += _colsum(acc)

    return _carried_call(
        body, carry, name="dwconv_bwd", grid=(nsteps,),
        in_specs=[_rows(tm, D), pl.BlockSpec((HALO, D), lambda i: (jnp.minimum((i + 1) * hb, S // HALO - 1), 0)),
                  _rows(tm, D), pl.BlockSpec((HALO, D), lambda i: (jnp.maximum(i * hb - 1, 0), 0)), _const((32, D))],
        out_specs=[_rows(tm, D), _acc((32, D))],
        out_shape=[_sds((S, D), F32), _sds((32, D), F32)],
        scratch_shapes=[pltpu.VMEM((tm + HALO, D), F32), pltpu.VMEM((tm + HALO, D), F32),
                        pltpu.VMEM((7, tm + 24, D), F32), pltpu.VMEM((7, tm + 24, D), F32)],
        compiler_params=_cp(48), args=[dc, dc, glu, glu, dw])


def _conv_in_bwd(dglu, u, x, dxo, g, gw, carry=None):
    S = x.shape[0]
    tm = min(512, S)

    def body(dgl_ref, u_ref, x_ref, dxo_ref, g_ref, w_ref, du_ref, dx_ref, db_ref, dg_ref, dh):
        dgl = dgl_ref[...]
        a = u_ref[:, :D].astype(F32)
        sg = jax.nn.sigmoid(u_ref[:, D:].astype(F32))
        da = dgl * sg
        dgt = dgl * a * sg * (1.0 - sg)
        _first_cols(db_ref, [(0, _colsum(da)), (D, _colsum(dgt))])
        dab, dgb = da.astype(BF16), dgt.astype(BF16)
        du_ref[:, :D] = dab
        du_ref[:, D:] = dgb
        for q in range(4):
            part = None
            for j in range(NDEV):
                src = dab if j < NDEV // 2 else dgb
                c0 = (j % (NDEV // 2)) * QW
                t = _dot_nt(src[:, c0:c0 + QW], w_ref[j, :, q * QW:(q + 1) * QW])
                part = t if part is None else part + t
            dh[:, q * QW:(q + 1) * QW] = part
        xv = x_ref[...]
        dxn, dg = _rms_bwd(dh[...], xv, _rms(xv), g_ref[...])
        dx_ref[...] = dxo_ref[...] + dxn
        _first(dg_ref, dg)

    return _carried_call(
        body, carry, name="conv_in_bwd", grid=(S // tm,),
        in_specs=[_rows(tm, D), _rows(tm, 2 * D), _rows(tm, D), _rows(tm, D), _const((1, D)), _const((NDEV, QW, D))],
        out_specs=[_rows(tm, 2 * D), _rows(tm, D), _acc((1, 2 * D)), _acc((1, D))],
        out_shape=[_sds((S, 2 * D), BF16), _sds((S, D), F32), _sds((1, 2 * D), F32), _sds((1, D), F32)],
        scratch_shapes=[pltpu.VMEM((tm, D), F32)], compiler_params=_cp(48), args=[dglu, u, x, dxo, g, gw])


def _coords():
    return lax.axis_index("x"), lax.axis_index("y"), lax.axis_index("c")


def _split(refs, *counts):
    out, k = [], 0
    for n in counts:
        out.append(refs[k:k + n])
        k += n
    return out


def _carried_call(body, carry, *, name, grid, in_specs, out_specs, out_shape, scratch_shapes, compiler_params, args):
    if carry is None:
        return pl.pallas_call(body, name=name, grid=grid, in_specs=in_specs, out_specs=out_specs, out_shape=out_shape,
                              scratch_shapes=scratch_shapes, compiler_params=compiler_params)(*args)
    counts = (len(in_specs), len(carry.args), len(out_specs), len(carry.out_shape), len(scratch_shapes), len(carry.scratch))
    nsteps = grid[0]

    def wrapped(*refs):
        ins, cin, outs, cout, scr, cscr = _split(refs, *counts)
        i = pl.program_id(0)
        carry.top(i, nsteps, cin, cout, cscr)
        body(*ins, *outs, *scr)
        carry.bottom(i, nsteps, cin, cout, cscr)

    return pl.pallas_call(
        wrapped, name=name, grid=grid, in_specs=list(in_specs) + carry.in_specs,
        out_specs=list(out_specs) + carry.out_specs, out_shape=list(out_shape) + carry.out_shape,
        scratch_shapes=list(scratch_shapes) + carry.scratch, compiler_params=compiler_params)(*args, *carry.args)


def _routes():
    x, y, c = _coords()
    return (x, y, c), (x, y, 1 - c), [(1 - x, y), (x, 1 - y), (1 - x, 1 - y)]


def _gather_copies(srcs, dsts, send, recv):
    me, sib, chips = _routes()
    c = me[2]

    def cp(a, k, block, to, own=False):
        idx = 4 * block[0] + 2 * block[1] + block[2]
        return pltpu.make_async_remote_copy(
            src_ref=srcs[a] if own else dsts[a].at[idx], dst_ref=dsts[a].at[idx], send_sem=send.at[a, k],
            recv_sem=recv.at[a, k], device_id=to, device_id_type=MESH)

    own, ici_in, fwd, sib_in = [], [], [], []
    for a in range(len(srcs)):
        own += [cp(a, 0, me, sib, True)] + [cp(a, 1 + j, me, (*ch, c), True) for j, ch in enumerate(chips)]
        ici_in += [cp(a, 1 + j, (*ch, c), me) for j, ch in enumerate(chips)]
        fwd += [cp(a, 4 + j, (*ch, c), sib) for j, ch in enumerate(chips)]
        sib_in += [cp(a, 0, sib, me)] + [cp(a, 4 + j, (*ch, 1 - c), me) for j, ch in enumerate(chips)]
    return own, ici_in, fwd, sib_in


class _Gather:
    def __init__(self, srcs):
        self.idx = [i for _, i in srcs]
        self.args = [a for a, _ in srcs]
        n = len(srcs)
        self.in_specs = [pl.BlockSpec(memory_space=pl.ANY)] * n
        self.out_specs = [pl.BlockSpec(memory_space=pl.ANY)] * n
        self.out_shape = [_sds((NDEV,) + (a.shape if i is None else a.shape[1:]), BF16) for a, i in srcs]
        self.scratch = [pltpu.SemaphoreType.DMA((n, 7)), pltpu.SemaphoreType.DMA((n, 7)), pltpu.SemaphoreType.DMA((n,))]

    def _copies(self, ins, outs, scr):
        send, recv, lsem = scr
        srcs = [r if i is None else r.at[i] for r, i in zip(ins, self.idx)]
        me = _routes()[0]
        slot = 4 * me[0] + 2 * me[1] + me[2]
        local = [pltpu.make_async_copy(s_, d_.at[slot], lsem.at[a]) for a, (s_, d_) in enumerate(zip(srcs, outs))]
        return _gather_copies(srcs, outs, send, recv) + (local,)

    def top(self, i, nsteps, ins, outs, scr):
        @pl.when(i == 0)
        def _():
            own, _, _, _, local = self._copies(ins, outs, scr)
            for cp in local + own:
                cp.start()

        @pl.when(i == (3 * nsteps) // 4)
        def _():
            _, ici_in, fwd, _, _ = self._copies(ins, outs, scr)
            for a_, f_ in zip(ici_in, fwd):
                a_.wait_recv()
                f_.start()

    def bottom(self, i, nsteps, ins, outs, scr):
        @pl.when(i == nsteps - 1)
        def _():
            own, _, fwd, sib_in, local = self._copies(ins, outs, scr)
            for cp in sib_in:
                cp.wait_recv()
            for cp in own + fwd:
                cp.wait_send()
            for cp in local:
                cp.wait()


def _scatter_copies(srcs, lands, send, recv):
    x, y, c = _coords()
    me = 4 * x + 2 * y + c
    sends, recvs = [], []
    for a in range(len(srcs)):
        for k in range(1, NDEV):
            peer = (x ^ ((k >> 2) & 1), y ^ ((k >> 1) & 1), c ^ (k & 1))
            pidx = me ^ k
            sends.append(pltpu.make_async_remote_copy(
                src_ref=srcs[a].at[pidx], dst_ref=lands[a].at[me], send_sem=send.at[a, k - 1],
                recv_sem=recv.at[a, k - 1], device_id=peer, device_id_type=MESH))
            recvs.append(pltpu.make_async_remote_copy(
                src_ref=srcs[a].at[pidx], dst_ref=lands[a].at[pidx], send_sem=send.at[a, k - 1],
                recv_sem=recv.at[a, k - 1], device_id=peer, device_id_type=MESH))
    return sends, recvs


class _Scatter:
    def __init__(self, srcs):
        n = len(srcs)
        self.args = list(srcs)
        self.in_specs = [pl.BlockSpec(memory_space=pl.ANY)] * n
        self.out_specs = [pl.BlockSpec(memory_space=pl.ANY)] * n
        self.out_shape = [_sds(a.shape, BF16) for a in srcs]
        self.scratch = [pltpu.SemaphoreType.DMA((n, 7)), pltpu.SemaphoreType.DMA((n, 7)), pltpu.SemaphoreType.DMA((n,))]

    def _copies(self, ins, outs, scr):
        send, recv, lsem = scr
        x, y, c = _coords()
        me = 4 * x + 2 * y + c
        local = [pltpu.make_async_copy(s_.at[me], d_.at[me], lsem.at[a]) for a, (s_, d_) in enumerate(zip(ins, outs))]
        return _scatter_copies(ins, outs, send, recv) + (local,)

    def top(self, i, nsteps, ins, outs, scr):
        @pl.when(i == 0)
        def _():
            sends, _, local = self._copies(ins, outs, scr)
            for cp in local + sends:
                cp.start()

    def bottom(self, i, nsteps, ins, outs, scr):
        @pl.when(i == nsteps - 1)
        def _():
            sends, recvs, local = self._copies(ins, outs, scr)
            for cp in recvs:
                cp.wait_recv()
            for cp in sends:
                cp.wait_send()
            for cp in local:
                cp.wait()


def _gather_first(w_in, w_out, w_qkv, w_o, w_up, w_down, smalls):
    ns = len(smalls)

    def body(*refs):
        (win_ref, wout_ref, wqkv_ref, wo_ref, wup_ref, wdn_ref), s_refs, (out_b, qkv_b, wo_b, up_b, dn_b, gin_ref), \
            gs_refs, (st_in, st_out, st_qkv, st_wo, st_up, st_dn, send, recv, lsem) = _split(refs, 6, ns, 6, ns, 9)
        for q in range(4):
            st_in[:, q * 256:(q + 1) * 256] = win_ref[q * 256:(q + 1) * 256, :].astype(BF16)
        st_out[...] = wout_ref[...].astype(BF16)
        st_qkv[...] = wqkv_ref[...].astype(BF16)
        st_wo[...] = wo_ref[...].astype(BF16)
        for l in range(2):
            for h_ in range(2):
                st_up[l, :, h_ * FFB:(h_ + 1) * FFB] = wup_ref[l, h_ * FFB:(h_ + 1) * FFB, :].astype(BF16)
            st_dn[l] = wdn_ref[l].astype(BF16)
        me = _routes()[0]
        slot = 4 * me[0] + 2 * me[1] + me[2]
        srcs, dsts = [st_in, *s_refs], [gin_ref, *gs_refs]
        local = [pltpu.make_async_copy(s_, d_, lsem.at[a]) for a, (s_, d_) in enumerate(
            [(st_out, out_b), (st_qkv, qkv_b), (st_wo, wo_b), (st_up, up_b), (st_dn, dn_b), (st_in, gin_ref.at[slot])])]
        for cp in local:
            cp.start()
        for s_ref, gs_ref in zip(s_refs, gs_refs):
            gs_ref[slot] = s_ref[...]
        own, ici_in, fwd, sib_in = _gather_copies(srcs, dsts, send, recv)
        for cp in own:
            cp.start()
        for a_, f_ in zip(ici_in, fwd):
            a_.wait_recv()
            f_.start()
        for cp in sib_in:
            cp.wait_recv()
        for cp in own + fwd:
            cp.wait_send()
        for cp in local:
            cp.wait()

    vmem = pl.BlockSpec(memory_space=pltpu.VMEM)
    hbm = pl.BlockSpec(memory_space=pl.ANY)
    return pl.pallas_call(
        body, name="gather_first",
        in_specs=[vmem] * (6 + ns),
        out_specs=[hbm] * 6 + [vmem] * ns,
        out_shape=[_sds((128, D), BF16), _sds((160, D), BF16), _sds((128, D), BF16), _sds((2, FFB, D), BF16),
                   _sds((2, FFB, D), BF16), _sds((NDEV, QW, D), BF16)] + [_sds((NDEV,) + a.shape, F32) for a in smalls],
        scratch_shapes=[pltpu.VMEM((256, D), BF16), pltpu.VMEM((128, D), BF16), pltpu.VMEM((160, D), BF16),
                        pltpu.VMEM((128, D), BF16), pltpu.VMEM((2, FFB, D), BF16), pltpu.VMEM((2, FFB, D), BF16),
                        pltpu.SemaphoreType.DMA((1 + ns, 7)), pltpu.SemaphoreType.DMA((1 + ns, 7)),
                        pltpu.SemaphoreType.DMA((6,))],
        compiler_params=pltpu.CompilerParams(vmem_limit_bytes=40 << 20))(w_in, w_out, w_qkv, w_o, w_up, w_down, *smalls)


SMALL_ROW = {"conv_norm_g": 0, "conv_b_in": 1, "conv_dw_b": 3, "conv_ln_g": 4, "conv_ln_b": 5, "conv_b_out": 6,
             "attn_norm_g": 7, "b_qkv": 8, "q_norm_g": 10, "k_norm_g": 11, "sinks": 12, "b_o": 13, "mlp_norm_g": 14,
             "conv_dw": 16}
SMALL_ROWS = 48


def _fill_small(pk, names, small, s_refs):
    pk[...] = jnp.zeros_like(pk)
    k = 0
    for nm in names:
        cnt = len(small[nm]) if isinstance(small[nm], (tuple, list)) else 1
        r = SMALL_ROW[nm]
        for ref in s_refs[k:k + cnt]:
            rows, lanes = ref.shape
            for c0 in range(0, lanes, D):
                w_ = min(D, lanes - c0)
                pk[r:r + rows, 0:w_] = ref[:, c0:c0 + w_]
                r += rows
        k += cnt


def _flat_small(small):
    flat = []
    for nm in small:
        flat += list(small[nm]) if isinstance(small[nm], (tuple, list)) else [small[nm]]
    return flat


def _allreduce_copies(srcs, lands, send, recv, a0=0):
    x, y, c = _coords()
    me = 4 * x + 2 * y + c
    sends, recvs = [], []
    for a in range(len(srcs)):
        for k in range(1, NDEV):
            peer = (x ^ ((k >> 2) & 1), y ^ ((k >> 1) & 1), c ^ (k & 1))
            pidx = me ^ k
            sends.append(pltpu.make_async_remote_copy(
                src_ref=srcs[a], dst_ref=lands[a].at[me], send_sem=send.at[a0 + a, k - 1],
                recv_sem=recv.at[a0 + a, k - 1], device_id=peer, device_id_type=MESH))
            recvs.append(pltpu.make_async_remote_copy(
                src_ref=srcs[a], dst_ref=lands[a].at[pidx], send_sem=send.at[a0 + a, k - 1],
                recv_sem=recv.at[a0 + a, k - 1], device_id=peer, device_id_type=MESH))
    return sends, recvs


def _device_order_sum(land):
    acc = land[0]
    for s_ in range(1, NDEV):
        acc = acc + land[s_]
    return acc


class _SmallReduce:
    def __init__(self, small, d_rel):
        self.small, self.names = small, list(small)
        self.args = _flat_small(small) + [d_rel]
        vmem = pl.BlockSpec(memory_space=pltpu.VMEM)
        self.in_specs = [vmem] * len(self.args)
        self.out_specs = [vmem, vmem]
        self.out_shape = [_sds((SMALL_ROWS, D), F32), _sds((NBKT, 128), F32)]
        self.scratch = [pltpu.VMEM((SMALL_ROWS, D), F32), pltpu.VMEM((NDEV, SMALL_ROWS, D), F32),
                        pltpu.VMEM((NDEV, NBKT, 128), F32), pltpu.SemaphoreType.DMA((2, 7)),
                        pltpu.SemaphoreType.DMA((2, 7))]

    def _copies(self, ins, scr):
        pk, sland, rland, send, recv = scr
        return _allreduce_copies([pk, ins[-1]], [sland, rland], send, recv)

    def top(self, i, nsteps, ins, outs, scr):
        @pl.when(i == 0)
        def _():
            pk, sland, rland = scr[:3]
            _fill_small(pk, self.names, self.small, ins[:-1])
            x, y, c = _coords()
            me = 4 * x + 2 * y + c
            sland[me] = pk[...]
            rland[me] = ins[-1][...]
            for cp in self._copies(ins, scr)[0]:
                cp.start()

    def bottom(self, i, nsteps, ins, outs, scr):
        @pl.when(i == nsteps - 1)
        def _():
            sends, recvs = self._copies(ins, scr)
            for cp in recvs:
                cp.wait_recv()
            for cp in sends:
                cp.wait_send()
            outs[0][...] = _device_order_sum(scr[1])
            outs[1][...] = _device_order_sum(scr[2])


LATE_ROWS = 8


def _scatter_last(gsrc, small, sred_early):
    n = gsrc.shape[1]
    names = list(small)
    flat = _flat_small(small)

    def body(*refs):
        (g_ref,), s_refs, (early_ref,), (land_ref, sred_ref), (pk, sland, send, recv, lsem) = _split(
            refs, 1, len(flat), 1, 2, 5)
        x, y, c = _coords()
        me = 4 * x + 2 * y + c
        _fill_small(pk, names, small, s_refs)
        small_out, small_in = _allreduce_copies([pk], [sland], send, recv, a0=1)
        big_out, big_in = _scatter_copies([g_ref], [land_ref], send, recv)
        mine = pltpu.make_async_copy(g_ref.at[me], land_ref.at[me], lsem)
        mine.start()
        for cp in big_out + small_out:
            cp.start()
        sland[me] = pk[...]
        for cp in big_in + small_in:
            cp.wait_recv()
        for cp in big_out + small_out:
            cp.wait_send()
        mine.wait()
        sred_ref[...] = early_ref[...]
        sred_ref[0:LATE_ROWS, :] += _device_order_sum(sland)

    vmem = pl.BlockSpec(memory_space=pltpu.VMEM)
    hbm = pl.BlockSpec(memory_space=pl.ANY)
    return pl.pallas_call(
        body, name="scatter_last",
        in_specs=[hbm] + [vmem] * (len(flat) + 1),
        out_specs=[hbm, vmem],
        out_shape=[_sds((NDEV, n, D), BF16), _sds((SMALL_ROWS, D), F32)],
        scratch_shapes=[pltpu.VMEM((LATE_ROWS, D), F32), pltpu.VMEM((NDEV, LATE_ROWS, D), F32),
                        pltpu.SemaphoreType.DMA((2, 7)), pltpu.SemaphoreType.DMA((2, 7)), pltpu.SemaphoreType.DMA],
        compiler_params=pltpu.CompilerParams(vmem_limit_bytes=16 << 20))(gsrc, *flat, sred_early)


def _adam_math(w, g, m, v):
    nm = B1 * m + (1.0 - B1) * g
    nv = B2 * v + (1.0 - B2) * jnp.square(g)
    m_hat = nm / (1.0 - B1 ** STEP)
    v_hat = nv / (1.0 - B2 ** STEP)
    return -LR * (m_hat / (jnp.sqrt(v_hat) + AEPS) + WD * w), nm, nv


def _adamw_small(sred, rred, g_bqkv, W, M_, V_):
    names = list(W)
    nn = len(names)

    def body(*refs):
        (sred_ref, rred_ref, gq_ref), w_refs, m_refs, v_refs, o_refs = _split(refs, 3, nn, nn, nn, 4 * nn)
        x, y, c = _coords()
        me = 4 * x + 2 * y + c
        for i, nm in enumerate(names):
            w_ref, m_ref, v_ref = w_refs[i], m_refs[i], v_refs[i]
            g_ref, d_ref, nm_ref, nv_ref = o_refs[4 * i:4 * i + 4]
            rows, lanes = w_ref.shape
            r = SMALL_ROW.get(nm)
            for c0 in range(0, lanes, D):
                w_ = min(D, lanes - c0)
                cs = slice(c0, c0 + w_)
                if nm == "b_qkv":
                    g = gq_ref[...]
                elif nm == "rel_bias":
                    g = rred_ref[:, 0:w_]
                elif nm in LANE_SHARDED:
                    g = jnp.zeros((rows, w_), F32)
                    for j in range(NDEV):
                        g = g + jnp.where(me == j, sred_ref[r:r + rows, j * w_:(j + 1) * w_], 0.0)
                else:
                    g = sred_ref[r:r + rows, 0:w_]
                    r += rows
                g_ref[:, cs] = g
                d_ref[:, cs], nm_ref[:, cs], nv_ref[:, cs] = _adam_math(w_ref[:, cs], g, m_ref[:, cs], v_ref[:, cs])

    vmem = pl.BlockSpec(memory_space=pltpu.VMEM)
    ws, ms, vs = ([t[nm] for nm in names] for t in (W, M_, V_))
    outs = pl.pallas_call(
        body, name="adamw_small", in_specs=[vmem] * (3 + 3 * nn), out_specs=[vmem] * (4 * nn),
        out_shape=[_sds(W[nm].shape, F32) for nm in names for _ in range(4)],
        compiler_params=pltpu.CompilerParams(vmem_limit_bytes=16 << 20))(sred, rred, g_bqkv, *ws, *ms, *vs)
    return {nm: outs[4 * i:4 * i + 4] for i, nm in enumerate(names)}


def _reduce_adamw(lands, w, m, v, nsplit, name):
    L = len(lands)
    R = lands[0].shape[1]
    wl = D // nsplit
    tr = next(c for c in (256, 160, 128) if R % c == 0)
    nr = R // tr

    def body(*refs):
        l_refs, (w_ref, m_ref, v_ref), (g_ref, d_ref, nm_ref, nv_ref) = _split(refs, L, 3, 4)
        for l in range(L):
            @pl.when(pl.program_id(0) == l)
            def _(l=l):
                g = l_refs[l][0].astype(F32)
                for s_ in range(1, NDEV):
                    g = g + l_refs[l][s_].astype(F32)
                g_ref[...] = g
                d_ref[...], nm_ref[...], nv_ref[...] = _adam_math(w_ref[...], g, m_ref[...], v_ref[...])

    def land_spec(l):
        return pl.BlockSpec((NDEV, tr, wl), lambda ll, h, i: (0, jnp.where(ll == l, i, 0), jnp.where(ll == l, h, 0)))

    spec = pl.BlockSpec((None, tr, wl), lambda ll, h, i: (ll, h * nr + i, 0))
    return pl.pallas_call(
        body, name=name, grid=(L, nsplit, nr),
        in_specs=[land_spec(l) for l in range(L)] + [spec] * 3, out_specs=[spec] * 4,
        out_shape=[_sds(w.shape, F32)] * 4, compiler_params=_cp(48, 3))(*lands, w, m, v)


BIG = ("conv_w_in", "conv_w_out", "w_qkv", "w_o", "w_up", "w_down")
SMALL = ("conv_norm_g", "conv_b_in", "conv_dw", "conv_dw_b", "conv_ln_g", "conv_ln_b", "conv_b_out", "attn_norm_g",
         "b_qkv", "q_norm_g", "k_norm_g", "sinks", "b_o", "rel_bias", "mlp_norm_g")
LANE_SHARDED = ("conv_dw", "attn_norm_g", "b_o")


def _cols_to_blocks(a, n):
    M = a.shape[0]
    return a.reshape(M, NDEV, n).transpose(1, 0, 2).reshape(NDEV, M * n // D, D)


def kernel(x, conv_norm_g, conv_w_in, conv_b_in, conv_dw, conv_dw_b, conv_ln_g, conv_ln_b, conv_w_out, conv_b_out, attn_norm_g, w_qkv, b_qkv, q_norm_g, k_norm_g, sinks, w_o, b_o, rel_bias, mlp_norm_g, w_up, w_down, loss_target, m_conv_norm_g, m_conv_w_in, m_conv_b_in, m_conv_dw, m_conv_dw_b, m_conv_ln_g, m_conv_ln_b, m_conv_w_out, m_conv_b_out, m_attn_norm_g, m_w_qkv, m_b_qkv, m_q_norm_g, m_k_norm_g, m_sinks, m_w_o, m_b_o, m_rel_bias, m_mlp_norm_g, m_w_up, m_w_down, v_conv_norm_g, v_conv_w_in, v_conv_b_in, v_conv_dw, v_conv_dw_b, v_conv_ln_g, v_conv_ln_b, v_conv_w_out, v_conv_b_out, v_attn_norm_g, v_w_qkv, v_b_qkv, v_q_norm_g, v_k_norm_g, v_sinks, v_w_o, v_b_o, v_rel_bias, v_mlp_norm_g, v_w_up, v_w_down):
    names = list(SMALL) + list(BIG)
    loc = dict(locals())
    W = {n: loc[n] for n in names}
    M_ = {n: loc["m_" + n] for n in names}
    V_ = {n: loc["v_" + n] for n in names}
    me = 4 * lax.axis_index("x") + 2 * lax.axis_index("y") + lax.axis_index("c")
    xs = x[0]

    wout_b, qkv_b, wo_b, up_b, dn_b, g_win, g_dw, g_ang, g_bq, g_bo = _gather_first(
        conv_w_in[0], conv_w_out[0], w_qkv.reshape(160, D), w_o[0], w_up, w_down,
        [conv_dw[0], attn_norm_g, b_qkv, b_o])
    dw32 = jnp.pad(g_dw.transpose(1, 0, 2).reshape(CONVW, D), ((0, 1), (0, 0)))
    attn_g, bqkv, bo = g_ang.reshape(1, D), g_bq.reshape(1, QKV), g_bo.reshape(1, D)
    qg = jnp.tile(q_norm_g, (1, NH))
    kg = jnp.tile(k_norm_g, (1, NKV))
    e16, e16t, e2, e2t, fold, fold2 = _seg_mats()
    bkt = jnp.asarray(_bucket_table().T)
    bkt4 = jnp.asarray(np.tile(_bucket_table().T, (1, 4)))

    h0, u, glu, g_wout = _conv_in_fwd(xs, conv_norm_g, g_win, conv_b_in, carry=_Gather([(wout_b, None)]))
    w_out = g_wout.reshape(D, D)
    cc, sb, x1, wu0, wd0 = _conv_mid_fwd(glu, xs, dw32, conv_dw_b, conv_ln_g, conv_ln_b, w_out, conv_b_out,
                                         carry=_Gather([(up_b, 0), (dn_b, 0)]))
    h1, act0, x2, g_qkv, g_wo = _mlp_fwd(x1, mlp_norm_g[0:1], wu0, wd0, carry=_Gather([(qkv_b, None), (wo_b, None)]))
    wqkv = g_qkv.reshape(NDEV, D, 160).transpose(1, 0, 2).reshape(D, QKV)
    wo = g_wo.reshape(D, D)
    h2, raw, qn, kn, vv = _attn_qkv_fwd(x2, attn_g, wqkv, bqkv, qg, kg, e16, e16t, e2, e2t)
    ob, x3, wu1, wd1 = _attn_fwd(qn, kn, vv, bkt, rel_bias, sinks, wo, bo, x2, carry=_Gather([(up_b, 1), (dn_b, 1)]))
    h3, act1, dx4, dx4b, sq = _mlp_fwd(x3, mlp_norm_g[1:2], wu1, wd1, target=loss_target[0])
    loss = lax.psum(jnp.sum(sq) * (0.5 / D), ("x", "y", "c"))

    dup1, dx3, dx3b, dg_mlp1 = _mlp_bwd(dx4, x3, mlp_norm_g[1:2], act1, wu1, wd1)
    g_dn1 = _wgrad(act1, dx4b, 512, D, "wgrad_down1").reshape(NDEV, FFB, D)
    g_up1 = _wgrad_split(h3, dup1, 2, "wgrad_up1")
    dqs, dk, dv, d_sinks, d_rel, d_bo, l_up1, l_dn1 = _attn_bwd(
        dx3, dx3b, qn, kn, vv, bkt, bkt4, rel_bias, sinks, wo, carry=_Scatter([g_up1, g_dn1]))
    g_wo = _wgrad(ob, dx3b, D, 512, "wgrad_o").reshape(NDEV, 128, D)
    dqkv, dx2, dx2b, d_bqkv, d_attn_g, d_qg, d_kg = _attn_qkv_bwd(
        dqs, dk, dv, raw, x2, dx3, attn_g, wqkv, qg, kg, e16, e16t, e2, e2t, fold, fold2)
    g_wqkv = _cols_to_blocks(_wgrad(h2, dqkv, D, 640, "wgrad_qkv"), 160)
    dup0, dx1, dx1b, dg_mlp0, l_qkv, l_wo = _mlp_bwd(dx2, x1, mlp_norm_g[0:1], act0, wu0, wd0,
                                                     carry=_Scatter([g_wqkv, g_wo]))
    g_dn0 = _wgrad(act0, dx2b, 512, D, "wgrad_down0").reshape(NDEV, FFB, D)
    g_up0 = _wgrad_split(h1, dup0, 2, "wgrad_up0")
    dc, d_bout, d_lng, d_lnb, d_dwb = _conv_mid_bwd(dx1, dx1b, cc, conv_ln_g, conv_ln_b, w_out)
    g_wout = _wgrad(sb, dx1b, D, 512, "wgrad_conv_out").reshape(NDEV, 128, D)
    dglu, d_dw, l_wout, l_up0, l_dn0 = _dwconv_bwd(dc, glu, dw32, carry=_Scatter([g_wout, g_up0, g_dn0]))
    early = {"conv_dw": d_dw, "conv_dw_b": d_dwb, "conv_ln_g": d_lng, "conv_ln_b": d_lnb, "conv_b_out": d_bout,
             "attn_norm_g": d_attn_g, "b_qkv": d_bqkv, "q_norm_g": d_qg, "k_norm_g": d_kg, "sinks": d_sinks, "b_o": d_bo,
             "mlp_norm_g": (dg_mlp0, dg_mlp1)}
    du, grad_x, d_bin, d_cng, sred_early, rred = _conv_in_bwd(dglu, u, xs, dx1, conv_norm_g, g_win,
                                                           carry=_SmallReduce(early, d_rel))
    g_win_grad = _wgrad_split(h0, du, 4, "wgrad_conv_in")

    l_win, sred = _scatter_last(g_win_grad, {"conv_norm_g": d_cng, "conv_b_in": d_bin}, sred_early)

    big = {}
    for n, lands, nsplit in (("conv_w_in", [l_win], 4), ("conv_w_out", [l_wout], 1), ("w_o", [l_wo], 1),
                             ("w_up", [l_up0, l_up1], 2), ("w_down", [l_dn0, l_dn1], 1)):
        big[n] = _reduce_adamw(lands, W[n], M_[n], V_[n], nsplit, "adamw_" + n)
    flat = lambda t: t.reshape(1, 160, D)
    big["w_qkv"] = [t.reshape(1, D, 160) for t in
                    _reduce_adamw([l_qkv], flat(w_qkv), flat(m_w_qkv), flat(v_w_qkv), 1, "adamw_w_qkv")]
    r = SMALL_ROW["b_qkv"]
    g_bqkv = lax.dynamic_slice_in_dim(jnp.concatenate([sred[r:r + 1], sred[r + 1:r + 2, :QKV - D]], axis=1),
                                      me * 160, 160, axis=1)
    two_d = lambda t: {n: (t[n][0] if n == "conv_dw" else t[n]) for n in SMALL}
    small = _adamw_small(sred, rred, g_bqkv, two_d(W), two_d(M_), two_d(V_))
    small["conv_dw"] = [t[None] for t in small["conv_dw"]]
    outs = [{**{n: big[n][k] for n in BIG}, **{n: small[n][k] for n in SMALL}} for k in range(4)]
    G, outs = outs[0], outs[1:]
    order = ["conv_norm_g", "conv_w_in", "conv_b_in", "conv_dw", "conv_dw_b", "conv_ln_g", "conv_ln_b", "conv_w_out",
             "conv_b_out", "attn_norm_g", "w_qkv", "b_qkv", "q_norm_g", "k_norm_g", "sinks", "w_o", "b_o", "rel_bias",
             "mlp_norm_g", "w_up", "w_down"]
    return (loss, grad_x[None], *[G[n] for n in order], *[outs[0][n] for n in order],
            *[outs[1][n] for n in order], *[outs[2][n] for n in order])
```

```python
import math

import numpy as np
import jax
import jax.numpy as jnp
from jax import lax
from jax.experimental import pallas as pl
from jax.experimental.pallas import tpu as pltpu

F32 = jnp.float32
BF16 = jnp.bfloat16

D = 1024
DFF = 4096
NH, NKV, HD = 16, 2, 64
NPAIR = NH // 2
KV = NKV * HD
QKV = D + 2 * KV
CONVW = 31
WIN = 128
NBKT = 32
EPS = 1e-6
NEG = -1e30
NDEV = 8
FFB = DFF // NDEV
HALO = 32

LR, B1, B2, AEPS, WD, STEP = 0.001, 0.9, 0.999, 1e-08, 0.01, 10

MESH = pl.DeviceIdType.MESH


def _cp(vmem_mb, n_axes=1):
    return pltpu.CompilerParams(dimension_semantics=("arbitrary",) * n_axes, vmem_limit_bytes=vmem_mb << 20)


def _const(shape):
    nd = len(shape)
    return pl.BlockSpec(shape, lambda *_: (0,) * nd, pipeline_mode=pl.Buffered(1))


def _acc(shape):
    nd = len(shape)
    return pl.BlockSpec(shape, lambda *_: (0,) * nd)


def _rows(tm, n):
    return pl.BlockSpec((tm, n), lambda i: (i, 0))


def _sds(shape, dtype):
    return jax.ShapeDtypeStruct(shape, dtype)


def _dot(a, b):
    return jnp.dot(a, b, preferred_element_type=F32)


def _dot_nt(a, b):
    return lax.dot_general(a, b, (((1,), (1,)), ((), ())), preferred_element_type=F32)


def _dot_tn(a, b):
    return lax.dot_general(a, b, (((0,), (0,)), ((), ())), preferred_element_type=F32)


def _dot_hi(x, e):
    x1 = x.astype(BF16)
    x2 = (x - x1.astype(F32)).astype(BF16)
    return _dot(x1, e) + _dot(x2, e)


def _rms(x):
    return lax.rsqrt(jnp.mean(x * x, axis=-1, keepdims=True) + EPS)


def _rms_bwd(dh, x, r, g):
    xh = x * r
    dxh = dh * g
    dx = r * (dxh - xh * jnp.mean(dxh * xh, axis=-1, keepdims=True))
    return dx, jnp.sum(dh * xh, axis=0, keepdims=True)


def _colsum(a):
    return jnp.sum(a, axis=0, keepdims=True)


def _first(ref, val):
    @pl.when(pl.program_id(0) == 0)
    def _():
        ref[...] = jnp.zeros_like(ref)
    ref[...] += val


def _first_cols(ref, vals):
    @pl.when(pl.program_id(0) == 0)
    def _():
        ref[...] = jnp.zeros_like(ref)
    for c0, val in vals:
        ref[:, c0:c0 + val.shape[1]] += val


QW = 256


def _conv_in_fwd(x, g, gw, b, carry=None):
    S = x.shape[0]
    tm = min(512, S)

    def body(x_ref, g_ref, w_ref, b_ref, h_ref, u_ref, glu_ref):
        xv = x_ref[...]
        hb = (xv * _rms(xv) * g_ref[...]).astype(BF16)
        h_ref[...] = hb
        us = []
        for j in range(NDEV):
            uj = b_ref[:, j * QW:(j + 1) * QW] + _dot(hb[:, 0:QW], w_ref[j, :, 0:QW])
            for q in range(1, 4):
                uj = uj + _dot(hb[:, q * QW:(q + 1) * QW], w_ref[j, :, q * QW:(q + 1) * QW])
            u_ref[:, j * QW:(j + 1) * QW] = uj.astype(BF16)
            us.append(uj)
        for j in range(NDEV // 2):
            glu_ref[:, j * QW:(j + 1) * QW] = us[j] * jax.nn.sigmoid(us[j + NDEV // 2])

    return _carried_call(
        body, carry, name="conv_in_fwd", grid=(S // tm,),
        in_specs=[_rows(tm, D), _const((1, D)), _const((NDEV, QW, D)), _const((1, 2 * D))],
        out_specs=[_rows(tm, D), _rows(tm, 2 * D), _rows(tm, D)],
        out_shape=[_sds((S, D), BF16), _sds((S, 2 * D), BF16), _sds((S, D), F32)],
        scratch_shapes=[], compiler_params=_cp(48), args=[x, g, gw, b])


def _fill_ext(ext, halo_ref, cur_ref, tm, keep):
    ext[0:HALO, :] = jnp.where(keep, halo_ref[...], 0.0)
    ext[HALO:HALO + tm, :] = cur_ref[...]


def _shift_copies(dst, src, n):
    for s_ in range(1, 8):
        dst[s_ - 1, 0:n, :] = src[s_:s_ + n, :]


def _tap(src, sh, o, r0, c0):
    a, s_ = divmod(o, 8)
    ref = src if s_ == 0 else sh.at[s_ - 1]
    return ref[r0 + 8 * a:r0 + 8 * a + 32, c0:c0 + 512]


def _conv_mid_fwd(glu, x, dw, dwb, lng, lnb, w, b, carry=None):
    S = x.shape[0]
    tm = min(256, S)
    hb = tm // HALO

    def body(gl_ref, halo_ref, x_ref, dw_ref, dwb_ref, lng_ref, lnb_ref, w_ref, b_ref, c_ref, s_ref, xo_ref, ext, esh):
        i = pl.program_id(0)
        _fill_ext(ext, halo_ref, gl_ref, tm, i > 0)
        _shift_copies(esh, ext, tm + 24)
        for r0 in range(0, tm, 32):
            for c0 in range(0, D, 512):
                acc = jnp.zeros((32, 512), F32) + dwb_ref[:, c0:c0 + 512]
                for k in range(CONVW):
                    acc = acc + dw_ref[k:k + 1, c0:c0 + 512] * _tap(ext, esh, 2 + k, r0, c0)
                c_ref[r0:r0 + 32, c0:c0 + 512] = acc
        c = c_ref[...]
        mu = jnp.mean(c, axis=-1, keepdims=True)
        xc = c - mu
        y = xc * lax.rsqrt(jnp.mean(xc * xc, axis=-1, keepdims=True) + EPS) * lng_ref[...] + lnb_ref[...]
        sb = (y * jax.nn.sigmoid(y)).astype(BF16)
        s_ref[...] = sb
        xo_ref[...] = x_ref[...] + _dot(sb, w_ref[...]) + b_ref[...]

    return _carried_call(
        body, carry, name="conv_mid_fwd", grid=(S // tm,),
        in_specs=[_rows(tm, D), pl.BlockSpec((HALO, D), lambda i: (jnp.maximum(i * hb - 1, 0), 0)), _rows(tm, D),
                  _const((32, D)), _const((1, D)), _const((1, D)), _const((1, D)), _const((D, D)), _const((1, D))],
        out_specs=[_rows(tm, D), _rows(tm, D), _rows(tm, D)],
        out_shape=[_sds((S, D), F32), _sds((S, D), BF16), _sds((S, D), F32)],
        scratch_shapes=[pltpu.VMEM((tm + HALO, D), F32), pltpu.VMEM((7, tm + 24, D), F32)],
        compiler_params=_cp(48), args=[glu, glu, x, dw, dwb, lng, lnb, w, b])


def _mlp_fwd(x, g, wu, wd, target=None, carry=None):
    S = x.shape[0]
    tm = min(512, S)
    last = target is not None

    def body(*refs):
        if last:
            x_ref, g_ref, wu_ref, wd_ref, t_ref, h_ref, act_ref, dy_ref, dyb_ref, sq_ref, acc = refs
        else:
            x_ref, g_ref, wu_ref, wd_ref, h_ref, act_ref, xo_ref, acc = refs
        xv = x_ref[...]
        hb = (xv * _rms(xv) * g_ref[...]).astype(BF16)
        h_ref[...] = hb
        for j in range(NDEV):
            up = _dot(hb[:, :FFB], wu_ref[j, :, :FFB]) + _dot(hb[:, FFB:], wu_ref[j, :, FFB:])
            a = jnp.square(jnp.maximum(up, 0.0)).astype(BF16)
            act_ref[:, j * FFB:(j + 1) * FFB] = a
            if j == 0:
                acc[...] = _dot(a, wd_ref[j])
            else:
                acc[...] += _dot(a, wd_ref[j])
        y = xv + acc[...]
        if last:
            diff = y - t_ref[...]
            dy = diff * (1.0 / D)
            dy_ref[...] = dy
            dyb_ref[...] = dy.astype(BF16)
            _first(sq_ref, _colsum(diff * diff))
        else:
            xo_ref[...] = y

    in_specs = [_rows(tm, D), _const((1, D)), _const((NDEV, FFB, D)), _const((NDEV, FFB, D))]
    args = [x, g, wu, wd]
    out_specs = [_rows(tm, D), _rows(tm, DFF)]
    out_shape = [_sds((S, D), BF16), _sds((S, DFF), BF16)]
    if last:
        in_specs.append(_rows(tm, D))
        args.append(target)
        out_specs += [_rows(tm, D), _rows(tm, D), _acc((1, D))]
        out_shape += [_sds((S, D), F32), _sds((S, D), BF16), _sds((1, D), F32)]
    else:
        out_specs.append(_rows(tm, D))
        out_shape.append(_sds((S, D), F32))
    return _carried_call(
        body, carry, name="mlp_fwd_loss" if last else "mlp_fwd", grid=(S // tm,),
        in_specs=in_specs, out_specs=out_specs, out_shape=out_shape,
        scratch_shapes=[pltpu.VMEM((tm, D), F32)],
        compiler_params=_cp(52), args=args)


def _mlp_bwd(dy, x, g, act, wu, wd, carry=None):
    S = x.shape[0]
    tm = min(512, S)

    def body(dy_ref, x_ref, g_ref, act_ref, wu_ref, wd_ref, dup_ref, dx_ref, dxb_ref, dg_ref, acc):
        db = dy_ref[...].astype(BF16)
        for j in range(NDEV):
            dact = _dot_nt(db, wd_ref[j])
            a = act_ref[:, j * FFB:(j + 1) * FFB].astype(F32)
            dup = (dact * (2.0 * jnp.sqrt(a))).astype(BF16)
            dup_ref[:, j * FFB:(j + 1) * FFB] = dup
            for h_ in range(2):
                part = _dot_nt(dup, wu_ref[j, :, h_ * FFB:(h_ + 1) * FFB])
                if j == 0:
                    acc[:, h_ * FFB:(h_ + 1) * FFB] = part
                else:
                    acc[:, h_ * FFB:(h_ + 1) * FFB] += part
        xv = x_ref[...]
        dxn, dg = _rms_bwd(acc[...], xv, _rms(xv), g_ref[...])
        dx = dy_ref[...] + dxn
        dx_ref[...] = dx
        dxb_ref[...] = dx.astype(BF16)
        _first(dg_ref, dg)

    return _carried_call(
        body, carry, name="mlp_bwd", grid=(S // tm,),
        in_specs=[_rows(tm, D), _rows(tm, D), _const((1, D)), _rows(tm, DFF),
                  _const((NDEV, FFB, D)), _const((NDEV, FFB, D))],
        out_specs=[_rows(tm, DFF), _rows(tm, D), _rows(tm, D), _acc((1, D))],
        out_shape=[_sds((S, DFF), BF16), _sds((S, D), F32), _sds((S, D), BF16), _sds((1, D), F32)],
        scratch_shapes=[pltpu.VMEM((tm, D), F32)],
        compiler_params=_cp(58), args=[dy, x, g, act, wu, wd])


def _wgrad(a, b, tm, tn, name):
    S, M = a.shape
    N = b.shape[1]
    tk = min(4096, S)
    nk = S // tk

    def body(a_ref, b_ref, o_ref, acc):
        k = pl.program_id(2)

        @pl.when(k == 0)
        def _():
            acc[...] = jnp.zeros_like(acc)

        acc[...] += _dot_tn(a_ref[...], b_ref[...])

        @pl.when(k == nk - 1)
        def _():
            o_ref[...] = acc[...].astype(BF16)

    return pl.pallas_call(
        body, name=name, grid=(M // tm, N // tn, nk),
        in_specs=[pl.BlockSpec((tk, tm), lambda i, j, k: (k, i)), pl.BlockSpec((tk, tn), lambda i, j, k: (k, j))],
        out_specs=pl.BlockSpec((tm, tn), lambda i, j, k: (i, j)),
        out_shape=_sds((M, N), BF16),
        scratch_shapes=[pltpu.VMEM((tm, tn), F32)],
        compiler_params=_cp(48, 3))(a, b)


def _wgrad_split(a, b, nsplit, name, carry=None):
    S, M = a.shape
    N = b.shape[1]
    wb = N // NDEV
    tm, tn = M // nsplit, D
    nb = tn // wb
    tk = min(4096, S)
    nk = S // tk

    def body(a_ref, b_ref, o_ref, acc):
        k = pl.program_id(2)

        @pl.when(k == 0)
        def _():
            acc[...] = jnp.zeros_like(acc)

        acc[...] += _dot_tn(a_ref[...], b_ref[...])

        @pl.when(k == nk - 1)
        def _():
            for jj in range(nb):
                o_ref[jj] = acc[:, jj * wb:(jj + 1) * wb].astype(BF16)

    out = _carried_call(
        body, carry, name=name, grid=(nsplit, N // tn, nk),
        in_specs=[pl.BlockSpec((tk, tm), lambda i, j, k: (k, i)), pl.BlockSpec((tk, tn), lambda i, j, k: (k, j))],
        out_specs=[pl.BlockSpec((nb, tm, wb), lambda i, j, k: (j, 0, i))],
        out_shape=[_sds((NDEV, tm, nsplit * wb), BF16)],
        scratch_shapes=[pltpu.VMEM((tm, tn), F32)],
        compiler_params=_cp(48, 3), args=[a, b])
    return out[0] if carry is None else out


def _bucket_table():
    q = np.arange(WIN)[:, None]
    k = np.arange(2 * WIN)[None, :]
    dist = q + WIN - k
    n = np.maximum(dist, 0)
    max_exact = NBKT // 2
    large = max_exact + (np.log(np.maximum(n, 1).astype(np.float32) / max_exact)
                         / math.log(WIN / max_exact) * (NBKT - max_exact)).astype(np.int32)
    large = np.minimum(large, NBKT - 1)
    bkt = np.where(n < max_exact, n, large).astype(np.int32)
    return np.where((dist >= 0) & (dist < WIN), bkt, -1).astype(np.int32)


def _seg_mats():
    e16 = np.zeros((D, 128), np.float32)
    e16[np.arange(D), np.arange(D) // HD] = 1.0
    e2 = np.zeros((KV, 128), np.float32)
    e2[np.arange(KV), np.arange(KV) // HD] = 1.0
    fold = np.zeros((D, 128), np.float32)
    fold[np.arange(D), np.arange(D) % HD] = 1.0
    fold2 = np.zeros((KV, 128), np.float32)
    fold2[np.arange(KV), np.arange(KV) % HD] = 1.0
    return [jnp.asarray(m, BF16) for m in (e16, e16.T, e2, e2.T, fold, fold2)]


def _head_rms(t, e, et):
    r = lax.rsqrt(_dot_hi(t * t, e) * (1.0 / HD) + EPS)
    return _dot_hi(r, et)


def _attn_qkv_fwd(x, g, w, b, qg, kg, e16, e16t, e2, e2t):
    S = x.shape[0]
    tm = min(512, S)

    def body(x_ref, g_ref, w_ref, b_ref, qg_ref, kg_ref, e16_ref, e16t_ref, e2_ref, e2t_ref,
             h_ref, raw_ref, qn_ref, kn_ref, v_ref):
        xv = x_ref[...]
        hb = (xv * _rms(xv) * g_ref[...]).astype(BF16)
        h_ref[...] = hb
        raw = _dot(hb, w_ref[...]) + b_ref[...]
        raw_ref[...] = raw
        q = raw[:, :D]
        k = raw[:, D:D + KV]
        qn = (q * _head_rms(q, e16_ref[...], e16t_ref[...]) * qg_ref[...] * 0.125).astype(BF16)
        for bb in range(tm // WIN):
            for p in range(NPAIR):
                r = (bb * NPAIR + p) * WIN
                qn_ref[r:r + WIN, :] = qn[bb * WIN:(bb + 1) * WIN, p * 128:(p + 1) * 128]
        kn_ref[...] = (k * _head_rms(k, e2_ref[...], e2t_ref[...]) * kg_ref[...]).astype(BF16)
        v_ref[...] = raw[:, D + KV:].astype(BF16)

    return pl.pallas_call(
        body, name="attn_qkv_fwd", grid=(S // tm,),
        in_specs=[_rows(tm, D), _const((1, D)), _const((D, QKV)), _const((1, QKV)), _const((1, D)), _const((1, KV)),
                  _const((D, 128)), _const((128, D)), _const((KV, 128)), _const((128, KV))],
        out_specs=[_rows(tm, D), _rows(tm, QKV), _rows(tm * NPAIR, 128), _rows(tm, KV), _rows(tm, KV)],
        out_shape=[_sds((S, D), BF16), _sds((S, QKV), F32), _sds((S * NPAIR, 128), BF16), _sds((S, KV), BF16),
                   _sds((S, KV), BF16)],
        compiler_params=_cp(48))(x, g, w, b, qg, kg, e16, e16t, e2, e2t)


def _build_bias(bkt_ref, rb_ref, sk_ref, bias_sc, sk_sc):
    bkt = bkt_ref[...]
    row = lax.broadcasted_iota(jnp.int32, (2 * WIN, WIN), 0)
    for h in range(NH):
        p, e = h // 2, h % 2
        g, pp = p // 4, p % 4

        def add(bk, acc, h=h):
            return acc + jnp.where(bkt == bk, rb_ref[bk, h], 0.0)

        bias = lax.fori_loop(0, NBKT, add, jnp.where(bkt < 0, NEG, 0.0).astype(F32))
        bias_sc[0, g, e, :, pp * WIN:(pp + 1) * WIN] = bias
        bias_sc[1, g, e, :, pp * WIN:(pp + 1) * WIN] = jnp.where(row < WIN, NEG, bias)
        sk_sc[g, e, :, pp * WIN:(pp + 1) * WIN] = jnp.zeros((1, WIN), F32) + sk_ref[0, h]


def _band(ref, p0, g0):
    return jnp.concatenate([ref[pl.ds(p0, WIN), :].astype(F32), ref[pl.ds(g0, WIN), :].astype(F32)], axis=0)


def _stacks(f, mlo):
    r = pltpu.roll(f, HD, 1)
    z = jnp.zeros_like(f)
    return ((jnp.where(mlo, f, z).astype(BF16), jnp.where(mlo, z, r).astype(BF16)),
            (jnp.where(mlo, r, z).astype(BF16), jnp.where(mlo, z, f).astype(BF16)))


def _unstack(d, mlo):
    z = jnp.zeros_like(d[0][0])
    return (jnp.where(mlo, d[0][0], z) + pltpu.roll(jnp.where(mlo, z, d[0][1]), HD, 1)
            + pltpu.roll(jnp.where(mlo, d[1][0], z), HD, 1) + jnp.where(mlo, z, d[1][1]))


def _softmax_sink(l, sk):
    m = jnp.maximum(jnp.max(l, axis=0, keepdims=True), sk)
    ex = jnp.exp(l - m)
    es = jnp.exp(sk - m)
    inv = 1.0 / (jnp.sum(ex, axis=0, keepdims=True) + es)
    return ex * inv, es * inv


GROWS = 4 * WIN


def _attn_fwd(qn, kn, vn, bktt, rel_bias, sinks, wo, bo, x, carry=None):
    S = x.shape[0]
    tq = min(512, S)
    nblk = tq // WIN

    def body(q_ref, k_ref, v_ref, bkt_ref, rb_ref, sk_ref, wo_ref, bo_ref, x_ref, o_ref, xo_ref, bias_sc, sk_sc):
        i = pl.program_id(0)

        @pl.when(i == 0)
        def _():
            _build_bias(bkt_ref, rb_ref, sk_ref, bias_sc, sk_sc)

        mlo = lax.broadcasted_iota(jnp.int32, (2 * WIN, KV), 1) < HD

        def blk(bb, carry):
            r0 = pl.multiple_of(bb * WIN, WIN)
            g0 = pl.multiple_of(i * tq + bb * WIN, WIN)
            first = (g0 == 0).astype(jnp.int32)
            p0 = pl.multiple_of(jnp.maximum(g0 - WIN, 0), WIN)
            ks = _stacks(_band(k_ref, p0, g0), mlo)
            vs = _stacks(_band(v_ref, p0, g0), mlo)
            for g in range(NKV):
                qs = q_ref[pl.ds(pl.multiple_of(bb * (NPAIR * WIN) + g * GROWS, GROWS), GROWS), :]
                og = jnp.zeros((GROWS, 128), F32)
                for e in range(2):
                    l = _dot_nt(ks[g][e], qs) + bias_sc[first, g, e]
                    pr, _ = _softmax_sink(l, sk_sc[g, e])
                    og = og + _dot_tn(pr.astype(BF16), vs[g][e])
                for pp in range(4):
                    p = 4 * g + pp
                    o_ref[pl.ds(r0, WIN), p * 128:(p + 1) * 128] = og[pp * WIN:(pp + 1) * WIN].astype(BF16)
            return carry

        lax.fori_loop(0, nblk, blk, 0)
        xo_ref[...] = x_ref[...] + _dot(o_ref[...], wo_ref[...]) + bo_ref[...]

    smem = pl.BlockSpec(memory_space=pltpu.SMEM)
    return _carried_call(
        body, carry, name="attn_fwd", grid=(S // tq,),
        in_specs=[_rows(tq * NPAIR, 128), _const((S, KV)), _const((S, KV)), _const((2 * WIN, WIN)), smem, smem,
                  _const((D, D)), _const((1, D)), _rows(tq, D)],
        out_specs=[_rows(tq, D), _rows(tq, D)],
        out_shape=[_sds((S, D), BF16), _sds((S, D), F32)],
        scratch_shapes=[pltpu.VMEM((2, NKV, 2, 2 * WIN, GROWS), F32), pltpu.VMEM((NKV, 2, 1, GROWS), F32)],
        compiler_params=_cp(48), args=[qn, kn, vn, bktt, rel_bias, sinks, wo, bo, x])


def _attn_bwd(dx, dxb, qn, kn, vn, bktt, bktt4, rel_bias, sinks, wo, carry=None):
    S = dx.shape[0]
    tq = min(512, S)
    nblk = tq // WIN
    nsteps = S // tq

    def body(dx_ref, dxb_ref, q_ref, k_ref, v_ref, bkt_ref, bkt4_ref, rb_ref, sk_ref, wo_ref,
             dq_ref, dk_ref, dv_ref, dsk_ref, drb_ref, dbo_ref, bias_sc, sk_sc, dbias_sc, dsk_sc, do_sc):
        i = pl.program_id(0)

        @pl.when(i == 0)
        def _():
            _build_bias(bkt_ref, rb_ref, sk_ref, bias_sc, sk_sc)
            dbias_sc[...] = jnp.zeros_like(dbias_sc)
            dsk_sc[...] = jnp.zeros_like(dsk_sc)
            dk_ref[...] = jnp.zeros_like(dk_ref)
            dv_ref[...] = jnp.zeros_like(dv_ref)

        _first(dbo_ref, _colsum(dx_ref[...]))
        do = _dot_nt(dxb_ref[...], wo_ref[...]).astype(BF16)
        for bb in range(nblk):
            for p in range(NPAIR):
                r = (bb * NPAIR + p) * WIN
                do_sc[r:r + WIN, :] = do[bb * WIN:(bb + 1) * WIN, p * 128:(p + 1) * 128]

        mlo = lax.broadcasted_iota(jnp.int32, (2 * WIN, KV), 1) < HD

        def blk(bb, carry):
            g0 = pl.multiple_of(i * tq + bb * WIN, WIN)
            first = (g0 == 0).astype(jnp.int32)
            p0 = pl.multiple_of(jnp.maximum(g0 - WIN, 0), WIN)
            ks = _stacks(_band(k_ref, p0, g0), mlo)
            vs = _stacks(_band(v_ref, p0, g0), mlo)
            dks = [[None, None], [None, None]]
            dvs = [[None, None], [None, None]]
            for g in range(NKV):
                rows = pl.ds(pl.multiple_of(bb * (NPAIR * WIN) + g * GROWS, GROWS), GROWS)
                qs = q_ref[rows, :]
                dos = do_sc[rows, :]
                dqs = jnp.zeros((GROWS, 128), F32)
                for e in range(2):
                    l = _dot_nt(ks[g][e], qs) + bias_sc[first, g, e]
                    pr, ps = _softmax_sink(l, sk_sc[g, e])
                    dp = _dot_nt(vs[g][e], dos)
                    dr = jnp.sum(pr * dp, axis=0, keepdims=True)
                    dl = pr * (dp - dr)
                    dsk_sc[g, e] -= ps * dr
                    dbias_sc[g, e] += dl
                    dlb = dl.astype(BF16)
                    dqs = dqs + _dot_tn(dlb, ks[g][e])
                    dks[g][e] = _dot(dlb, qs)
                    dvs[g][e] = _dot(pr.astype(BF16), dos)
                dq_ref[rows, :] = dqs
            for ref, d in ((dk_ref, _unstack(dks, mlo)), (dv_ref, _unstack(dvs, mlo))):
                ref[pl.ds(p0, WIN), :] += d[:WIN]
                ref[pl.ds(g0, WIN), :] += d[WIN:]
            return carry

        lax.fori_loop(0, nblk, blk, 0)

        @pl.when(i == nsteps - 1)
        def _():
            bkt4v = bkt4_ref[...]
            rid = lax.broadcasted_iota(jnp.int32, (NBKT, 128), 0)
            lid = lax.broadcasted_iota(jnp.int32, (NBKT, 128), 1)
            lid1 = lax.broadcasted_iota(jnp.int32, (1, 128), 1)
            dsk = jnp.zeros((1, 128), F32)
            for g in range(NKV):
                for e in range(2):
                    for pp in range(4):
                        h = 2 * (4 * g + pp) + e
                        t = jnp.sum(dsk_sc[g, e, :, pp * WIN:(pp + 1) * WIN], axis=1, keepdims=True)
                        dsk = dsk + jnp.where(lid1 == h, t, 0.0)
            dsk_ref[...] = dsk

            def per_bucket(bk, acc):
                mb = bkt4v == bk
                for g in range(NKV):
                    for e in range(2):
                        t = jnp.sum(jnp.where(mb, dbias_sc[g, e], 0.0), axis=0, keepdims=True)
                        for pp in range(4):
                            h = 2 * (4 * g + pp) + e
                            tt = jnp.sum(t[:, pp * WIN:(pp + 1) * WIN], axis=1, keepdims=True)
                            acc = acc + jnp.where((rid == bk) & (lid == h), tt, 0.0)
                return acc

            drb_ref[...] = lax.fori_loop(0, NBKT, per_bucket, jnp.zeros((NBKT, 128), F32))

    smem = pl.BlockSpec(memory_space=pltpu.SMEM)
    return _carried_call(
        body, carry, name="attn_bwd", grid=(nsteps,),
        in_specs=[_rows(tq, D), _rows(tq, D), _rows(tq * NPAIR, 128), _const((S, KV)), _const((S, KV)),
                  _const((2 * WIN, WIN)), _const((2 * WIN, GROWS)), smem, smem, _const((D, D))],
        out_specs=[_rows(tq * NPAIR, 128), _acc((S, KV)), _acc((S, KV)), _acc((1, 128)), _acc((NBKT, 128)),
                   _acc((1, D))],
        out_shape=[_sds((S * NPAIR, 128), F32), _sds((S, KV), F32), _sds((S, KV), F32), _sds((1, 128), F32),
                   _sds((NBKT, 128), F32), _sds((1, D), F32)],
        scratch_shapes=[pltpu.VMEM((2, NKV, 2, 2 * WIN, GROWS), F32), pltpu.VMEM((NKV, 2, 1, GROWS), F32),
                        pltpu.VMEM((NKV, 2, 2 * WIN, GROWS), F32), pltpu.VMEM((NKV, 2, 1, GROWS), F32),
                        pltpu.VMEM((tq * NPAIR, 128), BF16)],
        compiler_params=_cp(52), args=[dx, dxb, qn, kn, vn, bktt, bktt4, rel_bias, sinks, wo])


def _head_norm_bwd(dn, t, e, et, gt):
    r = _head_rms(t, e, et)
    th = t * r
    dth = dn * gt
    dt = r * (dth - th * _dot_hi(_dot_hi(dth * th, e) * (1.0 / HD), et))
    return dt, _colsum(dn * th)


def _attn_qkv_bwd(dqs, dk, dv, raw, x, dxo, g, w, qg, kg, e16, e16t, e2, e2t, fold, fold2):
    S = x.shape[0]
    tm = min(512, S)

    def body(dq_ref, dk_ref, dv_ref, raw_ref, x_ref, dxo_ref, g_ref, w_ref, qg_ref, kg_ref,
             e16_ref, e16t_ref, e2_ref, e2t_ref, fold_ref, fold2_ref,
             dqkv_ref, dx_ref, dxb_ref, db_ref, dg_ref, dqg_ref, dkg_ref, dq_sc):
        for bb in range(tm // WIN):
            for p in range(NPAIR):
                r = (bb * NPAIR + p) * WIN
                dq_sc[bb * WIN:(bb + 1) * WIN, p * 128:(p + 1) * 128] = dq_ref[r:r + WIN, :]
        dq, cq = _head_norm_bwd(dq_sc[...] * 0.125, raw_ref[:, :D], e16_ref[...], e16t_ref[...], qg_ref[...])
        dk_, ck = _head_norm_bwd(dk_ref[...], raw_ref[:, D:D + KV], e2_ref[...], e2t_ref[...], kg_ref[...])
        dv_ = dv_ref[...]
        _first(dqg_ref, _dot_hi(cq, fold_ref[...]))
        _first(dkg_ref, _dot_hi(ck, fold2_ref[...]))
        _first_cols(db_ref, [(0, _colsum(dq)), (D, _colsum(dk_)), (D + KV, _colsum(dv_))])
        dqb, dkb, dvb = dq.astype(BF16), dk_.astype(BF16), dv_.astype(BF16)
        dqkv_ref[:, :D] = dqb
        dqkv_ref[:, D:D + KV] = dkb
        dqkv_ref[:, D + KV:] = dvb
        dh = (_dot_nt(dqb, w_ref[:, :D]) + _dot_nt(dkb, w_ref[:, D:D + KV]) + _dot_nt(dvb, w_ref[:, D + KV:]))
        xv = x_ref[...]
        dxn, dg = _rms_bwd(dh, xv, _rms(xv), g_ref[...])
        dx = dxo_ref[...] + dxn
        dx_ref[...] = dx
        dxb_ref[...] = dx.astype(BF16)
        _first(dg_ref, dg)

    return pl.pallas_call(
        body, name="attn_qkv_bwd", grid=(S // tm,),
        in_specs=[_rows(tm * NPAIR, 128), _rows(tm, KV), _rows(tm, KV), _rows(tm, QKV), _rows(tm, D), _rows(tm, D),
                  _const((1, D)), _const((D, QKV)), _const((1, D)), _const((1, KV)),
                  _const((D, 128)), _const((128, D)), _const((KV, 128)), _const((128, KV)),
                  _const((D, 128)), _const((KV, 128))],
        out_specs=[_rows(tm, QKV), _rows(tm, D), _rows(tm, D), _acc((1, QKV)), _acc((1, D)), _acc((1, 128)), _acc((1, 128))],
        out_shape=[_sds((S, QKV), BF16), _sds((S, D), F32), _sds((S, D), BF16), _sds((1, QKV), F32), _sds((1, D), F32),
                   _sds((1, 128), F32), _sds((1, 128), F32)],
        scratch_shapes=[pltpu.VMEM((tm, D), F32)],
        compiler_params=_cp(48))(dqs, dk, dv, raw, x, dxo, g, w, qg, kg, e16, e16t, e2, e2t, fold, fold2)


def _conv_mid_bwd(dx, dxb, c, lng, lnb, w):
    S = dx.shape[0]
    tm = min(512, S)

    def body(dx_ref, dxb_ref, c_ref, lng_ref, lnb_ref, w_ref, dc_ref, dbo_ref, dlg_ref, dlb_ref, ddwb_ref):
        _first(dbo_ref, _colsum(dx_ref[...]))
        ds = _dot_nt(dxb_ref[...], w_ref[...])
        c = c_ref[...]
        xc = c - jnp.mean(c, axis=-1, keepdims=True)
        rstd = lax.rsqrt(jnp.mean(xc * xc, axis=-1, keepdims=True) + EPS)
        ch = xc * rstd
        y = ch * lng_ref[...] + lnb_ref[...]
        sg = jax.nn.sigmoid(y)
        dy = ds * (sg * (1.0 + y * (1.0 - sg)))
        _first(dlg_ref, _colsum(dy * ch))
        _first(dlb_ref, _colsum(dy))
        dch = dy * lng_ref[...]
        dc = rstd * (dch - jnp.mean(dch, axis=-1, keepdims=True) - ch * jnp.mean(dch * ch, axis=-1, keepdims=True))
        dc_ref[...] = dc
        _first(ddwb_ref, _colsum(dc))

    return pl.pallas_call(
        body, name="conv_mid_bwd", grid=(S // tm,),
        in_specs=[_rows(tm, D), _rows(tm, D), _rows(tm, D), _const((1, D)), _const((1, D)), _const((D, D))],
        out_specs=[_rows(tm, D), _acc((1, D)), _acc((1, D)), _acc((1, D)), _acc((1, D))],
        out_shape=[_sds((S, D), F32)] + [_sds((1, D), F32)] * 4,
        compiler_params=_cp(40))(dx, dxb, c, lng, lnb, w)


def _dwconv_bwd(dc, glu, dw, carry=None):
    S = dc.shape[0]
    tm = min(256, S)
    hb = tm // HALO
    nsteps = S // tm

    def body(dc_ref, nxt_ref, gl_ref, halo_ref, dw_ref, dgl_ref, ddw_ref, ext, dext, esh, dsh):
        i = pl.program_id(0)
        _fill_ext(ext, halo_ref, gl_ref, tm, i > 0)
        dext[0:tm, :] = dc_ref[...]
        dext[tm:tm + HALO, :] = jnp.where(i < nsteps - 1, nxt_ref[...], 0.0)
        _shift_copies(esh, ext, tm + 24)
        _shift_copies(dsh, dext, tm + 24)

        @pl.when(i == 0)
        def _():
            ddw_ref[...] = jnp.zeros_like(ddw_ref)

        for c0 in range(0, D, 512):
            for r0 in range(0, tm, 32):
                acc = jnp.zeros((32, 512), F32)
                for j in range(CONVW):
                    acc = acc + dw_ref[CONVW - 1 - j:CONVW - j, c0:c0 + 512] * _tap(dext, dsh, j, r0, c0)
                dgl_ref[r0:r0 + 32, c0:c0 + 512] = acc
            for k in range(CONVW):
                acc = jnp.zeros((32, 512), F32)
                for r0 in range(0, tm, 32):
                    acc = acc + dext[r0:r0 + 32, c0:c0 + 512] * _tap(ext, esh, 2 + k, r0, c0)
                ddw_ref[k:k + 1, c0:c0 + 512] += _colsum(acc)

    return _carried_call(
        body, carry, name="dwconv_bwd", grid=(nsteps,),
        in_specs=[_rows(tm, D), pl.BlockSpec((HALO, D), lambda i: (jnp.minimum((i + 1) * hb, S // HALO - 1), 0)),
                  _rows(tm, D), pl.BlockSpec((HALO, D), lambda i: (jnp.maximum(i * hb - 1, 0), 0)), _const((32, D))],
        out_specs=[_rows(tm, D), _acc((32, D))],
        out_shape=[_sds((S, D), F32), _sds((32, D), F32)],
        scratch_shapes=[pltpu.VMEM((tm + HALO, D), F32), pltpu.VMEM((tm + HALO, D), F32),
                        pltpu.VMEM((7, tm + 24, D), F32), pltpu.VMEM((7, tm + 24, D), F32)],
        compiler_params=_cp(48), args=[dc, dc, glu, glu, dw])


def _conv_in_bwd(dglu, u, x, dxo, g, gw, carry=None):
    S = x.shape[0]
    tm = min(512, S)

    def body(dgl_ref, u_ref, x_ref, dxo_ref, g_ref, w_ref, du_ref, dx_ref, db_ref, dg_ref, dh):
        dgl = dgl_ref[...]
        a = u_ref[:, :D].astype(F32)
        sg = jax.nn.sigmoid(u_ref[:, D:].astype(F32))
        da = dgl * sg
        dgt = dgl * a * sg * (1.0 - sg)
        _first_cols(db_ref, [(0, _colsum(da)), (D, _colsum(dgt))])
        dab, dgb = da.astype(BF16), dgt.astype(BF16)
        du_ref[:, :D] = dab
        du_ref[:, D:] = dgb
        for q in range(4):
            part = None
            for j in range(NDEV):
                src = dab if j < NDEV // 2 else dgb
                c0 = (j % (NDEV // 2)) * QW
                t = _dot_nt(src[:, c0:c0 + QW], w_ref[j, :, q * QW:(q + 1) * QW])
                part = t if part is None else part + t
            dh[:, q * QW:(q + 1) * QW] = part
        xv = x_ref[...]
        dxn, dg = _rms_bwd(dh[...], xv, _rms(xv), g_ref[...])
        dx_ref[...] = dxo_ref[...] + dxn
        _first(dg_ref, dg)

    return _carried_call(
        body, carry, name="conv_in_bwd", grid=(S // tm,),
        in_specs=[_rows(tm, D), _rows(tm, 2 * D), _rows(tm, D), _rows(tm, D), _const((1, D)), _const((NDEV, QW, D))],
        out_specs=[_rows(tm, 2 * D), _rows(tm, D), _acc((1, 2 * D)), _acc((1, D))],
        out_shape=[_sds((S, 2 * D), BF16), _sds((S, D), F32), _sds((1, 2 * D), F32), _sds((1, D), F32)],
        scratch_shapes=[pltpu.VMEM((tm, D), F32)], compiler_params=_cp(48), args=[dglu, u, x, dxo, g, gw])


def _coords():
    return lax.axis_index("x"), lax.axis_index("y"), lax.axis_index("c")


def _split(refs, *counts):
    out, k = [], 0
    for n in counts:
        out.append(refs[k:k + n])
        k += n
    return out


def _carried_call(body, carry, *, name, grid, in_specs, out_specs, out_shape, scratch_shapes, compiler_params, args):
    if carry is None:
        return pl.pallas_call(body, name=name, grid=grid, in_specs=in_specs, out_specs=out_specs, out_shape=out_shape,
                              scratch_shapes=scratch_shapes, compiler_params=compiler_params)(*args)
    counts = (len(in_specs), len(carry.args), len(out_specs), len(carry.out_shape), len(scratch_shapes), len(carry.scratch))
    nsteps = int(np.prod(grid))

    def wrapped(*refs):
        ins, cin, outs, cout, scr, cscr = _split(refs, *counts)
        i = pl.program_id(0)
        for d in range(1, len(grid)):
            i = i * grid[d] + pl.program_id(d)
        carry.top(i, nsteps, cin, cout, cscr)
        body(*ins, *outs, *scr)
        carry.bottom(i, nsteps, cin, cout, cscr)

    return pl.pallas_call(
        wrapped, name=name, grid=grid, in_specs=list(in_specs) + carry.in_specs,
        out_specs=list(out_specs) + carry.out_specs, out_shape=list(out_shape) + carry.out_shape,
        scratch_shapes=list(scratch_shapes) + carry.scratch, compiler_params=compiler_params)(*args, *carry.args)


def _routes():
    x, y, c = _coords()
    return (x, y, c), (x, y, 1 - c), [(1 - x, y), (x, 1 - y), (1 - x, 1 - y)]


def _gather_copies(srcs, dsts, send, recv):
    me, sib, chips = _routes()
    c = me[2]

    def cp(a, k, block, to, own=False):
        idx = 4 * block[0] + 2 * block[1] + block[2]
        return pltpu.make_async_remote_copy(
            src_ref=srcs[a] if own else dsts[a].at[idx], dst_ref=dsts[a].at[idx], send_sem=send.at[a, k],
            recv_sem=recv.at[a, k], device_id=to, device_id_type=MESH)

    own, ici_in, fwd, sib_in = [], [], [], []
    for a in range(len(srcs)):
        own += [cp(a, 0, me, sib, True)] + [cp(a, 1 + j, me, (*ch, c), True) for j, ch in enumerate(chips)]
        ici_in += [cp(a, 1 + j, (*ch, c), me) for j, ch in enumerate(chips)]
        fwd += [cp(a, 4 + j, (*ch, c), sib) for j, ch in enumerate(chips)]
        sib_in += [cp(a, 0, sib, me)] + [cp(a, 4 + j, (*ch, 1 - c), me) for j, ch in enumerate(chips)]
    return own, ici_in, fwd, sib_in


class _Gather:
    def __init__(self, srcs):
        self.idx = [i for _, i in srcs]
        self.args = [a for a, _ in srcs]
        n = len(srcs)
        self.in_specs = [pl.BlockSpec(memory_space=pl.ANY)] * n
        self.out_specs = [pl.BlockSpec(memory_space=pl.ANY)] * n
        self.out_shape = [_sds((NDEV,) + (a.shape if i is None else a.shape[1:]), BF16) for a, i in srcs]
        self.scratch = [pltpu.SemaphoreType.DMA((n, 7)), pltpu.SemaphoreType.DMA((n, 7)), pltpu.SemaphoreType.DMA((n,))]

    def _copies(self, ins, outs, scr):
        send, recv, lsem = scr
        srcs = [r if i is None else r.at[i] for r, i in zip(ins, self.idx)]
        me = _routes()[0]
        slot = 4 * me[0] + 2 * me[1] + me[2]
        local = [pltpu.make_async_copy(s_, d_.at[slot], lsem.at[a]) for a, (s_, d_) in enumerate(zip(srcs, outs))]
        return _gather_copies(srcs, outs, send, recv) + (local,)

    def top(self, i, nsteps, ins, outs, scr):
        @pl.when(i == 0)
        def _():
            own, _, _, _, local = self._copies(ins, outs, scr)
            for cp in local + own:
                cp.start()

        @pl.when(i == (3 * nsteps) // 4)
        def _():
            _, ici_in, fwd, _, _ = self._copies(ins, outs, scr)
            for a_, f_ in zip(ici_in, fwd):
                a_.wait_recv()
                f_.start()

    def bottom(self, i, nsteps, ins, outs, scr):
        @pl.when(i == nsteps - 1)
        def _():
            own, _, fwd, sib_in, local = self._copies(ins, outs, scr)
            for cp in sib_in:
                cp.wait_recv()
            for cp in own + fwd:
                cp.wait_send()
            for cp in local:
                cp.wait()


def _scatter_copies(srcs, lands, send, recv):
    x, y, c = _coords()
    me = 4 * x + 2 * y + c
    sends, recvs = [], []
    for a in range(len(srcs)):
        for k in range(1, NDEV):
            peer = (x ^ ((k >> 2) & 1), y ^ ((k >> 1) & 1), c ^ (k & 1))
            pidx = me ^ k
            sends.append(pltpu.make_async_remote_copy(
                src_ref=srcs[a].at[pidx], dst_ref=lands[a].at[me], send_sem=send.at[a, k - 1],
                recv_sem=recv.at[a, k - 1], device_id=peer, device_id_type=MESH))
            recvs.append(pltpu.make_async_remote_copy(
                src_ref=srcs[a].at[pidx], dst_ref=lands[a].at[pidx], send_sem=send.at[a, k - 1],
                recv_sem=recv.at[a, k - 1], device_id=peer, device_id_type=MESH))
    return sends, recvs


class _Scatter:
    def __init__(self, srcs):
        n = len(srcs)
        self.args = list(srcs)
        self.in_specs = [pl.BlockSpec(memory_space=pl.ANY)] * n
        self.out_specs = [pl.BlockSpec(memory_space=pl.ANY)] * n
        self.out_shape = [_sds(a.shape, BF16) for a in srcs]
        self.scratch = [pltpu.SemaphoreType.DMA((n, 7)), pltpu.SemaphoreType.DMA((n, 7)), pltpu.SemaphoreType.DMA((n,))]

    def _copies(self, ins, outs, scr):
        send, recv, lsem = scr
        x, y, c = _coords()
        me = 4 * x + 2 * y + c
        local = [pltpu.make_async_copy(s_.at[me], d_.at[me], lsem.at[a]) for a, (s_, d_) in enumerate(zip(ins, outs))]
        return _scatter_copies(ins, outs, send, recv) + (local,)

    def top(self, i, nsteps, ins, outs, scr):
        @pl.when(i == 0)
        def _():
            sends, _, local = self._copies(ins, outs, scr)
            for cp in local + sends:
                cp.start()

    def bottom(self, i, nsteps, ins, outs, scr):
        @pl.when(i == nsteps - 1)
        def _():
            sends, recvs, local = self._copies(ins, outs, scr)
            for cp in recvs:
                cp.wait_recv()
            for cp in sends:
                cp.wait_send()
            for cp in local:
                cp.wait()


def _gather_first(w_in, w_out, w_qkv, w_o, w_up, w_down, smalls):
    ns = len(smalls)

    def body(*refs):
        (win_ref, wout_ref, wqkv_ref, wo_ref, wup_ref, wdn_ref), s_refs, (out_b, qkv_b, wo_b, up_b, dn_b, gin_ref), \
            gs_refs, (st_in, st_out, st_qkv, st_wo, st_up, st_dn, send, recv, lsem) = _split(refs, 6, ns, 6, ns, 9)
        for q in range(4):
            st_in[:, q * 256:(q + 1) * 256] = win_ref[q * 256:(q + 1) * 256, :].astype(BF16)
        st_out[...] = wout_ref[...].astype(BF16)
        st_qkv[...] = wqkv_ref[...].astype(BF16)
        st_wo[...] = wo_ref[...].astype(BF16)
        for l in range(2):
            for h_ in range(2):
                st_up[l, :, h_ * FFB:(h_ + 1) * FFB] = wup_ref[l, h_ * FFB:(h_ + 1) * FFB, :].astype(BF16)
            st_dn[l] = wdn_ref[l].astype(BF16)
        me = _routes()[0]
        slot = 4 * me[0] + 2 * me[1] + me[2]
        srcs, dsts = [st_in, *s_refs], [gin_ref, *gs_refs]
        local = [pltpu.make_async_copy(s_, d_, lsem.at[a]) for a, (s_, d_) in enumerate(
            [(st_out, out_b), (st_qkv, qkv_b), (st_wo, wo_b), (st_up, up_b), (st_dn, dn_b), (st_in, gin_ref.at[slot])])]
        for cp in local:
            cp.start()
        for s_ref, gs_ref in zip(s_refs, gs_refs):
            gs_ref[slot] = s_ref[...]
        own, ici_in, fwd, sib_in = _gather_copies(srcs, dsts, send, recv)
        for cp in own:
            cp.start()
        for a_, f_ in zip(ici_in, fwd):
            a_.wait_recv()
            f_.start()
        for cp in sib_in:
            cp.wait_recv()
        for cp in own + fwd:
            cp.wait_send()
        for cp in local:
            cp.wait()

    vmem = pl.BlockSpec(memory_space=pltpu.VMEM)
    hbm = pl.BlockSpec(memory_space=pl.ANY)
    return pl.pallas_call(
        body, name="gather_first",
        in_specs=[vmem] * (6 + ns),
        out_specs=[hbm] * 6 + [vmem] * ns,
        out_shape=[_sds((128, D), BF16), _sds((160, D), BF16), _sds((128, D), BF16), _sds((2, FFB, D), BF16),
                   _sds((2, FFB, D), BF16), _sds((NDEV, QW, D), BF16)] + [_sds((NDEV,) + a.shape, F32) for a in smalls],
        scratch_shapes=[pltpu.VMEM((256, D), BF16), pltpu.VMEM((128, D), BF16), pltpu.VMEM((160, D), BF16),
                        pltpu.VMEM((128, D), BF16), pltpu.VMEM((2, FFB, D), BF16), pltpu.VMEM((2, FFB, D), BF16),
                        pltpu.SemaphoreType.DMA((1 + ns, 7)), pltpu.SemaphoreType.DMA((1 + ns, 7)),
                        pltpu.SemaphoreType.DMA((6,))],
        compiler_params=pltpu.CompilerParams(vmem_limit_bytes=40 << 20))(w_in, w_out, w_qkv, w_o, w_up, w_down, *smalls)


SMALL_ROW = {"conv_norm_g": 0, "conv_b_in": 1, "conv_dw_b": 3, "conv_ln_g": 4, "conv_ln_b": 5, "conv_b_out": 6,
             "attn_norm_g": 7, "b_qkv": 8, "q_norm_g": 10, "k_norm_g": 11, "sinks": 12, "b_o": 13, "mlp_norm_g": 14,
             "conv_dw": 16}
SMALL_ROWS = 48


def _fill_small(pk, names, small, s_refs):
    pk[...] = jnp.zeros_like(pk)
    k = 0
    for nm in names:
        cnt = len(small[nm]) if isinstance(small[nm], (tuple, list)) else 1
        r = SMALL_ROW[nm]
        for ref in s_refs[k:k + cnt]:
            rows, lanes = ref.shape
            for c0 in range(0, lanes, D):
                w_ = min(D, lanes - c0)
                pk[r:r + rows, 0:w_] = ref[:, c0:c0 + w_]
                r += rows
        k += cnt


def _flat_small(small):
    flat = []
    for nm in small:
        flat += list(small[nm]) if isinstance(small[nm], (tuple, list)) else [small[nm]]
    return flat


def _allreduce_copies(srcs, lands, send, recv, a0=0):
    x, y, c = _coords()
    me = 4 * x + 2 * y + c
    sends, recvs = [], []
    for a in range(len(srcs)):
        for k in range(1, NDEV):
            peer = (x ^ ((k >> 2) & 1), y ^ ((k >> 1) & 1), c ^ (k & 1))
            pidx = me ^ k
            sends.append(pltpu.make_async_remote_copy(
                src_ref=srcs[a], dst_ref=lands[a].at[me], send_sem=send.at[a0 + a, k - 1],
                recv_sem=recv.at[a0 + a, k - 1], device_id=peer, device_id_type=MESH))
            recvs.append(pltpu.make_async_remote_copy(
                src_ref=srcs[a], dst_ref=lands[a].at[pidx], send_sem=send.at[a0 + a, k - 1],
                recv_sem=recv.at[a0 + a, k - 1], device_id=peer, device_id_type=MESH))
    return sends, recvs


def _device_order_sum(land):
    acc = land[0]
    for s_ in range(1, NDEV):
        acc = acc + land[s_]
    return acc


class _SmallReduce:
    def __init__(self, small, d_rel):
        self.small, self.names = small, list(small)
        self.args = _flat_small(small) + [d_rel]
        vmem = pl.BlockSpec(memory_space=pltpu.VMEM)
        self.in_specs = [vmem] * len(self.args)
        self.out_specs = [vmem, vmem]
        self.out_shape = [_sds((SMALL_ROWS, D), F32), _sds((NBKT, 128), F32)]
        self.scratch = [pltpu.VMEM((SMALL_ROWS, D), F32), pltpu.VMEM((NDEV, SMALL_ROWS, D), F32),
                        pltpu.VMEM((NDEV, NBKT, 128), F32), pltpu.SemaphoreType.DMA((2, 7)),
                        pltpu.SemaphoreType.DMA((2, 7))]

    def _copies(self, ins, scr):
        pk, sland, rland, send, recv = scr
        return _allreduce_copies([pk, ins[-1]], [sland, rland], send, recv)

    def top(self, i, nsteps, ins, outs, scr):
        @pl.when(i == 0)
        def _():
            pk, sland, rland = scr[:3]
            _fill_small(pk, self.names, self.small, ins[:-1])
            x, y, c = _coords()
            me = 4 * x + 2 * y + c
            sland[me] = pk[...]
            rland[me] = ins[-1][...]
            for cp in self._copies(ins, scr)[0]:
                cp.start()

    def bottom(self, i, nsteps, ins, outs, scr):
        @pl.when(i == nsteps - 1)
        def _():
            sends, recvs = self._copies(ins, scr)
            for cp in recvs:
                cp.wait_recv()
            for cp in sends:
                cp.wait_send()
            outs[0][...] = _device_order_sum(scr[1])
            outs[1][...] = _device_order_sum(scr[2])


def _scatter_last(gsrc):
    def body(g_ref, land_ref, send, recv, lsem):
        x, y, c = _coords()
        me = 4 * x + 2 * y + c
        sends, recvs = _scatter_copies([g_ref], [land_ref], send, recv)
        mine = pltpu.make_async_copy(g_ref.at[me], land_ref.at[me], lsem)
        mine.start()
        for cp in sends:
            cp.start()
        for cp in recvs:
            cp.wait_recv()
        for cp in sends:
            cp.wait_send()
        mine.wait()

    hbm = pl.BlockSpec(memory_space=pl.ANY)
    return pl.pallas_call(
        body, name="scatter_last", in_specs=[hbm], out_specs=hbm, out_shape=_sds(gsrc.shape, BF16),
        scratch_shapes=[pltpu.SemaphoreType.DMA((1, 7)), pltpu.SemaphoreType.DMA((1, 7)), pltpu.SemaphoreType.DMA])(gsrc)


def _adam_math(w, g, m, v):
    nm = B1 * m + (1.0 - B1) * g
    nv = B2 * v + (1.0 - B2) * jnp.square(g)
    m_hat = nm / (1.0 - B1 ** STEP)
    v_hat = nv / (1.0 - B2 ** STEP)
    return -LR * (m_hat / (jnp.sqrt(v_hat) + AEPS) + WD * w), nm, nv


def _adamw_small(sred, rred, g_bqkv, W, M_, V_):
    names = list(W)
    nn = len(names)

    def body(*refs):
        (sred_ref, rred_ref, gq_ref), w_refs, m_refs, v_refs, o_refs = _split(refs, 3, nn, nn, nn, 4 * nn)
        x, y, c = _coords()
        me = 4 * x + 2 * y + c
        for i, nm in enumerate(names):
            w_ref, m_ref, v_ref = w_refs[i], m_refs[i], v_refs[i]
            g_ref, d_ref, nm_ref, nv_ref = o_refs[4 * i:4 * i + 4]
            rows, lanes = w_ref.shape
            r = SMALL_ROW.get(nm)
            for c0 in range(0, lanes, D):
                w_ = min(D, lanes - c0)
                cs = slice(c0, c0 + w_)
                if nm == "b_qkv":
                    g = gq_ref[...]
                elif nm == "rel_bias":
                    g = rred_ref[:, 0:w_]
                elif nm in LANE_SHARDED:
                    g = jnp.zeros((rows, w_), F32)
                    for j in range(NDEV):
                        g = g + jnp.where(me == j, sred_ref[r:r + rows, j * w_:(j + 1) * w_], 0.0)
                else:
                    g = sred_ref[r:r + rows, 0:w_]
                    r += rows
                g_ref[:, cs] = g
                d_ref[:, cs], nm_ref[:, cs], nv_ref[:, cs] = _adam_math(w_ref[:, cs], g, m_ref[:, cs], v_ref[:, cs])

    vmem = pl.BlockSpec(memory_space=pltpu.VMEM)
    ws, ms, vs = ([t[nm] for nm in names] for t in (W, M_, V_))
    outs = pl.pallas_call(
        body, name="adamw_small", in_specs=[vmem] * (3 + 3 * nn), out_specs=[vmem] * (4 * nn),
        out_shape=[_sds(W[nm].shape, F32) for nm in names for _ in range(4)],
        compiler_params=pltpu.CompilerParams(vmem_limit_bytes=16 << 20))(sred, rred, g_bqkv, *ws, *ms, *vs)
    return {nm: outs[4 * i:4 * i + 4] for i, nm in enumerate(names)}


def _reduce_adamw(lands, w, m, v, nsplit, name):
    L = len(lands)
    R = lands[0].shape[1]
    wl = D // nsplit
    tr = next(c for c in (256, 160, 128) if R % c == 0)
    nr = R // tr

    def body(*refs):
        l_refs, (w_ref, m_ref, v_ref), (g_ref, d_ref, nm_ref, nv_ref) = _split(refs, L, 3, 4)
        for l in range(L):
            @pl.when(pl.program_id(0) == l)
            def _(l=l):
                g = l_refs[l][0].astype(F32)
                for s_ in range(1, NDEV):
                    g = g + l_refs[l][s_].astype(F32)
                g_ref[...] = g
                d_ref[...], nm_ref[...], nv_ref[...] = _adam_math(w_ref[...], g, m_ref[...], v_ref[...])

    def land_spec(l):
        return pl.BlockSpec((NDEV, tr, wl), lambda ll, h, i: (0, jnp.where(ll == l, i, 0), jnp.where(ll == l, h, 0)))

    spec = pl.BlockSpec((None, tr, wl), lambda ll, h, i: (ll, h * nr + i, 0))
    return pl.pallas_call(
        body, name=name, grid=(L, nsplit, nr),
        in_specs=[land_spec(l) for l in range(L)] + [spec] * 3, out_specs=[spec] * 4,
        out_shape=[_sds(w.shape, F32)] * 4, compiler_params=_cp(48, 3))(*lands, w, m, v)


BIG = ("conv_w_in", "conv_w_out", "w_qkv", "w_o", "w_up", "w_down")
SMALL = ("conv_norm_g", "conv_b_in", "conv_dw", "conv_dw_b", "conv_ln_g", "conv_ln_b", "conv_b_out", "attn_norm_g",
         "b_qkv", "q_norm_g", "k_norm_g", "sinks", "b_o", "rel_bias", "mlp_norm_g")
LANE_SHARDED = ("conv_dw", "attn_norm_g", "b_o")


def _cols_to_blocks(a, n):
    M = a.shape[0]
    return a.reshape(M, NDEV, n).transpose(1, 0, 2).reshape(NDEV, M * n // D, D)


def kernel(x, conv_norm_g, conv_w_in, conv_b_in, conv_dw, conv_dw_b, conv_ln_g, conv_ln_b, conv_w_out, conv_b_out, attn_norm_g, w_qkv, b_qkv, q_norm_g, k_norm_g, sinks, w_o, b_o, rel_bias, mlp_norm_g, w_up, w_down, loss_target, m_conv_norm_g, m_conv_w_in, m_conv_b_in, m_conv_dw, m_conv_dw_b, m_conv_ln_g, m_conv_ln_b, m_conv_w_out, m_conv_b_out, m_attn_norm_g, m_w_qkv, m_b_qkv, m_q_norm_g, m_k_norm_g, m_sinks, m_w_o, m_b_o, m_rel_bias, m_mlp_norm_g, m_w_up, m_w_down, v_conv_norm_g, v_conv_w_in, v_conv_b_in, v_conv_dw, v_conv_dw_b, v_conv_ln_g, v_conv_ln_b, v_conv_w_out, v_conv_b_out, v_attn_norm_g, v_w_qkv, v_b_qkv, v_q_norm_g, v_k_norm_g, v_sinks, v_w_o, v_b_o, v_rel_bias, v_mlp_norm_g, v_w_up, v_w_down):
    names = list(SMALL) + list(BIG)
    loc = dict(locals())
    W = {n: loc[n] for n in names}
    M_ = {n: loc["m_" + n] for n in names}
    V_ = {n: loc["v_" + n] for n in names}
    me = 4 * lax.axis_index("x") + 2 * lax.axis_index("y") + lax.axis_index("c")
    xs = x[0]

    wout_b, qkv_b, wo_b, up_b, dn_b, g_win, g_dw, g_ang, g_bq, g_bo = _gather_first(
        conv_w_in[0], conv_w_out[0], w_qkv.reshape(160, D), w_o[0], w_up, w_down,
        [conv_dw[0], attn_norm_g, b_qkv, b_o])
    dw32 = jnp.pad(g_dw.transpose(1, 0, 2).reshape(CONVW, D), ((0, 1), (0, 0)))
    attn_g, bqkv, bo = g_ang.reshape(1, D), g_bq.reshape(1, QKV), g_bo.reshape(1, D)
    qg = jnp.tile(q_norm_g, (1, NH))
    kg = jnp.tile(k_norm_g, (1, NKV))
    e16, e16t, e2, e2t, fold, fold2 = _seg_mats()
    bkt = jnp.asarray(_bucket_table().T)
    bkt4 = jnp.asarray(np.tile(_bucket_table().T, (1, 4)))

    h0, u, glu, g_wout = _conv_in_fwd(xs, conv_norm_g, g_win, conv_b_in, carry=_Gather([(wout_b, None)]))
    w_out = g_wout.reshape(D, D)
    cc, sb, x1, wu0, wd0 = _conv_mid_fwd(glu, xs, dw32, conv_dw_b, conv_ln_g, conv_ln_b, w_out, conv_b_out,
                                         carry=_Gather([(up_b, 0), (dn_b, 0)]))
    h1, act0, x2, g_qkv, g_wo = _mlp_fwd(x1, mlp_norm_g[0:1], wu0, wd0, carry=_Gather([(qkv_b, None), (wo_b, None)]))
    wqkv = g_qkv.reshape(NDEV, D, 160).transpose(1, 0, 2).reshape(D, QKV)
    wo = g_wo.reshape(D, D)
    h2, raw, qn, kn, vv = _attn_qkv_fwd(x2, attn_g, wqkv, bqkv, qg, kg, e16, e16t, e2, e2t)
    ob, x3, wu1, wd1 = _attn_fwd(qn, kn, vv, bkt, rel_bias, sinks, wo, bo, x2, carry=_Gather([(up_b, 1), (dn_b, 1)]))
    h3, act1, dx4, dx4b, sq = _mlp_fwd(x3, mlp_norm_g[1:2], wu1, wd1, target=loss_target[0])
    loss = lax.psum(jnp.sum(sq) * (0.5 / D), ("x", "y", "c"))

    dup1, dx3, dx3b, dg_mlp1 = _mlp_bwd(dx4, x3, mlp_norm_g[1:2], act1, wu1, wd1)
    g_dn1 = _wgrad(act1, dx4b, 512, D, "wgrad_down1").reshape(NDEV, FFB, D)
    g_up1 = _wgrad_split(h3, dup1, 2, "wgrad_up1")
    dqs, dk, dv, d_sinks, d_rel, d_bo, l_up1, l_dn1 = _attn_bwd(
        dx3, dx3b, qn, kn, vv, bkt, bkt4, rel_bias, sinks, wo, carry=_Scatter([g_up1, g_dn1]))
    g_wo = _wgrad(ob, dx3b, D, 512, "wgrad_o").reshape(NDEV, 128, D)
    dqkv, dx2, dx2b, d_bqkv, d_attn_g, d_qg, d_kg = _attn_qkv_bwd(
        dqs, dk, dv, raw, x2, dx3, attn_g, wqkv, qg, kg, e16, e16t, e2, e2t, fold, fold2)
    g_wqkv = _cols_to_blocks(_wgrad(h2, dqkv, D, 640, "wgrad_qkv"), 160)
    dup0, dx1, dx1b, dg_mlp0, l_qkv, l_wo = _mlp_bwd(dx2, x1, mlp_norm_g[0:1], act0, wu0, wd0,
                                                     carry=_Scatter([g_wqkv, g_wo]))
    g_dn0 = _wgrad(act0, dx2b, 512, D, "wgrad_down0").reshape(NDEV, FFB, D)
    g_up0 = _wgrad_split(h1, dup0, 2, "wgrad_up0")
    dc, d_bout, d_lng, d_lnb, d_dwb = _conv_mid_bwd(dx1, dx1b, cc, conv_ln_g, conv_ln_b, w_out)
    g_wout = _wgrad(sb, dx1b, D, 512, "wgrad_conv_out").reshape(NDEV, 128, D)
    dglu, d_dw, l_wout, l_up0, l_dn0 = _dwconv_bwd(dc, glu, dw32, carry=_Scatter([g_wout, g_up0, g_dn0]))
    du, grad_x, d_bin, d_cng = _conv_in_bwd(dglu, u, xs, dx1, conv_norm_g, g_win)
    small_grads = {"conv_norm_g": d_cng, "conv_b_in": d_bin, "conv_dw": d_dw, "conv_dw_b": d_dwb, "conv_ln_g": d_lng,
                   "conv_ln_b": d_lnb, "conv_b_out": d_bout, "attn_norm_g": d_attn_g, "b_qkv": d_bqkv, "q_norm_g": d_qg,
                   "k_norm_g": d_kg, "sinks": d_sinks, "b_o": d_bo, "mlp_norm_g": (dg_mlp0, dg_mlp1)}
    g_win_grad, sred, rred = _wgrad_split(h0, du, 4, "wgrad_conv_in", carry=_SmallReduce(small_grads, d_rel))
    l_win = _scatter_last(g_win_grad)

    big = {}
    for n, lands, nsplit in (("conv_w_in", [l_win], 4), ("conv_w_out", [l_wout], 1), ("w_o", [l_wo], 1),
                             ("w_up", [l_up0, l_up1], 2), ("w_down", [l_dn0, l_dn1], 1)):
        big[n] = _reduce_adamw(lands, W[n], M_[n], V_[n], nsplit, "adamw_" + n)
    flat = lambda t: t.reshape(1, 160, D)
    big["w_qkv"] = [t.reshape(1, D, 160) for t in
                    _reduce_adamw([l_qkv], flat(w_qkv), flat(m_w_qkv), flat(v_w_qkv), 1, "adamw_w_qkv")]
    r = SMALL_ROW["b_qkv"]
    g_bqkv = lax.dynamic_slice_in_dim(jnp.concatenate([sred[r:r + 1], sred[r + 1:r + 2, :QKV - D]], axis=1),
                                      me * 160, 160, axis=1)
    two_d = lambda t: {n: (t[n][0] if n == "conv_dw" else t[n]) for n in SMALL}
    small = _adamw_small(sred, rred, g_bqkv, two_d(W), two_d(M_), two_d(V_))
    small["conv_dw"] = [t[None] for t in small["conv_dw"]]
    outs = [{**{n: big[n][k] for n in BIG}, **{n: small[n][k] for n in SMALL}} for k in range(4)]
    G, outs = outs[0], outs[1:]
    order = ["conv_norm_g", "conv_w_in", "conv_b_in", "conv_dw", "conv_dw_b", "conv_ln_g", "conv_ln_b", "conv_w_out",
             "conv_b_out", "attn_norm_g", "w_qkv", "b_qkv", "q_norm_g", "k_norm_g", "sinks", "w_o", "b_o", "rel_bias",
             "mlp_norm_g", "w_up", "w_down"]
    return (loss, grad_x[None], *[G[n] for n in order], *[outs[0][n] for n in order],
            *[outs[1][n] for n in order], *[outs[2][n] for n in order])
```

```python
import math

import numpy as np
import jax
import jax.numpy as jnp
from jax import lax
from jax.experimental import pallas as pl
from jax.experimental.pallas import tpu as pltpu

F32 = jnp.float32
BF16 = jnp.bfloat16

D = 1024
DFF = 4096
NH, NKV, HD = 16, 2, 64
NPAIR = NH // 2
KV = NKV * HD
QKV = D + 2 * KV
CONVW = 31
WIN = 128
NBKT = 32
EPS = 1e-6
NEG = -1e30
NDEV = 8
FFB = DFF // NDEV
HALO = 32

LR, B1, B2, AEPS, WD, STEP = 0.001, 0.9, 0.999, 1e-08, 0.01, 10

MESH = pl.DeviceIdType.MESH


def _cp(vmem_mb, n_axes=1):
    return pltpu.CompilerParams(dimension_semantics=("arbitrary",) * n_axes, vmem_limit_bytes=vmem_mb << 20)


def _const(shape):
    nd = len(shape)
    return pl.BlockSpec(shape, lambda *_: (0,) * nd, pipeline_mode=pl.Buffered(1))


def _acc(shape):
    nd = len(shape)
    return pl.BlockSpec(shape, lambda *_: (0,) * nd)


def _rows(tm, n):
    return pl.BlockSpec((tm, n), lambda i: (i, 0))


def _sds(shape, dtype):
    return jax.ShapeDtypeStruct(shape, dtype)


def _dot(a, b):
    return jnp.dot(a, b, preferred_element_type=F32)


def _dot_nt(a, b):
    return lax.dot_general(a, b, (((1,), (1,)), ((), ())), preferred_element_type=F32)


def _dot_tn(a, b):
    return lax.dot_general(a, b, (((0,), (0,)), ((), ())), preferred_element_type=F32)


def _dot_hi(x, e):
    x1 = x.astype(BF16)
    x2 = (x - x1.astype(F32)).astype(BF16)
    return _dot(x1, e) + _dot(x2, e)


def _rms(x):
    return lax.rsqrt(jnp.mean(x * x, axis=-1, keepdims=True) + EPS)


def _rms_bwd(dh, x, r, g):
    xh = x * r
    dxh = dh * g
    dx = r * (dxh - xh * jnp.mean(dxh * xh, axis=-1, keepdims=True))
    return dx, jnp.sum(dh * xh, axis=0, keepdims=True)


def _colsum(a):
    return jnp.sum(a, axis=0, keepdims=True)


def _first(ref, val):
    @pl.when(pl.program_id(0) == 0)
    def _():
        ref[...] = jnp.zeros_like(ref)
    ref[...] += val


def _first_cols(ref, vals):
    @pl.when(pl.program_id(0) == 0)
    def _():
        ref[...] = jnp.zeros_like(ref)
    for c0, val in vals:
        ref[:, c0:c0 + val.shape[1]] += val


QW = 256


def _conv_in_fwd(x, g, gw, b, carry=None):
    S = x.shape[0]
    tm = min(512, S)

    def body(x_ref, g_ref, w_ref, b_ref, h_ref, u_ref, glu_ref):
        xv = x_ref[...]
        hb = (xv * _rms(xv) * g_ref[...]).astype(BF16)
        h_ref[...] = hb
        us = []
        for j in range(NDEV):
            uj = b_ref[:, j * QW:(j + 1) * QW] + _dot(hb[:, 0:QW], w_ref[j, :, 0:QW])
            for q in range(1, 4):
                uj = uj + _dot(hb[:, q * QW:(q + 1) * QW], w_ref[j, :, q * QW:(q + 1) * QW])
            u_ref[:, j * QW:(j + 1) * QW] = uj.astype(BF16)
            us.append(uj)
        for j in range(NDEV // 2):
            glu_ref[:, j * QW:(j + 1) * QW] = us[j] * jax.nn.sigmoid(us[j + NDEV // 2])

    return _carried_call(
        body, carry, name="conv_in_fwd", grid=(S // tm,),
        in_specs=[_rows(tm, D), _const((1, D)), _const((NDEV, QW, D)), _const((1, 2 * D))],
        out_specs=[_rows(tm, D), _rows(tm, 2 * D), _rows(tm, D)],
        out_shape=[_sds((S, D), BF16), _sds((S, 2 * D), BF16), _sds((S, D), F32)],
        scratch_shapes=[], compiler_params=_cp(48), args=[x, g, gw, b])


def _fill_ext(ext, halo_ref, cur_ref, tm, keep):
    ext[0:HALO, :] = jnp.where(keep, halo_ref[...], 0.0)
    ext[HALO:HALO + tm, :] = cur_ref[...]


def _shift_copies(dst, src, n):
    for s_ in range(1, 8):
        dst[s_ - 1, 0:n, :] = src[s_:s_ + n, :]


def _tap(src, sh, o, r0, c0):
    a, s_ = divmod(o, 8)
    ref = src if s_ == 0 else sh.at[s_ - 1]
    return ref[r0 + 8 * a:r0 + 8 * a + 32, c0:c0 + 512]


def _conv_mid_fwd(glu, x, dw, dwb, lng, lnb, w, b, carry=None):
    S = x.shape[0]
    tm = min(256, S)
    hb = tm // HALO

    def body(gl_ref, halo_ref, x_ref, dw_ref, dwb_ref, lng_ref, lnb_ref, w_ref, b_ref, c_ref, s_ref, xo_ref, ext, esh):
        i = pl.program_id(0)
        _fill_ext(ext, halo_ref, gl_ref, tm, i > 0)
        _shift_copies(esh, ext, tm + 24)
        for r0 in range(0, tm, 32):
            for c0 in range(0, D, 512):
                acc = jnp.zeros((32, 512), F32) + dwb_ref[:, c0:c0 + 512]
                for k in range(CONVW):
                    acc = acc + dw_ref[k:k + 1, c0:c0 + 512] * _tap(ext, esh, 2 + k, r0, c0)
                c_ref[r0:r0 + 32, c0:c0 + 512] = acc
        c = c_ref[...]
        mu = jnp.mean(c, axis=-1, keepdims=True)
        xc = c - mu
        y = xc * lax.rsqrt(jnp.mean(xc * xc, axis=-1, keepdims=True) + EPS) * lng_ref[...] + lnb_ref[...]
        sb = (y * jax.nn.sigmoid(y)).astype(BF16)
        s_ref[...] = sb
        xo_ref[...] = x_ref[...] + _dot(sb, w_ref[...]) + b_ref[...]

    return _carried_call(
        body, carry, name="conv_mid_fwd", grid=(S // tm,),
        in_specs=[_rows(tm, D), pl.BlockSpec((HALO, D), lambda i: (jnp.maximum(i * hb - 1, 0), 0)), _rows(tm, D),
                  _const((32, D)), _const((1, D)), _const((1, D)), _const((1, D)), _const((D, D)), _const((1, D))],
        out_specs=[_rows(tm, D), _rows(tm, D), _rows(tm, D)],
        out_shape=[_sds((S, D), F32), _sds((S, D), BF16), _sds((S, D), F32)],
        scratch_shapes=[pltpu.VMEM((tm + HALO, D), F32), pltpu.VMEM((7, tm + 24, D), F32)],
        compiler_params=_cp(48), args=[glu, glu, x, dw, dwb, lng, lnb, w, b])


def _mlp_fwd(x, g, wu, wd, target=None, carry=None):
    S = x.shape[0]
    tm = min(512, S)
    last = target is not None

    def body(*refs):
        if last:
            x_ref, g_ref, wu_ref, wd_ref, t_ref, h_ref, act_ref, dy_ref, dyb_ref, sq_ref, acc = refs
        else:
            x_ref, g_ref, wu_ref, wd_ref, h_ref, act_ref, xo_ref, acc = refs
        xv = x_ref[...]
        hb = (xv * _rms(xv) * g_ref[...]).astype(BF16)
        h_ref[...] = hb
        for j in range(NDEV):
            up = _dot(hb[:, :FFB], wu_ref[j, :, :FFB]) + _dot(hb[:, FFB:], wu_ref[j, :, FFB:])
            a = jnp.square(jnp.maximum(up, 0.0)).astype(BF16)
            act_ref[:, j * FFB:(j + 1) * FFB] = a
            if j == 0:
                acc[...] = _dot(a, wd_ref[j])
            else:
                acc[...] += _dot(a, wd_ref[j])
        y = xv + acc[...]
        if last:
            diff = y - t_ref[...]
            dy = diff * (1.0 / D)
            dy_ref[...] = dy
            dyb_ref[...] = dy.astype(BF16)
            _first(sq_ref, _colsum(diff * diff))
        else:
            xo_ref[...] = y

    in_specs = [_rows(tm, D), _const((1, D)), _const((NDEV, FFB, D)), _const((NDEV, FFB, D))]
    args = [x, g, wu, wd]
    out_specs = [_rows(tm, D), _rows(tm, DFF)]
    out_shape = [_sds((S, D), BF16), _sds((S, DFF), BF16)]
    if last:
        in_specs.append(_rows(tm, D))
        args.append(target)
        out_specs += [_rows(tm, D), _rows(tm, D), _acc((1, D))]
        out_shape += [_sds((S, D), F32), _sds((S, D), BF16), _sds((1, D), F32)]
    else:
        out_specs.append(_rows(tm, D))
        out_shape.append(_sds((S, D), F32))
    return _carried_call(
        body, carry, name="mlp_fwd_loss" if last else "mlp_fwd", grid=(S // tm,),
        in_specs=in_specs, out_specs=out_specs, out_shape=out_shape,
        scratch_shapes=[pltpu.VMEM((tm, D), F32)],
        compiler_params=_cp(52), args=args)


def _mlp_bwd(dy, x, g, act, wu, wd, carry=None):
    S = x.shape[0]
    tm = min(512, S)

    def body(dy_ref, x_ref, g_ref, act_ref, wu_ref, wd_ref, dup_ref, dx_ref, dxb_ref, dg_ref, acc):
        db = dy_ref[...].astype(BF16)
        for j in range(NDEV):
            dact = _dot_nt(db, wd_ref[j])
            a = act_ref[:, j * FFB:(j + 1) * FFB].astype(F32)
            dup = (dact * (2.0 * jnp.sqrt(a))).astype(BF16)
            dup_ref[:, j * FFB:(j + 1) * FFB] = dup
            for h_ in range(2):
                part = _dot_nt(dup, wu_ref[j, :, h_ * FFB:(h_ + 1) * FFB])
                if j == 0:
                    acc[:, h_ * FFB:(h_ + 1) * FFB] = part
                else:
                    acc[:, h_ * FFB:(h_ + 1) * FFB] += part
        xv = x_ref[...]
        dxn, dg = _rms_bwd(acc[...], xv, _rms(xv), g_ref[...])
        dx = dy_ref[...] + dxn
        dx_ref[...] = dx
        dxb_ref[...] = dx.astype(BF16)
        _first(dg_ref, dg)

    return _carried_call(
        body, carry, name="mlp_bwd", grid=(S // tm,),
        in_specs=[_rows(tm, D), _rows(tm, D), _const((1, D)), _rows(tm, DFF),
                  _const((NDEV, FFB, D)), _const((NDEV, FFB, D))],
        out_specs=[_rows(tm, DFF), _rows(tm, D), _rows(tm, D), _acc((1, D))],
        out_shape=[_sds((S, DFF), BF16), _sds((S, D), F32), _sds((S, D), BF16), _sds((1, D), F32)],
        scratch_shapes=[pltpu.VMEM((tm, D), F32)],
        compiler_params=_cp(58), args=[dy, x, g, act, wu, wd])


def _wgrad(a, b, tm, tn, name):
    S, M = a.shape
    N = b.shape[1]
    tk = min(4096, S)
    nk = S // tk

    def body(a_ref, b_ref, o_ref, acc):
        k = pl.program_id(2)

        @pl.when(k == 0)
        def _():
            acc[...] = jnp.zeros_like(acc)

        acc[...] += _dot_tn(a_ref[...], b_ref[...])

        @pl.when(k == nk - 1)
        def _():
            o_ref[...] = acc[...].astype(BF16)

    return pl.pallas_call(
        body, name=name, grid=(M // tm, N // tn, nk),
        in_specs=[pl.BlockSpec((tk, tm), lambda i, j, k: (k, i)), pl.BlockSpec((tk, tn), lambda i, j, k: (k, j))],
        out_specs=pl.BlockSpec((tm, tn), lambda i, j, k: (i, j)),
        out_shape=_sds((M, N), BF16),
        scratch_shapes=[pltpu.VMEM((tm, tn), F32)],
        compiler_params=_cp(48, 3))(a, b)


def _wgrad_split(a, b, nsplit, name, carry=None):
    S, M = a.shape
    N = b.shape[1]
    wb = N // NDEV
    tm, tn = M // nsplit, D
    nb = tn // wb
    tk = min(4096, S)
    nk = S // tk

    def body(a_ref, b_ref, o_ref, acc):
        k = pl.program_id(2)

        @pl.when(k == 0)
        def _():
            acc[...] = jnp.zeros_like(acc)

        acc[...] += _dot_tn(a_ref[...], b_ref[...])

        @pl.when(k == nk - 1)
        def _():
            for jj in range(nb):
                o_ref[jj] = acc[:, jj * wb:(jj + 1) * wb].astype(BF16)

    out = _carried_call(
        body, carry, name=name, grid=(nsplit, N // tn, nk),
        in_specs=[pl.BlockSpec((tk, tm), lambda i, j, k: (k, i)), pl.BlockSpec((tk, tn), lambda i, j, k: (k, j))],
        out_specs=[pl.BlockSpec((nb, tm, wb), lambda i, j, k: (j, 0, i))],
        out_shape=[_sds((NDEV, tm, nsplit * wb), BF16)],
        scratch_shapes=[pltpu.VMEM((tm, tn), F32)],
        compiler_params=_cp(48, 3), args=[a, b])
    return out[0] if carry is None else out


def _bucket_table():
    q = np.arange(WIN)[:, None]
    k = np.arange(2 * WIN)[None, :]
    dist = q + WIN - k
    n = np.maximum(dist, 0)
    max_exact = NBKT // 2
    large = max_exact + (np.log(np.maximum(n, 1).astype(np.float32) / max_exact)
                         / math.log(WIN / max_exact) * (NBKT - max_exact)).astype(np.int32)
    large = np.minimum(large, NBKT - 1)
    bkt = np.where(n < max_exact, n, large).astype(np.int32)
    return np.where((dist >= 0) & (dist < WIN), bkt, -1).astype(np.int32)


def _seg_mats():
    e16 = np.zeros((D, 128), np.float32)
    e16[np.arange(D), np.arange(D) // HD] = 1.0
    e2 = np.zeros((KV, 128), np.float32)
    e2[np.arange(KV), np.arange(KV) // HD] = 1.0
    fold = np.zeros((D, 128), np.float32)
    fold[np.arange(D), np.arange(D) % HD] = 1.0
    fold2 = np.zeros((KV, 128), np.float32)
    fold2[np.arange(KV), np.arange(KV) % HD] = 1.0
    return [jnp.asarray(m, BF16) for m in (e16, e16.T, e2, e2.T, fold, fold2)]


def _head_rms(t, e, et):
    r = lax.rsqrt(_dot_hi(t * t, e) * (1.0 / HD) + EPS)
    return _dot_hi(r, et)


def _attn_qkv_fwd(x, g, w, b, qg, kg, e16, e16t, e2, e2t):
    S = x.shape[0]
    tm = min(512, S)

    def body(x_ref, g_ref, w_ref, b_ref, qg_ref, kg_ref, e16_ref, e16t_ref, e2_ref, e2t_ref,
             h_ref, raw_ref, qn_ref, kn_ref, v_ref):
        xv = x_ref[...]
        hb = (xv * _rms(xv) * g_ref[...]).astype(BF16)
        h_ref[...] = hb
        raw = _dot(hb, w_ref[...]) + b_ref[...]
        raw_ref[...] = raw.astype(BF16)
        q = raw[:, :D]
        k = raw[:, D:D + KV]
        qn = (q * _head_rms(q, e16_ref[...], e16t_ref[...]) * qg_ref[...] * 0.125).astype(BF16)
        for bb in range(tm // WIN):
            for p in range(NPAIR):
                r = (bb * NPAIR + p) * WIN
                qn_ref[r:r + WIN, :] = qn[bb * WIN:(bb + 1) * WIN, p * 128:(p + 1) * 128]
        kn_ref[...] = (k * _head_rms(k, e2_ref[...], e2t_ref[...]) * kg_ref[...]).astype(BF16)
        v_ref[...] = raw[:, D + KV:].astype(BF16)

    return pl.pallas_call(
        body, name="attn_qkv_fwd", grid=(S // tm,),
        in_specs=[_rows(tm, D), _const((1, D)), _const((D, QKV)), _const((1, QKV)), _const((1, D)), _const((1, KV)),
                  _const((D, 128)), _const((128, D)), _const((KV, 128)), _const((128, KV))],
        out_specs=[_rows(tm, D), _rows(tm, QKV), _rows(tm * NPAIR, 128), _rows(tm, KV), _rows(tm, KV)],
        out_shape=[_sds((S, D), BF16), _sds((S, QKV), BF16), _sds((S * NPAIR, 128), BF16), _sds((S, KV), BF16),
                   _sds((S, KV), BF16)],
        compiler_params=_cp(48))(x, g, w, b, qg, kg, e16, e16t, e2, e2t)


def _build_bias(bkt_ref, rb_ref, sk_ref, bias_sc, sk_sc):
    bkt = bkt_ref[...]
    row = lax.broadcasted_iota(jnp.int32, (2 * WIN, WIN), 0)
    for h in range(NH):
        p, e = h // 2, h % 2
        g, pp = p // 4, p % 4

        def add(bk, acc, h=h):
            return acc + jnp.where(bkt == bk, rb_ref[bk, h], 0.0)

        bias = lax.fori_loop(0, NBKT, add, jnp.where(bkt < 0, NEG, 0.0).astype(F32))
        bias_sc[0, g, e, :, pp * WIN:(pp + 1) * WIN] = bias
        bias_sc[1, g, e, :, pp * WIN:(pp + 1) * WIN] = jnp.where(row < WIN, NEG, bias)
        sk_sc[g, e, :, pp * WIN:(pp + 1) * WIN] = jnp.zeros((1, WIN), F32) + sk_ref[0, h]


def _band(ref, p0, g0):
    return jnp.concatenate([ref[pl.ds(p0, WIN), :].astype(F32), ref[pl.ds(g0, WIN), :].astype(F32)], axis=0)


def _stacks(f, mlo):
    r = pltpu.roll(f, HD, 1)
    z = jnp.zeros_like(f)
    return ((jnp.where(mlo, f, z).astype(BF16), jnp.where(mlo, z, r).astype(BF16)),
            (jnp.where(mlo, r, z).astype(BF16), jnp.where(mlo, z, f).astype(BF16)))


def _unstack(d, mlo):
    z = jnp.zeros_like(d[0][0])
    return (jnp.where(mlo, d[0][0], z) + pltpu.roll(jnp.where(mlo, z, d[0][1]), HD, 1)
            + pltpu.roll(jnp.where(mlo, d[1][0], z), HD, 1) + jnp.where(mlo, z, d[1][1]))


def _softmax_sink(l, sk):
    m = jnp.maximum(jnp.max(l, axis=0, keepdims=True), sk)
    ex = jnp.exp(l - m)
    es = jnp.exp(sk - m)
    inv = 1.0 / (jnp.sum(ex, axis=0, keepdims=True) + es)
    return ex * inv, es * inv


GROWS = 4 * WIN


def _attn_fwd(qn, kn, vn, bktt, rel_bias, sinks, wo, bo, x, carry=None):
    S = x.shape[0]
    tq = min(512, S)
    nblk = tq // WIN

    def body(q_ref, k_ref, v_ref, bkt_ref, rb_ref, sk_ref, wo_ref, bo_ref, x_ref, o_ref, xo_ref, bias_sc, sk_sc):
        i = pl.program_id(0)

        @pl.when(i == 0)
        def _():
            _build_bias(bkt_ref, rb_ref, sk_ref, bias_sc, sk_sc)

        mlo = lax.broadcasted_iota(jnp.int32, (2 * WIN, KV), 1) < HD

        def blk(bb, carry):
            r0 = pl.multiple_of(bb * WIN, WIN)
            g0 = pl.multiple_of(i * tq + bb * WIN, WIN)
            first = (g0 == 0).astype(jnp.int32)
            p0 = pl.multiple_of(jnp.maximum(g0 - WIN, 0), WIN)
            ks = _stacks(_band(k_ref, p0, g0), mlo)
            vs = _stacks(_band(v_ref, p0, g0), mlo)
            for g in range(NKV):
                qs = q_ref[pl.ds(pl.multiple_of(bb * (NPAIR * WIN) + g * GROWS, GROWS), GROWS), :]
                og = jnp.zeros((GROWS, 128), F32)
                for e in range(2):
                    l = _dot_nt(ks[g][e], qs) + bias_sc[first, g, e]
                    pr, _ = _softmax_sink(l, sk_sc[g, e])
                    og = og + _dot_tn(pr.astype(BF16), vs[g][e])
                for pp in range(4):
                    p = 4 * g + pp
                    o_ref[pl.ds(r0, WIN), p * 128:(p + 1) * 128] = og[pp * WIN:(pp + 1) * WIN].astype(BF16)
            return carry

        lax.fori_loop(0, nblk, blk, 0)
        xo_ref[...] = x_ref[...] + _dot(o_ref[...], wo_ref[...]) + bo_ref[...]

    smem = pl.BlockSpec(memory_space=pltpu.SMEM)
    return _carried_call(
        body, carry, name="attn_fwd", grid=(S // tq,),
        in_specs=[_rows(tq * NPAIR, 128), _const((S, KV)), _const((S, KV)), _const((2 * WIN, WIN)), smem, smem,
                  _const((D, D)), _const((1, D)), _rows(tq, D)],
        out_specs=[_rows(tq, D), _rows(tq, D)],
        out_shape=[_sds((S, D), BF16), _sds((S, D), F32)],
        scratch_shapes=[pltpu.VMEM((2, NKV, 2, 2 * WIN, GROWS), F32), pltpu.VMEM((NKV, 2, 1, GROWS), F32)],
        compiler_params=_cp(48), args=[qn, kn, vn, bktt, rel_bias, sinks, wo, bo, x])


def _attn_bwd(dx, dxb, qn, kn, vn, bktt, bktt4, rel_bias, sinks, wo, carry=None):
    S = dx.shape[0]
    tq = min(512, S)
    nblk = tq // WIN
    nsteps = S // tq

    def body(dx_ref, dxb_ref, q_ref, k_ref, v_ref, bkt_ref, bkt4_ref, rb_ref, sk_ref, wo_ref,
             dq_ref, dk_ref, dv_ref, dsk_ref, drb_ref, dbo_ref, bias_sc, sk_sc, dbias_sc, dsk_sc, do_sc):
        i = pl.program_id(0)

        @pl.when(i == 0)
        def _():
            _build_bias(bkt_ref, rb_ref, sk_ref, bias_sc, sk_sc)
            dbias_sc[...] = jnp.zeros_like(dbias_sc)
            dsk_sc[...] = jnp.zeros_like(dsk_sc)
            dk_ref[...] = jnp.zeros_like(dk_ref)
            dv_ref[...] = jnp.zeros_like(dv_ref)

        _first(dbo_ref, _colsum(dx_ref[...]))
        do = _dot_nt(dxb_ref[...], wo_ref[...]).astype(BF16)
        for bb in range(nblk):
            for p in range(NPAIR):
                r = (bb * NPAIR + p) * WIN
                do_sc[r:r + WIN, :] = do[bb * WIN:(bb + 1) * WIN, p * 128:(p + 1) * 128]

        mlo = lax.broadcasted_iota(jnp.int32, (2 * WIN, KV), 1) < HD

        def blk(bb, carry):
            g0 = pl.multiple_of(i * tq + bb * WIN, WIN)
            first = (g0 == 0).astype(jnp.int32)
            p0 = pl.multiple_of(jnp.maximum(g0 - WIN, 0), WIN)
            ks = _stacks(_band(k_ref, p0, g0), mlo)
            vs = _stacks(_band(v_ref, p0, g0), mlo)
            dks = [[None, None], [None, None]]
            dvs = [[None, None], [None, None]]
            for g in range(NKV):
                rows = pl.ds(pl.multiple_of(bb * (NPAIR * WIN) + g * GROWS, GROWS), GROWS)
                qs = q_ref[rows, :]
                dos = do_sc[rows, :]
                dqs = jnp.zeros((GROWS, 128), F32)
                for e in range(2):
                    l = _dot_nt(ks[g][e], qs) + bias_sc[first, g, e]
                    pr, ps = _softmax_sink(l, sk_sc[g, e])
                    dp = _dot_nt(vs[g][e], dos)
                    dr = jnp.sum(pr * dp, axis=0, keepdims=True)
                    dl = pr * (dp - dr)
                    dsk_sc[g, e] -= ps * dr
                    dbias_sc[g, e] += dl
                    dlb = dl.astype(BF16)
                    dqs = dqs + _dot_tn(dlb, ks[g][e])
                    dks[g][e] = _dot(dlb, qs)
                    dvs[g][e] = _dot(pr.astype(BF16), dos)
                dq_ref[rows, :] = dqs
            for ref, d in ((dk_ref, _unstack(dks, mlo)), (dv_ref, _unstack(dvs, mlo))):
                ref[pl.ds(p0, WIN), :] += d[:WIN]
                ref[pl.ds(g0, WIN), :] += d[WIN:]
            return carry

        lax.fori_loop(0, nblk, blk, 0)

        @pl.when(i == nsteps - 1)
        def _():
            bkt4v = bkt4_ref[...]
            rid = lax.broadcasted_iota(jnp.int32, (NBKT, 128), 0)
            lid = lax.broadcasted_iota(jnp.int32, (NBKT, 128), 1)
            lid1 = lax.broadcasted_iota(jnp.int32, (1, 128), 1)
            dsk = jnp.zeros((1, 128), F32)
            for g in range(NKV):
                for e in range(2):
                    for pp in range(4):
                        h = 2 * (4 * g + pp) + e
                        t = jnp.sum(dsk_sc[g, e, :, pp * WIN:(pp + 1) * WIN], axis=1, keepdims=True)
                        dsk = dsk + jnp.where(lid1 == h, t, 0.0)
            dsk_ref[...] = dsk

            def per_bucket(bk, acc):
                mb = bkt4v == bk
                for g in range(NKV):
                    for e in range(2):
                        t = jnp.sum(jnp.where(mb, dbias_sc[g, e], 0.0), axis=0, keepdims=True)
                        for pp in range(4):
                            h = 2 * (4 * g + pp) + e
                            tt = jnp.sum(t[:, pp * WIN:(pp + 1) * WIN], axis=1, keepdims=True)
                            acc = acc + jnp.where((rid == bk) & (lid == h), tt, 0.0)
                return acc

            drb_ref[...] = lax.fori_loop(0, NBKT, per_bucket, jnp.zeros((NBKT, 128), F32))

    smem = pl.BlockSpec(memory_space=pltpu.SMEM)
    return _carried_call(
        body, carry, name="attn_bwd", grid=(nsteps,),
        in_specs=[_rows(tq, D), _rows(tq, D), _rows(tq * NPAIR, 128), _const((S, KV)), _const((S, KV)),
                  _const((2 * WIN, WIN)), _const((2 * WIN, GROWS)), smem, smem, _const((D, D))],
        out_specs=[_rows(tq * NPAIR, 128), _acc((S, KV)), _acc((S, KV)), _acc((1, 128)), _acc((NBKT, 128)),
                   _acc((1, D))],
        out_shape=[_sds((S * NPAIR, 128), F32), _sds((S, KV), F32), _sds((S, KV), F32), _sds((1, 128), F32),
                   _sds((NBKT, 128), F32), _sds((1, D), F32)],
        scratch_shapes=[pltpu.VMEM((2, NKV, 2, 2 * WIN, GROWS), F32), pltpu.VMEM((NKV, 2, 1, GROWS), F32),
                        pltpu.VMEM((NKV, 2, 2 * WIN, GROWS), F32), pltpu.VMEM((NKV, 2, 1, GROWS), F32),
                        pltpu.VMEM((tq * NPAIR, 128), BF16)],
        compiler_params=_cp(52), args=[dx, dxb, qn, kn, vn, bktt, bktt4, rel_bias, sinks, wo])


def _head_norm_bwd(dn, t, e, et, gt):
    r = _head_rms(t, e, et)
    th = t * r
    dth = dn * gt
    dt = r * (dth - th * _dot_hi(_dot_hi(dth * th, e) * (1.0 / HD), et))
    return dt, _colsum(dn * th)


def _attn_qkv_bwd(dqs, dk, dv, raw, x, dxo, g, w, qg, kg, e16, e16t, e2, e2t, fold, fold2):
    S = x.shape[0]
    tm = min(512, S)

    def body(dq_ref, dk_ref, dv_ref, raw_ref, x_ref, dxo_ref, g_ref, w_ref, qg_ref, kg_ref,
             e16_ref, e16t_ref, e2_ref, e2t_ref, fold_ref, fold2_ref,
             dqkv_ref, dx_ref, dxb_ref, db_ref, dg_ref, dqg_ref, dkg_ref, dq_sc):
        for bb in range(tm // WIN):
            for p in range(NPAIR):
                r = (bb * NPAIR + p) * WIN
                dq_sc[bb * WIN:(bb + 1) * WIN, p * 128:(p + 1) * 128] = dq_ref[r:r + WIN, :]
        dq, cq = _head_norm_bwd(dq_sc[...] * 0.125, raw_ref[:, :D].astype(F32), e16_ref[...], e16t_ref[...], qg_ref[...])
        dk_, ck = _head_norm_bwd(dk_ref[...], raw_ref[:, D:D + KV].astype(F32), e2_ref[...], e2t_ref[...], kg_ref[...])
        dv_ = dv_ref[...]
        _first(dqg_ref, _dot_hi(cq, fold_ref[...]))
        _first(dkg_ref, _dot_hi(ck, fold2_ref[...]))
        _first_cols(db_ref, [(0, _colsum(dq)), (D, _colsum(dk_)), (D + KV, _colsum(dv_))])
        dqb, dkb, dvb = dq.astype(BF16), dk_.astype(BF16), dv_.astype(BF16)
        dqkv_ref[:, :D] = dqb
        dqkv_ref[:, D:D + KV] = dkb
        dqkv_ref[:, D + KV:] = dvb
        dh = (_dot_nt(dqb, w_ref[:, :D]) + _dot_nt(dkb, w_ref[:, D:D + KV]) + _dot_nt(dvb, w_ref[:, D + KV:]))
        xv = x_ref[...]
        dxn, dg = _rms_bwd(dh, xv, _rms(xv), g_ref[...])
        dx = dxo_ref[...] + dxn
        dx_ref[...] = dx
        dxb_ref[...] = dx.astype(BF16)
        _first(dg_ref, dg)

    return pl.pallas_call(
        body, name="attn_qkv_bwd", grid=(S // tm,),
        in_specs=[_rows(tm * NPAIR, 128), _rows(tm, KV), _rows(tm, KV), _rows(tm, QKV), _rows(tm, D), _rows(tm, D),
                  _const((1, D)), _const((D, QKV)), _const((1, D)), _const((1, KV)),
                  _const((D, 128)), _const((128, D)), _const((KV, 128)), _const((128, KV)),
                  _const((D, 128)), _const((KV, 128))],
        out_specs=[_rows(tm, QKV), _rows(tm, D), _rows(tm, D), _acc((1, QKV)), _acc((1, D)), _acc((1, 128)), _acc((1, 128))],
        out_shape=[_sds((S, QKV), BF16), _sds((S, D), F32), _sds((S, D), BF16), _sds((1, QKV), F32), _sds((1, D), F32),
                   _sds((1, 128), F32), _sds((1, 128), F32)],
        scratch_shapes=[pltpu.VMEM((tm, D), F32)],
        compiler_params=_cp(48))(dqs, dk, dv, raw, x, dxo, g, w, qg, kg, e16, e16t, e2, e2t, fold, fold2)


def _conv_mid_bwd(dx, dxb, c, lng, lnb, w):
    S = dx.shape[0]
    tm = min(512, S)

    def body(dx_ref, dxb_ref, c_ref, lng_ref, lnb_ref, w_ref, dc_ref, dbo_ref, dlg_ref, dlb_ref, ddwb_ref):
        _first(dbo_ref, _colsum(dx_ref[...]))
        ds = _dot_nt(dxb_ref[...], w_ref[...])
        c = c_ref[...]
        xc = c - jnp.mean(c, axis=-1, keepdims=True)
        rstd = lax.rsqrt(jnp.mean(xc * xc, axis=-1, keepdims=True) + EPS)
        ch = xc * rstd
        y = ch * lng_ref[...] + lnb_ref[...]
        sg = jax.nn.sigmoid(y)
        dy = ds * (sg * (1.0 + y * (1.0 - sg)))
        _first(dlg_ref, _colsum(dy * ch))
        _first(dlb_ref, _colsum(dy))
        dch = dy * lng_ref[...]
        dc = rstd * (dch - jnp.mean(dch, axis=-1, keepdims=True) - ch * jnp.mean(dch * ch, axis=-1, keepdims=True))
        dc_ref[...] = dc
        _first(ddwb_ref, _colsum(dc))

    return pl.pallas_call(
        body, name="conv_mid_bwd", grid=(S // tm,),
        in_specs=[_rows(tm, D), _rows(tm, D), _rows(tm, D), _const((1, D)), _const((1, D)), _const((D, D))],
        out_specs=[_rows(tm, D), _acc((1, D)), _acc((1, D)), _acc((1, D)), _acc((1, D))],
        out_shape=[_sds((S, D), F32)] + [_sds((1, D), F32)] * 4,
        compiler_params=_cp(40))(dx, dxb, c, lng, lnb, w)


def _dwconv_bwd(dc, glu, dw, carry=None):
    S = dc.shape[0]
    tm = min(256, S)
    hb = tm // HALO
    nsteps = S // tm

    def body(dc_ref, nxt_ref, gl_ref, halo_ref, dw_ref, dgl_ref, ddw_ref, ext, dext, esh, dsh):
        i = pl.program_id(0)
        _fill_ext(ext, halo_ref, gl_ref, tm, i > 0)
        dext[0:tm, :] = dc_ref[...]
        dext[tm:tm + HALO, :] = jnp.where(i < nsteps - 1, nxt_ref[...], 0.0)
        _shift_copies(esh, ext, tm + 24)
        _shift_copies(dsh, dext, tm + 24)

        @pl.when(i == 0)
        def _():
            ddw_ref[...] = jnp.zeros_like(ddw_ref)

        for c0 in range(0, D, 512):
            for r0 in range(0, tm, 32):
                acc = jnp.zeros((32, 512), F32)
                for j in range(CONVW):
                    acc = acc + dw_ref[CONVW - 1 - j:CONVW - j, c0:c0 + 512] * _tap(dext, dsh, j, r0, c0)
                dgl_ref[r0:r0 + 32, c0:c0 + 512] = acc.astype(BF16)
            for k in range(CONVW):
                acc = jnp.zeros((32, 512), F32)
                for r0 in range(0, tm, 32):
                    acc = acc + dext[r0:r0 + 32, c0:c0 + 512] * _tap(ext, esh, 2 + k, r0, c0)
                ddw_ref[k:k + 1, c0:c0 + 512] += _colsum(acc)

    return _carried_call(
        body, carry, name="dwconv_bwd", grid=(nsteps,),
        in_specs=[_rows(tm, D), pl.BlockSpec((HALO, D), lambda i: (jnp.minimum((i + 1) * hb, S // HALO - 1), 0)),
                  _rows(tm, D), pl.BlockSpec((HALO, D), lambda i: (jnp.maximum(i * hb - 1, 0), 0)), _const((32, D))],
        out_specs=[_rows(tm, D), _acc((32, D))],
        out_shape=[_sds((S, D), BF16), _sds((32, D), F32)],
        scratch_shapes=[pltpu.VMEM((tm + HALO, D), F32), pltpu.VMEM((tm + HALO, D), F32),
                        pltpu.VMEM((7, tm + 24, D), F32), pltpu.VMEM((7, tm + 24, D), F32)],
        compiler_params=_cp(48), args=[dc, dc, glu, glu, dw])


def _conv_in_bwd(dglu, u, x, dxo, g, gw, carry=None):
    S = x.shape[0]
    tm = min(512, S)

    def body(dgl_ref, u_ref, x_ref, dxo_ref, g_ref, w_ref, du_ref, dx_ref, db_ref, dg_ref, dh):
        dgl = dgl_ref[...].astype(F32)
        a = u_ref[:, :D].astype(F32)
        sg = jax.nn.sigmoid(u_ref[:, D:].astype(F32))
        da = dgl * sg
        dgt = dgl * a * sg * (1.0 - sg)
        _first_cols(db_ref, [(0, _colsum(da)), (D, _colsum(dgt))])
        dab, dgb = da.astype(BF16), dgt.astype(BF16)
        du_ref[:, :D] = dab
        du_ref[:, D:] = dgb
        for q in range(4):
            part = None
            for j in range(NDEV):
                src = dab if j < NDEV // 2 else dgb
                c0 = (j % (NDEV // 2)) * QW
                t = _dot_nt(src[:, c0:c0 + QW], w_ref[j, :, q * QW:(q + 1) * QW])
                part = t if part is None else part + t
            dh[:, q * QW:(q + 1) * QW] = part
        xv = x_ref[...]
        dxn, dg = _rms_bwd(dh[...], xv, _rms(xv), g_ref[...])
        dx_ref[...] = dxo_ref[...] + dxn
        _first(dg_ref, dg)

    return _carried_call(
        body, carry, name="conv_in_bwd", grid=(S // tm,),
        in_specs=[_rows(tm, D), _rows(tm, 2 * D), _rows(tm, D), _rows(tm, D), _const((1, D)), _const((NDEV, QW, D))],
        out_specs=[_rows(tm, 2 * D), _rows(tm, D), _acc((1, 2 * D)), _acc((1, D))],
        out_shape=[_sds((S, 2 * D), BF16), _sds((S, D), F32), _sds((1, 2 * D), F32), _sds((1, D), F32)],
        scratch_shapes=[pltpu.VMEM((tm, D), F32)], compiler_params=_cp(48), args=[dglu, u, x, dxo, g, gw])


def _coords():
    return lax.axis_index("x"), lax.axis_index("y"), lax.axis_index("c")


def _split(refs, *counts):
    out, k = [], 0
    for n in counts:
        out.append(refs[k:k + n])
        k += n
    return out


def _carried_call(body, carry, *, name, grid, in_specs, out_specs, out_shape, scratch_shapes, compiler_params, args):
    if carry is None:
        return pl.pallas_call(body, name=name, grid=grid, in_specs=in_specs, out_specs=out_specs, out_shape=out_shape,
                              scratch_shapes=scratch_shapes, compiler_params=compiler_params)(*args)
    counts = (len(in_specs), len(carry.args), len(out_specs), len(carry.out_shape), len(scratch_shapes), len(carry.scratch))
    nsteps = int(np.prod(grid))

    def wrapped(*refs):
        ins, cin, outs, cout, scr, cscr = _split(refs, *counts)
        i = pl.program_id(0)
        for d in range(1, len(grid)):
            i = i * grid[d] + pl.program_id(d)
        carry.top(i, nsteps, cin, cout, cscr)
        body(*ins, *outs, *scr)
        carry.bottom(i, nsteps, cin, cout, cscr)

    return pl.pallas_call(
        wrapped, name=name, grid=grid, in_specs=list(in_specs) + carry.in_specs,
        out_specs=list(out_specs) + carry.out_specs, out_shape=list(out_shape) + carry.out_shape,
        scratch_shapes=list(scratch_shapes) + carry.scratch, compiler_params=compiler_params)(*args, *carry.args)


def _routes():
    x, y, c = _coords()
    return (x, y, c), (x, y, 1 - c), [(1 - x, y), (x, 1 - y), (1 - x, 1 - y)]


def _gather_copies(srcs, dsts, send, recv):
    me, sib, chips = _routes()
    c = me[2]

    def cp(a, k, block, to, own=False):
        idx = 4 * block[0] + 2 * block[1] + block[2]
        return pltpu.make_async_remote_copy(
            src_ref=srcs[a] if own else dsts[a].at[idx], dst_ref=dsts[a].at[idx], send_sem=send.at[a, k],
            recv_sem=recv.at[a, k], device_id=to, device_id_type=MESH)

    own, ici_in, fwd, sib_in = [], [], [], []
    for a in range(len(srcs)):
        own += [cp(a, 0, me, sib, True)] + [cp(a, 1 + j, me, (*ch, c), True) for j, ch in enumerate(chips)]
        ici_in += [cp(a, 1 + j, (*ch, c), me) for j, ch in enumerate(chips)]
        fwd += [cp(a, 4 + j, (*ch, c), sib) for j, ch in enumerate(chips)]
        sib_in += [cp(a, 0, sib, me)] + [cp(a, 4 + j, (*ch, 1 - c), me) for j, ch in enumerate(chips)]
    return own, ici_in, fwd, sib_in


class _Gather:
    def __init__(self, srcs):
        self.idx = [i for _, i in srcs]
        self.args = [a for a, _ in srcs]
        n = len(srcs)
        self.in_specs = [pl.BlockSpec(memory_space=pl.ANY)] * n
        self.out_specs = [pl.BlockSpec(memory_space=pl.ANY)] * n
        self.out_shape = [_sds((NDEV,) + (a.shape if i is None else a.shape[1:]), BF16) for a, i in srcs]
        self.scratch = [pltpu.SemaphoreType.DMA((n, 7)), pltpu.SemaphoreType.DMA((n, 7)), pltpu.SemaphoreType.DMA((n,))]

    def _copies(self, ins, outs, scr):
        send, recv, lsem = scr
        srcs = [r if i is None else r.at[i] for r, i in zip(ins, self.idx)]
        me = _routes()[0]
        slot = 4 * me[0] + 2 * me[1] + me[2]
        local = [pltpu.make_async_copy(s_, d_.at[slot], lsem.at[a]) for a, (s_, d_) in enumerate(zip(srcs, outs))]
        return _gather_copies(srcs, outs, send, recv) + (local,)

    def top(self, i, nsteps, ins, outs, scr):
        @pl.when(i == 0)
        def _():
            own, _, _, _, local = self._copies(ins, outs, scr)
            for cp in local + own:
                cp.start()

        @pl.when(i == (3 * nsteps) // 4)
        def _():
            _, ici_in, fwd, _, _ = self._copies(ins, outs, scr)
            for a_, f_ in zip(ici_in, fwd):
                a_.wait_recv()
                f_.start()

    def bottom(self, i, nsteps, ins, outs, scr):
        @pl.when(i == nsteps - 1)
        def _():
            own, _, fwd, sib_in, local = self._copies(ins, outs, scr)
            for cp in sib_in:
                cp.wait_recv()
            for cp in own + fwd:
                cp.wait_send()
            for cp in local:
                cp.wait()


def _scatter_copies(srcs, lands, send, recv):
    x, y, c = _coords()
    me = 4 * x + 2 * y + c
    sends, recvs = [], []
    for a in range(len(srcs)):
        for k in range(1, NDEV):
            peer = (x ^ ((k >> 2) & 1), y ^ ((k >> 1) & 1), c ^ (k & 1))
            pidx = me ^ k
            sends.append(pltpu.make_async_remote_copy(
                src_ref=srcs[a].at[pidx], dst_ref=lands[a].at[me], send_sem=send.at[a, k - 1],
                recv_sem=recv.at[a, k - 1], device_id=peer, device_id_type=MESH))
            recvs.append(pltpu.make_async_remote_copy(
                src_ref=srcs[a].at[pidx], dst_ref=lands[a].at[pidx], send_sem=send.at[a, k - 1],
                recv_sem=recv.at[a, k - 1], device_id=peer, device_id_type=MESH))
    return sends, recvs


class _Scatter:
    def __init__(self, srcs):
        n = len(srcs)
        self.args = list(srcs)
        self.in_specs = [pl.BlockSpec(memory_space=pl.ANY)] * n
        self.out_specs = [pl.BlockSpec(memory_space=pl.ANY)] * n
        self.out_shape = [_sds(a.shape, BF16) for a in srcs]
        self.scratch = [pltpu.SemaphoreType.DMA((n, 7)), pltpu.SemaphoreType.DMA((n, 7)), pltpu.SemaphoreType.DMA((n,))]

    def _copies(self, ins, outs, scr):
        send, recv, lsem = scr
        x, y, c = _coords()
        me = 4 * x + 2 * y + c
        local = [pltpu.make_async_copy(s_.at[me], d_.at[me], lsem.at[a]) for a, (s_, d_) in enumerate(zip(ins, outs))]
        return _scatter_copies(ins, outs, send, recv) + (local,)

    def top(self, i, nsteps, ins, outs, scr):
        @pl.when(i == 0)
        def _():
            sends, _, local = self._copies(ins, outs, scr)
            for cp in local + sends:
                cp.start()

    def bottom(self, i, nsteps, ins, outs, scr):
        @pl.when(i == nsteps - 1)
        def _():
            sends, recvs, local = self._copies(ins, outs, scr)
            for cp in recvs:
                cp.wait_recv()
            for cp in sends:
                cp.wait_send()
            for cp in local:
                cp.wait()


def _gather_first(w_in, w_out, w_qkv, w_o, w_up, w_down, smalls):
    ns = len(smalls)

    def body(*refs):
        (win_ref, wout_ref, wqkv_ref, wo_ref, wup_ref, wdn_ref), s_refs, (out_b, qkv_b, wo_b, up_b, dn_b, gin_ref), \
            gs_refs, (st_in, st_out, st_qkv, st_wo, st_up, st_dn, send, recv, lsem) = _split(refs, 6, ns, 6, ns, 9)
        for q in range(4):
            st_in[:, q * 256:(q + 1) * 256] = win_ref[q * 256:(q + 1) * 256, :].astype(BF16)
        st_out[...] = wout_ref[...].astype(BF16)
        st_qkv[...] = wqkv_ref[...].astype(BF16)
        st_wo[...] = wo_ref[...].astype(BF16)
        for l in range(2):
            for h_ in range(2):
                st_up[l, :, h_ * FFB:(h_ + 1) * FFB] = wup_ref[l, h_ * FFB:(h_ + 1) * FFB, :].astype(BF16)
            st_dn[l] = wdn_ref[l].astype(BF16)
        me = _routes()[0]
        slot = 4 * me[0] + 2 * me[1] + me[2]
        srcs, dsts = [st_in, *s_refs], [gin_ref, *gs_refs]
        local = [pltpu.make_async_copy(s_, d_, lsem.at[a]) for a, (s_, d_) in enumerate(
            [(st_out, out_b), (st_qkv, qkv_b), (st_wo, wo_b), (st_up, up_b), (st_dn, dn_b), (st_in, gin_ref.at[slot])])]
        for cp in local:
            cp.start()
        for s_ref, gs_ref in zip(s_refs, gs_refs):
            gs_ref[slot] = s_ref[...]
        own, ici_in, fwd, sib_in = _gather_copies(srcs, dsts, send, recv)
        for cp in own:
            cp.start()
        for a_, f_ in zip(ici_in, fwd):
            a_.wait_recv()
            f_.start()
        for cp in sib_in:
            cp.wait_recv()
        for cp in own + fwd:
            cp.wait_send()
        for cp in local:
            cp.wait()

    vmem = pl.BlockSpec(memory_space=pltpu.VMEM)
    hbm = pl.BlockSpec(memory_space=pl.ANY)
    return pl.pallas_call(
        body, name="gather_first",
        in_specs=[vmem] * (6 + ns),
        out_specs=[hbm] * 6 + [vmem] * ns,
        out_shape=[_sds((128, D), BF16), _sds((160, D), BF16), _sds((128, D), BF16), _sds((2, FFB, D), BF16),
                   _sds((2, FFB, D), BF16), _sds((NDEV, QW, D), BF16)] + [_sds((NDEV,) + a.shape, F32) for a in smalls],
        scratch_shapes=[pltpu.VMEM((256, D), BF16), pltpu.VMEM((128, D), BF16), pltpu.VMEM((160, D), BF16),
                        pltpu.VMEM((128, D), BF16), pltpu.VMEM((2, FFB, D), BF16), pltpu.VMEM((2, FFB, D), BF16),
                        pltpu.SemaphoreType.DMA((1 + ns, 7)), pltpu.SemaphoreType.DMA((1 + ns, 7)),
                        pltpu.SemaphoreType.DMA((6,))],
        compiler_params=pltpu.CompilerParams(vmem_limit_bytes=40 << 20))(w_in, w_out, w_qkv, w_o, w_up, w_down, *smalls)


SMALL_ROW = {"conv_norm_g": 0, "conv_b_in": 1, "conv_dw_b": 3, "conv_ln_g": 4, "conv_ln_b": 5, "conv_b_out": 6,
             "attn_norm_g": 7, "b_qkv": 8, "q_norm_g": 10, "k_norm_g": 11, "sinks": 12, "b_o": 13, "mlp_norm_g": 14,
             "conv_dw": 16}
SMALL_ROWS = 48


def _fill_small(pk, names, small, s_refs):
    pk[...] = jnp.zeros_like(pk)
    k = 0
    for nm in names:
        cnt = len(small[nm]) if isinstance(small[nm], (tuple, list)) else 1
        r = SMALL_ROW[nm]
        for ref in s_refs[k:k + cnt]:
            rows, lanes = ref.shape
            for c0 in range(0, lanes, D):
                w_ = min(D, lanes - c0)
                pk[r:r + rows, 0:w_] = ref[:, c0:c0 + w_]
                r += rows
        k += cnt


def _flat_small(small):
    flat = []
    for nm in small:
        flat += list(small[nm]) if isinstance(small[nm], (tuple, list)) else [small[nm]]
    return flat


def _allreduce_copies(srcs, lands, send, recv, a0=0):
    x, y, c = _coords()
    me = 4 * x + 2 * y + c
    sends, recvs = [], []
    for a in range(len(srcs)):
        for k in range(1, NDEV):
            peer = (x ^ ((k >> 2) & 1), y ^ ((k >> 1) & 1), c ^ (k & 1))
            pidx = me ^ k
            sends.append(pltpu.make_async_remote_copy(
                src_ref=srcs[a], dst_ref=lands[a].at[me], send_sem=send.at[a0 + a, k - 1],
                recv_sem=recv.at[a0 + a, k - 1], device_id=peer, device_id_type=MESH))
            recvs.append(pltpu.make_async_remote_copy(
                src_ref=srcs[a], dst_ref=lands[a].at[pidx], send_sem=send.at[a0 + a, k - 1],
                recv_sem=recv.at[a0 + a, k - 1], device_id=peer, device_id_type=MESH))
    return sends, recvs


def _device_order_sum(land):
    acc = land[0]
    for s_ in range(1, NDEV):
        acc = acc + land[s_]
    return acc


class _SmallReduce:
    def __init__(self, small, d_rel):
        self.small, self.names = small, list(small)
        self.args = _flat_small(small) + [d_rel]
        vmem = pl.BlockSpec(memory_space=pltpu.VMEM)
        self.in_specs = [vmem] * len(self.args)
        self.out_specs = [vmem, vmem]
        self.out_shape = [_sds((SMALL_ROWS, D), F32), _sds((NBKT, 128), F32)]
        self.scratch = [pltpu.VMEM((SMALL_ROWS, D), F32), pltpu.VMEM((NDEV, SMALL_ROWS, D), F32),
                        pltpu.VMEM((NDEV, NBKT, 128), F32), pltpu.SemaphoreType.DMA((2, 7)),
                        pltpu.SemaphoreType.DMA((2, 7))]

    def _copies(self, ins, scr):
        pk, sland, rland, send, recv = scr
        return _allreduce_copies([pk, ins[-1]], [sland, rland], send, recv)

    def top(self, i, nsteps, ins, outs, scr):
        @pl.when(i == 0)
        def _():
            pk, sland, rland = scr[:3]
            _fill_small(pk, self.names, self.small, ins[:-1])
            x, y, c = _coords()
            me = 4 * x + 2 * y + c
            sland[me] = pk[...]
            rland[me] = ins[-1][...]
            for cp in self._copies(ins, scr)[0]:
                cp.start()

    def bottom(self, i, nsteps, ins, outs, scr):
        @pl.when(i == nsteps - 1)
        def _():
            sends, recvs = self._copies(ins, scr)
            for cp in recvs:
                cp.wait_recv()
            for cp in sends:
                cp.wait_send()
            outs[0][...] = _device_order_sum(scr[1])
            outs[1][...] = _device_order_sum(scr[2])


def _scatter_last(gsrc):
    def body(g_ref, land_ref, send, recv, lsem):
        x, y, c = _coords()
        me = 4 * x + 2 * y + c
        sends, recvs = _scatter_copies([g_ref], [land_ref], send, recv)
        mine = pltpu.make_async_copy(g_ref.at[me], land_ref.at[me], lsem)
        mine.start()
        for cp in sends:
            cp.start()
        for cp in recvs:
            cp.wait_recv()
        for cp in sends:
            cp.wait_send()
        mine.wait()

    hbm = pl.BlockSpec(memory_space=pl.ANY)
    return pl.pallas_call(
        body, name="scatter_last", in_specs=[hbm], out_specs=hbm, out_shape=_sds(gsrc.shape, BF16),
        scratch_shapes=[pltpu.SemaphoreType.DMA((1, 7)), pltpu.SemaphoreType.DMA((1, 7)), pltpu.SemaphoreType.DMA])(gsrc)


def _adam_math(w, g, m, v):
    nm = B1 * m + (1.0 - B1) * g
    nv = B2 * v + (1.0 - B2) * jnp.square(g)
    m_hat = nm / (1.0 - B1 ** STEP)
    v_hat = nv / (1.0 - B2 ** STEP)
    return -LR * (m_hat / (jnp.sqrt(v_hat) + AEPS) + WD * w), nm, nv


def _adamw_small(sred, rred, g_bqkv, W, M_, V_):
    names = list(W)
    nn = len(names)

    def body(*refs):
        (sred_ref, rred_ref, gq_ref), w_refs, m_refs, v_refs, o_refs = _split(refs, 3, nn, nn, nn, 4 * nn)
        x, y, c = _coords()
        me = 4 * x + 2 * y + c
        for i, nm in enumerate(names):
            w_ref, m_ref, v_ref = w_refs[i], m_refs[i], v_refs[i]
            g_ref, d_ref, nm_ref, nv_ref = o_refs[4 * i:4 * i + 4]
            rows, lanes = w_ref.shape
            r = SMALL_ROW.get(nm)
            for c0 in range(0, lanes, D):
                w_ = min(D, lanes - c0)
                cs = slice(c0, c0 + w_)
                if nm == "b_qkv":
                    g = gq_ref[...]
                elif nm == "rel_bias":
                    g = rred_ref[:, 0:w_]
                elif nm in LANE_SHARDED:
                    g = jnp.zeros((rows, w_), F32)
                    for j in range(NDEV):
                        g = g + jnp.where(me == j, sred_ref[r:r + rows, j * w_:(j + 1) * w_], 0.0)
                else:
                    g = sred_ref[r:r + rows, 0:w_]
                    r += rows
                g_ref[:, cs] = g
                d_ref[:, cs], nm_ref[:, cs], nv_ref[:, cs] = _adam_math(w_ref[:, cs], g, m_ref[:, cs], v_ref[:, cs])

    vmem = pl.BlockSpec(memory_space=pltpu.VMEM)
    ws, ms, vs = ([t[nm] for nm in names] for t in (W, M_, V_))
    outs = pl.pallas_call(
        body, name="adamw_small", in_specs=[vmem] * (3 + 3 * nn), out_specs=[vmem] * (4 * nn),
        out_shape=[_sds(W[nm].shape, F32) for nm in names for _ in range(4)],
        compiler_params=pltpu.CompilerParams(vmem_limit_bytes=16 << 20))(sred, rred, g_bqkv, *ws, *ms, *vs)
    return {nm: outs[4 * i:4 * i + 4] for i, nm in enumerate(names)}


def _reduce_adamw(lands, w, m, v, nsplit, name):
    L = len(lands)
    R = lands[0].shape[1]
    wl = D // nsplit
    tr = next(c for c in (256, 160, 128) if R % c == 0)
    nr = R // tr

    def body(*refs):
        l_refs, (w_ref, m_ref, v_ref), (g_ref, d_ref, nm_ref, nv_ref) = _split(refs, L, 3, 4)
        for l in range(L):
            @pl.when(pl.program_id(0) == l)
            def _(l=l):
                g = l_refs[l][0].astype(F32)
                for s_ in range(1, NDEV):
                    g = g + l_refs[l][s_].astype(F32)
                g_ref[...] = g
                d_ref[...], nm_ref[...], nv_ref[...] = _adam_math(w_ref[...], g, m_ref[...], v_ref[...])

    def land_spec(l):
        return pl.BlockSpec((NDEV, tr, wl), lambda ll, h, i: (0, jnp.where(ll == l, i, 0), jnp.where(ll == l, h, 0)))

    spec = pl.BlockSpec((None, tr, wl), lambda ll, h, i: (ll, h * nr + i, 0))
    return pl.pallas_call(
        body, name=name, grid=(L, nsplit, nr),
        in_specs=[land_spec(l) for l in range(L)] + [spec] * 3, out_specs=[spec] * 4,
        out_shape=[_sds(w.shape, F32)] * 4, compiler_params=_cp(48, 3))(*lands, w, m, v)


BIG = ("conv_w_in", "conv_w_out", "w_qkv", "w_o", "w_up", "w_down")
SMALL = ("conv_norm_g", "conv_b_in", "conv_dw", "conv_dw_b", "conv_ln_g", "conv_ln_b", "conv_b_out", "attn_norm_g",
         "b_qkv", "q_norm_g", "k_norm_g", "sinks", "b_o", "rel_bias", "mlp_norm_g")
LANE_SHARDED = ("conv_dw", "attn_norm_g", "b_o")


def _cols_to_blocks(a, n):
    M = a.shape[0]
    return a.reshape(M, NDEV, n).transpose(1, 0, 2).reshape(NDEV, M * n // D, D)


def kernel(x, conv_norm_g, conv_w_in, conv_b_in, conv_dw, conv_dw_b, conv_ln_g, conv_ln_b, conv_w_out, conv_b_out, attn_norm_g, w_qkv, b_qkv, q_norm_g, k_norm_g, sinks, w_o, b_o, rel_bias, mlp_norm_g, w_up, w_down, loss_target, m_conv_norm_g, m_conv_w_in, m_conv_b_in, m_conv_dw, m_conv_dw_b, m_conv_ln_g, m_conv_ln_b, m_conv_w_out, m_conv_b_out, m_attn_norm_g, m_w_qkv, m_b_qkv, m_q_norm_g, m_k_norm_g, m_sinks, m_w_o, m_b_o, m_rel_bias, m_mlp_norm_g, m_w_up, m_w_down, v_conv_norm_g, v_conv_w_in, v_conv_b_in, v_conv_dw, v_conv_dw_b, v_conv_ln_g, v_conv_ln_b, v_conv_w_out, v_conv_b_out, v_attn_norm_g, v_w_qkv, v_b_qkv, v_q_norm_g, v_k_norm_g, v_sinks, v_w_o, v_b_o, v_rel_bias, v_mlp_norm_g, v_w_up, v_w_down):
    names = list(SMALL) + list(BIG)
    loc = dict(locals())
    W = {n: loc[n] for n in names}
    M_ = {n: loc["m_" + n] for n in names}
    V_ = {n: loc["v_" + n] for n in names}
    me = 4 * lax.axis_index("x") + 2 * lax.axis_index("y") + lax.axis_index("c")
    xs = x[0]

    wout_b, qkv_b, wo_b, up_b, dn_b, g_win, g_dw, g_ang, g_bq, g_bo = _gather_first(
        conv_w_in[0], conv_w_out[0], w_qkv.reshape(160, D), w_o[0], w_up, w_down,
        [conv_dw[0], attn_norm_g, b_qkv, b_o])
    dw32 = jnp.pad(g_dw.transpose(1, 0, 2).reshape(CONVW, D), ((0, 1), (0, 0)))
    attn_g, bqkv, bo = g_ang.reshape(1, D), g_bq.reshape(1, QKV), g_bo.reshape(1, D)
    qg = jnp.tile(q_norm_g, (1, NH))
    kg = jnp.tile(k_norm_g, (1, NKV))
    e16, e16t, e2, e2t, fold, fold2 = _seg_mats()
    bkt = jnp.asarray(_bucket_table().T)
    bkt4 = jnp.asarray(np.tile(_bucket_table().T, (1, 4)))

    h0, u, glu, g_wout = _conv_in_fwd(xs, conv_norm_g, g_win, conv_b_in, carry=_Gather([(wout_b, None)]))
    w_out = g_wout.reshape(D, D)
    cc, sb, x1, wu0, wd0 = _conv_mid_fwd(glu, xs, dw32, conv_dw_b, conv_ln_g, conv_ln_b, w_out, conv_b_out,
                                         carry=_Gather([(up_b, 0), (dn_b, 0)]))
    h1, act0, x2, g_qkv, g_wo = _mlp_fwd(x1, mlp_norm_g[0:1], wu0, wd0, carry=_Gather([(qkv_b, None), (wo_b, None)]))
    wqkv = g_qkv.reshape(NDEV, D, 160).transpose(1, 0, 2).reshape(D, QKV)
    wo = g_wo.reshape(D, D)
    h2, raw, qn, kn, vv = _attn_qkv_fwd(x2, attn_g, wqkv, bqkv, qg, kg, e16, e16t, e2, e2t)
    ob, x3, wu1, wd1 = _attn_fwd(qn, kn, vv, bkt, rel_bias, sinks, wo, bo, x2, carry=_Gather([(up_b, 1), (dn_b, 1)]))
    h3, act1, dx4, dx4b, sq = _mlp_fwd(x3, mlp_norm_g[1:2], wu1, wd1, target=loss_target[0])
    loss = lax.psum(jnp.sum(sq) * (0.5 / D), ("x", "y", "c"))

    dup1, dx3, dx3b, dg_mlp1 = _mlp_bwd(dx4, x3, mlp_norm_g[1:2], act1, wu1, wd1)
    g_dn1 = _wgrad(act1, dx4b, 512, D, "wgrad_down1").reshape(NDEV, FFB, D)
    g_up1 = _wgrad_split(h3, dup1, 2, "wgrad_up1")
    dqs, dk, dv, d_sinks, d_rel, d_bo, l_up1, l_dn1 = _attn_bwd(
        dx3, dx3b, qn, kn, vv, bkt, bkt4, rel_bias, sinks, wo, carry=_Scatter([g_up1, g_dn1]))
    g_wo = _wgrad(ob, dx3b, D, 512, "wgrad_o").reshape(NDEV, 128, D)
    dqkv, dx2, dx2b, d_bqkv, d_attn_g, d_qg, d_kg = _attn_qkv_bwd(
        dqs, dk, dv, raw, x2, dx3, attn_g, wqkv, qg, kg, e16, e16t, e2, e2t, fold, fold2)
    g_wqkv = _cols_to_blocks(_wgrad(h2, dqkv, D, 640, "wgrad_qkv"), 160)
    dup0, dx1, dx1b, dg_mlp0, l_qkv, l_wo = _mlp_bwd(dx2, x1, mlp_norm_g[0:1], act0, wu0, wd0,
                                                     carry=_Scatter([g_wqkv, g_wo]))
    g_dn0 = _wgrad(act0, dx2b, 512, D, "wgrad_down0").reshape(NDEV, FFB, D)
    g_up0 = _wgrad_split(h1, dup0, 2, "wgrad_up0")
    dc, d_bout, d_lng, d_lnb, d_dwb = _conv_mid_bwd(dx1, dx1b, cc, conv_ln_g, conv_ln_b, w_out)
    g_wout = _wgrad(sb, dx1b, D, 512, "wgrad_conv_out").reshape(NDEV, 128, D)
    dglu, d_dw, l_wout, l_up0, l_dn0 = _dwconv_bwd(dc, glu, dw32, carry=_Scatter([g_wout, g_up0, g_dn0]))
    du, grad_x, d_bin, d_cng = _conv_in_bwd(dglu, u, xs, dx1, conv_norm_g, g_win)
    small_grads = {"conv_norm_g": d_cng, "conv_b_in": d_bin, "conv_dw": d_dw, "conv_dw_b": d_dwb, "conv_ln_g": d_lng,
                   "conv_ln_b": d_lnb, "conv_b_out": d_bout, "attn_norm_g": d_attn_g, "b_qkv": d_bqkv, "q_norm_g": d_qg,
                   "k_norm_g": d_kg, "sinks": d_sinks, "b_o": d_bo, "mlp_norm_g": (dg_mlp0, dg_mlp1)}
    g_win_grad, sred, rred = _wgrad_split(h0, du, 4, "wgrad_conv_in", carry=_SmallReduce(small_grads, d_rel))
    l_win = _scatter_last(g_win_grad)

    big = {}
    for n, lands, nsplit in (("conv_w_in", [l_win], 4), ("conv_w_out", [l_wout], 1), ("w_o", [l_wo], 1),
                             ("w_up", [l_up0, l_up1], 2), ("w_down", [l_dn0, l_dn1], 1)):
        big[n] = _reduce_adamw(lands, W[n], M_[n], V_[n], nsplit, "adamw_" + n)
    flat = lambda t: t.reshape(1, 160, D)
    big["w_qkv"] = [t.reshape(1, D, 160) for t in
                    _reduce_adamw([l_qkv], flat(w_qkv), flat(m_w_qkv), flat(v_w_qkv), 1, "adamw_w_qkv")]
    r = SMALL_ROW["b_qkv"]
    g_bqkv = lax.dynamic_slice_in_dim(jnp.concatenate([sred[r:r + 1], sred[r + 1:r + 2, :QKV - D]], axis=1),
                                      me * 160, 160, axis=1)
    two_d = lambda t: {n: (t[n][0] if n == "conv_dw" else t[n]) for n in SMALL}
    small = _adamw_small(sred, rred, g_bqkv, two_d(W), two_d(M_), two_d(V_))
    small["conv_dw"] = [t[None] for t in small["conv_dw"]]
    outs = [{**{n: big[n][k] for n in BIG}, **{n: small[n][k] for n in SMALL}} for k in range(4)]
    G, outs = outs[0], outs[1:]
    order = ["conv_norm_g", "conv_w_in", "conv_b_in", "conv_dw", "conv_dw_b", "conv_ln_g", "conv_ln_b", "conv_w_out",
             "conv_b_out", "attn_norm_g", "w_qkv", "b_qkv", "q_norm_g", "k_norm_g", "sinks", "w_o", "b_o", "rel_bias",
             "mlp_norm_g", "w_up", "w_down"]
    return (loss, grad_x[None], *[G[n] for n in order], *[outs[0][n] for n in order],
            *[outs[1][n] for n in order], *[outs[2][n] for n in order])
```

```python
import math

import numpy as np
import jax
import jax.numpy as jnp
from jax import lax
from jax.experimental import pallas as pl
from jax.experimental.pallas import tpu as pltpu

F32 = jnp.float32
BF16 = jnp.bfloat16

D = 1024
DFF = 4096
NH, NKV, HD = 16, 2, 64
NPAIR = NH // 2
KV = NKV * HD
QKV = D + 2 * KV
CONVW = 31
WIN = 128
NBKT = 32
EPS = 1e-6
NEG = -1e30
NDEV = 8
FFB = DFF // NDEV
HALO = 32

LR, B1, B2, AEPS, WD, STEP = 0.001, 0.9, 0.999, 1e-08, 0.01, 10

MESH = pl.DeviceIdType.MESH


def _cp(vmem_mb, n_axes=1):
    return pltpu.CompilerParams(dimension_semantics=("arbitrary",) * n_axes, vmem_limit_bytes=vmem_mb << 20)


def _const(shape):
    nd = len(shape)
    return pl.BlockSpec(shape, lambda *_: (0,) * nd, pipeline_mode=pl.Buffered(1))


def _acc(shape):
    nd = len(shape)
    return pl.BlockSpec(shape, lambda *_: (0,) * nd)


def _rows(tm, n):
    return pl.BlockSpec((tm, n), lambda i: (i, 0))


def _sds(shape, dtype):
    return jax.ShapeDtypeStruct(shape, dtype)


def _dot(a, b):
    return jnp.dot(a, b, preferred_element_type=F32)


def _dot_nt(a, b):
    return lax.dot_general(a, b, (((1,), (1,)), ((), ())), preferred_element_type=F32)


def _dot_tn(a, b):
    return lax.dot_general(a, b, (((0,), (0,)), ((), ())), preferred_element_type=F32)


def _dot_hi(x, e):
    x1 = x.astype(BF16)
    x2 = (x - x1.astype(F32)).astype(BF16)
    return _dot(x1, e) + _dot(x2, e)


def _rms(x):
    return lax.rsqrt(jnp.mean(x * x, axis=-1, keepdims=True) + EPS)


def _rms_bwd(dh, x, r, g):
    xh = x * r
    dxh = dh * g
    dx = r * (dxh - xh * jnp.mean(dxh * xh, axis=-1, keepdims=True))
    return dx, jnp.sum(dh * xh, axis=0, keepdims=True)


def _colsum(a):
    return jnp.sum(a, axis=0, keepdims=True)


def _first(ref, val):
    @pl.when(pl.program_id(0) == 0)
    def _():
        ref[...] = jnp.zeros_like(ref)
    ref[...] += val


def _first_cols(ref, vals):
    @pl.when(pl.program_id(0) == 0)
    def _():
        ref[...] = jnp.zeros_like(ref)
    for c0, val in vals:
        ref[:, c0:c0 + val.shape[1]] += val


QW = 256


def _conv_in_fwd(x, g, gw, b, carry=None):
    S = x.shape[0]
    tm = min(512, S)

    def body(x_ref, g_ref, w_ref, b_ref, h_ref, u_ref, glu_ref):
        xv = x_ref[...]
        hb = (xv * _rms(xv) * g_ref[...]).astype(BF16)
        h_ref[...] = hb
        us = []
        for j in range(NDEV):
            uj = b_ref[:, j * QW:(j + 1) * QW] + _dot(hb[:, 0:QW], w_ref[j, :, 0:QW])
            for q in range(1, 4):
                uj = uj + _dot(hb[:, q * QW:(q + 1) * QW], w_ref[j, :, q * QW:(q + 1) * QW])
            u_ref[:, j * QW:(j + 1) * QW] = uj.astype(BF16)
            us.append(uj)
        for j in range(NDEV // 2):
            glu_ref[:, j * QW:(j + 1) * QW] = us[j] * jax.nn.sigmoid(us[j + NDEV // 2])

    return _carried_call(
        body, carry, name="conv_in_fwd", grid=(S // tm,),
        in_specs=[_rows(tm, D), _const((1, D)), _const((NDEV, QW, D)), _const((1, 2 * D))],
        out_specs=[_rows(tm, D), _rows(tm, 2 * D), _rows(tm, D)],
        out_shape=[_sds((S, D), BF16), _sds((S, 2 * D), BF16), _sds((S, D), F32)],
        scratch_shapes=[], compiler_params=_cp(48), args=[x, g, gw, b])


def _fill_ext(ext, halo_ref, cur_ref, tm, keep):
    ext[0:HALO, :] = jnp.where(keep, halo_ref[...], 0.0)
    ext[HALO:HALO + tm, :] = cur_ref[...]


def _shift_copies(dst, src, n):
    for s_ in range(1, 8):
        dst[s_ - 1, 0:n, :] = src[s_:s_ + n, :]


def _tap(src, sh, o, r0, c0):
    a, s_ = divmod(o, 8)
    ref = src if s_ == 0 else sh.at[s_ - 1]
    return ref[r0 + 8 * a:r0 + 8 * a + 32, c0:c0 + 512]


def _conv_mid_fwd(glu, x, dw, dwb, lng, lnb, w, b, carry=None):
    S = x.shape[0]
    tm = min(256, S)
    hb = tm // HALO

    def body(gl_ref, halo_ref, x_ref, dw_ref, dwb_ref, lng_ref, lnb_ref, w_ref, b_ref, c_ref, s_ref, xo_ref, ext, esh):
        i = pl.program_id(0)
        _fill_ext(ext, halo_ref, gl_ref, tm, i > 0)
        _shift_copies(esh, ext, tm + 24)
        for r0 in range(0, tm, 32):
            for c0 in range(0, D, 512):
                acc = jnp.zeros((32, 512), F32) + dwb_ref[:, c0:c0 + 512]
                for k in range(CONVW):
                    acc = acc + dw_ref[k:k + 1, c0:c0 + 512] * _tap(ext, esh, 2 + k, r0, c0)
                c_ref[r0:r0 + 32, c0:c0 + 512] = acc
        c = c_ref[...]
        mu = jnp.mean(c, axis=-1, keepdims=True)
        xc = c - mu
        y = xc * lax.rsqrt(jnp.mean(xc * xc, axis=-1, keepdims=True) + EPS) * lng_ref[...] + lnb_ref[...]
        sb = (y * jax.nn.sigmoid(y)).astype(BF16)
        s_ref[...] = sb
        xo_ref[...] = x_ref[...] + _dot(sb, w_ref[...]) + b_ref[...]

    return _carried_call(
        body, carry, name="conv_mid_fwd", grid=(S // tm,),
        in_specs=[_rows(tm, D), pl.BlockSpec((HALO, D), lambda i: (jnp.maximum(i * hb - 1, 0), 0)), _rows(tm, D),
                  _const((32, D)), _const((1, D)), _const((1, D)), _const((1, D)), _const((D, D)), _const((1, D))],
        out_specs=[_rows(tm, D), _rows(tm, D), _rows(tm, D)],
        out_shape=[_sds((S, D), F32), _sds((S, D), BF16), _sds((S, D), F32)],
        scratch_shapes=[pltpu.VMEM((tm + HALO, D), F32), pltpu.VMEM((7, tm + 24, D), F32)],
        compiler_params=_cp(48), args=[glu, glu, x, dw, dwb, lng, lnb, w, b])


def _mlp_fwd(x, g, wu, wd, target=None, carry=None):
    S = x.shape[0]
    tm = min(512, S)
    last = target is not None

    def body(*refs):
        if last:
            x_ref, g_ref, wu_ref, wd_ref, t_ref, h_ref, act_ref, dy_ref, dyb_ref, sq_ref, acc = refs
        else:
            x_ref, g_ref, wu_ref, wd_ref, h_ref, act_ref, xo_ref, acc = refs
        xv = x_ref[...]
        hb = (xv * _rms(xv) * g_ref[...]).astype(BF16)
        h_ref[...] = hb
        for j in range(NDEV):
            up = _dot(hb[:, :FFB], wu_ref[j, :, :FFB]) + _dot(hb[:, FFB:], wu_ref[j, :, FFB:])
            a = jnp.square(jnp.maximum(up, 0.0)).astype(BF16)
            act_ref[:, j * FFB:(j + 1) * FFB] = a
            if j == 0:
                acc[...] = _dot(a, wd_ref[j])
            else:
                acc[...] += _dot(a, wd_ref[j])
        y = xv + acc[...]
        if last:
            diff = y - t_ref[...]
            dy = diff * (1.0 / D)
            dy_ref[...] = dy
            dyb_ref[...] = dy.astype(BF16)
            _first(sq_ref, _colsum(diff * diff))
        else:
            xo_ref[...] = y

    in_specs = [_rows(tm, D), _const((1, D)), _const((NDEV, FFB, D)), _const((NDEV, FFB, D))]
    args = [x, g, wu, wd]
    out_specs = [_rows(tm, D), _rows(tm, DFF)]
    out_shape = [_sds((S, D), BF16), _sds((S, DFF), BF16)]
    if last:
        in_specs.append(_rows(tm, D))
        args.append(target)
        out_specs += [_rows(tm, D), _rows(tm, D), _acc((1, D))]
        out_shape += [_sds((S, D), F32), _sds((S, D), BF16), _sds((1, D), F32)]
    else:
        out_specs.append(_rows(tm, D))
        out_shape.append(_sds((S, D), F32))
    return _carried_call(
        body, carry, name="mlp_fwd_loss" if last else "mlp_fwd", grid=(S // tm,),
        in_specs=in_specs, out_specs=out_specs, out_shape=out_shape,
        scratch_shapes=[pltpu.VMEM((tm, D), F32)],
        compiler_params=_cp(52), args=args)


def _mlp_bwd(dy, x, g, act, wu, wd, carry=None):
    S = x.shape[0]
    tm = min(512, S)

    def body(dy_ref, x_ref, g_ref, act_ref, wu_ref, wd_ref, dup_ref, dx_ref, dxb_ref, dg_ref, acc):
        db = dy_ref[...].astype(BF16)
        for j in range(NDEV):
            dact = _dot_nt(db, wd_ref[j])
            a = act_ref[:, j * FFB:(j + 1) * FFB].astype(F32)
            dup = (dact * (2.0 * jnp.sqrt(a))).astype(BF16)
            dup_ref[:, j * FFB:(j + 1) * FFB] = dup
            for h_ in range(2):
                part = _dot_nt(dup, wu_ref[j, :, h_ * FFB:(h_ + 1) * FFB])
                if j == 0:
                    acc[:, h_ * FFB:(h_ + 1) * FFB] = part
                else:
                    acc[:, h_ * FFB:(h_ + 1) * FFB] += part
        xv = x_ref[...]
        dxn, dg = _rms_bwd(acc[...], xv, _rms(xv), g_ref[...])
        dx = dy_ref[...] + dxn
        dx_ref[...] = dx
        dxb_ref[...] = dx.astype(BF16)
        _first(dg_ref, dg)

    return _carried_call(
        body, carry, name="mlp_bwd", grid=(S // tm,),
        in_specs=[_rows(tm, D), _rows(tm, D), _const((1, D)), _rows(tm, DFF),
                  _const((NDEV, FFB, D)), _const((NDEV, FFB, D))],
        out_specs=[_rows(tm, DFF), _rows(tm, D), _rows(tm, D), _acc((1, D))],
        out_shape=[_sds((S, DFF), BF16), _sds((S, D), F32), _sds((S, D), BF16), _sds((1, D), F32)],
        scratch_shapes=[pltpu.VMEM((tm, D), F32)],
        compiler_params=_cp(58), args=[dy, x, g, act, wu, wd])


def _wgrad(a, b, tm, tn, name):
    S, M = a.shape
    N = b.shape[1]
    tk = min(4096, S)
    nk = S // tk

    def body(a_ref, b_ref, o_ref, acc):
        k = pl.program_id(2)

        @pl.when(k == 0)
        def _():
            acc[...] = jnp.zeros_like(acc)

        acc[...] += _dot_tn(a_ref[...], b_ref[...])

        @pl.when(k == nk - 1)
        def _():
            o_ref[...] = acc[...].astype(BF16)

    return pl.pallas_call(
        body, name=name, grid=(M // tm, N // tn, nk),
        in_specs=[pl.BlockSpec((tk, tm), lambda i, j, k: (k, i)), pl.BlockSpec((tk, tn), lambda i, j, k: (k, j))],
        out_specs=pl.BlockSpec((tm, tn), lambda i, j, k: (i, j)),
        out_shape=_sds((M, N), BF16),
        scratch_shapes=[pltpu.VMEM((tm, tn), F32)],
        compiler_params=_cp(48, 3))(a, b)


def _wgrad_split(a, b, nsplit, name, carry=None):
    S, M = a.shape
    N = b.shape[1]
    wb = N // NDEV
    tm, tn = M // nsplit, D
    nb = tn // wb
    tk = min(4096, S)
    nk = S // tk

    def body(a_ref, b_ref, o_ref, acc):
        k = pl.program_id(2)

        @pl.when(k == 0)
        def _():
            acc[...] = jnp.zeros_like(acc)

        acc[...] += _dot_tn(a_ref[...], b_ref[...])

        @pl.when(k == nk - 1)
        def _():
            for jj in range(nb):
                o_ref[jj] = acc[:, jj * wb:(jj + 1) * wb].astype(BF16)

    out = _carried_call(
        body, carry, name=name, grid=(nsplit, N // tn, nk),
        in_specs=[pl.BlockSpec((tk, tm), lambda i, j, k: (k, i)), pl.BlockSpec((tk, tn), lambda i, j, k: (k, j))],
        out_specs=[pl.BlockSpec((nb, tm, wb), lambda i, j, k: (j, 0, i))],
        out_shape=[_sds((NDEV, tm, nsplit * wb), BF16)],
        scratch_shapes=[pltpu.VMEM((tm, tn), F32)],
        compiler_params=_cp(48, 3), args=[a, b])
    return out[0] if carry is None else out


def _bucket_table():
    q = np.arange(WIN)[:, None]
    k = np.arange(2 * WIN)[None, :]
    dist = q + WIN - k
    n = np.maximum(dist, 0)
    max_exact = NBKT // 2
    large = max_exact + (np.log(np.maximum(n, 1).astype(np.float32) / max_exact)
                         / math.log(WIN / max_exact) * (NBKT - max_exact)).astype(np.int32)
    large = np.minimum(large, NBKT - 1)
    bkt = np.where(n < max_exact, n, large).astype(np.int32)
    return np.where((dist >= 0) & (dist < WIN), bkt, -1).astype(np.int32)


def _seg_mats():
    e16 = np.zeros((D, 128), np.float32)
    e16[np.arange(D), np.arange(D) // HD] = 1.0
    e2 = np.zeros((KV, 128), np.float32)
    e2[np.arange(KV), np.arange(KV) // HD] = 1.0
    fold = np.zeros((D, 128), np.float32)
    fold[np.arange(D), np.arange(D) % HD] = 1.0
    fold2 = np.zeros((KV, 128), np.float32)
    fold2[np.arange(KV), np.arange(KV) % HD] = 1.0
    return [jnp.asarray(m, BF16) for m in (e16, e16.T, e2, e2.T, fold, fold2)]


def _head_rms(t, e, et):
    r = lax.rsqrt(_dot_hi(t * t, e) * (1.0 / HD) + EPS)
    return _dot_hi(r, et)


def _attn_qkv_fwd(x, g, w, b, qg, kg, e16, e16t, e2, e2t):
    S = x.shape[0]
    tm = min(512, S)

    def body(x_ref, g_ref, w_ref, b_ref, qg_ref, kg_ref, e16_ref, e16t_ref, e2_ref, e2t_ref,
             h_ref, raw_ref, qn_ref, kn_ref, v_ref):
        xv = x_ref[...]
        hb = (xv * _rms(xv) * g_ref[...]).astype(BF16)
        h_ref[...] = hb
        raw = _dot(hb, w_ref[...]) + b_ref[...]
        raw_ref[...] = raw.astype(BF16)
        q = raw[:, :D]
        k = raw[:, D:D + KV]
        qn = (q * _head_rms(q, e16_ref[...], e16t_ref[...]) * qg_ref[...] * 0.125).astype(BF16)
        for bb in range(tm // WIN):
            for p in range(NPAIR):
                r = (bb * NPAIR + p) * WIN
                qn_ref[r:r + WIN, :] = qn[bb * WIN:(bb + 1) * WIN, p * 128:(p + 1) * 128]
        kn_ref[...] = (k * _head_rms(k, e2_ref[...], e2t_ref[...]) * kg_ref[...]).astype(BF16)
        v_ref[...] = raw[:, D + KV:].astype(BF16)

    return pl.pallas_call(
        body, name="attn_qkv_fwd", grid=(S // tm,),
        in_specs=[_rows(tm, D), _const((1, D)), _const((D, QKV)), _const((1, QKV)), _const((1, D)), _const((1, KV)),
                  _const((D, 128)), _const((128, D)), _const((KV, 128)), _const((128, KV))],
        out_specs=[_rows(tm, D), _rows(tm, QKV), _rows(tm * NPAIR, 128), _rows(tm, KV), _rows(tm, KV)],
        out_shape=[_sds((S, D), BF16), _sds((S, QKV), BF16), _sds((S * NPAIR, 128), BF16), _sds((S, KV), BF16),
                   _sds((S, KV), BF16)],
        compiler_params=_cp(48))(x, g, w, b, qg, kg, e16, e16t, e2, e2t)


def _build_bias(bkt_ref, rb_ref, sk_ref, bias_sc, sk_sc):
    bkt = bkt_ref[...]
    row = lax.broadcasted_iota(jnp.int32, (2 * WIN, WIN), 0)
    for h in range(NH):
        p, e = h // 2, h % 2
        g, pp = p // 4, p % 4

        def add(bk, acc, h=h):
            return acc + jnp.where(bkt == bk, rb_ref[bk, h], 0.0)

        bias = lax.fori_loop(0, NBKT, add, jnp.where(bkt < 0, NEG, 0.0).astype(F32))
        bias_sc[0, g, e, :, pp * WIN:(pp + 1) * WIN] = bias
        bias_sc[1, g, e, :, pp * WIN:(pp + 1) * WIN] = jnp.where(row < WIN, NEG, bias)
        sk_sc[g, e, :, pp * WIN:(pp + 1) * WIN] = jnp.zeros((1, WIN), F32) + sk_ref[0, h]


def _band(ref, p0, g0):
    return jnp.concatenate([ref[pl.ds(p0, WIN), :].astype(F32), ref[pl.ds(g0, WIN), :].astype(F32)], axis=0)


def _stacks(f, mlo):
    r = pltpu.roll(f, HD, 1)
    z = jnp.zeros_like(f)
    return ((jnp.where(mlo, f, z).astype(BF16), jnp.where(mlo, z, r).astype(BF16)),
            (jnp.where(mlo, r, z).astype(BF16), jnp.where(mlo, z, f).astype(BF16)))


def _unstack(d, mlo):
    z = jnp.zeros_like(d[0][0])
    return (jnp.where(mlo, d[0][0], z) + pltpu.roll(jnp.where(mlo, z, d[0][1]), HD, 1)
            + pltpu.roll(jnp.where(mlo, d[1][0], z), HD, 1) + jnp.where(mlo, z, d[1][1]))


def _softmax_sink(l, sk):
    m = jnp.maximum(jnp.max(l, axis=0, keepdims=True), sk)
    ex = jnp.exp(l - m)
    es = jnp.exp(sk - m)
    inv = 1.0 / (jnp.sum(ex, axis=0, keepdims=True) + es)
    return ex * inv, es * inv


GROWS = 4 * WIN


def _attn_fwd(qn, kn, vn, bktt, rel_bias, sinks, wo, bo, x, carry=None):
    S = x.shape[0]
    tq = min(512, S)
    nblk = tq // WIN

    def body(q_ref, k_ref, v_ref, bkt_ref, rb_ref, sk_ref, wo_ref, bo_ref, x_ref, o_ref, xo_ref, bias_sc, sk_sc):
        i = pl.program_id(0)

        @pl.when(i == 0)
        def _():
            _build_bias(bkt_ref, rb_ref, sk_ref, bias_sc, sk_sc)

        mlo = lax.broadcasted_iota(jnp.int32, (2 * WIN, KV), 1) < HD

        def blk(bb, carry):
            r0 = pl.multiple_of(bb * WIN, WIN)
            g0 = pl.multiple_of(i * tq + bb * WIN, WIN)
            first = (g0 == 0).astype(jnp.int32)
            p0 = pl.multiple_of(jnp.maximum(g0 - WIN, 0), WIN)
            ks = _stacks(_band(k_ref, p0, g0), mlo)
            vs = _stacks(_band(v_ref, p0, g0), mlo)
            for g in range(NKV):
                qs = q_ref[pl.ds(pl.multiple_of(bb * (NPAIR * WIN) + g * GROWS, GROWS), GROWS), :]
                og = jnp.zeros((GROWS, 128), F32)
                for e in range(2):
                    l = _dot_nt(ks[g][e], qs) + bias_sc[first, g, e]
                    pr, _ = _softmax_sink(l, sk_sc[g, e])
                    og = og + _dot_tn(pr.astype(BF16), vs[g][e])
                for pp in range(4):
                    p = 4 * g + pp
                    o_ref[pl.ds(r0, WIN), p * 128:(p + 1) * 128] = og[pp * WIN:(pp + 1) * WIN].astype(BF16)
            return carry

        lax.fori_loop(0, nblk, blk, 0)
        xo_ref[...] = x_ref[...] + _dot(o_ref[...], wo_ref[...]) + bo_ref[...]

    smem = pl.BlockSpec(memory_space=pltpu.SMEM)
    return _carried_call(
        body, carry, name="attn_fwd", grid=(S // tq,),
        in_specs=[_rows(tq * NPAIR, 128), _const((S, KV)), _const((S, KV)), _const((2 * WIN, WIN)), smem, smem,
                  _const((D, D)), _const((1, D)), _rows(tq, D)],
        out_specs=[_rows(tq, D), _rows(tq, D)],
        out_shape=[_sds((S, D), BF16), _sds((S, D), F32)],
        scratch_shapes=[pltpu.VMEM((2, NKV, 2, 2 * WIN, GROWS), F32), pltpu.VMEM((NKV, 2, 1, GROWS), F32)],
        compiler_params=_cp(48), args=[qn, kn, vn, bktt, rel_bias, sinks, wo, bo, x])


def _attn_bwd(dx, dxb, qn, kn, vn, bktt, bktt4, rel_bias, sinks, wo, carry=None):
    S = dx.shape[0]
    tq = min(512, S)
    nblk = tq // WIN
    nsteps = S // tq

    def body(dx_ref, dxb_ref, q_ref, k_ref, v_ref, bkt_ref, bkt4_ref, rb_ref, sk_ref, wo_ref,
             dq_ref, dk_ref, dv_ref, dsk_ref, drb_ref, dbo_ref, bias_sc, sk_sc, dbias_sc, dsk_sc, do_sc):
        i = pl.program_id(0)

        @pl.when(i == 0)
        def _():
            _build_bias(bkt_ref, rb_ref, sk_ref, bias_sc, sk_sc)
            dbias_sc[...] = jnp.zeros_like(dbias_sc)
            dsk_sc[...] = jnp.zeros_like(dsk_sc)
            dk_ref[...] = jnp.zeros_like(dk_ref)
            dv_ref[...] = jnp.zeros_like(dv_ref)

        _first(dbo_ref, _colsum(dx_ref[...]))
        do = _dot_nt(dxb_ref[...], wo_ref[...]).astype(BF16)
        for bb in range(nblk):
            for p in range(NPAIR):
                r = (bb * NPAIR + p) * WIN
                do_sc[r:r + WIN, :] = do[bb * WIN:(bb + 1) * WIN, p * 128:(p + 1) * 128]

        mlo = lax.broadcasted_iota(jnp.int32, (2 * WIN, KV), 1) < HD

        def blk(bb, carry):
            g0 = pl.multiple_of(i * tq + bb * WIN, WIN)
            first = (g0 == 0).astype(jnp.int32)
            p0 = pl.multiple_of(jnp.maximum(g0 - WIN, 0), WIN)
            ks = _stacks(_band(k_ref, p0, g0), mlo)
            vs = _stacks(_band(v_ref, p0, g0), mlo)
            dks = [[None, None], [None, None]]
            dvs = [[None, None], [None, None]]
            for g in range(NKV):
                rows = pl.ds(pl.multiple_of(bb * (NPAIR * WIN) + g * GROWS, GROWS), GROWS)
                qs = q_ref[rows, :]
                dos = do_sc[rows, :]
                dqs = jnp.zeros((GROWS, 128), F32)
                for e in range(2):
                    l = _dot_nt(ks[g][e], qs) + bias_sc[first, g, e]
                    pr, ps = _softmax_sink(l, sk_sc[g, e])
                    dp = _dot_nt(vs[g][e], dos)
                    dr = jnp.sum(pr * dp, axis=0, keepdims=True)
                    dl = pr * (dp - dr)
                    dsk_sc[g, e] -= ps * dr
                    dbias_sc[g, e] += dl
                    dlb = dl.astype(BF16)
                    dqs = dqs + _dot_tn(dlb, ks[g][e])
                    dks[g][e] = _dot(dlb, qs)
                    dvs[g][e] = _dot(pr.astype(BF16), dos)
                dq_ref[rows, :] = dqs
            for ref, d in ((dk_ref, _unstack(dks, mlo)), (dv_ref, _unstack(dvs, mlo))):
                ref[pl.ds(p0, WIN), :] += d[:WIN]
                ref[pl.ds(g0, WIN), :] += d[WIN:]
            return carry

        lax.fori_loop(0, nblk, blk, 0)

        @pl.when(i == nsteps - 1)
        def _():
            bkt4v = bkt4_ref[...]
            rid = lax.broadcasted_iota(jnp.int32, (NBKT, 128), 0)
            lid = lax.broadcasted_iota(jnp.int32, (NBKT, 128), 1)
            lid1 = lax.broadcasted_iota(jnp.int32, (1, 128), 1)
            dsk = jnp.zeros((1, 128), F32)
            for g in range(NKV):
                for e in range(2):
                    for pp in range(4):
                        h = 2 * (4 * g + pp) + e
                        t = jnp.sum(dsk_sc[g, e, :, pp * WIN:(pp + 1) * WIN], axis=1, keepdims=True)
                        dsk = dsk + jnp.where(lid1 == h, t, 0.0)
            dsk_ref[...] = dsk

            def per_bucket(bk, acc):
                mb = bkt4v == bk
                for g in range(NKV):
                    for e in range(2):
                        t = jnp.sum(jnp.where(mb, dbias_sc[g, e], 0.0), axis=0, keepdims=True)
                        for pp in range(4):
                            h = 2 * (4 * g + pp) + e
                            tt = jnp.sum(t[:, pp * WIN:(pp + 1) * WIN], axis=1, keepdims=True)
                            acc = acc + jnp.where((rid == bk) & (lid == h), tt, 0.0)
                return acc

            drb_ref[...] = lax.fori_loop(0, NBKT, per_bucket, jnp.zeros((NBKT, 128), F32))

    smem = pl.BlockSpec(memory_space=pltpu.SMEM)
    return _carried_call(
        body, carry, name="attn_bwd", grid=(nsteps,),
        in_specs=[_rows(tq, D), _rows(tq, D), _rows(tq * NPAIR, 128), _const((S, KV)), _const((S, KV)),
                  _const((2 * WIN, WIN)), _const((2 * WIN, GROWS)), smem, smem, _const((D, D))],
        out_specs=[_rows(tq * NPAIR, 128), _acc((S, KV)), _acc((S, KV)), _acc((1, 128)), _acc((NBKT, 128)),
                   _acc((1, D))],
        out_shape=[_sds((S * NPAIR, 128), F32), _sds((S, KV), F32), _sds((S, KV), F32), _sds((1, 128), F32),
                   _sds((NBKT, 128), F32), _sds((1, D), F32)],
        scratch_shapes=[pltpu.VMEM((2, NKV, 2, 2 * WIN, GROWS), F32), pltpu.VMEM((NKV, 2, 1, GROWS), F32),
                        pltpu.VMEM((NKV, 2, 2 * WIN, GROWS), F32), pltpu.VMEM((NKV, 2, 1, GROWS), F32),
                        pltpu.VMEM((tq * NPAIR, 128), BF16)],
        compiler_params=_cp(52), args=[dx, dxb, qn, kn, vn, bktt, bktt4, rel_bias, sinks, wo])


def _head_norm_bwd(dn, t, e, et, gt):
    r = _head_rms(t, e, et)
    th = t * r
    dth = dn * gt
    dt = r * (dth - th * _dot_hi(_dot_hi(dth * th, e) * (1.0 / HD), et))
    return dt, _colsum(dn * th)


def _attn_qkv_bwd(dqs, dk, dv, raw, x, dxo, g, w, qg, kg, e16, e16t, e2, e2t, fold, fold2):
    S = x.shape[0]
    tm = min(512, S)

    def body(dq_ref, dk_ref, dv_ref, raw_ref, x_ref, dxo_ref, g_ref, w_ref, qg_ref, kg_ref,
             e16_ref, e16t_ref, e2_ref, e2t_ref, fold_ref, fold2_ref,
             dqkv_ref, dx_ref, dxb_ref, db_ref, dg_ref, dqg_ref, dkg_ref, dq_sc):
        for bb in range(tm // WIN):
            for p in range(NPAIR):
                r = (bb * NPAIR + p) * WIN
                dq_sc[bb * WIN:(bb + 1) * WIN, p * 128:(p + 1) * 128] = dq_ref[r:r + WIN, :]
        dq, cq = _head_norm_bwd(dq_sc[...] * 0.125, raw_ref[:, :D].astype(F32), e16_ref[...], e16t_ref[...], qg_ref[...])
        dk_, ck = _head_norm_bwd(dk_ref[...], raw_ref[:, D:D + KV].astype(F32), e2_ref[...], e2t_ref[...], kg_ref[...])
        dv_ = dv_ref[...]
        _first(dqg_ref, _dot_hi(cq, fold_ref[...]))
        _first(dkg_ref, _dot_hi(ck, fold2_ref[...]))
        _first_cols(db_ref, [(0, _colsum(dq)), (D, _colsum(dk_)), (D + KV, _colsum(dv_))])
        dqb, dkb, dvb = dq.astype(BF16), dk_.astype(BF16), dv_.astype(BF16)
        dqkv_ref[:, :D] = dqb
        dqkv_ref[:, D:D + KV] = dkb
        dqkv_ref[:, D + KV:] = dvb
        dh = (_dot_nt(dqb, w_ref[:, :D]) + _dot_nt(dkb, w_ref[:, D:D + KV]) + _dot_nt(dvb, w_ref[:, D + KV:]))
        xv = x_ref[...]
        dxn, dg = _rms_bwd(dh, xv, _rms(xv), g_ref[...])
        dx = dxo_ref[...] + dxn
        dx_ref[...] = dx
        dxb_ref[...] = dx.astype(BF16)
        _first(dg_ref, dg)

    return pl.pallas_call(
        body, name="attn_qkv_bwd", grid=(S // tm,),
        in_specs=[_rows(tm * NPAIR, 128), _rows(tm, KV), _rows(tm, KV), _rows(tm, QKV), _rows(tm, D), _rows(tm, D),
                  _const((1, D)), _const((D, QKV)), _const((1, D)), _const((1, KV)),
                  _const((D, 128)), _const((128, D)), _const((KV, 128)), _const((128, KV)),
                  _const((D, 128)), _const((KV, 128))],
        out_specs=[_rows(tm, QKV), _rows(tm, D), _rows(tm, D), _acc((1, QKV)), _acc((1, D)), _acc((1, 128)), _acc((1, 128))],
        out_shape=[_sds((S, QKV), BF16), _sds((S, D), F32), _sds((S, D), BF16), _sds((1, QKV), F32), _sds((1, D), F32),
                   _sds((1, 128), F32), _sds((1, 128), F32)],
        scratch_shapes=[pltpu.VMEM((tm, D), F32)],
        compiler_params=_cp(48))(dqs, dk, dv, raw, x, dxo, g, w, qg, kg, e16, e16t, e2, e2t, fold, fold2)


def _conv_mid_bwd(dx, dxb, c, lng, lnb, w):
    S = dx.shape[0]
    tm = min(512, S)

    def body(dx_ref, dxb_ref, c_ref, lng_ref, lnb_ref, w_ref, dc_ref, dbo_ref, dlg_ref, dlb_ref, ddwb_ref):
        _first(dbo_ref, _colsum(dx_ref[...]))
        ds = _dot_nt(dxb_ref[...], w_ref[...])
        c = c_ref[...]
        xc = c - jnp.mean(c, axis=-1, keepdims=True)
        rstd = lax.rsqrt(jnp.mean(xc * xc, axis=-1, keepdims=True) + EPS)
        ch = xc * rstd
        y = ch * lng_ref[...] + lnb_ref[...]
        sg = jax.nn.sigmoid(y)
        dy = ds * (sg * (1.0 + y * (1.0 - sg)))
        _first(dlg_ref, _colsum(dy * ch))
        _first(dlb_ref, _colsum(dy))
        dch = dy * lng_ref[...]
        dc = rstd * (dch - jnp.mean(dch, axis=-1, keepdims=True) - ch * jnp.mean(dch * ch, axis=-1, keepdims=True))
        dc_ref[...] = dc.astype(BF16)
        _first(ddwb_ref, _colsum(dc))

    return pl.pallas_call(
        body, name="conv_mid_bwd", grid=(S // tm,),
        in_specs=[_rows(tm, D), _rows(tm, D), _rows(tm, D), _const((1, D)), _const((1, D)), _const((D, D))],
        out_specs=[_rows(tm, D), _acc((1, D)), _acc((1, D)), _acc((1, D)), _acc((1, D))],
        out_shape=[_sds((S, D), BF16)] + [_sds((1, D), F32)] * 4,
        compiler_params=_cp(40))(dx, dxb, c, lng, lnb, w)


def _dwconv_bwd(dc, glu, dw, carry=None):
    S = dc.shape[0]
    tm = min(256, S)
    hb = tm // HALO
    nsteps = S // tm

    def body(dc_ref, nxt_ref, gl_ref, halo_ref, dw_ref, dgl_ref, ddw_ref, ext, dext, esh, dsh):
        i = pl.program_id(0)
        _fill_ext(ext, halo_ref, gl_ref, tm, i > 0)
        dext[0:tm, :] = dc_ref[...].astype(F32)
        dext[tm:tm + HALO, :] = jnp.where(i < nsteps - 1, nxt_ref[...].astype(F32), 0.0)
        _shift_copies(esh, ext, tm + 24)
        _shift_copies(dsh, dext, tm + 24)

        @pl.when(i == 0)
        def _():
            ddw_ref[...] = jnp.zeros_like(ddw_ref)

        for c0 in range(0, D, 512):
            for r0 in range(0, tm, 32):
                acc = jnp.zeros((32, 512), F32)
                for j in range(CONVW):
                    acc = acc + dw_ref[CONVW - 1 - j:CONVW - j, c0:c0 + 512] * _tap(dext, dsh, j, r0, c0)
                dgl_ref[r0:r0 + 32, c0:c0 + 512] = acc.astype(BF16)
            for k in range(CONVW):
                acc = jnp.zeros((32, 512), F32)
                for r0 in range(0, tm, 32):
                    acc = acc + dext[r0:r0 + 32, c0:c0 + 512] * _tap(ext, esh, 2 + k, r0, c0)
                ddw_ref[k:k + 1, c0:c0 + 512] += _colsum(acc)

    return _carried_call(
        body, carry, name="dwconv_bwd", grid=(nsteps,),
        in_specs=[_rows(tm, D), pl.BlockSpec((HALO, D), lambda i: (jnp.minimum((i + 1) * hb, S // HALO - 1), 0)),
                  _rows(tm, D), pl.BlockSpec((HALO, D), lambda i: (jnp.maximum(i * hb - 1, 0), 0)), _const((32, D))],
        out_specs=[_rows(tm, D), _acc((32, D))],
        out_shape=[_sds((S, D), BF16), _sds((32, D), F32)],
        scratch_shapes=[pltpu.VMEM((tm + HALO, D), F32), pltpu.VMEM((tm + HALO, D), F32),
                        pltpu.VMEM((7, tm + 24, D), F32), pltpu.VMEM((7, tm + 24, D), F32)],
        compiler_params=_cp(48), args=[dc, dc, glu, glu, dw])


def _conv_in_bwd(dglu, u, x, dxo, g, gw, carry=None):
    S = x.shape[0]
    tm = min(512, S)

    def body(dgl_ref, u_ref, x_ref, dxo_ref, g_ref, w_ref, du_ref, dx_ref, db_ref, dg_ref, dh):
        dgl = dgl_ref[...].astype(F32)
        a = u_ref[:, :D].astype(F32)
        sg = jax.nn.sigmoid(u_ref[:, D:].astype(F32))
        da = dgl * sg
        dgt = dgl * a * sg * (1.0 - sg)
        _first_cols(db_ref, [(0, _colsum(da)), (D, _colsum(dgt))])
        dab, dgb = da.astype(BF16), dgt.astype(BF16)
        du_ref[:, :D] = dab
        du_ref[:, D:] = dgb
        for q in range(4):
            part = None
            for j in range(NDEV):
                src = dab if j < NDEV // 2 else dgb
                c0 = (j % (NDEV // 2)) * QW
                t = _dot_nt(src[:, c0:c0 + QW], w_ref[j, :, q * QW:(q + 1) * QW])
                part = t if part is None else part + t
            dh[:, q * QW:(q + 1) * QW] = part
        xv = x_ref[...]
        dxn, dg = _rms_bwd(dh[...], xv, _rms(xv), g_ref[...])
        dx_ref[...] = dxo_ref[...] + dxn
        _first(dg_ref, dg)

    return _carried_call(
        body, carry, name="conv_in_bwd", grid=(S // tm,),
        in_specs=[_rows(tm, D), _rows(tm, 2 * D), _rows(tm, D), _rows(tm, D), _const((1, D)), _const((NDEV, QW, D))],
        out_specs=[_rows(tm, 2 * D), _rows(tm, D), _acc((1, 2 * D)), _acc((1, D))],
        out_shape=[_sds((S, 2 * D), BF16), _sds((S, D), F32), _sds((1, 2 * D), F32), _sds((1, D), F32)],
        scratch_shapes=[pltpu.VMEM((tm, D), F32)], compiler_params=_cp(48), args=[dglu, u, x, dxo, g, gw])


def _coords():
    return lax.axis_index("x"), lax.axis_index("y"), lax.axis_index("c")


def _split(refs, *counts):
    out, k = [], 0
    for n in counts:
        out.append(refs[k:k + n])
        k += n
    return out


def _carried_call(body, carry, *, name, grid, in_specs, out_specs, out_shape, scratch_shapes, compiler_params, args):
    if carry is None:
        return pl.pallas_call(body, name=name, grid=grid, in_specs=in_specs, out_specs=out_specs, out_shape=out_shape,
                              scratch_shapes=scratch_shapes, compiler_params=compiler_params)(*args)
    counts = (len(in_specs), len(carry.args), len(out_specs), len(carry.out_shape), len(scratch_shapes), len(carry.scratch))
    nsteps = int(np.prod(grid))

    def wrapped(*refs):
        ins, cin, outs, cout, scr, cscr = _split(refs, *counts)
        i = pl.program_id(0)
        for d in range(1, len(grid)):
            i = i * grid[d] + pl.program_id(d)
        carry.top(i, nsteps, cin, cout, cscr)
        body(*ins, *outs, *scr)
        carry.bottom(i, nsteps, cin, cout, cscr)

    return pl.pallas_call(
        wrapped, name=name, grid=grid, in_specs=list(in_specs) + carry.in_specs,
        out_specs=list(out_specs) + carry.out_specs, out_shape=list(out_shape) + carry.out_shape,
        scratch_shapes=list(scratch_shapes) + carry.scratch, compiler_params=compiler_params)(*args, *carry.args)


def _routes():
    x, y, c = _coords()
    return (x, y, c), (x, y, 1 - c), [(1 - x, y), (x, 1 - y), (1 - x, 1 - y)]


def _gather_copies(srcs, dsts, send, recv):
    me, sib, chips = _routes()
    c = me[2]

    def cp(a, k, block, to, own=False):
        idx = 4 * block[0] + 2 * block[1] + block[2]
        return pltpu.make_async_remote_copy(
            src_ref=srcs[a] if own else dsts[a].at[idx], dst_ref=dsts[a].at[idx], send_sem=send.at[a, k],
            recv_sem=recv.at[a, k], device_id=to, device_id_type=MESH)

    own, ici_in, fwd, sib_in = [], [], [], []
    for a in range(len(srcs)):
        own += [cp(a, 0, me, sib, True)] + [cp(a, 1 + j, me, (*ch, c), True) for j, ch in enumerate(chips)]
        ici_in += [cp(a, 1 + j, (*ch, c), me) for j, ch in enumerate(chips)]
        fwd += [cp(a, 4 + j, (*ch, c), sib) for j, ch in enumerate(chips)]
        sib_in += [cp(a, 0, sib, me)] + [cp(a, 4 + j, (*ch, 1 - c), me) for j, ch in enumerate(chips)]
    return own, ici_in, fwd, sib_in


class _Gather:
    def __init__(self, srcs):
        self.idx = [i for _, i in srcs]
        self.args = [a for a, _ in srcs]
        n = len(srcs)
        self.in_specs = [pl.BlockSpec(memory_space=pl.ANY)] * n
        self.out_specs = [pl.BlockSpec(memory_space=pl.ANY)] * n
        self.out_shape = [_sds((NDEV,) + (a.shape if i is None else a.shape[1:]), BF16) for a, i in srcs]
        self.scratch = [pltpu.SemaphoreType.DMA((n, 7)), pltpu.SemaphoreType.DMA((n, 7)), pltpu.SemaphoreType.DMA((n,))]

    def _copies(self, ins, outs, scr):
        send, recv, lsem = scr
        srcs = [r if i is None else r.at[i] for r, i in zip(ins, self.idx)]
        me = _routes()[0]
        slot = 4 * me[0] + 2 * me[1] + me[2]
        local = [pltpu.make_async_copy(s_, d_.at[slot], lsem.at[a]) for a, (s_, d_) in enumerate(zip(srcs, outs))]
        return _gather_copies(srcs, outs, send, recv) + (local,)

    def top(self, i, nsteps, ins, outs, scr):
        @pl.when(i == 0)
        def _():
            own, _, _, _, local = self._copies(ins, outs, scr)
            for cp in local + own:
                cp.start()

        @pl.when(i == (3 * nsteps) // 4)
        def _():
            _, ici_in, fwd, _, _ = self._copies(ins, outs, scr)
            for a_, f_ in zip(ici_in, fwd):
                a_.wait_recv()
                f_.start()

    def bottom(self, i, nsteps, ins, outs, scr):
        @pl.when(i == nsteps - 1)
        def _():
            own, _, fwd, sib_in, local = self._copies(ins, outs, scr)
            for cp in sib_in:
                cp.wait_recv()
            for cp in own + fwd:
                cp.wait_send()
            for cp in local:
                cp.wait()


def _scatter_copies(srcs, lands, send, recv):
    x, y, c = _coords()
    me = 4 * x + 2 * y + c
    sends, recvs = [], []
    for a in range(len(srcs)):
        for k in range(1, NDEV):
            peer = (x ^ ((k >> 2) & 1), y ^ ((k >> 1) & 1), c ^ (k & 1))
            pidx = me ^ k
            sends.append(pltpu.make_async_remote_copy(
                src_ref=srcs[a].at[pidx], dst_ref=lands[a].at[me], send_sem=send.at[a, k - 1],
                recv_sem=recv.at[a, k - 1], device_id=peer, device_id_type=MESH))
            recvs.append(pltpu.make_async_remote_copy(
                src_ref=srcs[a].at[pidx], dst_ref=lands[a].at[pidx], send_sem=send.at[a, k - 1],
                recv_sem=recv.at[a, k - 1], device_id=peer, device_id_type=MESH))
    return sends, recvs


class _Scatter:
    def __init__(self, srcs):
        n = len(srcs)
        self.args = list(srcs)
        self.in_specs = [pl.BlockSpec(memory_space=pl.ANY)] * n
        self.out_specs = [pl.BlockSpec(memory_space=pl.ANY)] * n
        self.out_shape = [_sds(a.shape, BF16) for a in srcs]
        self.scratch = [pltpu.SemaphoreType.DMA((n, 7)), pltpu.SemaphoreType.DMA((n, 7)), pltpu.SemaphoreType.DMA((n,))]

    def _copies(self, ins, outs, scr):
        send, recv, lsem = scr
        x, y, c = _coords()
        me = 4 * x + 2 * y + c
        local = [pltpu.make_async_copy(s_.at[me], d_.at[me], lsem.at[a]) for a, (s_, d_) in enumerate(zip(ins, outs))]
        return _scatter_copies(ins, outs, send, recv) + (local,)

    def top(self, i, nsteps, ins, outs, scr):
        @pl.when(i == 0)
        def _():
            sends, _, local = self._copies(ins, outs, scr)
            for cp in local + sends:
                cp.start()

    def bottom(self, i, nsteps, ins, outs, scr):
        @pl.when(i == nsteps - 1)
        def _():
            sends, recvs, local = self._copies(ins, outs, scr)
            for cp in recvs:
                cp.wait_recv()
            for cp in sends:
                cp.wait_send()
            for cp in local:
                cp.wait()


def _gather_first(w_in, w_out, w_qkv, w_o, w_up, w_down, smalls):
    ns = len(smalls)

    def body(*refs):
        (win_ref, wout_ref, wqkv_ref, wo_ref, wup_ref, wdn_ref), s_refs, (out_b, qkv_b, wo_b, up_b, dn_b, gin_ref), \
            gs_refs, (st_in, st_out, st_qkv, st_wo, st_up, st_dn, send, recv, lsem) = _split(refs, 6, ns, 6, ns, 9)
        for q in range(4):
            st_in[:, q * 256:(q + 1) * 256] = win_ref[q * 256:(q + 1) * 256, :].astype(BF16)
        st_out[...] = wout_ref[...].astype(BF16)
        st_qkv[...] = wqkv_ref[...].astype(BF16)
        st_wo[...] = wo_ref[...].astype(BF16)
        for l in range(2):
            for h_ in range(2):
                st_up[l, :, h_ * FFB:(h_ + 1) * FFB] = wup_ref[l, h_ * FFB:(h_ + 1) * FFB, :].astype(BF16)
            st_dn[l] = wdn_ref[l].astype(BF16)
        me = _routes()[0]
        slot = 4 * me[0] + 2 * me[1] + me[2]
        srcs, dsts = [st_in, *s_refs], [gin_ref, *gs_refs]
        local = [pltpu.make_async_copy(s_, d_, lsem.at[a]) for a, (s_, d_) in enumerate(
            [(st_out, out_b), (st_qkv, qkv_b), (st_wo, wo_b), (st_up, up_b), (st_dn, dn_b), (st_in, gin_ref.at[slot])])]
        for cp in local:
            cp.start()
        for s_ref, gs_ref in zip(s_refs, gs_refs):
            gs_ref[slot] = s_ref[...]
        own, ici_in, fwd, sib_in = _gather_copies(srcs, dsts, send, recv)
        for cp in own:
            cp.start()
        for a_, f_ in zip(ici_in, fwd):
            a_.wait_recv()
            f_.start()
        for cp in sib_in:
            cp.wait_recv()
        for cp in own + fwd:
            cp.wait_send()
        for cp in local:
            cp.wait()

    vmem = pl.BlockSpec(memory_space=pltpu.VMEM)
    hbm = pl.BlockSpec(memory_space=pl.ANY)
    return pl.pallas_call(
        body, name="gather_first",
        in_specs=[vmem] * (6 + ns),
        out_specs=[hbm] * 6 + [vmem] * ns,
        out_shape=[_sds((128, D), BF16), _sds((160, D), BF16), _sds((128, D), BF16), _sds((2, FFB, D), BF16),
                   _sds((2, FFB, D), BF16), _sds((NDEV, QW, D), BF16)] + [_sds((NDEV,) + a.shape, F32) for a in smalls],
        scratch_shapes=[pltpu.VMEM((256, D), BF16), pltpu.VMEM((128, D), BF16), pltpu.VMEM((160, D), BF16),
                        pltpu.VMEM((128, D), BF16), pltpu.VMEM((2, FFB, D), BF16), pltpu.VMEM((2, FFB, D), BF16),
                        pltpu.SemaphoreType.DMA((1 + ns, 7)), pltpu.SemaphoreType.DMA((1 + ns, 7)),
                        pltpu.SemaphoreType.DMA((6,))],
        compiler_params=pltpu.CompilerParams(vmem_limit_bytes=40 << 20))(w_in, w_out, w_qkv, w_o, w_up, w_down, *smalls)


SMALL_ROW = {"conv_norm_g": 0, "conv_b_in": 1, "conv_dw_b": 3, "conv_ln_g": 4, "conv_ln_b": 5, "conv_b_out": 6,
             "attn_norm_g": 7, "b_qkv": 8, "q_norm_g": 10, "k_norm_g": 11, "sinks": 12, "b_o": 13, "mlp_norm_g": 14,
             "conv_dw": 16}
SMALL_ROWS = 48


def _fill_small(pk, names, small, s_refs):
    pk[...] = jnp.zeros_like(pk)
    k = 0
    for nm in names:
        cnt = len(small[nm]) if isinstance(small[nm], (tuple, list)) else 1
        r = SMALL_ROW[nm]
        for ref in s_refs[k:k + cnt]:
            rows, lanes = ref.shape
            for c0 in range(0, lanes, D):
                w_ = min(D, lanes - c0)
                pk[r:r + rows, 0:w_] = ref[:, c0:c0 + w_]
                r += rows
        k += cnt


def _flat_small(small):
    flat = []
    for nm in small:
        flat += list(small[nm]) if isinstance(small[nm], (tuple, list)) else [small[nm]]
    return flat


def _allreduce_copies(srcs, lands, send, recv, a0=0):
    x, y, c = _coords()
    me = 4 * x + 2 * y + c
    sends, recvs = [], []
    for a in range(len(srcs)):
        for k in range(1, NDEV):
            peer = (x ^ ((k >> 2) & 1), y ^ ((k >> 1) & 1), c ^ (k & 1))
            pidx = me ^ k
            sends.append(pltpu.make_async_remote_copy(
                src_ref=srcs[a], dst_ref=lands[a].at[me], send_sem=send.at[a0 + a, k - 1],
                recv_sem=recv.at[a0 + a, k - 1], device_id=peer, device_id_type=MESH))
            recvs.append(pltpu.make_async_remote_copy(
                src_ref=srcs[a], dst_ref=lands[a].at[pidx], send_sem=send.at[a0 + a, k - 1],
                recv_sem=recv.at[a0 + a, k - 1], device_id=peer, device_id_type=MESH))
    return sends, recvs


def _device_order_sum(land):
    acc = land[0]
    for s_ in range(1, NDEV):
        acc = acc + land[s_]
    return acc


class _SmallReduce:
    def __init__(self, small, d_rel):
        self.small, self.names = small, list(small)
        self.args = _flat_small(small) + [d_rel]
        vmem = pl.BlockSpec(memory_space=pltpu.VMEM)
        self.in_specs = [vmem] * len(self.args)
        self.out_specs = [vmem, vmem]
        self.out_shape = [_sds((SMALL_ROWS, D), F32), _sds((NBKT, 128), F32)]
        self.scratch = [pltpu.VMEM((SMALL_ROWS, D), F32), pltpu.VMEM((NDEV, SMALL_ROWS, D), F32),
                        pltpu.VMEM((NDEV, NBKT, 128), F32), pltpu.SemaphoreType.DMA((2, 7)),
                        pltpu.SemaphoreType.DMA((2, 7))]

    def _copies(self, ins, scr):
        pk, sland, rland, send, recv = scr
        return _allreduce_copies([pk, ins[-1]], [sland, rland], send, recv)

    def top(self, i, nsteps, ins, outs, scr):
        @pl.when(i == 0)
        def _():
            pk, sland, rland = scr[:3]
            _fill_small(pk, self.names, self.small, ins[:-1])
            x, y, c = _coords()
            me = 4 * x + 2 * y + c
            sland[me] = pk[...]
            rland[me] = ins[-1][...]
            for cp in self._copies(ins, scr)[0]:
                cp.start()

    def bottom(self, i, nsteps, ins, outs, scr):
        @pl.when(i == nsteps - 1)
        def _():
            sends, recvs = self._copies(ins, scr)
            for cp in recvs:
                cp.wait_recv()
            for cp in sends:
                cp.wait_send()
            outs[0][...] = _device_order_sum(scr[1])
            outs[1][...] = _device_order_sum(scr[2])


def _scatter_last(gsrc):
    def body(g_ref, land_ref, send, recv, lsem):
        x, y, c = _coords()
        me = 4 * x + 2 * y + c
        sends, recvs = _scatter_copies([g_ref], [land_ref], send, recv)
        mine = pltpu.make_async_copy(g_ref.at[me], land_ref.at[me], lsem)
        mine.start()
        for cp in sends:
            cp.start()
        for cp in recvs:
            cp.wait_recv()
        for cp in sends:
            cp.wait_send()
        mine.wait()

    hbm = pl.BlockSpec(memory_space=pl.ANY)
    return pl.pallas_call(
        body, name="scatter_last", in_specs=[hbm], out_specs=hbm, out_shape=_sds(gsrc.shape, BF16),
        scratch_shapes=[pltpu.SemaphoreType.DMA((1, 7)), pltpu.SemaphoreType.DMA((1, 7)), pltpu.SemaphoreType.DMA])(gsrc)


def _adam_math(w, g, m, v):
    nm = B1 * m + (1.0 - B1) * g
    nv = B2 * v + (1.0 - B2) * jnp.square(g)
    m_hat = nm / (1.0 - B1 ** STEP)
    v_hat = nv / (1.0 - B2 ** STEP)
    return -LR * (m_hat / (jnp.sqrt(v_hat) + AEPS) + WD * w), nm, nv


def _adamw_small(sred, rred, g_bqkv, W, M_, V_):
    names = list(W)
    nn = len(names)

    def body(*refs):
        (sred_ref, rred_ref, gq_ref), w_refs, m_refs, v_refs, o_refs = _split(refs, 3, nn, nn, nn, 4 * nn)
        x, y, c = _coords()
        me = 4 * x + 2 * y + c
        for i, nm in enumerate(names):
            w_ref, m_ref, v_ref = w_refs[i], m_refs[i], v_refs[i]
            g_ref, d_ref, nm_ref, nv_ref = o_refs[4 * i:4 * i + 4]
            rows, lanes = w_ref.shape
            r = SMALL_ROW.get(nm)
            for c0 in range(0, lanes, D):
                w_ = min(D, lanes - c0)
                cs = slice(c0, c0 + w_)
                if nm == "b_qkv":
                    g = gq_ref[...]
                elif nm == "rel_bias":
                    g = rred_ref[:, 0:w_]
                elif nm in LANE_SHARDED:
                    g = jnp.zeros((rows, w_), F32)
                    for j in range(NDEV):
                        g = g + jnp.where(me == j, sred_ref[r:r + rows, j * w_:(j + 1) * w_], 0.0)
                else:
                    g = sred_ref[r:r + rows, 0:w_]
                    r += rows
                g_ref[:, cs] = g
                d_ref[:, cs], nm_ref[:, cs], nv_ref[:, cs] = _adam_math(w_ref[:, cs], g, m_ref[:, cs], v_ref[:, cs])

    vmem = pl.BlockSpec(memory_space=pltpu.VMEM)
    ws, ms, vs = ([t[nm] for nm in names] for t in (W, M_, V_))
    outs = pl.pallas_call(
        body, name="adamw_small", in_specs=[vmem] * (3 + 3 * nn), out_specs=[vmem] * (4 * nn),
        out_shape=[_sds(W[nm].shape, F32) for nm in names for _ in range(4)],
        compiler_params=pltpu.CompilerParams(vmem_limit_bytes=16 << 20))(sred, rred, g_bqkv, *ws, *ms, *vs)
    return {nm: outs[4 * i:4 * i + 4] for i, nm in enumerate(names)}


def _reduce_adamw(lands, w, m, v, nsplit, name):
    L = len(lands)
    R = lands[0].shape[1]
    wl = D // nsplit
    tr = next(c for c in (256, 160, 128) if R % c == 0)
    nr = R // tr

    def body(*refs):
        l_refs, (w_ref, m_ref, v_ref), (g_ref, d_ref, nm_ref, nv_ref) = _split(refs, L, 3, 4)
        for l in range(L):
            @pl.when(pl.program_id(0) == l)
            def _(l=l):
                g = l_refs[l][0].astype(F32)
                for s_ in range(1, NDEV):
                    g = g + l_refs[l][s_].astype(F32)
                g_ref[...] = g
                d_ref[...], nm_ref[...], nv_ref[...] = _adam_math(w_ref[...], g, m_ref[...], v_ref[...])

    def land_spec(l):
        return pl.BlockSpec((NDEV, tr, wl), lambda ll, h, i: (0, jnp.where(ll == l, i, 0), jnp.where(ll == l, h, 0)))

    spec = pl.BlockSpec((None, tr, wl), lambda ll, h, i: (ll, h * nr + i, 0))
    return pl.pallas_call(
        body, name=name, grid=(L, nsplit, nr),
        in_specs=[land_spec(l) for l in range(L)] + [spec] * 3, out_specs=[spec] * 4,
        out_shape=[_sds(w.shape, F32)] * 4, compiler_params=_cp(48, 3))(*lands, w, m, v)


BIG = ("conv_w_in", "conv_w_out", "w_qkv", "w_o", "w_up", "w_down")
SMALL = ("conv_norm_g", "conv_b_in", "conv_dw", "conv_dw_b", "conv_ln_g", "conv_ln_b", "conv_b_out", "attn_norm_g",
         "b_qkv", "q_norm_g", "k_norm_g", "sinks", "b_o", "rel_bias", "mlp_norm_g")
LANE_SHARDED = ("conv_dw", "attn_norm_g", "b_o")


def _cols_to_blocks(a, n):
    M = a.shape[0]
    return a.reshape(M, NDEV, n).transpose(1, 0, 2).reshape(NDEV, M * n // D, D)


def kernel(x, conv_norm_g, conv_w_in, conv_b_in, conv_dw, conv_dw_b, conv_ln_g, conv_ln_b, conv_w_out, conv_b_out, attn_norm_g, w_qkv, b_qkv, q_norm_g, k_norm_g, sinks, w_o, b_o, rel_bias, mlp_norm_g, w_up, w_down, loss_target, m_conv_norm_g, m_conv_w_in, m_conv_b_in, m_conv_dw, m_conv_dw_b, m_conv_ln_g, m_conv_ln_b, m_conv_w_out, m_conv_b_out, m_attn_norm_g, m_w_qkv, m_b_qkv, m_q_norm_g, m_k_norm_g, m_sinks, m_w_o, m_b_o, m_rel_bias, m_mlp_norm_g, m_w_up, m_w_down, v_conv_norm_g, v_conv_w_in, v_conv_b_in, v_conv_dw, v_conv_dw_b, v_conv_ln_g, v_conv_ln_b, v_conv_w_out, v_conv_b_out, v_attn_norm_g, v_w_qkv, v_b_qkv, v_q_norm_g, v_k_norm_g, v_sinks, v_w_o, v_b_o, v_rel_bias, v_mlp_norm_g, v_w_up, v_w_down):
    names = list(SMALL) + list(BIG)
    loc = dict(locals())
    W = {n: loc[n] for n in names}
    M_ = {n: loc["m_" + n] for n in names}
    V_ = {n: loc["v_" + n] for n in names}
    me = 4 * lax.axis_index("x") + 2 * lax.axis_index("y") + lax.axis_index("c")
    xs = x[0]

    wout_b, qkv_b, wo_b, up_b, dn_b, g_win, g_dw, g_ang, g_bq, g_bo = _gather_first(
        conv_w_in[0], conv_w_out[0], w_qkv.reshape(160, D), w_o[0], w_up, w_down,
        [conv_dw[0], attn_norm_g, b_qkv, b_o])
    dw32 = jnp.pad(g_dw.transpose(1, 0, 2).reshape(CONVW, D), ((0, 1), (0, 0)))
    attn_g, bqkv, bo = g_ang.reshape(1, D), g_bq.reshape(1, QKV), g_bo.reshape(1, D)
    qg = jnp.tile(q_norm_g, (1, NH))
    kg = jnp.tile(k_norm_g, (1, NKV))
    e16, e16t, e2, e2t, fold, fold2 = _seg_mats()
    bkt = jnp.asarray(_bucket_table().T)
    bkt4 = jnp.asarray(np.tile(_bucket_table().T, (1, 4)))

    h0, u, glu, g_wout = _conv_in_fwd(xs, conv_norm_g, g_win, conv_b_in, carry=_Gather([(wout_b, None)]))
    w_out = g_wout.reshape(D, D)
    cc, sb, x1, wu0, wd0 = _conv_mid_fwd(glu, xs, dw32, conv_dw_b, conv_ln_g, conv_ln_b, w_out, conv_b_out,
                                         carry=_Gather([(up_b, 0), (dn_b, 0)]))
    h1, act0, x2, g_qkv, g_wo = _mlp_fwd(x1, mlp_norm_g[0:1], wu0, wd0, carry=_Gather([(qkv_b, None), (wo_b, None)]))
    wqkv = g_qkv.reshape(NDEV, D, 160).transpose(1, 0, 2).reshape(D, QKV)
    wo = g_wo.reshape(D, D)
    h2, raw, qn, kn, vv = _attn_qkv_fwd(x2, attn_g, wqkv, bqkv, qg, kg, e16, e16t, e2, e2t)
    ob, x3, wu1, wd1 = _attn_fwd(qn, kn, vv, bkt, rel_bias, sinks, wo, bo, x2, carry=_Gather([(up_b, 1), (dn_b, 1)]))
    h3, act1, dx4, dx4b, sq = _mlp_fwd(x3, mlp_norm_g[1:2], wu1, wd1, target=loss_target[0])
    loss = lax.psum(jnp.sum(sq) * (0.5 / D), ("x", "y", "c"))

    dup1, dx3, dx3b, dg_mlp1 = _mlp_bwd(dx4, x3, mlp_norm_g[1:2], act1, wu1, wd1)
    g_dn1 = _wgrad(act1, dx4b, 512, D, "wgrad_down1").reshape(NDEV, FFB, D)
    g_up1 = _wgrad_split(h3, dup1, 2, "wgrad_up1")
    dqs, dk, dv, d_sinks, d_rel, d_bo, l_up1, l_dn1 = _attn_bwd(
        dx3, dx3b, qn, kn, vv, bkt, bkt4, rel_bias, sinks, wo, carry=_Scatter([g_up1, g_dn1]))
    g_wo = _wgrad(ob, dx3b, D, 512, "wgrad_o").reshape(NDEV, 128, D)
    dqkv, dx2, dx2b, d_bqkv, d_attn_g, d_qg, d_kg = _attn_qkv_bwd(
        dqs, dk, dv, raw, x2, dx3, attn_g, wqkv, qg, kg, e16, e16t, e2, e2t, fold, fold2)
    g_wqkv = _cols_to_blocks(_wgrad(h2, dqkv, D, 640, "wgrad_qkv"), 160)
    dup0, dx1, dx1b, dg_mlp0, l_qkv, l_wo = _mlp_bwd(dx2, x1, mlp_norm_g[0:1], act0, wu0, wd0,
                                                     carry=_Scatter([g_wqkv, g_wo]))
    g_dn0 = _wgrad(act0, dx2b, 512, D, "wgrad_down0").reshape(NDEV, FFB, D)
    g_up0 = _wgrad_split(h1, dup0, 2, "wgrad_up0")
    dc, d_bout, d_lng, d_lnb, d_dwb = _conv_mid_bwd(dx1, dx1b, cc, conv_ln_g, conv_ln_b, w_out)
    g_wout = _wgrad(sb, dx1b, D, 512, "wgrad_conv_out").reshape(NDEV, 128, D)
    dglu, d_dw, l_wout, l_up0, l_dn0 = _dwconv_bwd(dc, glu, dw32, carry=_Scatter([g_wout, g_up0, g_dn0]))
    du, grad_x, d_bin, d_cng = _conv_in_bwd(dglu, u, xs, dx1, conv_norm_g, g_win)
    small_grads = {"conv_norm_g": d_cng, "conv_b_in": d_bin, "conv_dw": d_dw, "conv_dw_b": d_dwb, "conv_ln_g": d_lng,
                   "conv_ln_b": d_lnb, "conv_b_out": d_bout, "attn_norm_g": d_attn_g, "b_qkv": d_bqkv, "q_norm_g": d_qg,
                   "k_norm_g": d_kg, "sinks": d_sinks, "b_o": d_bo, "mlp_norm_g": (dg_mlp0, dg_mlp1)}
    g_win_grad, sred, rred = _wgrad_split(h0, du, 4, "wgrad_conv_in", carry=_SmallReduce(small_grads, d_rel))
    l_win = _scatter_last(g_win_grad)

    big = {}
    for n, lands, nsplit in (("conv_w_in", [l_win], 4), ("conv_w_out", [l_wout], 1), ("w_o", [l_wo], 1),
                             ("w_up", [l_up0, l_up1], 2), ("w_down", [l_dn0, l_dn1], 1)):
        big[n] = _reduce_adamw(lands, W[n], M_[n], V_[n], nsplit, "adamw_" + n)
    flat = lambda t: t.reshape(1, 160, D)
    big["w_qkv"] = [t.reshape(1, D, 160) for t in
                    _reduce_adamw([l_qkv], flat(w_qkv), flat(m_w_qkv), flat(v_w_qkv), 1, "adamw_w_qkv")]
    r = SMALL_ROW["b_qkv"]
    g_bqkv = lax.dynamic_slice_in_dim(jnp.concatenate([sred[r:r + 1], sred[r + 1:r + 2, :QKV - D]], axis=1),
                                      me * 160, 160, axis=1)
    two_d = lambda t: {n: (t[n][0] if n == "conv_dw" else t[n]) for n in SMALL}
    small = _adamw_small(sred, rred, g_bqkv, two_d(W), two_d(M_), two_d(V_))
    small["conv_dw"] = [t[None] for t in small["conv_dw"]]
    outs = [{**{n: big[n][k] for n in BIG}, **{n: small[n][k] for n in SMALL}} for k in range(4)]
    G, outs = outs[0], outs[1:]
    order = ["conv_norm_g", "conv_w_in", "conv_b_in", "conv_dw", "conv_dw_b", "conv_ln_g", "conv_ln_b", "conv_w_out",
             "conv_b_out", "attn_norm_g", "w_qkv", "b_qkv", "q_norm_g", "k_norm_g", "sinks", "w_o", "b_o", "rel_bias",
             "mlp_norm_g", "w_up", "w_down"]
    return (loss, grad_x[None], *[G[n] for n in order], *[outs[0][n] for n in order],
            *[outs[1][n] for n in order], *[outs[2][n] for n in order])
```

```python
import math

import numpy as np
import jax
import jax.numpy as jnp
from jax import lax
from jax.experimental import pallas as pl
from jax.experimental.pallas import tpu as pltpu

F32 = jnp.float32
BF16 = jnp.bfloat16

D = 1024
DFF = 4096
NH, NKV, HD = 16, 2, 64
NPAIR = NH // 2
KV = NKV * HD
QKV = D + 2 * KV
CONVW = 31
WIN = 128
NBKT = 32
EPS = 1e-6
NEG = -1e30
NDEV = 8
FFB = DFF // NDEV
HALO = 32

LR, B1, B2, AEPS, WD, STEP = 0.001, 0.9, 0.999, 1e-08, 0.01, 10

MESH = pl.DeviceIdType.MESH


def _cp(vmem_mb, n_axes=1):
    return pltpu.CompilerParams(dimension_semantics=("arbitrary",) * n_axes, vmem_limit_bytes=vmem_mb << 20)


def _const(shape):
    nd = len(shape)
    return pl.BlockSpec(shape, lambda *_: (0,) * nd, pipeline_mode=pl.Buffered(1))


def _acc(shape):
    nd = len(shape)
    return pl.BlockSpec(shape, lambda *_: (0,) * nd)


def _rows(tm, n):
    return pl.BlockSpec((tm, n), lambda i: (i, 0))


def _sds(shape, dtype):
    return jax.ShapeDtypeStruct(shape, dtype)


def _dot(a, b):
    return jnp.dot(a, b, preferred_element_type=F32)


def _dot_nt(a, b):
    return lax.dot_general(a, b, (((1,), (1,)), ((), ())), preferred_element_type=F32)


def _dot_tn(a, b):
    return lax.dot_general(a, b, (((0,), (0,)), ((), ())), preferred_element_type=F32)


def _dot_hi(x, e):
    x1 = x.astype(BF16)
    x2 = (x - x1.astype(F32)).astype(BF16)
    return _dot(x1, e) + _dot(x2, e)


def _rms(x):
    return lax.rsqrt(jnp.mean(x * x, axis=-1, keepdims=True) + EPS)


def _rms_bwd(dh, x, r, g):
    xh = x * r
    dxh = dh * g
    dx = r * (dxh - xh * jnp.mean(dxh * xh, axis=-1, keepdims=True))
    return dx, jnp.sum(dh * xh, axis=0, keepdims=True)


def _colsum(a):
    return jnp.sum(a, axis=0, keepdims=True)


def _first(ref, val):
    @pl.when(pl.program_id(0) == 0)
    def _():
        ref[...] = jnp.zeros_like(ref)
    ref[...] += val


def _first_cols(ref, vals):
    @pl.when(pl.program_id(0) == 0)
    def _():
        ref[...] = jnp.zeros_like(ref)
    for c0, val in vals:
        ref[:, c0:c0 + val.shape[1]] += val


QW = 256


def _conv_in_fwd(x, g, gw, b, carry=None):
    S = x.shape[0]
    tm = min(512, S)

    def body(x_ref, g_ref, w_ref, b_ref, h_ref, u_ref, glu_ref):
        xv = x_ref[...]
        hb = (xv * _rms(xv) * g_ref[...]).astype(BF16)
        h_ref[...] = hb
        us = []
        for j in range(NDEV):
            uj = b_ref[:, j * QW:(j + 1) * QW] + _dot(hb[:, 0:QW], w_ref[j, :, 0:QW])
            for q in range(1, 4):
                uj = uj + _dot(hb[:, q * QW:(q + 1) * QW], w_ref[j, :, q * QW:(q + 1) * QW])
            u_ref[:, j * QW:(j + 1) * QW] = uj.astype(BF16)
            us.append(uj)
        for j in range(NDEV // 2):
            glu_ref[:, j * QW:(j + 1) * QW] = us[j] * jax.nn.sigmoid(us[j + NDEV // 2])

    return _carried_call(
        body, carry, name="conv_in_fwd", grid=(S // tm,),
        in_specs=[_rows(tm, D), _const((1, D)), _const((NDEV, QW, D)), _const((1, 2 * D))],
        out_specs=[_rows(tm, D), _rows(tm, 2 * D), _rows(tm, D)],
        out_shape=[_sds((S, D), BF16), _sds((S, 2 * D), BF16), _sds((S, D), F32)],
        scratch_shapes=[], compiler_params=_cp(48), args=[x, g, gw, b])


def _fill_ext(ext, halo_ref, cur_ref, tm, keep):
    ext[0:HALO, :] = jnp.where(keep, halo_ref[...], 0.0)
    ext[HALO:HALO + tm, :] = cur_ref[...]


def _shift_copies(dst, src, n):
    for s_ in range(1, 8):
        dst[s_ - 1, 0:n, :] = src[s_:s_ + n, :]


def _tap(src, sh, o, r0, c0):
    a, s_ = divmod(o, 8)
    ref = src if s_ == 0 else sh.at[s_ - 1]
    return ref[r0 + 8 * a:r0 + 8 * a + 32, c0:c0 + 512]


def _conv_mid_fwd(glu, x, dw, dwb, lng, lnb, w, b, carry=None):
    S = x.shape[0]
    tm = min(256, S)
    hb = tm // HALO

    def body(gl_ref, halo_ref, x_ref, dw_ref, dwb_ref, lng_ref, lnb_ref, w_ref, b_ref, c_ref, s_ref, xo_ref, ext, esh):
        i = pl.program_id(0)
        _fill_ext(ext, halo_ref, gl_ref, tm, i > 0)
        _shift_copies(esh, ext, tm + 24)
        for r0 in range(0, tm, 32):
            for c0 in range(0, D, 512):
                acc = jnp.zeros((32, 512), F32) + dwb_ref[:, c0:c0 + 512]
                for k in range(CONVW):
                    acc = acc + dw_ref[k:k + 1, c0:c0 + 512] * _tap(ext, esh, 2 + k, r0, c0)
                c_ref[r0:r0 + 32, c0:c0 + 512] = acc
        c = c_ref[...]
        mu = jnp.mean(c, axis=-1, keepdims=True)
        xc = c - mu
        y = xc * lax.rsqrt(jnp.mean(xc * xc, axis=-1, keepdims=True) + EPS) * lng_ref[...] + lnb_ref[...]
        sb = (y * jax.nn.sigmoid(y)).astype(BF16)
        s_ref[...] = sb
        xo_ref[...] = x_ref[...] + _dot(sb, w_ref[...]) + b_ref[...]

    return _carried_call(
        body, carry, name="conv_mid_fwd", grid=(S // tm,),
        in_specs=[_rows(tm, D), pl.BlockSpec((HALO, D), lambda i: (jnp.maximum(i * hb - 1, 0), 0)), _rows(tm, D),
                  _const((32, D)), _const((1, D)), _const((1, D)), _const((1, D)), _const((D, D)), _const((1, D))],
        out_specs=[_rows(tm, D), _rows(tm, D), _rows(tm, D)],
        out_shape=[_sds((S, D), F32), _sds((S, D), BF16), _sds((S, D), F32)],
        scratch_shapes=[pltpu.VMEM((tm + HALO, D), F32), pltpu.VMEM((7, tm + 24, D), F32)],
        compiler_params=_cp(48), args=[glu, glu, x, dw, dwb, lng, lnb, w, b])


def _mlp_fwd(x, g, wu, wd, target=None, carry=None):
    S = x.shape[0]
    tm = min(512, S)
    last = target is not None

    def body(*refs):
        if last:
            x_ref, g_ref, wu_ref, wd_ref, t_ref, h_ref, act_ref, dy_ref, dyb_ref, sq_ref, acc = refs
        else:
            x_ref, g_ref, wu_ref, wd_ref, h_ref, act_ref, xo_ref, acc = refs
        xv = x_ref[...]
        hb = (xv * _rms(xv) * g_ref[...]).astype(BF16)
        h_ref[...] = hb
        for j in range(NDEV):
            up = _dot(hb[:, :FFB], wu_ref[j, :, :FFB]) + _dot(hb[:, FFB:], wu_ref[j, :, FFB:])
            a = jnp.square(jnp.maximum(up, 0.0)).astype(BF16)
            act_ref[:, j * FFB:(j + 1) * FFB] = a
            if j == 0:
                acc[...] = _dot(a, wd_ref[j])
            else:
                acc[...] += _dot(a, wd_ref[j])
        y = xv + acc[...]
        if last:
            diff = y - t_ref[...]
            dy = diff * (1.0 / D)
            dy_ref[...] = dy
            dyb_ref[...] = dy.astype(BF16)
            _first(sq_ref, _colsum(diff * diff))
        else:
            xo_ref[...] = y

    in_specs = [_rows(tm, D), _const((1, D)), _const((NDEV, FFB, D)), _const((NDEV, FFB, D))]
    args = [x, g, wu, wd]
    out_specs = [_rows(tm, D), _rows(tm, DFF)]
    out_shape = [_sds((S, D), BF16), _sds((S, DFF), BF16)]
    if last:
        in_specs.append(_rows(tm, D))
        args.append(target)
        out_specs += [_rows(tm, D), _rows(tm, D), _acc((1, D))]
        out_shape += [_sds((S, D), F32), _sds((S, D), BF16), _sds((1, D), F32)]
    else:
        out_specs.append(_rows(tm, D))
        out_shape.append(_sds((S, D), F32))
    return _carried_call(
        body, carry, name="mlp_fwd_loss" if last else "mlp_fwd", grid=(S // tm,),
        in_specs=in_specs, out_specs=out_specs, out_shape=out_shape,
        scratch_shapes=[pltpu.VMEM((tm, D), F32)],
        compiler_params=_cp(52), args=args)


def _mlp_bwd(dy, x, g, act, wu, wd, carry=None):
    S = x.shape[0]
    tm = min(512, S)

    def body(dy_ref, x_ref, g_ref, act_ref, wu_ref, wd_ref, dup_ref, dx_ref, dxb_ref, dg_ref, acc):
        db = dy_ref[...].astype(BF16)
        for j in range(NDEV):
            dact = _dot_nt(db, wd_ref[j])
            a = act_ref[:, j * FFB:(j + 1) * FFB].astype(F32)
            dup = (dact * (2.0 * jnp.sqrt(a))).astype(BF16)
            dup_ref[:, j * FFB:(j + 1) * FFB] = dup
            for h_ in range(2):
                part = _dot_nt(dup, wu_ref[j, :, h_ * FFB:(h_ + 1) * FFB])
                if j == 0:
                    acc[:, h_ * FFB:(h_ + 1) * FFB] = part
                else:
                    acc[:, h_ * FFB:(h_ + 1) * FFB] += part
        xv = x_ref[...]
        dxn, dg = _rms_bwd(acc[...], xv, _rms(xv), g_ref[...])
        dx = dy_ref[...] + dxn
        dx_ref[...] = dx
        dxb_ref[...] = dx.astype(BF16)
        _first(dg_ref, dg)

    return _carried_call(
        body, carry, name="mlp_bwd", grid=(S // tm,),
        in_specs=[_rows(tm, D), _rows(tm, D), _const((1, D)), _rows(tm, DFF),
                  _const((NDEV, FFB, D)), _const((NDEV, FFB, D))],
        out_specs=[_rows(tm, DFF), _rows(tm, D), _rows(tm, D), _acc((1, D))],
        out_shape=[_sds((S, DFF), BF16), _sds((S, D), F32), _sds((S, D), BF16), _sds((1, D), F32)],
        scratch_shapes=[pltpu.VMEM((tm, D), F32)],
        compiler_params=_cp(58), args=[dy, x, g, act, wu, wd])


def _wgrad(a, b, tm, tn, name):
    S, M = a.shape
    N = b.shape[1]
    tk = min(4096, S)
    nk = S // tk

    def body(a_ref, b_ref, o_ref, acc):
        k = pl.program_id(2)

        @pl.when(k == 0)
        def _():
            acc[...] = jnp.zeros_like(acc)

        acc[...] += _dot_tn(a_ref[...], b_ref[...])

        @pl.when(k == nk - 1)
        def _():
            o_ref[...] = acc[...].astype(BF16)

    return pl.pallas_call(
        body, name=name, grid=(M // tm, N // tn, nk),
        in_specs=[pl.BlockSpec((tk, tm), lambda i, j, k: (k, i)), pl.BlockSpec((tk, tn), lambda i, j, k: (k, j))],
        out_specs=pl.BlockSpec((tm, tn), lambda i, j, k: (i, j)),
        out_shape=_sds((M, N), BF16),
        scratch_shapes=[pltpu.VMEM((tm, tn), F32)],
        compiler_params=_cp(48, 3))(a, b)


def _wgrad_split(a, b, nsplit, name, carry=None):
    S, M = a.shape
    N = b.shape[1]
    wb = N // NDEV
    tm, tn = M // nsplit, D
    nb = tn // wb
    tk = min(4096, S)
    nk = S // tk

    def body(a_ref, b_ref, o_ref, acc):
        k = pl.program_id(2)

        @pl.when(k == 0)
        def _():
            acc[...] = jnp.zeros_like(acc)

        acc[...] += _dot_tn(a_ref[...], b_ref[...])

        @pl.when(k == nk - 1)
        def _():
            for jj in range(nb):
                o_ref[jj] = acc[:, jj * wb:(jj + 1) * wb].astype(BF16)

    out = _carried_call(
        body, carry, name=name, grid=(nsplit, N // tn, nk),
        in_specs=[pl.BlockSpec((tk, tm), lambda i, j, k: (k, i)), pl.BlockSpec((tk, tn), lambda i, j, k: (k, j))],
        out_specs=[pl.BlockSpec((nb, tm, wb), lambda i, j, k: (j, 0, i))],
        out_shape=[_sds((NDEV, tm, nsplit * wb), BF16)],
        scratch_shapes=[pltpu.VMEM((tm, tn), F32)],
        compiler_params=_cp(48, 3), args=[a, b])
    return out[0] if carry is None else out


def _bucket_table():
    q = np.arange(WIN)[:, None]
    k = np.arange(2 * WIN)[None, :]
    dist = q + WIN - k
    n = np.maximum(dist, 0)
    max_exact = NBKT // 2
    large = max_exact + (np.log(np.maximum(n, 1).astype(np.float32) / max_exact)
                         / math.log(WIN / max_exact) * (NBKT - max_exact)).astype(np.int32)
    large = np.minimum(large, NBKT - 1)
    bkt = np.where(n < max_exact, n, large).astype(np.int32)
    return np.where((dist >= 0) & (dist < WIN), bkt, -1).astype(np.int32)


def _seg_mats():
    e16 = np.zeros((D, 128), np.float32)
    e16[np.arange(D), np.arange(D) // HD] = 1.0
    e2 = np.zeros((KV, 128), np.float32)
    e2[np.arange(KV), np.arange(KV) // HD] = 1.0
    fold = np.zeros((D, 128), np.float32)
    fold[np.arange(D), np.arange(D) % HD] = 1.0
    fold2 = np.zeros((KV, 128), np.float32)
    fold2[np.arange(KV), np.arange(KV) % HD] = 1.0
    return [jnp.asarray(m, BF16) for m in (e16, e16.T, e2, e2.T, fold, fold2)]


def _head_rms(t, e, et):
    r = lax.rsqrt(_dot_hi(t * t, e) * (1.0 / HD) + EPS)
    return _dot_hi(r, et)


def _attn_qkv_fwd(x, g, w, b, qg, kg, e16, e16t, e2, e2t):
    S = x.shape[0]
    tm = min(512, S)

    def body(x_ref, g_ref, w_ref, b_ref, qg_ref, kg_ref, e16_ref, e16t_ref, e2_ref, e2t_ref,
             h_ref, raw_ref, qn_ref, kn_ref, v_ref):
        xv = x_ref[...]
        hb = (xv * _rms(xv) * g_ref[...]).astype(BF16)
        h_ref[...] = hb
        raw = _dot(hb, w_ref[...]) + b_ref[...]
        raw_ref[...] = raw.astype(BF16)
        q = raw[:, :D]
        k = raw[:, D:D + KV]
        qn = (q * _head_rms(q, e16_ref[...], e16t_ref[...]) * qg_ref[...] * 0.125).astype(BF16)
        for bb in range(tm // WIN):
            for p in range(NPAIR):
                r = (bb * NPAIR + p) * WIN
                qn_ref[r:r + WIN, :] = qn[bb * WIN:(bb + 1) * WIN, p * 128:(p + 1) * 128]
        kn_ref[...] = (k * _head_rms(k, e2_ref[...], e2t_ref[...]) * kg_ref[...]).astype(BF16)
        v_ref[...] = raw[:, D + KV:].astype(BF16)

    return pl.pallas_call(
        body, name="attn_qkv_fwd", grid=(S // tm,),
        in_specs=[_rows(tm, D), _const((1, D)), _const((D, QKV)), _const((1, QKV)), _const((1, D)), _const((1, KV)),
                  _const((D, 128)), _const((128, D)), _const((KV, 128)), _const((128, KV))],
        out_specs=[_rows(tm, D), _rows(tm, QKV), _rows(tm * NPAIR, 128), _rows(tm, KV), _rows(tm, KV)],
        out_shape=[_sds((S, D), BF16), _sds((S, QKV), BF16), _sds((S * NPAIR, 128), BF16), _sds((S, KV), BF16),
                   _sds((S, KV), BF16)],
        compiler_params=_cp(48))(x, g, w, b, qg, kg, e16, e16t, e2, e2t)


def _build_bias(bkt_ref, rb_ref, sk_ref, bias_sc, sk_sc):
    bkt = bkt_ref[...]
    row = lax.broadcasted_iota(jnp.int32, (2 * WIN, WIN), 0)
    for h in range(NH):
        p, e = h // 2, h % 2
        g, pp = p // 4, p % 4

        def add(bk, acc, h=h):
            return acc + jnp.where(bkt == bk, rb_ref[bk, h], 0.0)

        bias = lax.fori_loop(0, NBKT, add, jnp.where(bkt < 0, NEG, 0.0).astype(F32))
        bias_sc[0, g, e, :, pp * WIN:(pp + 1) * WIN] = bias
        bias_sc[1, g, e, :, pp * WIN:(pp + 1) * WIN] = jnp.where(row < WIN, NEG, bias)
        sk_sc[g, e, :, pp * WIN:(pp + 1) * WIN] = jnp.zeros((1, WIN), F32) + sk_ref[0, h]


def _band(ref, p0, g0):
    return jnp.concatenate([ref[pl.ds(p0, WIN), :].astype(F32), ref[pl.ds(g0, WIN), :].astype(F32)], axis=0)


def _stacks(f, mlo):
    r = pltpu.roll(f, HD, 1)
    z = jnp.zeros_like(f)
    return ((jnp.where(mlo, f, z).astype(BF16), jnp.where(mlo, z, r).astype(BF16)),
            (jnp.where(mlo, r, z).astype(BF16), jnp.where(mlo, z, f).astype(BF16)))


def _unstack(d, mlo):
    z = jnp.zeros_like(d[0][0])
    return (jnp.where(mlo, d[0][0], z) + pltpu.roll(jnp.where(mlo, z, d[0][1]), HD, 1)
            + pltpu.roll(jnp.where(mlo, d[1][0], z), HD, 1) + jnp.where(mlo, z, d[1][1]))


def _softmax_sink(l, sk):
    m = jnp.maximum(jnp.max(l, axis=0, keepdims=True), sk)
    ex = jnp.exp(l - m)
    es = jnp.exp(sk - m)
    inv = 1.0 / (jnp.sum(ex, axis=0, keepdims=True) + es)
    return ex * inv, es * inv


GROWS = 4 * WIN


def _attn_fwd(qn, kn, vn, bktt, rel_bias, sinks, wo, bo, x, carry=None):
    S = x.shape[0]
    tq = min(512, S)
    nblk = tq // WIN

    def body(q_ref, k_ref, v_ref, bkt_ref, rb_ref, sk_ref, wo_ref, bo_ref, x_ref, o_ref, xo_ref, bias_sc, sk_sc):
        i = pl.program_id(0)

        @pl.when(i == 0)
        def _():
            _build_bias(bkt_ref, rb_ref, sk_ref, bias_sc, sk_sc)

        mlo = lax.broadcasted_iota(jnp.int32, (2 * WIN, KV), 1) < HD

        def blk(bb, carry):
            r0 = pl.multiple_of(bb * WIN, WIN)
            g0 = pl.multiple_of(i * tq + bb * WIN, WIN)
            first = (g0 == 0).astype(jnp.int32)
            p0 = pl.multiple_of(jnp.maximum(g0 - WIN, 0), WIN)
            ks = _stacks(_band(k_ref, p0, g0), mlo)
            vs = _stacks(_band(v_ref, p0, g0), mlo)
            for g in range(NKV):
                qs = q_ref[pl.ds(pl.multiple_of(bb * (NPAIR * WIN) + g * GROWS, GROWS), GROWS), :]
                og = jnp.zeros((GROWS, 128), F32)
                for e in range(2):
                    l = _dot_nt(ks[g][e], qs) + bias_sc[first, g, e]
                    pr, _ = _softmax_sink(l, sk_sc[g, e])
                    og = og + _dot_tn(pr.astype(BF16), vs[g][e])
                for pp in range(4):
                    p = 4 * g + pp
                    o_ref[pl.ds(r0, WIN), p * 128:(p + 1) * 128] = og[pp * WIN:(pp + 1) * WIN].astype(BF16)
            return carry

        lax.fori_loop(0, nblk, blk, 0)
        xo_ref[...] = x_ref[...] + _dot(o_ref[...], wo_ref[...]) + bo_ref[...]

    smem = pl.BlockSpec(memory_space=pltpu.SMEM)
    return _carried_call(
        body, carry, name="attn_fwd", grid=(S // tq,),
        in_specs=[_rows(tq * NPAIR, 128), _const((S, KV)), _const((S, KV)), _const((2 * WIN, WIN)), smem, smem,
                  _const((D, D)), _const((1, D)), _rows(tq, D)],
        out_specs=[_rows(tq, D), _rows(tq, D)],
        out_shape=[_sds((S, D), BF16), _sds((S, D), F32)],
        scratch_shapes=[pltpu.VMEM((2, NKV, 2, 2 * WIN, GROWS), F32), pltpu.VMEM((NKV, 2, 1, GROWS), F32)],
        compiler_params=_cp(48), args=[qn, kn, vn, bktt, rel_bias, sinks, wo, bo, x])


def _attn_bwd(dx, dxb, qn, kn, vn, bktt, bktt4, rel_bias, sinks, wo, carry=None):
    S = dx.shape[0]
    tq = min(512, S)
    nblk = tq // WIN
    nsteps = S // tq

    def body(dx_ref, dxb_ref, q_ref, k_ref, v_ref, bkt_ref, bkt4_ref, rb_ref, sk_ref, wo_ref,
             dq_ref, dk_ref, dv_ref, dsk_ref, drb_ref, dbo_ref, bias_sc, sk_sc, dbias_sc, dsk_sc, do_sc):
        i = pl.program_id(0)

        @pl.when(i == 0)
        def _():
            _build_bias(bkt_ref, rb_ref, sk_ref, bias_sc, sk_sc)
            dbias_sc[...] = jnp.zeros_like(dbias_sc)
            dsk_sc[...] = jnp.zeros_like(dsk_sc)
            dk_ref[...] = jnp.zeros_like(dk_ref)
            dv_ref[...] = jnp.zeros_like(dv_ref)

        _first(dbo_ref, _colsum(dx_ref[...]))
        do = _dot_nt(dxb_ref[...], wo_ref[...]).astype(BF16)
        for bb in range(nblk):
            for p in range(NPAIR):
                r = (bb * NPAIR + p) * WIN
                do_sc[r:r + WIN, :] = do[bb * WIN:(bb + 1) * WIN, p * 128:(p + 1) * 128]

        mlo = lax.broadcasted_iota(jnp.int32, (2 * WIN, KV), 1) < HD

        def blk(bb, carry):
            g0 = pl.multiple_of(i * tq + bb * WIN, WIN)
            first = (g0 == 0).astype(jnp.int32)
            p0 = pl.multiple_of(jnp.maximum(g0 - WIN, 0), WIN)
            ks = _stacks(_band(k_ref, p0, g0), mlo)
            vs = _stacks(_band(v_ref, p0, g0), mlo)
            dks = [[None, None], [None, None]]
            dvs = [[None, None], [None, None]]
            for g in range(NKV):
                rows = pl.ds(pl.multiple_of(bb * (NPAIR * WIN) + g * GROWS, GROWS), GROWS)
                qs = q_ref[rows, :]
                dos = do_sc[rows, :]
                dqs = jnp.zeros((GROWS, 128), F32)
                for e in range(2):
                    l = _dot_nt(ks[g][e], qs) + bias_sc[first, g, e]
                    pr, ps = _softmax_sink(l, sk_sc[g, e])
                    dp = _dot_nt(vs[g][e], dos)
                    dr = jnp.sum(pr * dp, axis=0, keepdims=True)
                    dl = pr * (dp - dr)
                    dsk_sc[g, e] -= ps * dr
                    dbias_sc[g, e] += dl
                    dlb = dl.astype(BF16)
                    dqs = dqs + _dot_tn(dlb, ks[g][e])
                    dks[g][e] = _dot(dlb, qs)
                    dvs[g][e] = _dot(pr.astype(BF16), dos)
                dq_ref[rows, :] = dqs
            for ref, d in ((dk_ref, _unstack(dks, mlo)), (dv_ref, _unstack(dvs, mlo))):
                ref[pl.ds(p0, WIN), :] += d[:WIN]
                ref[pl.ds(g0, WIN), :] += d[WIN:]
            return carry

        lax.fori_loop(0, nblk, blk, 0)

        @pl.when(i == nsteps - 1)
        def _():
            bkt4v = bkt4_ref[...]
            rid = lax.broadcasted_iota(jnp.int32, (NBKT, 128), 0)
            lid = lax.broadcasted_iota(jnp.int32, (NBKT, 128), 1)
            lid1 = lax.broadcasted_iota(jnp.int32, (1, 128), 1)
            dsk = jnp.zeros((1, 128), F32)
            for g in range(NKV):
                for e in range(2):
                    for pp in range(4):
                        h = 2 * (4 * g + pp) + e
                        t = jnp.sum(dsk_sc[g, e, :, pp * WIN:(pp + 1) * WIN], axis=1, keepdims=True)
                        dsk = dsk + jnp.where(lid1 == h, t, 0.0)
            dsk_ref[...] = dsk

            def per_bucket(bk, acc):
                mb = bkt4v == bk
                for g in range(NKV):
                    for e in range(2):
                        t = jnp.sum(jnp.where(mb, dbias_sc[g, e], 0.0), axis=0, keepdims=True)
                        for pp in range(4):
                            h = 2 * (4 * g + pp) + e
                            tt = jnp.sum(t[:, pp * WIN:(pp + 1) * WIN], axis=1, keepdims=True)
                            acc = acc + jnp.where((rid == bk) & (lid == h), tt, 0.0)
                return acc

            drb_ref[...] = lax.fori_loop(0, NBKT, per_bucket, jnp.zeros((NBKT, 128), F32))

    smem = pl.BlockSpec(memory_space=pltpu.SMEM)
    return _carried_call(
        body, carry, name="attn_bwd", grid=(nsteps,),
        in_specs=[_rows(tq, D), _rows(tq, D), _rows(tq * NPAIR, 128), _const((S, KV)), _const((S, KV)),
                  _const((2 * WIN, WIN)), _const((2 * WIN, GROWS)), smem, smem, _const((D, D))],
        out_specs=[_rows(tq * NPAIR, 128), _acc((S, KV)), _acc((S, KV)), _acc((1, 128)), _acc((NBKT, 128)),
                   _acc((1, D))],
        out_shape=[_sds((S * NPAIR, 128), F32), _sds((S, KV), F32), _sds((S, KV), F32), _sds((1, 128), F32),
                   _sds((NBKT, 128), F32), _sds((1, D), F32)],
        scratch_shapes=[pltpu.VMEM((2, NKV, 2, 2 * WIN, GROWS), F32), pltpu.VMEM((NKV, 2, 1, GROWS), F32),
                        pltpu.VMEM((NKV, 2, 2 * WIN, GROWS), F32), pltpu.VMEM((NKV, 2, 1, GROWS), F32),
                        pltpu.VMEM((tq * NPAIR, 128), BF16)],
        compiler_params=_cp(52), args=[dx, dxb, qn, kn, vn, bktt, bktt4, rel_bias, sinks, wo])


def _head_norm_bwd(dn, t, e, et, gt):
    r = _head_rms(t, e, et)
    th = t * r
    dth = dn * gt
    dt = r * (dth - th * _dot_hi(_dot_hi(dth * th, e) * (1.0 / HD), et))
    return dt, _colsum(dn * th)


def _attn_qkv_bwd(dqs, dk, dv, raw, x, dxo, g, w, qg, kg, e16, e16t, e2, e2t, fold, fold2):
    S = x.shape[0]
    tm = min(512, S)

    def body(dq_ref, dk_ref, dv_ref, raw_ref, x_ref, dxo_ref, g_ref, w_ref, qg_ref, kg_ref,
             e16_ref, e16t_ref, e2_ref, e2t_ref, fold_ref, fold2_ref,
             dqkv_ref, dx_ref, dxb_ref, db_ref, dg_ref, dqg_ref, dkg_ref, dq_sc):
        for bb in range(tm // WIN):
            for p in range(NPAIR):
                r = (bb * NPAIR + p) * WIN
                dq_sc[bb * WIN:(bb + 1) * WIN, p * 128:(p + 1) * 128] = dq_ref[r:r + WIN, :]
        dq, cq = _head_norm_bwd(dq_sc[...] * 0.125, raw_ref[:, :D].astype(F32), e16_ref[...], e16t_ref[...], qg_ref[...])
        dk_, ck = _head_norm_bwd(dk_ref[...], raw_ref[:, D:D + KV].astype(F32), e2_ref[...], e2t_ref[...], kg_ref[...])
        dv_ = dv_ref[...]
        _first(dqg_ref, _dot_hi(cq, fold_ref[...]))
        _first(dkg_ref, _dot_hi(ck, fold2_ref[...]))
        _first_cols(db_ref, [(0, _colsum(dq)), (D, _colsum(dk_)), (D + KV, _colsum(dv_))])
        dqb, dkb, dvb = dq.astype(BF16), dk_.astype(BF16), dv_.astype(BF16)
        dqkv_ref[:, :D] = dqb
        dqkv_ref[:, D:D + KV] = dkb
        dqkv_ref[:, D + KV:] = dvb
        dh = (_dot_nt(dqb, w_ref[:, :D]) + _dot_nt(dkb, w_ref[:, D:D + KV]) + _dot_nt(dvb, w_ref[:, D + KV:]))
        xv = x_ref[...]
        dxn, dg = _rms_bwd(dh, xv, _rms(xv), g_ref[...])
        dx = dxo_ref[...] + dxn
        dx_ref[...] = dx
        dxb_ref[...] = dx.astype(BF16)
        _first(dg_ref, dg)

    return pl.pallas_call(
        body, name="attn_qkv_bwd", grid=(S // tm,),
        in_specs=[_rows(tm * NPAIR, 128), _rows(tm, KV), _rows(tm, KV), _rows(tm, QKV), _rows(tm, D), _rows(tm, D),
                  _const((1, D)), _const((D, QKV)), _const((1, D)), _const((1, KV)),
                  _const((D, 128)), _const((128, D)), _const((KV, 128)), _const((128, KV)),
                  _const((D, 128)), _const((KV, 128))],
        out_specs=[_rows(tm, QKV), _rows(tm, D), _rows(tm, D), _acc((1, QKV)), _acc((1, D)), _acc((1, 128)), _acc((1, 128))],
        out_shape=[_sds((S, QKV), BF16), _sds((S, D), F32), _sds((S, D), BF16), _sds((1, QKV), F32), _sds((1, D), F32),
                   _sds((1, 128), F32), _sds((1, 128), F32)],
        scratch_shapes=[pltpu.VMEM((tm, D), F32)],
        compiler_params=_cp(48))(dqs, dk, dv, raw, x, dxo, g, w, qg, kg, e16, e16t, e2, e2t, fold, fold2)


def _conv_mid_bwd(dx, dxb, c, lng, lnb, w):
    S = dx.shape[0]
    tm = min(512, S)

    def body(dx_ref, dxb_ref, c_ref, lng_ref, lnb_ref, w_ref, dc_ref, dbo_ref, dlg_ref, dlb_ref, ddwb_ref):
        _first(dbo_ref, _colsum(dx_ref[...]))
        ds = _dot_nt(dxb_ref[...], w_ref[...])
        c = c_ref[...]
        xc = c - jnp.mean(c, axis=-1, keepdims=True)
        rstd = lax.rsqrt(jnp.mean(xc * xc, axis=-1, keepdims=True) + EPS)
        ch = xc * rstd
        y = ch * lng_ref[...] + lnb_ref[...]
        sg = jax.nn.sigmoid(y)
        dy = ds * (sg * (1.0 + y * (1.0 - sg)))
        _first(dlg_ref, _colsum(dy * ch))
        _first(dlb_ref, _colsum(dy))
        dch = dy * lng_ref[...]
        dc = rstd * (dch - jnp.mean(dch, axis=-1, keepdims=True) - ch * jnp.mean(dch * ch, axis=-1, keepdims=True))
        dc_ref[...] = dc.astype(BF16)
        _first(ddwb_ref, _colsum(dc))

    return pl.pallas_call(
        body, name="conv_mid_bwd", grid=(S // tm,),
        in_specs=[_rows(tm, D), _rows(tm, D), _rows(tm, D), _const((1, D)), _const((1, D)), _const((D, D))],
        out_specs=[_rows(tm, D), _acc((1, D)), _acc((1, D)), _acc((1, D)), _acc((1, D))],
        out_shape=[_sds((S, D), BF16)] + [_sds((1, D), F32)] * 4,
        compiler_params=_cp(40))(dx, dxb, c, lng, lnb, w)


def _dwconv_bwd(dc, glu, dw, carry=None):
    S = dc.shape[0]
    tm = min(256, S)
    hb = tm // HALO
    nsteps = S // tm

    def body(dc_ref, nxt_ref, gl_ref, halo_ref, dw_ref, dgl_ref, ddw_ref, ext, dext, esh, dsh):
        i = pl.program_id(0)
        _fill_ext(ext, halo_ref, gl_ref, tm, i > 0)
        dext[0:tm, :] = dc_ref[...].astype(F32)
        dext[tm:tm + HALO, :] = jnp.where(i < nsteps - 1, nxt_ref[...].astype(F32), 0.0)
        _shift_copies(esh, ext, tm + 24)
        _shift_copies(dsh, dext, tm + 24)

        @pl.when(i == 0)
        def _():
            ddw_ref[...] = jnp.zeros_like(ddw_ref)

        for c0 in range(0, D, 512):
            for r0 in range(0, tm, 32):
                acc = jnp.zeros((32, 512), F32)
                for j in range(CONVW):
                    acc = acc + dw_ref[CONVW - 1 - j:CONVW - j, c0:c0 + 512] * _tap(dext, dsh, j, r0, c0)
                dgl_ref[r0:r0 + 32, c0:c0 + 512] = acc.astype(BF16)
            for k in range(CONVW):
                acc = jnp.zeros((32, 512), F32)
                for r0 in range(0, tm, 32):
                    acc = acc + dext[r0:r0 + 32, c0:c0 + 512] * _tap(ext, esh, 2 + k, r0, c0)
                ddw_ref[k:k + 1, c0:c0 + 512] += _colsum(acc)

    return _carried_call(
        body, carry, name="dwconv_bwd", grid=(nsteps,),
        in_specs=[_rows(tm, D), pl.BlockSpec((HALO, D), lambda i: (jnp.minimum((i + 1) * hb, S // HALO - 1), 0)),
                  _rows(tm, D), pl.BlockSpec((HALO, D), lambda i: (jnp.maximum(i * hb - 1, 0), 0)), _const((32, D))],
        out_specs=[_rows(tm, D), _acc((32, D))],
        out_shape=[_sds((S, D), BF16), _sds((32, D), F32)],
        scratch_shapes=[pltpu.VMEM((tm + HALO, D), F32), pltpu.VMEM((tm + HALO, D), F32),
                        pltpu.VMEM((7, tm + 24, D), F32), pltpu.VMEM((7, tm + 24, D), F32)],
        compiler_params=_cp(48), args=[dc, dc, glu, glu, dw])


def _conv_in_bwd(dglu, u, x, dxo, g, gw, carry=None):
    S = x.shape[0]
    tm = min(512, S)

    def body(dgl_ref, u_ref, x_ref, dxo_ref, g_ref, w_ref, du_ref, dx_ref, db_ref, dg_ref, dh):
        dgl = dgl_ref[...].astype(F32)
        a = u_ref[:, :D].astype(F32)
        sg = jax.nn.sigmoid(u_ref[:, D:].astype(F32))
        da = dgl * sg
        dgt = dgl * a * sg * (1.0 - sg)
        _first_cols(db_ref, [(0, _colsum(da)), (D, _colsum(dgt))])
        dab, dgb = da.astype(BF16), dgt.astype(BF16)
        du_ref[:, :D] = dab
        du_ref[:, D:] = dgb
        for q in range(4):
            part = None
            for j in range(NDEV):
                src = dab if j < NDEV // 2 else dgb
                c0 = (j % (NDEV // 2)) * QW
                t = _dot_nt(src[:, c0:c0 + QW], w_ref[j, :, q * QW:(q + 1) * QW])
                part = t if part is None else part + t
            dh[:, q * QW:(q + 1) * QW] = part
        xv = x_ref[...]
        dxn, dg = _rms_bwd(dh[...], xv, _rms(xv), g_ref[...])
        dx_ref[...] = dxo_ref[...] + dxn
        _first(dg_ref, dg)

    return _carried_call(
        body, carry, name="conv_in_bwd", grid=(S // tm,),
        in_specs=[_rows(tm, D), _rows(tm, 2 * D), _rows(tm, D), _rows(tm, D), _const((1, D)), _const((NDEV, QW, D))],
        out_specs=[_rows(tm, 2 * D), _rows(tm, D), _acc((1, 2 * D)), _acc((1, D))],
        out_shape=[_sds((S, 2 * D), BF16), _sds((S, D), F32), _sds((1, 2 * D), F32), _sds((1, D), F32)],
        scratch_shapes=[pltpu.VMEM((tm, D), F32)], compiler_params=_cp(48), args=[dglu, u, x, dxo, g, gw])


def _coords():
    return lax.axis_index("x"), lax.axis_index("y"), lax.axis_index("c")


def _split(refs, *counts):
    out, k = [], 0
    for n in counts:
        out.append(refs[k:k + n])
        k += n
    return out


def _carried_call(body, carry, *, name, grid, in_specs, out_specs, out_shape, scratch_shapes, compiler_params, args):
    if carry is None:
        return pl.pallas_call(body, name=name, grid=grid, in_specs=in_specs, out_specs=out_specs, out_shape=out_shape,
                              scratch_shapes=scratch_shapes, compiler_params=compiler_params)(*args)
    counts = (len(in_specs), len(carry.args), len(out_specs), len(carry.out_shape), len(scratch_shapes), len(carry.scratch))
    nsteps = int(np.prod(grid))

    def wrapped(*refs):
        ins, cin, outs, cout, scr, cscr = _split(refs, *counts)
        i = pl.program_id(0)
        for d in range(1, len(grid)):
            i = i * grid[d] + pl.program_id(d)
        carry.top(i, nsteps, cin, cout, cscr)
        body(*ins, *outs, *scr)
        carry.bottom(i, nsteps, cin, cout, cscr)

    return pl.pallas_call(
        wrapped, name=name, grid=grid, in_specs=list(in_specs) + carry.in_specs,
        out_specs=list(out_specs) + carry.out_specs, out_shape=list(out_shape) + carry.out_shape,
        scratch_shapes=list(scratch_shapes) + carry.scratch, compiler_params=compiler_params)(*args, *carry.args)


def _routes():
    x, y, c = _coords()
    return (x, y, c), (x, y, 1 - c), [(1 - x, y), (x, 1 - y), (1 - x, 1 - y)]


def _gather_copies(srcs, dsts, send, recv):
    me, sib, chips = _routes()
    c = me[2]

    def cp(a, k, block, to, own=False):
        idx = 4 * block[0] + 2 * block[1] + block[2]
        return pltpu.make_async_remote_copy(
            src_ref=srcs[a] if own else dsts[a].at[idx], dst_ref=dsts[a].at[idx], send_sem=send.at[a, k],
            recv_sem=recv.at[a, k], device_id=to, device_id_type=MESH)

    own, ici_in, fwd, sib_in = [], [], [], []
    for a in range(len(srcs)):
        own += [cp(a, 0, me, sib, True)] + [cp(a, 1 + j, me, (*ch, c), True) for j, ch in enumerate(chips)]
        ici_in += [cp(a, 1 + j, (*ch, c), me) for j, ch in enumerate(chips)]
        fwd += [cp(a, 4 + j, (*ch, c), sib) for j, ch in enumerate(chips)]
        sib_in += [cp(a, 0, sib, me)] + [cp(a, 4 + j, (*ch, 1 - c), me) for j, ch in enumerate(chips)]
    return own, ici_in, fwd, sib_in


class _Gather:
    def __init__(self, srcs):
        self.idx = [i for _, i in srcs]
        self.args = [a for a, _ in srcs]
        n = len(srcs)
        self.in_specs = [pl.BlockSpec(memory_space=pl.ANY)] * n
        self.out_specs = [pl.BlockSpec(memory_space=pl.ANY)] * n
        self.out_shape = [_sds((NDEV,) + (a.shape if i is None else a.shape[1:]), BF16) for a, i in srcs]
        self.scratch = [pltpu.SemaphoreType.DMA((n, 7)), pltpu.SemaphoreType.DMA((n, 7)), pltpu.SemaphoreType.DMA((n,))]

    def _copies(self, ins, outs, scr):
        send, recv, lsem = scr
        srcs = [r if i is None else r.at[i] for r, i in zip(ins, self.idx)]
        me = _routes()[0]
        slot = 4 * me[0] + 2 * me[1] + me[2]
        local = [pltpu.make_async_copy(s_, d_.at[slot], lsem.at[a]) for a, (s_, d_) in enumerate(zip(srcs, outs))]
        return _gather_copies(srcs, outs, send, recv) + (local,)

    def top(self, i, nsteps, ins, outs, scr):
        @pl.when(i == 0)
        def _():
            own, _, _, _, local = self._copies(ins, outs, scr)
            for cp in local + own:
                cp.start()

        @pl.when(i == (3 * nsteps) // 4)
        def _():
            _, ici_in, fwd, _, _ = self._copies(ins, outs, scr)
            for a_, f_ in zip(ici_in, fwd):
                a_.wait_recv()
                f_.start()

    def bottom(self, i, nsteps, ins, outs, scr):
        @pl.when(i == nsteps - 1)
        def _():
            own, _, fwd, sib_in, local = self._copies(ins, outs, scr)
            for cp in sib_in:
                cp.wait_recv()
            for cp in own + fwd:
                cp.wait_send()
            for cp in local:
                cp.wait()


def _scatter_copies(srcs, lands, send, recv):
    x, y, c = _coords()
    me = 4 * x + 2 * y + c
    sends, recvs = [], []
    for a in range(len(srcs)):
        for k in range(1, NDEV):
            peer = (x ^ ((k >> 2) & 1), y ^ ((k >> 1) & 1), c ^ (k & 1))
            pidx = me ^ k
            sends.append(pltpu.make_async_remote_copy(
                src_ref=srcs[a].at[pidx], dst_ref=lands[a].at[me], send_sem=send.at[a, k - 1],
                recv_sem=recv.at[a, k - 1], device_id=peer, device_id_type=MESH))
            recvs.append(pltpu.make_async_remote_copy(
                src_ref=srcs[a].at[pidx], dst_ref=lands[a].at[pidx], send_sem=send.at[a, k - 1],
                recv_sem=recv.at[a, k - 1], device_id=peer, device_id_type=MESH))
    return sends, recvs


class _Scatter:
    def __init__(self, srcs):
        n = len(srcs)
        self.args = list(srcs)
        self.in_specs = [pl.BlockSpec(memory_space=pl.ANY)] * n
        self.out_specs = [pl.BlockSpec(memory_space=pl.ANY)] * n
        self.out_shape = [_sds(a.shape, BF16) for a in srcs]
        self.scratch = [pltpu.SemaphoreType.DMA((n, 7)), pltpu.SemaphoreType.DMA((n, 7)), pltpu.SemaphoreType.DMA((n,))]

    def _copies(self, ins, outs, scr):
        send, recv, lsem = scr
        x, y, c = _coords()
        me = 4 * x + 2 * y + c
        local = [pltpu.make_async_copy(s_.at[me], d_.at[me], lsem.at[a]) for a, (s_, d_) in enumerate(zip(ins, outs))]
        return _scatter_copies(ins, outs, send, recv) + (local,)

    def top(self, i, nsteps, ins, outs, scr):
        @pl.when(i == 0)
        def _():
            sends, _, local = self._copies(ins, outs, scr)
            for cp in local + sends:
                cp.start()

    def bottom(self, i, nsteps, ins, outs, scr):
        @pl.when(i == nsteps - 1)
        def _():
            sends, recvs, local = self._copies(ins, outs, scr)
            for cp in recvs:
                cp.wait_recv()
            for cp in sends:
                cp.wait_send()
            for cp in local:
                cp.wait()


def _gather_first(w_in, w_out, w_qkv, w_o, w_up, w_down, smalls):
    ns = len(smalls)

    def body(*refs):
        (win_ref, wout_ref, wqkv_ref, wo_ref, wup_ref, wdn_ref), s_refs, (out_b, qkv_b, wo_b, up_b, dn_b, gin_ref), \
            gs_refs, (st_in, st_out, st_qkv, st_wo, st_up, st_dn, send, recv, lsem) = _split(refs, 6, ns, 6, ns, 9)
        for q in range(4):
            st_in[:, q * 256:(q + 1) * 256] = win_ref[q * 256:(q + 1) * 256, :].astype(BF16)
        st_out[...] = wout_ref[...].astype(BF16)
        st_qkv[...] = wqkv_ref[...].astype(BF16)
        st_wo[...] = wo_ref[...].astype(BF16)
        for l in range(2):
            for h_ in range(2):
                st_up[l, :, h_ * FFB:(h_ + 1) * FFB] = wup_ref[l, h_ * FFB:(h_ + 1) * FFB, :].astype(BF16)
            st_dn[l] = wdn_ref[l].astype(BF16)
        me = _routes()[0]
        slot = 4 * me[0] + 2 * me[1] + me[2]
        srcs, dsts = [st_in, *s_refs], [gin_ref, *gs_refs]
        local = [pltpu.make_async_copy(s_, d_, lsem.at[a]) for a, (s_, d_) in enumerate(
            [(st_out, out_b), (st_qkv, qkv_b), (st_wo, wo_b), (st_up, up_b), (st_dn, dn_b), (st_in, gin_ref.at[slot])])]
        for cp in local:
            cp.start()
        for s_ref, gs_ref in zip(s_refs, gs_refs):
            gs_ref[slot] = s_ref[...]
        own, ici_in, fwd, sib_in = _gather_copies(srcs, dsts, send, recv)
        for cp in own:
            cp.start()
        for a_, f_ in zip(ici_in, fwd):
            a_.wait_recv()
            f_.start()
        for cp in sib_in:
            cp.wait_recv()
        for cp in own + fwd:
            cp.wait_send()
        for cp in local:
            cp.wait()

    vmem = pl.BlockSpec(memory_space=pltpu.VMEM)
    hbm = pl.BlockSpec(memory_space=pl.ANY)
    return pl.pallas_call(
        body, name="gather_first",
        in_specs=[vmem] * (6 + ns),
        out_specs=[hbm] * 6 + [vmem] * ns,
        out_shape=[_sds((128, D), BF16), _sds((160, D), BF16), _sds((128, D), BF16), _sds((2, FFB, D), BF16),
                   _sds((2, FFB, D), BF16), _sds((NDEV, QW, D), BF16)] + [_sds((NDEV,) + a.shape, F32) for a in smalls],
        scratch_shapes=[pltpu.VMEM((256, D), BF16), pltpu.VMEM((128, D), BF16), pltpu.VMEM((160, D), BF16),
                        pltpu.VMEM((128, D), BF16), pltpu.VMEM((2, FFB, D), BF16), pltpu.VMEM((2, FFB, D), BF16),
                        pltpu.SemaphoreType.DMA((1 + ns, 7)), pltpu.SemaphoreType.DMA((1 + ns, 7)),
                        pltpu.SemaphoreType.DMA((6,))],
        compiler_params=pltpu.CompilerParams(vmem_limit_bytes=40 << 20))(w_in, w_out, w_qkv, w_o, w_up, w_down, *smalls)


SMALL_ROW = {"conv_norm_g": 0, "conv_b_in": 1, "conv_dw_b": 3, "conv_ln_g": 4, "conv_ln_b": 5, "conv_b_out": 6,
             "attn_norm_g": 7, "b_qkv": 8, "q_norm_g": 10, "k_norm_g": 11, "sinks": 12, "b_o": 13, "mlp_norm_g": 14,
             "conv_dw": 16, "loss_sq": 48}
SMALL_ROWS = 56


def _fill_small(pk, names, small, s_refs):
    pk[...] = jnp.zeros_like(pk)
    k = 0
    for nm in names:
        cnt = len(small[nm]) if isinstance(small[nm], (tuple, list)) else 1
        r = SMALL_ROW[nm]
        for ref in s_refs[k:k + cnt]:
            rows, lanes = ref.shape
            for c0 in range(0, lanes, D):
                w_ = min(D, lanes - c0)
                pk[r:r + rows, 0:w_] = ref[:, c0:c0 + w_]
                r += rows
        k += cnt


def _flat_small(small):
    flat = []
    for nm in small:
        flat += list(small[nm]) if isinstance(small[nm], (tuple, list)) else [small[nm]]
    return flat


def _allreduce_copies(srcs, lands, send, recv, a0=0):
    x, y, c = _coords()
    me = 4 * x + 2 * y + c
    sends, recvs = [], []
    for a in range(len(srcs)):
        for k in range(1, NDEV):
            peer = (x ^ ((k >> 2) & 1), y ^ ((k >> 1) & 1), c ^ (k & 1))
            pidx = me ^ k
            sends.append(pltpu.make_async_remote_copy(
                src_ref=srcs[a], dst_ref=lands[a].at[me], send_sem=send.at[a0 + a, k - 1],
                recv_sem=recv.at[a0 + a, k - 1], device_id=peer, device_id_type=MESH))
            recvs.append(pltpu.make_async_remote_copy(
                src_ref=srcs[a], dst_ref=lands[a].at[pidx], send_sem=send.at[a0 + a, k - 1],
                recv_sem=recv.at[a0 + a, k - 1], device_id=peer, device_id_type=MESH))
    return sends, recvs


def _device_order_sum(land):
    acc = land[0]
    for s_ in range(1, NDEV):
        acc = acc + land[s_]
    return acc


class _SmallReduce:
    def __init__(self, small, d_rel):
        self.small, self.names = small, list(small)
        self.args = _flat_small(small) + [d_rel]
        vmem = pl.BlockSpec(memory_space=pltpu.VMEM)
        self.in_specs = [vmem] * len(self.args)
        self.out_specs = [vmem, vmem]
        self.out_shape = [_sds((SMALL_ROWS, D), F32), _sds((NBKT, 128), F32)]
        self.scratch = [pltpu.VMEM((SMALL_ROWS, D), F32), pltpu.VMEM((NDEV, SMALL_ROWS, D), F32),
                        pltpu.VMEM((NDEV, NBKT, 128), F32), pltpu.SemaphoreType.DMA((2, 7)),
                        pltpu.SemaphoreType.DMA((2, 7))]

    def _copies(self, ins, scr):
        pk, sland, rland, send, recv = scr
        return _allreduce_copies([pk, ins[-1]], [sland, rland], send, recv)

    def top(self, i, nsteps, ins, outs, scr):
        @pl.when(i == 0)
        def _():
            pk, sland, rland = scr[:3]
            _fill_small(pk, self.names, self.small, ins[:-1])
            x, y, c = _coords()
            me = 4 * x + 2 * y + c
            sland[me] = pk[...]
            rland[me] = ins[-1][...]
            for cp in self._copies(ins, scr)[0]:
                cp.start()

    def bottom(self, i, nsteps, ins, outs, scr):
        @pl.when(i == nsteps - 1)
        def _():
            sends, recvs = self._copies(ins, scr)
            for cp in recvs:
                cp.wait_recv()
            for cp in sends:
                cp.wait_send()
            outs[0][...] = _device_order_sum(scr[1])
            outs[1][...] = _device_order_sum(scr[2])


def _scatter_last(gsrc):
    def body(g_ref, land_ref, send, recv, lsem):
        x, y, c = _coords()
        me = 4 * x + 2 * y + c
        sends, recvs = _scatter_copies([g_ref], [land_ref], send, recv)
        mine = pltpu.make_async_copy(g_ref.at[me], land_ref.at[me], lsem)
        mine.start()
        for cp in sends:
            cp.start()
        for cp in recvs:
            cp.wait_recv()
        for cp in sends:
            cp.wait_send()
        mine.wait()

    hbm = pl.BlockSpec(memory_space=pl.ANY)
    return pl.pallas_call(
        body, name="scatter_last", in_specs=[hbm], out_specs=hbm, out_shape=_sds(gsrc.shape, BF16),
        scratch_shapes=[pltpu.SemaphoreType.DMA((1, 7)), pltpu.SemaphoreType.DMA((1, 7)), pltpu.SemaphoreType.DMA])(gsrc)


def _adam_math(w, g, m, v):
    nm = B1 * m + (1.0 - B1) * g
    nv = B2 * v + (1.0 - B2) * jnp.square(g)
    m_hat = nm / (1.0 - B1 ** STEP)
    v_hat = nv / (1.0 - B2 ** STEP)
    return -LR * (m_hat / (jnp.sqrt(v_hat) + AEPS) + WD * w), nm, nv


def _adamw_small(sred, rred, g_bqkv, W, M_, V_):
    names = list(W)
    nn = len(names)

    def body(*refs):
        (sred_ref, rred_ref, gq_ref), w_refs, m_refs, v_refs, o_refs = _split(refs, 3, nn, nn, nn, 4 * nn)
        x, y, c = _coords()
        me = 4 * x + 2 * y + c
        for i, nm in enumerate(names):
            w_ref, m_ref, v_ref = w_refs[i], m_refs[i], v_refs[i]
            g_ref, d_ref, nm_ref, nv_ref = o_refs[4 * i:4 * i + 4]
            rows, lanes = w_ref.shape
            r = SMALL_ROW.get(nm)
            for c0 in range(0, lanes, D):
                w_ = min(D, lanes - c0)
                cs = slice(c0, c0 + w_)
                if nm == "b_qkv":
                    g = gq_ref[...]
                elif nm == "rel_bias":
                    g = rred_ref[:, 0:w_]
                elif nm in LANE_SHARDED:
                    g = jnp.zeros((rows, w_), F32)
                    for j in range(NDEV):
                        g = g + jnp.where(me == j, sred_ref[r:r + rows, j * w_:(j + 1) * w_], 0.0)
                else:
                    g = sred_ref[r:r + rows, 0:w_]
                    r += rows
                g_ref[:, cs] = g
                d_ref[:, cs], nm_ref[:, cs], nv_ref[:, cs] = _adam_math(w_ref[:, cs], g, m_ref[:, cs], v_ref[:, cs])

    vmem = pl.BlockSpec(memory_space=pltpu.VMEM)
    ws, ms, vs = ([t[nm] for nm in names] for t in (W, M_, V_))
    outs = pl.pallas_call(
        body, name="adamw_small", in_specs=[vmem] * (3 + 3 * nn), out_specs=[vmem] * (4 * nn),
        out_shape=[_sds(W[nm].shape, F32) for nm in names for _ in range(4)],
        compiler_params=pltpu.CompilerParams(vmem_limit_bytes=16 << 20))(sred, rred, g_bqkv, *ws, *ms, *vs)
    return {nm: outs[4 * i:4 * i + 4] for i, nm in enumerate(names)}


def _reduce_adamw(lands, w, m, v, nsplit, name):
    L = len(lands)
    R = lands[0].shape[1]
    wl = D // nsplit
    tr = next(c for c in (256, 160, 128) if R % c == 0)
    nr = R // tr

    def body(*refs):
        l_refs, (w_ref, m_ref, v_ref), (g_ref, d_ref, nm_ref, nv_ref) = _split(refs, L, 3, 4)
        for l in range(L):
            @pl.when(pl.program_id(0) == l)
            def _(l=l):
                g = l_refs[l][0].astype(F32)
                for s_ in range(1, NDEV):
                    g = g + l_refs[l][s_].astype(F32)
                g_ref[...] = g
                d_ref[...], nm_ref[...], nv_ref[...] = _adam_math(w_ref[...], g, m_ref[...], v_ref[...])

    def land_spec(l):
        return pl.BlockSpec((NDEV, tr, wl), lambda ll, h, i: (0, jnp.where(ll == l, i, 0), jnp.where(ll == l, h, 0)))

    spec = pl.BlockSpec((None, tr, wl), lambda ll, h, i: (ll, h * nr + i, 0))
    return pl.pallas_call(
        body, name=name, grid=(L, nsplit, nr),
        in_specs=[land_spec(l) for l in range(L)] + [spec] * 3, out_specs=[spec] * 4,
        out_shape=[_sds(w.shape, F32)] * 4, compiler_params=_cp(48, 3))(*lands, w, m, v)


BIG = ("conv_w_in", "conv_w_out", "w_qkv", "w_o", "w_up", "w_down")
SMALL = ("conv_norm_g", "conv_b_in", "conv_dw", "conv_dw_b", "conv_ln_g", "conv_ln_b", "conv_b_out", "attn_norm_g",
         "b_qkv", "q_norm_g", "k_norm_g", "sinks", "b_o", "rel_bias", "mlp_norm_g")
LANE_SHARDED = ("conv_dw", "attn_norm_g", "b_o")


def _cols_to_blocks(a, n):
    M = a.shape[0]
    return a.reshape(M, NDEV, n).transpose(1, 0, 2).reshape(NDEV, M * n // D, D)


def kernel(x, conv_norm_g, conv_w_in, conv_b_in, conv_dw, conv_dw_b, conv_ln_g, conv_ln_b, conv_w_out, conv_b_out, attn_norm_g, w_qkv, b_qkv, q_norm_g, k_norm_g, sinks, w_o, b_o, rel_bias, mlp_norm_g, w_up, w_down, loss_target, m_conv_norm_g, m_conv_w_in, m_conv_b_in, m_conv_dw, m_conv_dw_b, m_conv_ln_g, m_conv_ln_b, m_conv_w_out, m_conv_b_out, m_attn_norm_g, m_w_qkv, m_b_qkv, m_q_norm_g, m_k_norm_g, m_sinks, m_w_o, m_b_o, m_rel_bias, m_mlp_norm_g, m_w_up, m_w_down, v_conv_norm_g, v_conv_w_in, v_conv_b_in, v_conv_dw, v_conv_dw_b, v_conv_ln_g, v_conv_ln_b, v_conv_w_out, v_conv_b_out, v_attn_norm_g, v_w_qkv, v_b_qkv, v_q_norm_g, v_k_norm_g, v_sinks, v_w_o, v_b_o, v_rel_bias, v_mlp_norm_g, v_w_up, v_w_down):
    names = list(SMALL) + list(BIG)
    loc = dict(locals())
    W = {n: loc[n] for n in names}
    M_ = {n: loc["m_" + n] for n in names}
    V_ = {n: loc["v_" + n] for n in names}
    me = 4 * lax.axis_index("x") + 2 * lax.axis_index("y") + lax.axis_index("c")
    xs = x[0]

    wout_b, qkv_b, wo_b, up_b, dn_b, g_win, g_dw, g_ang, g_bq, g_bo = _gather_first(
        conv_w_in[0], conv_w_out[0], w_qkv.reshape(160, D), w_o[0], w_up, w_down,
        [conv_dw[0], attn_norm_g, b_qkv, b_o])
    dw32 = jnp.pad(g_dw.transpose(1, 0, 2).reshape(CONVW, D), ((0, 1), (0, 0)))
    attn_g, bqkv, bo = g_ang.reshape(1, D), g_bq.reshape(1, QKV), g_bo.reshape(1, D)
    qg = jnp.tile(q_norm_g, (1, NH))
    kg = jnp.tile(k_norm_g, (1, NKV))
    e16, e16t, e2, e2t, fold, fold2 = _seg_mats()
    bkt = jnp.asarray(_bucket_table().T)
    bkt4 = jnp.asarray(np.tile(_bucket_table().T, (1, 4)))

    h0, u, glu, g_wout = _conv_in_fwd(xs, conv_norm_g, g_win, conv_b_in, carry=_Gather([(wout_b, None)]))
    w_out = g_wout.reshape(D, D)
    cc, sb, x1, wu0, wd0 = _conv_mid_fwd(glu, xs, dw32, conv_dw_b, conv_ln_g, conv_ln_b, w_out, conv_b_out,
                                         carry=_Gather([(up_b, 0), (dn_b, 0)]))
    h1, act0, x2, g_qkv, g_wo = _mlp_fwd(x1, mlp_norm_g[0:1], wu0, wd0, carry=_Gather([(qkv_b, None), (wo_b, None)]))
    wqkv = g_qkv.reshape(NDEV, D, 160).transpose(1, 0, 2).reshape(D, QKV)
    wo = g_wo.reshape(D, D)
    h2, raw, qn, kn, vv = _attn_qkv_fwd(x2, attn_g, wqkv, bqkv, qg, kg, e16, e16t, e2, e2t)
    ob, x3, wu1, wd1 = _attn_fwd(qn, kn, vv, bkt, rel_bias, sinks, wo, bo, x2, carry=_Gather([(up_b, 1), (dn_b, 1)]))
    h3, act1, dx4, dx4b, sq = _mlp_fwd(x3, mlp_norm_g[1:2], wu1, wd1, target=loss_target[0])

    dup1, dx3, dx3b, dg_mlp1 = _mlp_bwd(dx4, x3, mlp_norm_g[1:2], act1, wu1, wd1)
    g_dn1 = _wgrad(act1, dx4b, 512, D, "wgrad_down1").reshape(NDEV, FFB, D)
    g_up1 = _wgrad_split(h3, dup1, 2, "wgrad_up1")
    dqs, dk, dv, d_sinks, d_rel, d_bo, l_up1, l_dn1 = _attn_bwd(
        dx3, dx3b, qn, kn, vv, bkt, bkt4, rel_bias, sinks, wo, carry=_Scatter([g_up1, g_dn1]))
    g_wo = _wgrad(ob, dx3b, D, 512, "wgrad_o").reshape(NDEV, 128, D)
    dqkv, dx2, dx2b, d_bqkv, d_attn_g, d_qg, d_kg = _attn_qkv_bwd(
        dqs, dk, dv, raw, x2, dx3, attn_g, wqkv, qg, kg, e16, e16t, e2, e2t, fold, fold2)
    g_wqkv = _cols_to_blocks(_wgrad(h2, dqkv, D, 640, "wgrad_qkv"), 160)
    dup0, dx1, dx1b, dg_mlp0, l_qkv, l_wo = _mlp_bwd(dx2, x1, mlp_norm_g[0:1], act0, wu0, wd0,
                                                     carry=_Scatter([g_wqkv, g_wo]))
    g_dn0 = _wgrad(act0, dx2b, 512, D, "wgrad_down0").reshape(NDEV, FFB, D)
    g_up0 = _wgrad_split(h1, dup0, 2, "wgrad_up0")
    dc, d_bout, d_lng, d_lnb, d_dwb = _conv_mid_bwd(dx1, dx1b, cc, conv_ln_g, conv_ln_b, w_out)
    g_wout = _wgrad(sb, dx1b, D, 512, "wgrad_conv_out").reshape(NDEV, 128, D)
    dglu, d_dw, l_wout, l_up0, l_dn0 = _dwconv_bwd(dc, glu, dw32, carry=_Scatter([g_wout, g_up0, g_dn0]))
    du, grad_x, d_bin, d_cng = _conv_in_bwd(dglu, u, xs, dx1, conv_norm_g, g_win)
    small_grads = {"loss_sq": sq, "conv_norm_g": d_cng, "conv_b_in": d_bin, "conv_dw": d_dw, "conv_dw_b": d_dwb, "conv_ln_g": d_lng,
                   "conv_ln_b": d_lnb, "conv_b_out": d_bout, "attn_norm_g": d_attn_g, "b_qkv": d_bqkv, "q_norm_g": d_qg,
                   "k_norm_g": d_kg, "sinks": d_sinks, "b_o": d_bo, "mlp_norm_g": (dg_mlp0, dg_mlp1)}
    g_win_grad, sred, rred = _wgrad_split(h0, du, 4, "wgrad_conv_in", carry=_SmallReduce(small_grads, d_rel))
    l_win = _scatter_last(g_win_grad)
    loss = jnp.sum(sred[SMALL_ROW["loss_sq"]]) * (0.5 / D)

    big = {}
    for n, lands, nsplit in (("conv_w_in", [l_win], 4), ("conv_w_out", [l_wout], 1), ("w_o", [l_wo], 1),
                             ("w_up", [l_up0, l_up1], 2), ("w_down", [l_dn0, l_dn1], 1)):
        big[n] = _reduce_adamw(lands, W[n], M_[n], V_[n], nsplit, "adamw_" + n)
    flat = lambda t: t.reshape(1, 160, D)
    big["w_qkv"] = [t.reshape(1, D, 160) for t in
                    _reduce_adamw([l_qkv], flat(w_qkv), flat(m_w_qkv), flat(v_w_qkv), 1, "adamw_w_qkv")]
    r = SMALL_ROW["b_qkv"]
    g_bqkv = lax.dynamic_slice_in_dim(jnp.concatenate([sred[r:r + 1], sred[r + 1:r + 2, :QKV - D]], axis=1),
                                      me * 160, 160, axis=1)
    two_d = lambda t: {n: (t[n][0] if n == "conv_dw" else t[n]) for n in SMALL}
    small = _adamw_small(sred, rred, g_bqkv, two_d(W), two_d(M_), two_d(V_))
    small["conv_dw"] = [t[None] for t in small["conv_dw"]]
    outs = [{**{n: big[n][k] for n in BIG}, **{n: small[n][k] for n in SMALL}} for k in range(4)]
    G, outs = outs[0], outs[1:]
    order = ["conv_norm_g", "conv_w_in", "conv_b_in", "conv_dw", "conv_dw_b", "conv_ln_g", "conv_ln_b", "conv_w_out",
             "conv_b_out", "attn_norm_g", "w_qkv", "b_qkv", "q_norm_g", "k_norm_g", "sinks", "w_o", "b_o", "rel_bias",
             "mlp_norm_g", "w_up", "w_down"]
    return (loss, grad_x[None], *[G[n] for n in order], *[outs[0][n] for n in order],
            *[outs[1][n] for n in order], *[outs[2][n] for n in order])
```
